```python
import math
import jax, jax.numpy as jnp
from jax import lax
import numpy as np

D_MODEL = 1024
BATCH = 32
SEQ = 2048
DEPTH = 4

HEAD_DIM = 64
Q_BLOCK = 128
A_HEADS = 8
A_KV = 1
A_WINDOW = 128
B_HEADS = 4
B_KV = 1
CMP_LEN = 32
CMP_STRIDE = 16
CMP_HIDDEN = 128
SLC_LEN = 64
SLC_TOP = 8
B_WINDOW = 512
C_HEADS = 4
A_Q = A_HEADS * HEAD_DIM
A_KVW = A_KV * HEAD_DIM
B_Q = B_HEADS * HEAD_DIM
B_KVW = B_KV * HEAD_DIM
C_W = C_HEADS * HEAD_DIM
MIX_WIDTH = A_Q + B_Q + C_W
D_FF = -(-8 * D_MODEL // (3 * 256)) * 256
SPLIT_SIZES = (A_Q, A_KVW, A_KVW, B_Q, B_KVW, B_KVW, B_KVW, B_KVW, B_KVW, B_KVW, B_HEADS * 3, C_W, C_W, C_W)
IN_WIDTH = sum(SPLIT_SIZES)
NEG_INF = -1e30
FORCE = 1e4
EPS = 1e-6

kernel_name = "hymba_style_swa_nsa_stickbreaking_hybrid"


def rmsnorm(x, g):
    xf = x.astype(jnp.float32)
    y = xf * lax.rsqrt(jnp.mean(xf * xf, axis=-1, keepdims=True) + EPS)
    return (y * g.astype(jnp.float32)).astype(x.dtype)


def alibi_slopes():
    n = A_HEADS + B_HEADS
    sl = 2.0 ** (-8.0 * np.arange(1, n + 1) / n)
    return jnp.asarray(sl[:A_HEADS], jnp.float32), jnp.asarray(sl[A_HEADS:], jnp.float32)


def unblock(o):
    o = jnp.moveaxis(o, 0, 1)
    return o.reshape((o.shape[0], o.shape[1] * o.shape[2]) + o.shape[3:])


def banded_attention(q, k, v, window, slopes, sinks=None):
    b, s, kv, g, hd = q.shape
    nblk = s // Q_BLOCK
    span = Q_BLOCK + window
    kp = jnp.pad(k, ((0, 0), (window, 0), (0, 0), (0, 0)))
    vp = jnp.pad(v, ((0, 0), (window, 0), (0, 0), (0, 0)))
    scale = HEAD_DIM ** -0.5

    def block(i):
        start = i * Q_BLOCK
        qb = lax.dynamic_slice_in_dim(q, start, Q_BLOCK, axis=1)
        kb = lax.dynamic_slice_in_dim(kp, start, span, axis=1)
        vb = lax.dynamic_slice_in_dim(vp, start, span, axis=1)
        t = start + jnp.arange(Q_BLOCK)
        sp = start - window + jnp.arange(span)
        dist = t[:, None] - sp[None, :]
        valid = (dist >= 0) & (dist < window) & (sp[None, :] >= 0)
        sc = jnp.einsum('bqkgd,bskd->bkgqs', qb, kb, preferred_element_type=jnp.float32) * scale
        sc = sc - slopes[:, :, None, None] * dist.astype(jnp.float32)
        sc = jnp.where(valid, sc, NEG_INF)
        if sinks is not None:
            sink_col = jnp.broadcast_to(sinks.astype(jnp.float32)[:, :, None, None], sc.shape[:-1] + (1,))
            p = jax.nn.softmax(jnp.concatenate([sc, sink_col], axis=-1), axis=-1)[..., :-1]
        else:
            p = jax.nn.softmax(sc, axis=-1)
        return jnp.einsum('bkgqs,bskd->bqkgd', p.astype(v.dtype), vb)

    return unblock(lax.map(block, jnp.arange(nblk)))


def compress(x_raw, pos_emb, w1, w2):
    b, s, kv, hd = x_raw.shape
    chunks = x_raw.reshape(b, s // CMP_STRIDE, CMP_STRIDE, kv, hd)
    blocks = jnp.concatenate([chunks[:, :-1], chunks[:, 1:]], axis=2)
    blocks = blocks + pos_emb[None, None, :, None, :]
    h = jax.nn.gelu(jnp.einsum('bnlkd,ldf->bnkf', blocks, w1))
    return jnp.einsum('bnkf,fd->bnkd', h, w2)


def overlap_matrix(n_cmp, n_slc):
    cs = np.arange(n_cmp)[:, None] * CMP_STRIDE
    ss = np.arange(n_slc)[None, :] * SLC_LEN
    ov = np.maximum(0, np.minimum(cs + CMP_LEN, ss + SLC_LEN) - np.maximum(cs, ss))
    return jnp.asarray(ov / CMP_STRIDE, jnp.float32)


def nsa_compressed_selected(q, k_cmp, v_cmp, k_slc, v_slc, slopes):
    b, s, kv, g, hd = q.shape
    nblk = s // Q_BLOCK
    n_cmp = k_cmp.shape[1]
    n_slc = s // SLC_LEN
    top = min(SLC_TOP, n_slc)
    scale = HEAD_DIM ** -0.5
    m_map = overlap_matrix(n_cmp, n_slc)
    cmp_end = jnp.arange(n_cmp) * CMP_STRIDE + CMP_LEN - 1
    kb_all = k_slc.reshape(b, n_slc, SLC_LEN, kv, hd).transpose(0, 3, 1, 2, 4)
    vb_all = v_slc.reshape(b, n_slc, SLC_LEN, kv, hd).transpose(0, 3, 1, 2, 4)
    bi = jnp.arange(b)[:, None, None, None]
    ki = jnp.arange(kv)[None, None, :, None]
    blk = jnp.arange(n_slc)
    sl = slopes[None, None, :, :, None]

    def block(i):
        start = i * Q_BLOCK
        qb = lax.dynamic_slice_in_dim(q, start, Q_BLOCK, axis=1)
        t = start + jnp.arange(Q_BLOCK)
        dist_c = t[:, None] - cmp_end[None, :]
        vis_c = dist_c >= 0
        any_vis = jnp.any(vis_c, axis=-1).astype(jnp.float32)
        sc = jnp.einsum('bqkgd,bnkd->bqkgn', qb, k_cmp, preferred_element_type=jnp.float32) * scale
        sc = sc - sl * dist_c.astype(jnp.float32)[None, :, None, None, :]
        sc = jnp.where(vis_c[None, :, None, None, :], sc, NEG_INF)
        p_c = jax.nn.softmax(sc, axis=-1) * any_vis[None, :, None, None, None]
        o_c = jnp.einsum('bqkgn,bnkd->bqkgd', p_c.astype(v_cmp.dtype), v_cmp)
        imp = jnp.einsum('bqkgn,nj->bqkj', p_c, m_map)
        cur = t // SLC_LEN
        valid_b = blk[None, :] * SLC_LEN <= t[:, None]
        forced = (blk[None, :] == 0) | (blk[None, :] == cur[:, None]) | (blk[None, :] == cur[:, None] - 1)
        imp = jnp.where(forced[None, :, None, :], FORCE, jnp.where(valid_b[None, :, None, :], imp, -FORCE))
        _, idx = lax.top_k(imp, top)
        ks = kb_all[bi, ki, idx].reshape(b, Q_BLOCK, kv, top * SLC_LEN, hd)
        vs = vb_all[bi, ki, idx].reshape(b, Q_BLOCK, kv, top * SLC_LEN, hd)
        pos = (idx[..., None] * SLC_LEN + jnp.arange(SLC_LEN)).reshape(b, Q_BLOCK, kv, top * SLC_LEN)
        dist_s = t[None, :, None, None] - pos
        sc_s = jnp.einsum('bqkgd,bqksd->bqkgs', qb, ks, preferred_element_type=jnp.float32) * scale
        sc_s = sc_s - sl * dist_s.astype(jnp.float32)[:, :, :, None, :]
        sc_s = jnp.where((dist_s >= 0)[:, :, :, None, :], sc_s, NEG_INF)
        p_s = jax.nn.softmax(sc_s, axis=-1)
        o_s = jnp.einsum('bqkgs,bqksd->bqkgd', p_s.astype(vs.dtype), vs)
        return o_c, o_s

    o_c, o_s = lax.map(block, jnp.arange(nblk))
    return unblock(o_c), unblock(o_s)


def stick_breaking_attention(q, k, v):
    b, s, h, hd = q.shape
    nblk = s // Q_BLOCK
    scale = HEAD_DIM ** -0.5
    kpos = jnp.arange(s)

    def block(i):
        start = i * Q_BLOCK
        qb = lax.dynamic_slice_in_dim(q, start, Q_BLOCK, axis=1)
        t = start + jnp.arange(Q_BLOCK)
        z = jnp.einsum('bqhd,bshd->bhqs', qb, k, preferred_element_type=jnp.float32) * scale
        past = kpos[None, :] < t[:, None]
        log_keep = jnp.where(past, jax.nn.log_sigmoid(-z), 0.0)
        shifted = jnp.concatenate([log_keep[..., 1:], jnp.zeros_like(log_keep[..., :1])], axis=-1)
        tail = lax.cumsum(shifted, axis=3, reverse=True)
        w = jnp.where(past, jnp.exp(jax.nn.log_sigmoid(z) + tail), 0.0)
        return jnp.einsum('bhqs,bshd->bqhd', w.astype(v.dtype), v)

    return unblock(lax.map(block, jnp.arange(nblk)))


def hybrid_layer(x, w_in, w_out, g_attn, g_ffn, g_out_a, g_out_b, g_out_c, sinks,
                 cmp_pos, cmp_w1, cmp_w2, w_gate, w_up, w_down):
    b, s, _ = x.shape
    hd = HEAD_DIM
    slopes_a, slopes_b = alibi_slopes()
    h = rmsnorm(x, g_attn)
    proj = h @ w_in
    splits = np.cumsum(SPLIT_SIZES)[:-1].tolist()
    (qa, ka, va, qb, kcb, vcb, ksb, vsb, kwb, vwb, gb, qc, kc, vc) = jnp.split(proj, splits, axis=-1)

    ga = A_HEADS // A_KV
    oa = banded_attention(qa.reshape(b, s, A_KV, ga, hd), ka.reshape(b, s, A_KV, hd), va.reshape(b, s, A_KV, hd),
                          A_WINDOW, slopes_a.reshape(A_KV, ga), sinks.reshape(A_KV, ga)).reshape(b, s, A_Q)

    gB = B_HEADS // B_KV
    qb4 = qb.reshape(b, s, B_KV, gB, hd)
    sl_b = slopes_b.reshape(B_KV, gB)
    k_cmp = compress(kcb.reshape(b, s, B_KV, hd), cmp_pos[0], cmp_w1[0], cmp_w2[0])
    v_cmp = compress(vcb.reshape(b, s, B_KV, hd), cmp_pos[1], cmp_w1[1], cmp_w2[1])
    o_cmp, o_slc = nsa_compressed_selected(qb4, k_cmp, v_cmp, ksb.reshape(b, s, B_KV, hd),
                                           vsb.reshape(b, s, B_KV, hd), sl_b)
    o_win = banded_attention(qb4, kwb.reshape(b, s, B_KV, hd), vwb.reshape(b, s, B_KV, hd), B_WINDOW, sl_b)
    gates = jax.nn.sigmoid(gb.reshape(b, s, B_KV, gB, 3))[..., None]
    ob = (gates[..., 0, :] * o_cmp + gates[..., 1, :] * o_slc + gates[..., 2, :] * o_win).reshape(b, s, B_Q)

    oc = stick_breaking_attention(qc.reshape(b, s, C_HEADS, hd), kc.reshape(b, s, C_HEADS, hd),
                                  vc.reshape(b, s, C_HEADS, hd)).reshape(b, s, C_W)

    mix = jnp.concatenate([rmsnorm(oa, g_out_a), rmsnorm(ob, g_out_b), rmsnorm(oc, g_out_c)], axis=-1)
    x = x + mix @ w_out
    h = rmsnorm(x, g_ffn)
    x = x + (jax.nn.silu(h @ w_gate) * (h @ w_up)) @ w_down
    return x


def setup_inputs(seed: int = 0) -> dict:
    key = jax.random.key(seed)
    ks = jax.random.split(key, 17)
    f32 = jnp.float32
    nrm = lambda k, shape, scale: jax.random.normal(k, shape, f32) * scale
    return {
        "x": jax.random.normal(ks[0], (BATCH, SEQ, D_MODEL), f32),
        "w_in": nrm(ks[1], (DEPTH, D_MODEL, IN_WIDTH), D_MODEL ** -0.5),
        "w_out": nrm(ks[2], (DEPTH, MIX_WIDTH, D_MODEL), MIX_WIDTH ** -0.5),
        "g_attn": 1.0 + nrm(ks[3], (DEPTH, D_MODEL), 0.02),
        "g_ffn": 1.0 + nrm(ks[4], (DEPTH, D_MODEL), 0.02),
        "g_out_a": 1.0 + nrm(ks[5], (DEPTH, A_Q), 0.02),
        "g_out_b": 1.0 + nrm(ks[6], (DEPTH, B_Q), 0.02),
        "g_out_c": 1.0 + nrm(ks[7], (DEPTH, C_W), 0.02),
        "sinks": nrm(ks[8], (DEPTH, A_HEADS), 0.5),
        "cmp_pos": nrm(ks[9], (DEPTH, 2, CMP_LEN, HEAD_DIM), 0.1),
        "cmp_w1": nrm(ks[10], (DEPTH, 2, CMP_LEN, HEAD_DIM, CMP_HIDDEN), (CMP_LEN * HEAD_DIM) ** -0.5),
        "cmp_w2": nrm(ks[11], (DEPTH, 2, CMP_HIDDEN, HEAD_DIM), CMP_HIDDEN ** -0.5),
        "w_gate": nrm(ks[12], (DEPTH, D_MODEL, D_FF), D_MODEL ** -0.5),
        "w_up": nrm(ks[13], (DEPTH, D_MODEL, D_FF), D_MODEL ** -0.5),
        "w_down": nrm(ks[14], (DEPTH, D_FF, D_MODEL), D_FF ** -0.5),
        "g_final": 1.0 + nrm(ks[15], (D_MODEL,), 0.02),
    }


def reference(x, w_in, w_out, g_attn, g_ffn, g_out_a, g_out_b, g_out_c, sinks,
              cmp_pos, cmp_w1, cmp_w2, w_gate, w_up, w_down, g_final):
    for l in range(DEPTH):
        x = hybrid_layer(x, w_in[l], w_out[l], g_attn[l], g_ffn[l], g_out_a[l], g_out_b[l], g_out_c[l],
                         sinks[l], cmp_pos[l], cmp_w1[l], cmp_w2[l], w_gate[l], w_up[l], w_down[l])
    return rmsnorm(x, g_final)
```

```python
import functools

import jax
import jax.numpy as jnp
import numpy as np
from jax import lax
from jax.experimental import pallas as pl
from jax.experimental.pallas import tpu as pltpu

F32 = jnp.float32
BF16 = jnp.bfloat16

D_MODEL = 1024
HEAD_DIM = 64
LANES = 128
Q_BLOCK = 128
A_HEADS = 8
A_WINDOW = 128
B_HEADS = 4
CMP_LEN = 32
CMP_STRIDE = 16
CMP_HIDDEN = 128
SLC_LEN = 64
SLC_TOP = 8
B_WINDOW = 512
C_HEADS = 4
A_Q = A_HEADS * HEAD_DIM
B_Q = B_HEADS * HEAD_DIM
C_W = C_HEADS * HEAD_DIM
MIX_WIDTH = A_Q + B_Q + C_W
D_FF = 2816
N_GATES = B_HEADS * 3
NEG_INF = -1e30
FORCE = 1e4
EPS = 1e-6
SCALE = HEAD_DIM ** -0.5

KEY_CHUNK = 256
FF_CHUNK = 256
TOKEN_TILE = 512
VMEM_LIMIT = 56 * 1024 * 1024

_G_QA, _G_KVA, _G_QB, _G_KVS, _G_KVW, _G_KVC, _G_GATE, _G_QC, _G_KC, _G_VC, IN_PAD = (
    0, 512, 640, 896, 1024, 1152, 1280, 1408, 1664, 1920, 2176)


def _alibi_slopes():
    n = A_HEADS + B_HEADS
    sl = 2.0 ** (-8.0 * np.arange(1, n + 1) / n)
    return [float(v) for v in sl[:A_HEADS]], [float(v) for v in sl[A_HEADS:]]


def _dot(a, b):
    return jnp.dot(a, b, preferred_element_type=F32)


def _dot_nt(a, b):
    return lax.dot_general(a, b, (((1,), (1,)), ((), ())), preferred_element_type=F32)


def _split_bf16(x):
    hi = x.astype(BF16)
    lo = (x - hi.astype(F32)).astype(BF16)
    return hi, lo


def _rms(x, g):
    ms = jnp.mean(x * x, axis=-1, keepdims=True)
    return x * lax.rsqrt(ms + EPS) * g


def _swap_halves(x):
    return jnp.concatenate([x[:, HEAD_DIM:], x[:, :HEAD_DIM]], axis=1)


def _head_masks(q_pair):
    lane = lax.broadcasted_iota(jnp.int32, q_pair.shape, 1)
    lo = lane < HEAD_DIM
    zero = jnp.zeros_like(q_pair)
    return jnp.where(lo, q_pair, zero), jnp.where(lo, zero, q_pair)


def _inproj_kernel(x_ref, g_ref, w_ref, qa_ref, kva_ref, qb_ref, kvs_ref, kvw_ref, kcr_ref, vcr_ref,
                   gate_ref, qc_ref, kc_ref, vc_ref):
    h = _rms(x_ref[...], g_ref[...]).astype(BF16)

    def proj(lo, hi):
        return _dot(h, w_ref[:, lo:hi])

    qa_ref[...] = proj(_G_QA, _G_KVA).astype(BF16)
    kva_ref[...] = proj(_G_KVA, _G_QB).astype(BF16)
    qb_ref[...] = proj(_G_QB, _G_KVS).astype(BF16)
    kvs_ref[...] = proj(_G_KVS, _G_KVW).astype(BF16)
    kvw_ref[...] = proj(_G_KVW, _G_KVC).astype(BF16)
    kvc = proj(_G_KVC, _G_GATE)
    kcr_ref[...] = kvc[:, :HEAD_DIM]
    vcr_ref[...] = kvc[:, HEAD_DIM:]
    gate_ref[...] = proj(_G_GATE, _G_QC)
    qc_ref[...] = proj(_G_QC, _G_KC).astype(BF16)
    kc_ref[...] = proj(_G_KC, _G_VC).astype(BF16)
    vc_ref[...] = proj(_G_VC, IN_PAD).astype(BF16)


def _inproj(x2, g, w):
    t = x2.shape[0]
    tm = TOKEN_TILE
    widths = [(A_Q, BF16), (LANES, BF16), (B_Q, BF16), (LANES, BF16), (LANES, BF16), (HEAD_DIM, F32),
              (HEAD_DIM, F32), (LANES, F32), (C_W, BF16), (C_W, BF16), (C_W, BF16)]
    return pl.pallas_call(
        _inproj_kernel,
        out_shape=[jax.ShapeDtypeStruct((t, wd), dt) for wd, dt in widths],
        grid=(t // tm,),
        in_specs=[pl.BlockSpec((tm, D_MODEL), lambda i: (i, 0)),
                  pl.BlockSpec((1, D_MODEL), lambda i: (0, 0)),
                  pl.BlockSpec((D_MODEL, IN_PAD), lambda i: (0, 0))],
        out_specs=[pl.BlockSpec((tm, wd), lambda i: (i, 0)) for wd, _ in widths],
        compiler_params=pltpu.CompilerParams(dimension_semantics=("arbitrary",), vmem_limit_bytes=VMEM_LIMIT),
        name="inproj",
    )(x2, g, w)


def _compress_kernel(xk_ref, xv_ref, pos_ref, w1_ref, w2_ref, o_ref):
    n = xk_ref.shape[1]
    out = jnp.zeros((n, LANES), F32)
    for t, x_ref in enumerate((xk_ref, xv_ref)):
        x = x_ref[0]
        h_lo = _dot((x + pos_ref[t, 0]).astype(BF16), w1_ref[t, 0])
        h_hi = _dot((x + pos_ref[t, 1]).astype(BF16), w1_ref[t, 1])
        h = h_lo + pltpu.roll(h_hi, n - 1, 0)
        out = out + _dot(jax.nn.gelu(h).astype(BF16), w2_ref[t])
    o_ref[0] = out.astype(BF16)


def _compress(xk, xv, pos, w1, w2):
    b, n, wdt = xk.shape
    return pl.pallas_call(
        _compress_kernel,
        out_shape=jax.ShapeDtypeStruct((b, n, LANES), BF16),
        grid=(b,),
        in_specs=[pl.BlockSpec((1, n, wdt), lambda i: (i, 0, 0)),
                  pl.BlockSpec((1, n, wdt), lambda i: (i, 0, 0)),
                  pl.BlockSpec((2, 2, 1, wdt), lambda i: (0, 0, 0, 0)),
                  pl.BlockSpec((2, 2, wdt, CMP_HIDDEN), lambda i: (0, 0, 0, 0)),
                  pl.BlockSpec((2, CMP_HIDDEN, LANES), lambda i: (0, 0, 0))],
        out_specs=pl.BlockSpec((1, n, LANES), lambda i: (i, 0, 0)),
        compiler_params=pltpu.CompilerParams(dimension_semantics=("arbitrary",), vmem_limit_bytes=VMEM_LIMIT),
        name="compress",
    )(xk, xv, pos, w1, w2)


def _mixa_kernel(sink_ref, q_ref, kvp_ref, kvc_ref, g_ref, o_ref, *, slopes):
    i = pl.program_id(1)
    q = q_ref[0]
    kv = jnp.concatenate([kvp_ref[0], kvc_ref[0]], axis=0)
    kv_sw = _swap_halves(kv)
    span = 2 * Q_BLOCK
    row = lax.broadcasted_iota(jnp.int32, (Q_BLOCK, span), 0)
    col = lax.broadcasted_iota(jnp.int32, (Q_BLOCK, span), 1)
    dist = row + Q_BLOCK - col
    valid = (dist >= 0) & (dist < A_WINDOW) & ((col >= Q_BLOCK) | (i > 0))
    distf = dist.astype(F32)
    lane = lax.broadcasted_iota(jnp.int32, (Q_BLOCK, LANES), 1)
    lo = lane < HEAD_DIM

    outs = []
    for p in range(A_HEADS // 2):
        qe, qo = _head_masks(q[:, p * LANES:(p + 1) * LANES])
        halves = []
        for par, (qh, k_side, v_side) in enumerate(((qe, kv, kv_sw), (qo, kv_sw, kv))):
            hd = 2 * p + par
            sink = sink_ref[hd]
            s = _dot_nt(qh, k_side)
            s = jnp.where(valid, s - slopes[hd] * distf, NEG_INF)
            m = jnp.maximum(jnp.max(s, axis=-1, keepdims=True), sink)
            e = jnp.exp(s - m)
            l = jnp.sum(e, axis=-1, keepdims=True) + jnp.exp(sink - m)
            halves.append(_dot(e.astype(BF16), v_side) * (1.0 / l))
        outs.append(jnp.where(lo, halves[0], halves[1]))
    oa = jnp.concatenate(outs, axis=1)
    o_ref[0] = _rms(oa, g_ref[...]).astype(BF16)


def _mixer_a(sinks, qa, kva, g):
    b, s, _ = qa.shape
    nblk = s // Q_BLOCK
    slopes, _ = _alibi_slopes()
    return pl.pallas_call(
        functools.partial(_mixa_kernel, slopes=slopes),
        out_shape=jax.ShapeDtypeStruct((b, s, A_Q), BF16),
        grid=(b, nblk),
        in_specs=[pl.BlockSpec(memory_space=pltpu.SMEM),
                  pl.BlockSpec((1, Q_BLOCK, A_Q), lambda bi, i: (bi, i, 0)),
                  pl.BlockSpec((1, Q_BLOCK, LANES), lambda bi, i: (bi, jnp.maximum(i - 1, 0), 0)),
                  pl.BlockSpec((1, Q_BLOCK, LANES), lambda bi, i: (bi, i, 0)),
                  pl.BlockSpec((1, A_Q), lambda bi, i: (0, 0))],
        out_specs=pl.BlockSpec((1, Q_BLOCK, A_Q), lambda bi, i: (bi, i, 0)),
        compiler_params=pltpu.CompilerParams(dimension_semantics=("arbitrary", "arbitrary"),
                                             vmem_limit_bytes=VMEM_LIMIT),
        name="mixer_a",
    )(sinks, qa, kva, kva, g)


def _mixb_kernel(q_ref, kvs_ref, kvw_ref, kvc_ref, gate_ref, mmap_ref, exp_ref, rep_ref, g_ref, o_ref,
                 acc_ref, m_ref, l_ref, *, slopes, n_cmp, n_slc, nblk):
    i = pl.program_id(1)
    t0 = i * Q_BLOCK
    q = q_ref[0]
    q_heads = []
    for p in range(B_HEADS // 2):
        q_heads.extend(_head_masks(q[:, p * LANES:(p + 1) * LANES]))
    lane = lax.broadcasted_iota(jnp.int32, (Q_BLOCK, LANES), 1)
    lo = lane < HEAD_DIM

    def pair_up(per_head):
        return jnp.concatenate([jnp.where(lo, per_head[2 * p], per_head[2 * p + 1])
                                for p in range(B_HEADS // 2)], axis=1)

    kvc = kvc_ref[0]
    kvc_sw = _swap_halves(kvc)
    n_pad = kvc.shape[0]
    row_c = lax.broadcasted_iota(jnp.int32, (Q_BLOCK, n_pad), 0)
    col_c = lax.broadcasted_iota(jnp.int32, (Q_BLOCK, n_pad), 1)
    dist_c = (t0 + row_c) - (col_c * CMP_STRIDE + (CMP_LEN - 1))
    vis_c = (dist_c >= 0) & (col_c < n_cmp)
    dist_cf = dist_c.astype(F32)
    row1 = lax.broadcasted_iota(jnp.int32, (Q_BLOCK, 1), 0)
    any_vis = ((t0 + row1) >= (CMP_LEN - 1)).astype(F32)
    o_cmp = []
    p_sum = jnp.zeros((Q_BLOCK, n_pad), F32)
    for hd in range(B_HEADS):
        k_side, v_side = (kvc, kvc_sw) if hd % 2 == 0 else (kvc_sw, kvc)
        s = _dot_nt(q_heads[hd], k_side)
        s = jnp.where(vis_c, s - slopes[hd] * dist_cf, NEG_INF)
        m = jnp.max(s, axis=-1, keepdims=True)
        e = jnp.exp(s - m)
        pr = e * (any_vis / jnp.sum(e, axis=-1, keepdims=True))
        p_sum = p_sum + pr
        o_cmp.append(_dot(pr.astype(BF16), v_side))
    o_cmp = pair_up(o_cmp)

    p_hi, p_lo = _split_bf16(p_sum)
    imp = (_dot_nt(mmap_ref[...], p_hi) + _dot_nt(mmap_ref[...], p_lo))[:n_slc]
    blk = lax.broadcasted_iota(jnp.int32, (n_slc, Q_BLOCK), 0)
    tq = t0 + lax.broadcasted_iota(jnp.int32, (n_slc, Q_BLOCK), 1)
    cur = tq // SLC_LEN
    valid_b = blk <= cur
    forced = (blk == 0) | (blk == cur) | (blk == cur - 1)
    val = jnp.where(forced, FORCE, jnp.where(valid_b, imp, -FORCE))
    rank = jnp.zeros((n_slc, Q_BLOCK), F32)
    for j in range(n_slc):
        vj = val[j:j + 1, :]
        ahead = (vj > val) | ((vj == val) & (blk > j))
        rank = rank + ahead.astype(F32)
    sel = ((rank < SLC_TOP) & valid_b).astype(F32)
    sel = jnp.concatenate([sel, jnp.zeros((LANES - n_slc, Q_BLOCK), F32)], axis=0)
    sel_q = sel.T.astype(BF16)

    acc_ref[...] = jnp.zeros(acc_ref.shape, F32)
    m_ref[...] = jnp.full(m_ref.shape, NEG_INF, F32)
    l_ref[...] = jnp.zeros(l_ref.shape, F32)
    row_k = lax.broadcasted_iota(jnp.int32, (Q_BLOCK, KEY_CHUNK), 0)
    col_k = lax.broadcasted_iota(jnp.int32, (Q_BLOCK, KEY_CHUNK), 1)
    rc = row_k - col_k

    def slc_chunk(c, carry):
        k0 = pl.multiple_of(c * KEY_CHUNK, KEY_CHUNK)
        kv = kvs_ref[0, pl.ds(k0, KEY_CHUNK), :]
        kv_sw = _swap_halves(kv)
        dist = rc + (t0 - k0)
        allowed = (_dot(sel_q, exp_ref[c]) > 0.5) & (dist >= 0)
        distf = dist.astype(F32)
        for hd in range(B_HEADS):
            k_side, v_side = (kv, kv_sw) if hd % 2 == 0 else (kv_sw, kv)
            s = _dot_nt(q_heads[hd], k_side)
            s = jnp.where(allowed, s - slopes[hd] * distf, NEG_INF)
            m_old = m_ref[hd]
            m_new = jnp.maximum(m_old, jnp.max(s, axis=-1, keepdims=True))
            alpha = jnp.exp(m_old - m_new)
            e = jnp.exp(s - m_new)
            l_ref[hd] = alpha * l_ref[hd] + jnp.sum(e, axis=-1, keepdims=True)
            acc_ref[hd] = alpha * acc_ref[hd] + _dot(e.astype(BF16), v_side)
            m_ref[hd] = m_new
        return carry

    lax.fori_loop(0, (t0 + Q_BLOCK + KEY_CHUNK - 1) // KEY_CHUNK, slc_chunk, 0)
    o_slc = pair_up([acc_ref[hd] * (1.0 / l_ref[hd]) for hd in range(B_HEADS)])

    wspan = B_WINDOW + Q_BLOCK
    w0 = pl.multiple_of(jnp.clip(i - B_WINDOW // Q_BLOCK, 0, nblk - wspan // Q_BLOCK) * Q_BLOCK, Q_BLOCK)
    kvw = kvw_ref[0, pl.ds(w0, wspan), :]
    kvw_sw = _swap_halves(kvw)
    row_w = lax.broadcasted_iota(jnp.int32, (Q_BLOCK, wspan), 0)
    col_w = lax.broadcasted_iota(jnp.int32, (Q_BLOCK, wspan), 1)
    dist_w = (row_w - col_w) + (t0 - w0)
    valid_w = (dist_w >= 0) & (dist_w < B_WINDOW)
    dist_wf = dist_w.astype(F32)
    o_win = []
    for hd in range(B_HEADS):
        k_side, v_side = (kvw, kvw_sw) if hd % 2 == 0 else (kvw_sw, kvw)
        s = _dot_nt(q_heads[hd], k_side)
        s = jnp.where(valid_w, s - slopes[hd] * dist_wf, NEG_INF)
        m = jnp.max(s, axis=-1, keepdims=True)
        e = jnp.exp(s - m)
        l = jnp.sum(e, axis=-1, keepdims=True)
        o_win.append(_dot(e.astype(BF16), v_side) * (1.0 / l))
    o_win = pair_up(o_win)

    g_hi, g_lo = _split_bf16(jax.nn.sigmoid(gate_ref[0]))
    gates = [_dot(g_hi, rep_ref[r]) + _dot(g_lo, rep_ref[r]) for r in range(3)]
    ob = gates[0] * o_cmp + gates[1] * o_slc + gates[2] * o_win
    o_ref[0] = _rms(ob, g_ref[...]).astype(BF16)


def _mixer_b(qb, kvs, kvw, kvc, gate, g):
    b, s, _ = qb.shape
    nblk = s // Q_BLOCK
    n_pad = kvc.shape[1]
    n_cmp = s // CMP_STRIDE - 1
    n_slc = s // SLC_LEN
    n_chunks = s // KEY_CHUNK
    _, slopes = _alibi_slopes()
    cs = np.arange(n_pad)[None, :] * CMP_STRIDE
    ss = np.arange(LANES)[:, None] * SLC_LEN
    ov = np.maximum(0, np.minimum(cs + CMP_LEN, ss + SLC_LEN) - np.maximum(cs, ss)) / CMP_STRIDE
    ov = ov * (np.arange(n_pad)[None, :] < n_cmp) * (np.arange(LANES)[:, None] < n_slc)
    mmap_t = jnp.asarray(ov, BF16)
    key_blk = (np.arange(n_chunks)[:, None, None] * KEY_CHUNK + np.arange(KEY_CHUNK)[None, None, :]) // SLC_LEN
    expand = jnp.asarray(key_blk == np.arange(LANES)[None, :, None], BF16)
    rep = np.zeros((3, LANES, B_Q), np.float32)
    for r in range(3):
        for h in range(B_HEADS):
            rep[r, 3 * h + r, h * HEAD_DIM:(h + 1) * HEAD_DIM] = 1.0
    rep = jnp.asarray(rep, BF16)
    return pl.pallas_call(
        functools.partial(_mixb_kernel, slopes=slopes, n_cmp=n_cmp, n_slc=n_slc, nblk=nblk),
        out_shape=jax.ShapeDtypeStruct((b, s, B_Q), BF16),
        grid=(b, nblk),
        in_specs=[pl.BlockSpec((1, Q_BLOCK, B_Q), lambda bi, i: (bi, i, 0)),
                  pl.BlockSpec((1, s, LANES), lambda bi, i: (bi, 0, 0)),
                  pl.BlockSpec((1, s, LANES), lambda bi, i: (bi, 0, 0)),
                  pl.BlockSpec((1, n_pad, LANES), lambda bi, i: (bi, 0, 0)),
                  pl.BlockSpec((1, Q_BLOCK, LANES), lambda bi, i: (bi, i, 0)),
                  pl.BlockSpec((LANES, n_pad), lambda bi, i: (0, 0)),
                  pl.BlockSpec((n_chunks, LANES, KEY_CHUNK), lambda bi, i: (0, 0, 0)),
                  pl.BlockSpec((3, LANES, B_Q), lambda bi, i: (0, 0, 0)),
                  pl.BlockSpec((1, B_Q), lambda bi, i: (0, 0))],
        out_specs=pl.BlockSpec((1, Q_BLOCK, B_Q), lambda bi, i: (bi, i, 0)),
        scratch_shapes=[pltpu.VMEM((B_HEADS, Q_BLOCK, LANES), F32),
                        pltpu.VMEM((B_HEADS, Q_BLOCK, 1), F32),
                        pltpu.VMEM((B_HEADS, Q_BLOCK, 1), F32)],
        compiler_params=pltpu.CompilerParams(dimension_semantics=("arbitrary", "arbitrary"),
                                             vmem_limit_bytes=VMEM_LIMIT),
        name="mixer_b",
    )(qb, kvs, kvw, kvc, gate, mmap_t, expand, rep, g)


def _mixc_kernel(q_ref, k_ref, v_ref, tri_ref, g_ref, o_ref, acc_ref, carry_ref):
    i = pl.program_id(1)
    t0 = i * Q_BLOCK
    q = q_ref[0]
    q_heads = []
    for p in range(C_HEADS // 2):
        q_heads.extend(_head_masks(q[:, p * LANES:(p + 1) * LANES]))
    acc_ref[...] = jnp.zeros(acc_ref.shape, F32)
    carry_ref[...] = jnp.zeros(carry_ref.shape, F32)
    row = lax.broadcasted_iota(jnp.int32, (Q_BLOCK, KEY_CHUNK), 0)
    col = lax.broadcasted_iota(jnp.int32, (Q_BLOCK, KEY_CHUNK), 1)
    cr = col - row
    n_chunks = (t0 + Q_BLOCK + KEY_CHUNK - 1) // KEY_CHUNK

    def chunk(j, carry):
        c = n_chunks - 1 - j
        k0 = pl.multiple_of(c * KEY_CHUNK, KEY_CHUNK)
        k = k_ref[0, pl.ds(k0, KEY_CHUNK), :]
        v = v_ref[0, pl.ds(k0, KEY_CHUNK), :]
        past = cr < (t0 - k0)
        tri = tri_ref[...]
        for hd in range(C_HEADS):
            p = hd // 2
            z = _dot_nt(q_heads[hd], k[:, p * LANES:(p + 1) * LANES])
            log_keep = -(jnp.maximum(z, 0.0) + jnp.log1p(jnp.exp(-jnp.abs(z))))
            lk = jnp.where(past, log_keep, 0.0)
            lk_hi, lk_lo = _split_bf16(lk)
            tail = _dot(lk_hi, tri) + _dot(lk_lo, tri) + carry_ref[hd]
            w = jnp.where(past, jnp.exp(log_keep + z + tail), 0.0)
            acc_ref[hd] = acc_ref[hd] + _dot(w.astype(BF16), v[:, p * LANES:(p + 1) * LANES])
            carry_ref[hd] = carry_ref[hd] + jnp.sum(lk, axis=-1, keepdims=True)
        return carry

    lax.fori_loop(0, n_chunks, chunk, 0)
    lane = lax.broadcasted_iota(jnp.int32, (Q_BLOCK, LANES), 1)
    lo = lane < HEAD_DIM
    oc = jnp.concatenate([jnp.where(lo, acc_ref[2 * p], acc_ref[2 * p + 1]) for p in range(C_HEADS // 2)], axis=1)
    o_ref[0] = _rms(oc, g_ref[...]).astype(BF16)


def _mixer_c(qc, kc, vc, g):
    b, s, _ = qc.shape
    nblk = s // Q_BLOCK
    tri = jnp.asarray(np.arange(KEY_CHUNK)[:, None] > np.arange(KEY_CHUNK)[None, :], BF16)
    return pl.pallas_call(
        _mixc_kernel,
        out_shape=jax.ShapeDtypeStruct((b, s, C_W), BF16),
        grid=(b, nblk),
        in_specs=[pl.BlockSpec((1, Q_BLOCK, C_W), lambda bi, i: (bi, i, 0)),
                  pl.BlockSpec((1, s, C_W), lambda bi, i: (bi, 0, 0)),
                  pl.BlockSpec((1, s, C_W), lambda bi, i: (bi, 0, 0)),
                  pl.BlockSpec((KEY_CHUNK, KEY_CHUNK), lambda bi, i: (0, 0)),
                  pl.BlockSpec((1, C_W), lambda bi, i: (0, 0))],
        out_specs=pl.BlockSpec((1, Q_BLOCK, C_W), lambda bi, i: (bi, i, 0)),
        scratch_shapes=[pltpu.VMEM((C_HEADS, Q_BLOCK, LANES), F32),
                        pltpu.VMEM((C_HEADS, Q_BLOCK, 1), F32)],
        compiler_params=pltpu.CompilerParams(dimension_semantics=("arbitrary", "arbitrary"),
                                             vmem_limit_bytes=VMEM_LIMIT),
        name="mixer_c",
    )(qc, kc, vc, tri, g)


def _outffn_kernel(x_ref, ma_ref, mb_ref, mc_ref, wo_ref, gf_ref, wg_ref, wu_ref, wd_ref, gl_ref, o_ref, *, final):
    x = (x_ref[...] + _dot(ma_ref[...], wo_ref[:A_Q, :]) + _dot(mb_ref[...], wo_ref[A_Q:A_Q + B_Q, :])
         + _dot(mc_ref[...], wo_ref[A_Q + B_Q:, :]))
    h = _rms(x, gf_ref[...]).astype(BF16)
    o_ref[...] = x
    for c in range(D_FF // FF_CHUNK):
        sl = slice(c * FF_CHUNK, (c + 1) * FF_CHUNK)
        gate = _dot(h, wg_ref[:, sl])
        up = _dot(h, wu_ref[:, sl])
        o_ref[...] += _dot((jax.nn.silu(gate) * up).astype(BF16), wd_ref[sl, :])
    if final:
        o_ref[...] = _rms(o_ref[...], gl_ref[...])


def _outffn(x2, ma, mb, mc, wo, gf, wg, wu, wd, gl, final):
    t = x2.shape[0]
    tm = TOKEN_TILE
    const = lambda i: (0, 0)
    tok = lambda i: (i, 0)
    return pl.pallas_call(
        functools.partial(_outffn_kernel, final=final),
        out_shape=jax.ShapeDtypeStruct((t, D_MODEL), F32),
        grid=(t // tm,),
        in_specs=[pl.BlockSpec((tm, D_MODEL), tok),
                  pl.BlockSpec((tm, A_Q), tok),
                  pl.BlockSpec((tm, B_Q), tok),
                  pl.BlockSpec((tm, C_W), tok),
                  pl.BlockSpec((MIX_WIDTH, D_MODEL), const),
                  pl.BlockSpec((1, D_MODEL), const),
                  pl.BlockSpec((D_MODEL, D_FF), const),
                  pl.BlockSpec((D_MODEL, D_FF), const),
                  pl.BlockSpec((D_FF, D_MODEL), const),
                  pl.BlockSpec((1, D_MODEL), const)],
        out_specs=pl.BlockSpec((tm, D_MODEL), tok),
        compiler_params=pltpu.CompilerParams(dimension_semantics=("arbitrary",), vmem_limit_bytes=VMEM_LIMIT),
        name="outproj_ffn",
    )(x2, ma, mb, mc, wo, gf, wg, wu, wd, gl)


def _regroup_w_in(w_in):
    sizes = (A_Q, HEAD_DIM, HEAD_DIM, B_Q, HEAD_DIM, HEAD_DIM, HEAD_DIM, HEAD_DIM, HEAD_DIM, HEAD_DIM,
             N_GATES, C_W, C_W, C_W)
    offs = np.concatenate([[0], np.cumsum(sizes)])
    qa, ka, va, qb, kcb, vcb, ksb, vsb, kwb, vwb, gb, qc, kc, vc = [w_in[:, offs[j]:offs[j + 1]]
                                                                     for j in range(len(sizes))]
    pad = jnp.zeros((w_in.shape[0], LANES - N_GATES), w_in.dtype)
    cols = [qa * SCALE, ka, va, qb * SCALE, ksb, vsb, kwb, vwb, kcb, vcb, gb, pad, qc * SCALE, kc, vc]
    return jnp.concatenate(cols, axis=1).astype(BF16)


def _layer(x2, b, s, w_in, w_out, g_attn, g_ffn, g_out_a, g_out_b, g_out_c, sinks, cmp_pos, cmp_w1, cmp_w2,
           w_gate, w_up, w_down, g_final, final):
    row = lambda v: v.reshape(1, -1)
    qa, kva, qb, kvs, kvw, kcr, vcr, gate, qc, kc, vc = _inproj(x2, row(g_attn), _regroup_w_in(w_in))
    r3 = lambda a: a.reshape(b, s, a.shape[-1])

    n_chunk = s // CMP_STRIDE
    half = CMP_LEN // 2
    flat = half * HEAD_DIM
    xk = kcr.reshape(b, n_chunk, flat)
    xv = vcr.reshape(b, n_chunk, flat)
    pos = cmp_pos.reshape(2, 2, 1, flat)
    w1 = cmp_w1.reshape(2, 2, flat, CMP_HIDDEN).astype(BF16)
    zeros = jnp.zeros((CMP_HIDDEN, HEAD_DIM), cmp_w2.dtype)
    w2 = jnp.stack([jnp.concatenate([cmp_w2[0], zeros], axis=1),
                    jnp.concatenate([zeros, cmp_w2[1]], axis=1)]).astype(BF16)
    kvc = _compress(xk, xv, pos, w1, w2)

    ma = _mixer_a(sinks, r3(qa), r3(kva), row(g_out_a))
    mb = _mixer_b(r3(qb), r3(kvs), r3(kvw), kvc, r3(gate), row(g_out_b))
    mc = _mixer_c(r3(qc), r3(kc), r3(vc), row(g_out_c))
    t = b * s
    return _outffn(x2, ma.reshape(t, A_Q), mb.reshape(t, B_Q), mc.reshape(t, C_W), w_out.astype(BF16),
                   row(g_ffn), w_gate.astype(BF16), w_up.astype(BF16), w_down.astype(BF16), row(g_final), final)


def kernel(x, w_in, w_out, g_attn, g_ffn, g_out_a, g_out_b, g_out_c, sinks, cmp_pos, cmp_w1, cmp_w2, w_gate, w_up,
           w_down, g_final):
    b, s, d = x.shape
    depth = w_in.shape[0]
    assert d == D_MODEL and s % KEY_CHUNK == 0 and (b * s) % TOKEN_TILE == 0
    assert s // Q_BLOCK >= (B_WINDOW + Q_BLOCK) // Q_BLOCK and (s // CMP_STRIDE) % LANES == 0
    x2 = x.reshape(b * s, d)
    for l in range(depth):
        x2 = _layer(x2, b, s, w_in[l], w_out[l], g_attn[l], g_ffn[l], g_out_a[l], g_out_b[l], g_out_c[l], sinks[l],
                    cmp_pos[l], cmp_w1[l], cmp_w2[l], w_gate[l], w_up[l], w_down[l], g_final, l == depth - 1)
    return x2.reshape(b, s, d)
```

```python
import functools

import jax
import jax.numpy as jnp
import numpy as np
from jax import lax
from jax.experimental import pallas as pl
from jax.experimental.pallas import tpu as pltpu

F32 = jnp.float32
BF16 = jnp.bfloat16

D_MODEL = 1024
HEAD_DIM = 64
LANES = 128
Q_BLOCK = 128
A_HEADS = 8
A_WINDOW = 128
B_HEADS = 4
CMP_LEN = 32
CMP_STRIDE = 16
CMP_HIDDEN = 128
SLC_LEN = 64
SLC_TOP = 8
B_WINDOW = 512
C_HEADS = 4
A_Q = A_HEADS * HEAD_DIM
B_Q = B_HEADS * HEAD_DIM
C_W = C_HEADS * HEAD_DIM
MIX_WIDTH = A_Q + B_Q + C_W
D_FF = 2816
N_GATES = B_HEADS * 3
NEG_INF = -1e30
FORCE = 1e4
EPS = 1e-6
SCALE = HEAD_DIM ** -0.5

KEY_CHUNK = 256
FF_CHUNK = 256
TOKEN_TILE = 512
VMEM_LIMIT = 56 * 1024 * 1024

_G_QA, _G_KVA, _G_QB, _G_KVS, _G_KVW, _G_KVC, _G_GATE, _G_QC, _G_KC, _G_VC, IN_PAD = (
    0, 512, 640, 896, 1024, 1152, 1280, 1408, 1664, 1920, 2176)


def _alibi_slopes():
    n = A_HEADS + B_HEADS
    sl = 2.0 ** (-8.0 * np.arange(1, n + 1) / n)
    return [float(v) for v in sl[:A_HEADS]], [float(v) for v in sl[A_HEADS:]]


def _dot(a, b):
    return jnp.dot(a, b, preferred_element_type=F32)


def _dot_nt(a, b):
    return lax.dot_general(a, b, (((1,), (1,)), ((), ())), preferred_element_type=F32)


def _split_bf16(x):
    hi = x.astype(BF16)
    lo = (x - hi.astype(F32)).astype(BF16)
    return hi, lo


def _rms(x, g):
    ms = jnp.mean(x * x, axis=-1, keepdims=True)
    return x * lax.rsqrt(ms + EPS) * g


def _lane_tile(x, width):
    return jnp.concatenate([x] * (width // LANES), axis=1)


def _swap_halves(x):
    return jnp.concatenate([x[:, HEAD_DIM:], x[:, :HEAD_DIM]], axis=1)


def _stack_heads(q, n_heads):
    rows = q.shape[0]
    lo = lax.broadcasted_iota(jnp.int32, (rows, LANES), 1) < HEAD_DIM
    tiles = []
    for p in range(n_heads // 2):
        pair = q[:, p * LANES:(p + 1) * LANES]
        zero = jnp.zeros_like(pair)
        tiles.append(jnp.where(lo, pair, zero))
        tiles.append(jnp.where(lo, _swap_halves(pair), zero))
    return jnp.concatenate(tiles, axis=0)


def _stack_pairs(q, n_heads):
    rows = q.shape[0]
    lo = lax.broadcasted_iota(jnp.int32, (rows, LANES), 1) < HEAD_DIM
    out = []
    for p in range(n_heads // 2):
        pair = q[:, p * LANES:(p + 1) * LANES]
        zero = jnp.zeros_like(pair)
        out.append(jnp.concatenate([jnp.where(lo, pair, zero), jnp.where(lo, zero, pair)], axis=0))
    return out


def _unstack_heads(o, n_heads):
    rows = o.shape[0] // n_heads
    lo = lax.broadcasted_iota(jnp.int32, (rows, LANES), 1) < HEAD_DIM
    pairs = []
    for p in range(n_heads // 2):
        even = o[(2 * p) * rows:(2 * p + 1) * rows]
        odd = o[(2 * p + 1) * rows:(2 * p + 2) * rows]
        pairs.append(jnp.where(lo, pltpu.roll(even, HEAD_DIM, 1), odd))
    return jnp.concatenate(pairs, axis=1)


def _inproj_kernel(x_ref, g_ref, w_ref, qa_ref, kva_ref, qb_ref, kvs_ref, kvw_ref, kcr_ref, vcr_ref,
                   gate_ref, qc_ref, kc_ref, vc_ref):
    h = _rms(x_ref[...], g_ref[...]).astype(BF16)

    def proj(lo, hi):
        return _dot(h, w_ref[:, lo:hi])

    qa_ref[...] = proj(_G_QA, _G_KVA).astype(BF16)
    kva_ref[...] = proj(_G_KVA, _G_QB).astype(BF16)
    qb_ref[...] = proj(_G_QB, _G_KVS).astype(BF16)
    kvs_ref[...] = proj(_G_KVS, _G_KVW).astype(BF16)
    kvw_ref[...] = proj(_G_KVW, _G_KVC).astype(BF16)
    kvc = proj(_G_KVC, _G_GATE)
    kcr_ref[...] = kvc[:, :HEAD_DIM]
    vcr_ref[...] = kvc[:, HEAD_DIM:]
    gate_ref[...] = proj(_G_GATE, _G_QC)
    qc_ref[...] = proj(_G_QC, _G_KC).astype(BF16)
    kc_ref[...] = proj(_G_KC, _G_VC).astype(BF16)
    vc_ref[...] = proj(_G_VC, IN_PAD).astype(BF16)


def _inproj(x2, g, w):
    t = x2.shape[0]
    tm = TOKEN_TILE
    widths = [(A_Q, BF16), (LANES, BF16), (B_Q, BF16), (LANES, BF16), (LANES, BF16), (HEAD_DIM, F32),
              (HEAD_DIM, F32), (LANES, F32), (C_W, BF16), (C_W, BF16), (C_W, BF16)]
    return pl.pallas_call(
        _inproj_kernel,
        out_shape=[jax.ShapeDtypeStruct((t, wd), dt) for wd, dt in widths],
        grid=(t // tm,),
        in_specs=[pl.BlockSpec((tm, D_MODEL), lambda i: (i, 0)),
                  pl.BlockSpec((1, D_MODEL), lambda i: (0, 0)),
                  pl.BlockSpec((D_MODEL, IN_PAD), lambda i: (0, 0))],
        out_specs=[pl.BlockSpec((tm, wd), lambda i: (i, 0)) for wd, _ in widths],
        compiler_params=pltpu.CompilerParams(dimension_semantics=("arbitrary",), vmem_limit_bytes=VMEM_LIMIT),
        name="inproj",
    )(x2, g, w)


def _compress_kernel(xk_ref, xv_ref, pos_ref, w1_ref, w2_ref, o_ref):
    n = xk_ref.shape[1]
    out = jnp.zeros((n, LANES), F32)
    for t, x_ref in enumerate((xk_ref, xv_ref)):
        x = x_ref[0]
        h_lo = _dot((x + pos_ref[t, 0]).astype(BF16), w1_ref[t, 0])
        h_hi = _dot((x + pos_ref[t, 1]).astype(BF16), w1_ref[t, 1])
        h = h_lo + pltpu.roll(h_hi, n - 1, 0)
        out = out + _dot(jax.nn.gelu(h).astype(BF16), w2_ref[t])
    o_ref[0] = out.astype(BF16)


def _compress(xk, xv, pos, w1, w2):
    b, n, wdt = xk.shape
    return pl.pallas_call(
        _compress_kernel,
        out_shape=jax.ShapeDtypeStruct((b, n, LANES), BF16),
        grid=(b,),
        in_specs=[pl.BlockSpec((1, n, wdt), lambda i: (i, 0, 0)),
                  pl.BlockSpec((1, n, wdt), lambda i: (i, 0, 0)),
                  pl.BlockSpec((2, 2, 1, wdt), lambda i: (0, 0, 0, 0)),
                  pl.BlockSpec((2, 2, wdt, CMP_HIDDEN), lambda i: (0, 0, 0, 0)),
                  pl.BlockSpec((2, CMP_HIDDEN, LANES), lambda i: (0, 0, 0))],
        out_specs=pl.BlockSpec((1, n, LANES), lambda i: (i, 0, 0)),
        compiler_params=pltpu.CompilerParams(dimension_semantics=("arbitrary",), vmem_limit_bytes=VMEM_LIMIT),
        name="compress",
    )(xk, xv, pos, w1, w2)


def _mixa_kernel(sink_ref, q_ref, kvp_ref, kvc_ref, g_ref, o_ref, *, slopes):
    i = pl.program_id(1)
    q = q_ref[0]
    kv = jnp.concatenate([kvp_ref[0], kvc_ref[0]], axis=0)
    span = 2 * Q_BLOCK
    row = lax.broadcasted_iota(jnp.int32, (Q_BLOCK, span), 0)
    col = lax.broadcasted_iota(jnp.int32, (Q_BLOCK, span), 1)
    dist = row + Q_BLOCK - col
    valid = (dist >= 0) & (dist < A_WINDOW) & ((col >= Q_BLOCK) | (i > 0))
    distf = dist.astype(F32)

    s_all = _dot_nt(_stack_heads(q, A_HEADS), kv)
    probs, inv_l = [], []
    for hd in range(A_HEADS):
        sink = sink_ref[hd]
        s = s_all[hd * Q_BLOCK:(hd + 1) * Q_BLOCK]
        s = jnp.where(valid, s - slopes[hd] * distf, NEG_INF)
        m = jnp.maximum(jnp.max(s, axis=-1, keepdims=True), sink)
        e = jnp.exp(s - m)
        inv_l.append(1.0 / (jnp.sum(e, axis=-1, keepdims=True) + jnp.exp(sink - m)))
        probs.append(e.astype(BF16))
    o_all = _dot(jnp.concatenate(probs, axis=0), kv)
    o_all = jnp.concatenate([o_all[hd * Q_BLOCK:(hd + 1) * Q_BLOCK] * inv_l[hd] for hd in range(A_HEADS)], axis=0)
    o_ref[0] = _rms(_unstack_heads(o_all, A_HEADS), g_ref[...]).astype(BF16)


def _mixer_a(sinks, qa, kva, g):
    b, s, _ = qa.shape
    nblk = s // Q_BLOCK
    slopes, _ = _alibi_slopes()
    return pl.pallas_call(
        functools.partial(_mixa_kernel, slopes=slopes),
        out_shape=jax.ShapeDtypeStruct((b, s, A_Q), BF16),
        grid=(b, nblk),
        in_specs=[pl.BlockSpec(memory_space=pltpu.SMEM),
                  pl.BlockSpec((1, Q_BLOCK, A_Q), lambda bi, i: (bi, i, 0)),
                  pl.BlockSpec((1, Q_BLOCK, LANES), lambda bi, i: (bi, jnp.maximum(i - 1, 0), 0)),
                  pl.BlockSpec((1, Q_BLOCK, LANES), lambda bi, i: (bi, i, 0)),
                  pl.BlockSpec((1, A_Q), lambda bi, i: (0, 0))],
        out_specs=pl.BlockSpec((1, Q_BLOCK, A_Q), lambda bi, i: (bi, i, 0)),
        compiler_params=pltpu.CompilerParams(dimension_semantics=("arbitrary", "arbitrary"),
                                             vmem_limit_bytes=VMEM_LIMIT),
        name="mixer_a",
    )(sinks, qa, kva, kva, g)


def _mixb_kernel(q_ref, kvs_ref, kvw_ref, kvc_ref, gate_ref, mmap_ref, exp_ref, rep_ref, g_ref, o_ref,
                 acc_ref, m_ref, l_ref, *, slopes, n_cmp, n_slc, nblk):
    i = pl.program_id(1)
    t0 = i * Q_BLOCK
    q_all = _stack_heads(q_ref[0], B_HEADS)

    def head_rows(x, hd):
        return x[hd * Q_BLOCK:(hd + 1) * Q_BLOCK]

    kvc = kvc_ref[0]
    n_pad = kvc.shape[0]
    row_c = lax.broadcasted_iota(jnp.int32, (Q_BLOCK, n_pad), 0)
    col_c = lax.broadcasted_iota(jnp.int32, (Q_BLOCK, n_pad), 1)
    dist_c = (t0 + row_c) - (col_c * CMP_STRIDE + (CMP_LEN - 1))
    vis_c = (dist_c >= 0) & (col_c < n_cmp)
    dist_cf = dist_c.astype(F32)
    row1 = lax.broadcasted_iota(jnp.int32, (Q_BLOCK, 1), 0)
    any_vis = ((t0 + row1) >= (CMP_LEN - 1)).astype(F32)
    s_all = _dot_nt(q_all, kvc)
    probs = []
    p_sum = jnp.zeros((Q_BLOCK, n_pad), F32)
    for hd in range(B_HEADS):
        s = jnp.where(vis_c, head_rows(s_all, hd) - slopes[hd] * dist_cf, NEG_INF)
        m = jnp.max(s, axis=-1, keepdims=True)
        e = jnp.exp(s - m)
        pr = e * (any_vis / jnp.sum(e, axis=-1, keepdims=True))
        p_sum = p_sum + pr
        probs.append(pr.astype(BF16))
    o_cmp = _unstack_heads(_dot(jnp.concatenate(probs, axis=0), kvc), B_HEADS)

    p_hi, p_lo = _split_bf16(p_sum)
    imp = (_dot_nt(mmap_ref[...], p_hi) + _dot_nt(mmap_ref[...], p_lo))[:n_slc]
    blk = lax.broadcasted_iota(jnp.int32, (n_slc, Q_BLOCK), 0)
    tq = t0 + lax.broadcasted_iota(jnp.int32, (n_slc, Q_BLOCK), 1)
    cur = tq // SLC_LEN
    valid_b = blk <= cur
    forced = (blk == 0) | (blk == cur) | (blk == cur - 1)
    val = jnp.where(forced, FORCE, jnp.where(valid_b, imp, -FORCE))
    rank = jnp.zeros((n_slc, Q_BLOCK), F32)
    for j in range(n_slc):
        vj = val[j:j + 1, :]
        ahead = (vj > val) | ((vj == val) & (blk > j))
        rank = rank + ahead.astype(F32)
    sel = ((rank < SLC_TOP) & valid_b).astype(F32)
    sel = jnp.concatenate([sel, jnp.zeros((LANES - n_slc, Q_BLOCK), F32)], axis=0)
    sel_q = sel.T.astype(BF16)

    acc_ref[...] = jnp.zeros(acc_ref.shape, F32)
    m_ref[...] = jnp.full(m_ref.shape, NEG_INF, F32)
    l_ref[...] = jnp.zeros(l_ref.shape, F32)
    row_k = lax.broadcasted_iota(jnp.int32, (Q_BLOCK, KEY_CHUNK), 0)
    col_k = lax.broadcasted_iota(jnp.int32, (Q_BLOCK, KEY_CHUNK), 1)
    rc = row_k - col_k

    def slc_chunk(c, carry):
        k0 = pl.multiple_of(c * KEY_CHUNK, KEY_CHUNK)
        kv = kvs_ref[0, pl.ds(k0, KEY_CHUNK), :]
        dist = rc + (t0 - k0)
        allowed = (_dot(sel_q, exp_ref[c]) > 0.5) & (dist >= 0)
        distf = dist.astype(F32)
        s_all = _dot_nt(q_all, kv)
        probs, alphas = [], []
        for hd in range(B_HEADS):
            s = jnp.where(allowed, head_rows(s_all, hd) - slopes[hd] * distf, NEG_INF)
            m_old = m_ref[hd]
            m_new = jnp.maximum(m_old, jnp.max(s, axis=-1, keepdims=True))
            alpha = jnp.exp(m_old - m_new)
            e = jnp.exp(s - _lane_tile(m_new, KEY_CHUNK))
            l_ref[hd] = alpha * l_ref[hd] + jnp.sum(e, axis=-1, keepdims=True)
            m_ref[hd] = m_new
            probs.append(e.astype(BF16))
            alphas.append(alpha)
        o_all = _dot(jnp.concatenate(probs, axis=0), kv)
        for hd in range(B_HEADS):
            acc_ref[hd] = alphas[hd] * acc_ref[hd] + head_rows(o_all, hd)
        return carry

    lax.fori_loop(0, (t0 + Q_BLOCK + KEY_CHUNK - 1) // KEY_CHUNK, slc_chunk, 0)
    o_slc = _unstack_heads(jnp.concatenate([acc_ref[hd] * (1.0 / l_ref[hd]) for hd in range(B_HEADS)], axis=0),
                           B_HEADS)

    wspan = B_WINDOW + Q_BLOCK
    w0 = pl.multiple_of(jnp.clip(i - B_WINDOW // Q_BLOCK, 0, nblk - wspan // Q_BLOCK) * Q_BLOCK, Q_BLOCK)
    kvw = kvw_ref[0, pl.ds(w0, wspan), :]
    row_w = lax.broadcasted_iota(jnp.int32, (Q_BLOCK, wspan), 0)
    col_w = lax.broadcasted_iota(jnp.int32, (Q_BLOCK, wspan), 1)
    dist_w = (row_w - col_w) + (t0 - w0)
    valid_w = (dist_w >= 0) & (dist_w < B_WINDOW)
    dist_wf = dist_w.astype(F32)
    s_all = _dot_nt(q_all, kvw)
    probs, inv_l = [], []
    for hd in range(B_HEADS):
        s = jnp.where(valid_w, head_rows(s_all, hd) - slopes[hd] * dist_wf, NEG_INF)
        m = jnp.max(s, axis=-1, keepdims=True)
        e = jnp.exp(s - m)
        inv_l.append(1.0 / jnp.sum(e, axis=-1, keepdims=True))
        probs.append(e.astype(BF16))
    o_all = _dot(jnp.concatenate(probs, axis=0), kvw)
    o_win = _unstack_heads(jnp.concatenate([head_rows(o_all, hd) * inv_l[hd] for hd in range(B_HEADS)], axis=0),
                           B_HEADS)

    g_hi, g_lo = _split_bf16(jax.nn.sigmoid(gate_ref[0]))
    gates = [_dot(g_hi, rep_ref[r]) + _dot(g_lo, rep_ref[r]) for r in range(3)]
    ob = gates[0] * o_cmp + gates[1] * o_slc + gates[2] * o_win
    o_ref[0] = _rms(ob, g_ref[...]).astype(BF16)


def _mixer_b(qb, kvs, kvw, kvc, gate, g):
    b, s, _ = qb.shape
    nblk = s // Q_BLOCK
    n_pad = kvc.shape[1]
    n_cmp = s // CMP_STRIDE - 1
    n_slc = s // SLC_LEN
    n_chunks = s // KEY_CHUNK
    _, slopes = _alibi_slopes()
    cs = np.arange(n_pad)[None, :] * CMP_STRIDE
    ss = np.arange(LANES)[:, None] * SLC_LEN
    ov = np.maximum(0, np.minimum(cs + CMP_LEN, ss + SLC_LEN) - np.maximum(cs, ss)) / CMP_STRIDE
    ov = ov * (np.arange(n_pad)[None, :] < n_cmp) * (np.arange(LANES)[:, None] < n_slc)
    mmap_t = jnp.asarray(ov, BF16)
    key_blk = (np.arange(n_chunks)[:, None, None] * KEY_CHUNK + np.arange(KEY_CHUNK)[None, None, :]) // SLC_LEN
    expand = jnp.asarray(key_blk == np.arange(LANES)[None, :, None], BF16)
    rep = np.zeros((3, LANES, B_Q), np.float32)
    for r in range(3):
        for h in range(B_HEADS):
            rep[r, 3 * h + r, h * HEAD_DIM:(h + 1) * HEAD_DIM] = 1.0
    rep = jnp.asarray(rep, BF16)
    return pl.pallas_call(
        functools.partial(_mixb_kernel, slopes=slopes, n_cmp=n_cmp, n_slc=n_slc, nblk=nblk),
        out_shape=jax.ShapeDtypeStruct((b, s, B_Q), BF16),
        grid=(b, nblk),
        in_specs=[pl.BlockSpec((1, Q_BLOCK, B_Q), lambda bi, i: (bi, i, 0)),
                  pl.BlockSpec((1, s, LANES), lambda bi, i: (bi, 0, 0)),
                  pl.BlockSpec((1, s, LANES), lambda bi, i: (bi, 0, 0)),
                  pl.BlockSpec((1, n_pad, LANES), lambda bi, i: (bi, 0, 0)),
                  pl.BlockSpec((1, Q_BLOCK, LANES), lambda bi, i: (bi, i, 0)),
                  pl.BlockSpec((LANES, n_pad), lambda bi, i: (0, 0)),
                  pl.BlockSpec((n_chunks, LANES, KEY_CHUNK), lambda bi, i: (0, 0, 0)),
                  pl.BlockSpec((3, LANES, B_Q), lambda bi, i: (0, 0, 0)),
                  pl.BlockSpec((1, B_Q), lambda bi, i: (0, 0))],
        out_specs=pl.BlockSpec((1, Q_BLOCK, B_Q), lambda bi, i: (bi, i, 0)),
        scratch_shapes=[pltpu.VMEM((B_HEADS, Q_BLOCK, LANES), F32),
                        pltpu.VMEM((B_HEADS, Q_BLOCK, LANES), F32),
                        pltpu.VMEM((B_HEADS, Q_BLOCK, LANES), F32)],
        compiler_params=pltpu.CompilerParams(dimension_semantics=("arbitrary", "arbitrary"),
                                             vmem_limit_bytes=VMEM_LIMIT),
        name="mixer_b",
    )(qb, kvs, kvw, kvc, gate, mmap_t, expand, rep, g)


def _mixc_kernel(q_ref, k_ref, v_ref, tri_ref, g_ref, o_ref, acc_ref, carry_ref):
    i = pl.program_id(1)
    t0 = i * Q_BLOCK
    q_pairs = _stack_pairs(q_ref[0], C_HEADS)
    acc_ref[...] = jnp.zeros(acc_ref.shape, F32)
    carry_ref[...] = jnp.zeros(carry_ref.shape, F32)
    row = lax.broadcasted_iota(jnp.int32, (Q_BLOCK, KEY_CHUNK), 0)
    col = lax.broadcasted_iota(jnp.int32, (Q_BLOCK, KEY_CHUNK), 1)
    cr = col - row
    n_chunks = (t0 + Q_BLOCK + KEY_CHUNK - 1) // KEY_CHUNK

    def chunk(j, carry):
        c = n_chunks - 1 - j
        k0 = pl.multiple_of(c * KEY_CHUNK, KEY_CHUNK)
        k = k_ref[0, pl.ds(k0, KEY_CHUNK), :]
        v = v_ref[0, pl.ds(k0, KEY_CHUNK), :]
        past = cr < (t0 - k0)
        z_all = []
        for p in range(C_HEADS // 2):
            zz = _dot_nt(q_pairs[p], k[:, p * LANES:(p + 1) * LANES])
            z_all += [zz[:Q_BLOCK], zz[Q_BLOCK:]]
        log_w, parts_hi, parts_lo = [], [], []
        for hd in range(C_HEADS):
            z = z_all[hd]
            log_keep = -(jnp.maximum(z, 0.0) + jnp.log1p(jnp.exp(-jnp.abs(z))))
            lk = jnp.where(past, log_keep, 0.0)
            lk_hi, lk_lo = _split_bf16(lk)
            parts_hi.append(lk_hi)
            parts_lo.append(lk_lo)
            log_w.append(log_keep + z + _lane_tile(carry_ref[hd], KEY_CHUNK))
            carry_ref[hd] = carry_ref[hd] + jnp.sum(lk, axis=-1, keepdims=True)
        tails = _dot(jnp.concatenate(parts_hi + parts_lo, axis=0), tri_ref[...])
        weights = []
        for hd in range(C_HEADS):
            tail = tails[hd * Q_BLOCK:(hd + 1) * Q_BLOCK] + tails[(C_HEADS + hd) * Q_BLOCK:(C_HEADS + hd + 1) * Q_BLOCK]
            weights.append(jnp.where(past, jnp.exp(log_w[hd] + tail), 0.0).astype(BF16))
        for p in range(C_HEADS // 2):
            o = _dot(jnp.concatenate(weights[2 * p:2 * p + 2], axis=0), v[:, p * LANES:(p + 1) * LANES])
            acc_ref[2 * p] = acc_ref[2 * p] + o[:Q_BLOCK]
            acc_ref[2 * p + 1] = acc_ref[2 * p + 1] + o[Q_BLOCK:]
        return carry

    lax.fori_loop(0, n_chunks, chunk, 0)
    lane = lax.broadcasted_iota(jnp.int32, (Q_BLOCK, LANES), 1)
    lo = lane < HEAD_DIM
    oc = jnp.concatenate([jnp.where(lo, acc_ref[2 * p], acc_ref[2 * p + 1]) for p in range(C_HEADS // 2)], axis=1)
    o_ref[0] = _rms(oc, g_ref[...]).astype(BF16)


def _mixer_c(qc, kc, vc, g):
    b, s, _ = qc.shape
    nblk = s // Q_BLOCK
    tri = jnp.asarray(np.arange(KEY_CHUNK)[:, None] > np.arange(KEY_CHUNK)[None, :], BF16)
    return pl.pallas_call(
        _mixc_kernel,
        out_shape=jax.ShapeDtypeStruct((b, s, C_W), BF16),
        grid=(b, nblk),
        in_specs=[pl.BlockSpec((1, Q_BLOCK, C_W), lambda bi, i: (bi, i, 0)),
                  pl.BlockSpec((1, s, C_W), lambda bi, i: (bi, 0, 0)),
                  pl.BlockSpec((1, s, C_W), lambda bi, i: (bi, 0, 0)),
                  pl.BlockSpec((KEY_CHUNK, KEY_CHUNK), lambda bi, i: (0, 0)),
                  pl.BlockSpec((1, C_W), lambda bi, i: (0, 0))],
        out_specs=pl.BlockSpec((1, Q_BLOCK, C_W), lambda bi, i: (bi, i, 0)),
        scratch_shapes=[pltpu.VMEM((C_HEADS, Q_BLOCK, LANES), F32),
                        pltpu.VMEM((C_HEADS, Q_BLOCK, LANES), F32)],
        compiler_params=pltpu.CompilerParams(dimension_semantics=("arbitrary", "arbitrary"),
                                             vmem_limit_bytes=VMEM_LIMIT),
        name="mixer_c",
    )(qc, kc, vc, tri, g)


def _outffn_kernel(x_ref, ma_ref, mb_ref, mc_ref, wo_ref, gf_ref, wg_ref, wu_ref, wd_ref, gl_ref, o_ref, *, final):
    x = (x_ref[...] + _dot(ma_ref[...], wo_ref[:A_Q, :]) + _dot(mb_ref[...], wo_ref[A_Q:A_Q + B_Q, :])
         + _dot(mc_ref[...], wo_ref[A_Q + B_Q:, :]))
    h = _rms(x, gf_ref[...]).astype(BF16)
    o_ref[...] = x
    for c in range(D_FF // FF_CHUNK):
        sl = slice(c * FF_CHUNK, (c + 1) * FF_CHUNK)
        gate = _dot(h, wg_ref[:, sl])
        up = _dot(h, wu_ref[:, sl])
        o_ref[...] += _dot((jax.nn.silu(gate) * up).astype(BF16), wd_ref[sl, :])
    if final:
        o_ref[...] = _rms(o_ref[...], gl_ref[...])


def _outffn(x2, ma, mb, mc, wo, gf, wg, wu, wd, gl, final):
    t = x2.shape[0]
    tm = TOKEN_TILE
    const = lambda i: (0, 0)
    tok = lambda i: (i, 0)
    return pl.pallas_call(
        functools.partial(_outffn_kernel, final=final),
        out_shape=jax.ShapeDtypeStruct((t, D_MODEL), F32),
        grid=(t // tm,),
        in_specs=[pl.BlockSpec((tm, D_MODEL), tok),
                  pl.BlockSpec((tm, A_Q), tok),
                  pl.BlockSpec((tm, B_Q), tok),
                  pl.BlockSpec((tm, C_W), tok),
                  pl.BlockSpec((MIX_WIDTH, D_MODEL), const),
                  pl.BlockSpec((1, D_MODEL), const),
                  pl.BlockSpec((D_MODEL, D_FF), const),
                  pl.BlockSpec((D_MODEL, D_FF), const),
                  pl.BlockSpec((D_FF, D_MODEL), const),
                  pl.BlockSpec((1, D_MODEL), const)],
        out_specs=pl.BlockSpec((tm, D_MODEL), tok),
        compiler_params=pltpu.CompilerParams(dimension_semantics=("arbitrary",), vmem_limit_bytes=VMEM_LIMIT),
        name="outproj_ffn",
    )(x2, ma, mb, mc, wo, gf, wg, wu, wd, gl)


def _regroup_w_in(w_in):
    sizes = (A_Q, HEAD_DIM, HEAD_DIM, B_Q, HEAD_DIM, HEAD_DIM, HEAD_DIM, HEAD_DIM, HEAD_DIM, HEAD_DIM,
             N_GATES, C_W, C_W, C_W)
    offs = np.concatenate([[0], np.cumsum(sizes)])
    qa, ka, va, qb, kcb, vcb, ksb, vsb, kwb, vwb, gb, qc, kc, vc = [w_in[:, offs[j]:offs[j + 1]]
                                                                     for j in range(len(sizes))]
    pad = jnp.zeros((w_in.shape[0], LANES - N_GATES), w_in.dtype)
    cols = [qa * SCALE, ka, va, qb * SCALE, ksb, vsb, kwb, vwb, kcb, vcb, gb, pad, qc * SCALE, kc, vc]
    return jnp.concatenate(cols, axis=1).astype(BF16)


def _layer(x2, b, s, w_in, w_out, g_attn, g_ffn, g_out_a, g_out_b, g_out_c, sinks, cmp_pos, cmp_w1, cmp_w2,
           w_gate, w_up, w_down, g_final, final):
    row = lambda v: v.reshape(1, -1)
    qa, kva, qb, kvs, kvw, kcr, vcr, gate, qc, kc, vc = _inproj(x2, row(g_attn), _regroup_w_in(w_in))
    r3 = lambda a: a.reshape(b, s, a.shape[-1])

    n_chunk = s // CMP_STRIDE
    half = CMP_LEN // 2
    flat = half * HEAD_DIM
    xk = kcr.reshape(b, n_chunk, flat)
    xv = vcr.reshape(b, n_chunk, flat)
    pos = cmp_pos.reshape(2, 2, 1, flat)
    w1 = cmp_w1.reshape(2, 2, flat, CMP_HIDDEN).astype(BF16)
    zeros = jnp.zeros((CMP_HIDDEN, HEAD_DIM), cmp_w2.dtype)
    w2 = jnp.stack([jnp.concatenate([cmp_w2[0], zeros], axis=1),
                    jnp.concatenate([zeros, cmp_w2[1]], axis=1)]).astype(BF16)
    kvc = _compress(xk, xv, pos, w1, w2)

    ma = _mixer_a(sinks, r3(qa), r3(kva), row(g_out_a))
    mb = _mixer_b(r3(qb), r3(kvs), r3(kvw), kvc, r3(gate), row(g_out_b))
    mc = _mixer_c(r3(qc), r3(kc), r3(vc), row(g_out_c))
    t = b * s
    return _outffn(x2, ma.reshape(t, A_Q), mb.reshape(t, B_Q), mc.reshape(t, C_W), w_out.astype(BF16),
                   row(g_ffn), w_gate.astype(BF16), w_up.astype(BF16), w_down.astype(BF16), row(g_final), final)


def kernel(x, w_in, w_out, g_attn, g_ffn, g_out_a, g_out_b, g_out_c, sinks, cmp_pos, cmp_w1, cmp_w2, w_gate, w_up,
           w_down, g_final):
    b, s, d = x.shape
    depth = w_in.shape[0]
    assert d == D_MODEL and s % KEY_CHUNK == 0 and (b * s) % TOKEN_TILE == 0
    assert s // Q_BLOCK >= (B_WINDOW + Q_BLOCK) // Q_BLOCK and (s // CMP_STRIDE) % LANES == 0
    x2 = x.reshape(b * s, d)
    for l in range(depth):
        x2 = _layer(x2, b, s, w_in[l], w_out[l], g_attn[l], g_ffn[l], g_out_a[l], g_out_b[l], g_out_c[l], sinks[l],
                    cmp_pos[l], cmp_w1[l], cmp_w2[l], w_gate[l], w_up[l], w_down[l], g_final, l == depth - 1)
    return x2.reshape(b, s, d)
```

```python
import functools

import jax
import jax.numpy as jnp
import numpy as np
from jax import lax
from jax.experimental import pallas as pl
from jax.experimental.pallas import tpu as pltpu

F32 = jnp.float32
BF16 = jnp.bfloat16

D_MODEL = 1024
HEAD_DIM = 64
LANES = 128
Q_BLOCK = 128
A_HEADS = 8
A_WINDOW = 128
B_HEADS = 4
CMP_LEN = 32
CMP_STRIDE = 16
CMP_HIDDEN = 128
SLC_LEN = 64
SLC_TOP = 8
B_WINDOW = 512
C_HEADS = 4
A_Q = A_HEADS * HEAD_DIM
B_Q = B_HEADS * HEAD_DIM
C_W = C_HEADS * HEAD_DIM
MIX_WIDTH = A_Q + B_Q + C_W
D_FF = 2816
N_GATES = B_HEADS * 3
NEG_INF = -1e30
FORCE = 1e4
EPS = 1e-6
SCALE = HEAD_DIM ** -0.5
LOG2E = 1.4426950408889634

KEY_CHUNK = 256
SLC_CHUNK = 512
FF_CHUNK = 256
TOKEN_TILE = 512
VMEM_LIMIT = 56 * 1024 * 1024

_G_QA, _G_KVA, _G_QB, _G_KVS, _G_KVW, _G_KVC, _G_GATE, _G_QC, _G_KC, _G_VC, IN_PAD = (
    0, 512, 640, 896, 1024, 1152, 1280, 1408, 1664, 1920, 2176)


def _alibi_slopes():
    n = A_HEADS + B_HEADS
    sl = 2.0 ** (-8.0 * np.arange(1, n + 1) / n) * LOG2E
    return [float(v) for v in sl[:A_HEADS]], [float(v) for v in sl[A_HEADS:]]


def _dot(a, b):
    return jnp.dot(a, b, preferred_element_type=F32)


def _dot_nt(a, b):
    return lax.dot_general(a, b, (((1,), (1,)), ((), ())), preferred_element_type=F32)


def _split_bf16(x):
    hi = x.astype(BF16)
    lo = (x - hi.astype(F32)).astype(BF16)
    return hi, lo


def _rms(x, g):
    ms = jnp.mean(x * x, axis=-1, keepdims=True)
    return x * lax.rsqrt(ms + EPS) * g


def _lane_tile(x, width):
    return jnp.concatenate([x] * (width // LANES), axis=1)


def _swap_halves(x):
    return jnp.concatenate([x[:, HEAD_DIM:], x[:, :HEAD_DIM]], axis=1)


def _stack_heads(q, n_heads):
    rows = q.shape[0]
    lo = lax.broadcasted_iota(jnp.int32, (rows, LANES), 1) < HEAD_DIM
    tiles = []
    for p in range(n_heads // 2):
        pair = q[:, p * LANES:(p + 1) * LANES]
        zero = jnp.zeros_like(pair)
        tiles.append(jnp.where(lo, pair, zero))
        tiles.append(jnp.where(lo, _swap_halves(pair), zero))
    return jnp.concatenate(tiles, axis=0)


def _stack_pairs(q, n_heads):
    rows = q.shape[0]
    lo = lax.broadcasted_iota(jnp.int32, (rows, LANES), 1) < HEAD_DIM
    out = []
    for p in range(n_heads // 2):
        pair = q[:, p * LANES:(p + 1) * LANES]
        zero = jnp.zeros_like(pair)
        out.append(jnp.concatenate([jnp.where(lo, pair, zero), jnp.where(lo, zero, pair)], axis=0))
    return out


def _unstack_heads(o, n_heads):
    rows = o.shape[0] // n_heads
    lo = lax.broadcasted_iota(jnp.int32, (rows, LANES), 1) < HEAD_DIM
    pairs = []
    for p in range(n_heads // 2):
        even = o[(2 * p) * rows:(2 * p + 1) * rows]
        odd = o[(2 * p + 1) * rows:(2 * p + 2) * rows]
        pairs.append(jnp.where(lo, pltpu.roll(even, HEAD_DIM, 1), odd))
    return jnp.concatenate(pairs, axis=1)


def _inproj_kernel(x_ref, g_ref, w_ref, qa_ref, kva_ref, qb_ref, kvs_ref, kvw_ref, kcr_ref, vcr_ref,
                   gate_ref, qc_ref, kc_ref, vc_ref):
    h = _rms(x_ref[...], g_ref[...]).astype(BF16)

    def proj(lo, hi):
        return _dot(h, w_ref[:, lo:hi])

    qa_ref[...] = (proj(_G_QA, _G_KVA) * LOG2E).astype(BF16)
    kva_ref[...] = proj(_G_KVA, _G_QB).astype(BF16)
    qb_ref[...] = (proj(_G_QB, _G_KVS) * LOG2E).astype(BF16)
    kvs_ref[...] = proj(_G_KVS, _G_KVW).astype(BF16)
    kvw_ref[...] = proj(_G_KVW, _G_KVC).astype(BF16)
    kvc = proj(_G_KVC, _G_GATE)
    kcr_ref[...] = kvc[:, :HEAD_DIM]
    vcr_ref[...] = kvc[:, HEAD_DIM:]
    gate_ref[...] = proj(_G_GATE, _G_QC)
    qc_ref[...] = (proj(_G_QC, _G_KC) * LOG2E).astype(BF16)
    kc_ref[...] = proj(_G_KC, _G_VC).astype(BF16)
    vc_ref[...] = proj(_G_VC, IN_PAD).astype(BF16)


def _inproj(x2, g, w):
    t = x2.shape[0]
    tm = TOKEN_TILE
    widths = [(A_Q, BF16), (LANES, BF16), (B_Q, BF16), (LANES, BF16), (LANES, BF16), (HEAD_DIM, F32),
              (HEAD_DIM, F32), (LANES, F32), (C_W, BF16), (C_W, BF16), (C_W, BF16)]
    return pl.pallas_call(
        _inproj_kernel,
        out_shape=[jax.ShapeDtypeStruct((t, wd), dt) for wd, dt in widths],
        grid=(t // tm,),
        in_specs=[pl.BlockSpec((tm, D_MODEL), lambda i: (i, 0)),
                  pl.BlockSpec((1, D_MODEL), lambda i: (0, 0)),
                  pl.BlockSpec((D_MODEL, IN_PAD), lambda i: (0, 0))],
        out_specs=[pl.BlockSpec((tm, wd), lambda i: (i, 0)) for wd, _ in widths],
        compiler_params=pltpu.CompilerParams(dimension_semantics=("arbitrary",), vmem_limit_bytes=VMEM_LIMIT),
        name="inproj",
    )(x2, g, w)


def _compress_kernel(xk_ref, xv_ref, pos_ref, w1_ref, w2_ref, o_ref):
    n = xk_ref.shape[1]
    out = jnp.zeros((n, LANES), F32)
    for t, x_ref in enumerate((xk_ref, xv_ref)):
        x = x_ref[0]
        h_lo = _dot((x + pos_ref[t, 0]).astype(BF16), w1_ref[t, 0])
        h_hi = _dot((x + pos_ref[t, 1]).astype(BF16), w1_ref[t, 1])
        h = h_lo + pltpu.roll(h_hi, n - 1, 0)
        out = out + _dot(jax.nn.gelu(h).astype(BF16), w2_ref[t])
    o_ref[0] = out.astype(BF16)


def _compress(xk, xv, pos, w1, w2):
    b, n, wdt = xk.shape
    return pl.pallas_call(
        _compress_kernel,
        out_shape=jax.ShapeDtypeStruct((b, n, LANES), BF16),
        grid=(b,),
        in_specs=[pl.BlockSpec((1, n, wdt), lambda i: (i, 0, 0)),
                  pl.BlockSpec((1, n, wdt), lambda i: (i, 0, 0)),
                  pl.BlockSpec((2, 2, 1, wdt), lambda i: (0, 0, 0, 0)),
                  pl.BlockSpec((2, 2, wdt, CMP_HIDDEN), lambda i: (0, 0, 0, 0)),
                  pl.BlockSpec((2, CMP_HIDDEN, LANES), lambda i: (0, 0, 0))],
        out_specs=pl.BlockSpec((1, n, LANES), lambda i: (i, 0, 0)),
        compiler_params=pltpu.CompilerParams(dimension_semantics=("arbitrary",), vmem_limit_bytes=VMEM_LIMIT),
        name="compress",
    )(xk, xv, pos, w1, w2)


def _mixa_kernel(sink_ref, q_ref, kvp_ref, kvc_ref, g_ref, o_ref, *, slopes):
    i = pl.program_id(1)
    q = q_ref[0]
    kv = jnp.concatenate([kvp_ref[0], kvc_ref[0]], axis=0)
    span = 2 * Q_BLOCK
    row = lax.broadcasted_iota(jnp.int32, (Q_BLOCK, span), 0)
    col = lax.broadcasted_iota(jnp.int32, (Q_BLOCK, span), 1)
    dist = row + Q_BLOCK - col
    valid = (dist >= 0) & (dist < A_WINDOW) & ((col >= Q_BLOCK) | (i > 0))
    distf = dist.astype(F32)

    s_all = _dot_nt(_stack_heads(q, A_HEADS), kv)
    probs, inv_l = [], []
    for hd in range(A_HEADS):
        sink = sink_ref[hd] * LOG2E
        s = s_all[hd * Q_BLOCK:(hd + 1) * Q_BLOCK]
        s = jnp.where(valid, s - slopes[hd] * distf, NEG_INF)
        m = jnp.maximum(jnp.max(s, axis=-1, keepdims=True), sink)
        e = jnp.exp2(s - m)
        inv_l.append(1.0 / (jnp.sum(e, axis=-1, keepdims=True) + jnp.exp2(sink - m)))
        probs.append(e.astype(BF16))
    o_all = _dot(jnp.concatenate(probs, axis=0), kv)
    o_all = jnp.concatenate([o_all[hd * Q_BLOCK:(hd + 1) * Q_BLOCK] * inv_l[hd] for hd in range(A_HEADS)], axis=0)
    o_ref[0] = _rms(_unstack_heads(o_all, A_HEADS), g_ref[...]).astype(BF16)


def _mixer_a(sinks, qa, kva, g):
    b, s, _ = qa.shape
    nblk = s // Q_BLOCK
    slopes, _ = _alibi_slopes()
    return pl.pallas_call(
        functools.partial(_mixa_kernel, slopes=slopes),
        out_shape=jax.ShapeDtypeStruct((b, s, A_Q), BF16),
        grid=(b, nblk),
        in_specs=[pl.BlockSpec(memory_space=pltpu.SMEM),
                  pl.BlockSpec((1, Q_BLOCK, A_Q), lambda bi, i: (bi, i, 0)),
                  pl.BlockSpec((1, Q_BLOCK, LANES), lambda bi, i: (bi, jnp.maximum(i - 1, 0), 0)),
                  pl.BlockSpec((1, Q_BLOCK, LANES), lambda bi, i: (bi, i, 0)),
                  pl.BlockSpec((1, A_Q), lambda bi, i: (0, 0))],
        out_specs=pl.BlockSpec((1, Q_BLOCK, A_Q), lambda bi, i: (bi, i, 0)),
        compiler_params=pltpu.CompilerParams(dimension_semantics=("arbitrary", "arbitrary"),
                                             vmem_limit_bytes=VMEM_LIMIT),
        name="mixer_a",
    )(sinks, qa, kva, kva, g)


def _mixb_kernel(q_ref, kvs_ref, kvw_ref, kvc_ref, gate_ref, mmap_ref, exp_ref, rep_ref, g_ref, o_ref,
                 acc_ref, m_ref, l_ref, *, slopes, n_cmp, n_slc, nblk):
    i = pl.program_id(1)
    t0 = i * Q_BLOCK
    q_all = _stack_heads(q_ref[0], B_HEADS)

    def head_rows(x, hd):
        return x[hd * Q_BLOCK:(hd + 1) * Q_BLOCK]

    kvc = kvc_ref[0]
    n_pad = kvc.shape[0]
    row_c = lax.broadcasted_iota(jnp.int32, (Q_BLOCK, n_pad), 0)
    col_c = lax.broadcasted_iota(jnp.int32, (Q_BLOCK, n_pad), 1)
    dist_c = (t0 + row_c) - (col_c * CMP_STRIDE + (CMP_LEN - 1))
    vis_c = (dist_c >= 0) & (col_c < n_cmp)
    dist_cf = dist_c.astype(F32)
    row1 = lax.broadcasted_iota(jnp.int32, (Q_BLOCK, 1), 0)
    any_vis = ((t0 + row1) >= (CMP_LEN - 1)).astype(F32)
    s_all = _dot_nt(q_all, kvc)
    probs = []
    p_sum = jnp.zeros((Q_BLOCK, n_pad), F32)
    for hd in range(B_HEADS):
        s = jnp.where(vis_c, head_rows(s_all, hd) - slopes[hd] * dist_cf, NEG_INF)
        m = jnp.max(s, axis=-1, keepdims=True)
        e = jnp.exp2(s - m)
        pr = e * (any_vis / jnp.sum(e, axis=-1, keepdims=True))
        p_sum = p_sum + pr
        probs.append(pr.astype(BF16))
    o_cmp = _unstack_heads(_dot(jnp.concatenate(probs, axis=0), kvc), B_HEADS)

    p_hi, p_lo = _split_bf16(p_sum)
    imp = (_dot_nt(mmap_ref[...], p_hi) + _dot_nt(mmap_ref[...], p_lo))[:n_slc]
    blk = lax.broadcasted_iota(jnp.int32, (n_slc, Q_BLOCK), 0)
    tq = t0 + lax.broadcasted_iota(jnp.int32, (n_slc, Q_BLOCK), 1)
    cur = tq // SLC_LEN
    valid_b = blk <= cur
    forced = (blk == 0) | (blk == cur) | (blk == cur - 1)
    val = jnp.where(forced, FORCE, jnp.where(valid_b, imp, -FORCE))
    rank = jnp.zeros((n_slc, Q_BLOCK), F32)
    for j in range(n_slc):
        vj = val[j:j + 1, :]
        ahead = (vj > val) | ((vj == val) & (blk > j))
        rank = rank + ahead.astype(F32)
    sel = ((rank < SLC_TOP) & valid_b).astype(F32)
    sel = jnp.concatenate([sel, jnp.zeros((LANES - n_slc, Q_BLOCK), F32)], axis=0)
    sel_q = sel.T.astype(BF16)

    wspan = B_WINDOW + Q_BLOCK
    w0 = pl.multiple_of(jnp.clip(i - B_WINDOW // Q_BLOCK, 0, nblk - wspan // Q_BLOCK) * Q_BLOCK, Q_BLOCK)
    kvw = kvw_ref[0, pl.ds(w0, wspan), :]
    row_w = lax.broadcasted_iota(jnp.int32, (Q_BLOCK, wspan), 0)
    col_w = lax.broadcasted_iota(jnp.int32, (Q_BLOCK, wspan), 1)
    dist_w = (row_w - col_w) + (t0 - w0)
    valid_w = (dist_w >= 0) & (dist_w < B_WINDOW)
    dist_wf = dist_w.astype(F32)
    s_all = _dot_nt(q_all, kvw)
    probs, inv_l = [], []
    for hd in range(B_HEADS):
        s = jnp.where(valid_w, head_rows(s_all, hd) - slopes[hd] * dist_wf, NEG_INF)
        m = jnp.max(s, axis=-1, keepdims=True)
        e = jnp.exp2(s - m)
        inv_l.append(1.0 / jnp.sum(e, axis=-1, keepdims=True))
        probs.append(e.astype(BF16))
    o_all = _dot(jnp.concatenate(probs, axis=0), kvw)
    o_win = _unstack_heads(jnp.concatenate([head_rows(o_all, hd) * inv_l[hd] for hd in range(B_HEADS)], axis=0),
                           B_HEADS)

    g_hi, g_lo = _split_bf16(jax.nn.sigmoid(gate_ref[0]))
    gates = [_dot(g_hi, rep_ref[r]) + _dot(g_lo, rep_ref[r]) for r in range(3)]
    ob_rest = gates[0] * o_cmp + gates[2] * o_win

    acc_ref[...] = jnp.zeros(acc_ref.shape, F32)
    m_ref[...] = jnp.full(m_ref.shape, NEG_INF, F32)
    l_ref[...] = jnp.zeros(l_ref.shape, F32)
    def slc_chunk(c, width):
        k0 = pl.multiple_of(c * SLC_CHUNK, SLC_CHUNK)
        kv = kvs_ref[0, pl.ds(k0, width), :]
        row_k = lax.broadcasted_iota(jnp.int32, (Q_BLOCK, width), 0)
        col_k = lax.broadcasted_iota(jnp.int32, (Q_BLOCK, width), 1)
        dist = (row_k - col_k) + (t0 - k0)
        allowed = (_dot(sel_q, exp_ref[c][:, :width]) > 0.5) & (dist >= 0)
        distf = dist.astype(F32)
        s_all = _dot_nt(q_all, kv)
        probs, alphas = [], []
        for hd in range(B_HEADS):
            s = jnp.where(allowed, head_rows(s_all, hd) - slopes[hd] * distf, NEG_INF)
            m_old = m_ref[hd]
            m_new = jnp.maximum(m_old, jnp.max(s, axis=-1, keepdims=True))
            alpha = jnp.exp2(m_old - m_new)
            e = jnp.exp2(s - _lane_tile(m_new, width))
            l_ref[hd] = alpha * l_ref[hd] + jnp.sum(e, axis=-1, keepdims=True)
            m_ref[hd] = m_new
            probs.append(e.astype(BF16))
            alphas.append(alpha)
        o_all = _dot(jnp.concatenate(probs, axis=0), kv)
        for hd in range(B_HEADS):
            acc_ref[hd] = alphas[hd] * acc_ref[hd] + head_rows(o_all, hd)

    n_half = lax.shift_right_logical(t0 + Q_BLOCK + SLC_CHUNK // 2 - 1, (SLC_CHUNK // 2).bit_length() - 1)
    n_full = lax.shift_right_logical(n_half, 1)

    def full_chunk(c, carry):
        slc_chunk(c, SLC_CHUNK)
        return carry

    lax.fori_loop(0, n_full, full_chunk, 0)

    @pl.when((n_half & 1) == 1)
    def _():
        slc_chunk(n_full, SLC_CHUNK // 2)

    o_slc = _unstack_heads(jnp.concatenate([acc_ref[hd] * (1.0 / l_ref[hd]) for hd in range(B_HEADS)], axis=0),
                           B_HEADS)

    ob = ob_rest + gates[1] * o_slc
    o_ref[0] = _rms(ob, g_ref[...]).astype(BF16)


def _mixer_b(qb, kvs, kvw, kvc, gate, g):
    b, s, _ = qb.shape
    nblk = s // Q_BLOCK
    n_pad = kvc.shape[1]
    n_cmp = s // CMP_STRIDE - 1
    n_slc = s // SLC_LEN
    n_chunks = s // SLC_CHUNK
    _, slopes = _alibi_slopes()
    cs = np.arange(n_pad)[None, :] * CMP_STRIDE
    ss = np.arange(LANES)[:, None] * SLC_LEN
    ov = np.maximum(0, np.minimum(cs + CMP_LEN, ss + SLC_LEN) - np.maximum(cs, ss)) / CMP_STRIDE
    ov = ov * (np.arange(n_pad)[None, :] < n_cmp) * (np.arange(LANES)[:, None] < n_slc)
    mmap_t = jnp.asarray(ov, BF16)
    key_blk = (np.arange(n_chunks)[:, None, None] * SLC_CHUNK + np.arange(SLC_CHUNK)[None, None, :]) // SLC_LEN
    expand = jnp.asarray(key_blk == np.arange(LANES)[None, :, None], BF16)
    rep = np.zeros((3, LANES, B_Q), np.float32)
    for r in range(3):
        for h in range(B_HEADS):
            rep[r, 3 * h + r, h * HEAD_DIM:(h + 1) * HEAD_DIM] = 1.0
    rep = jnp.asarray(rep, BF16)
    return pl.pallas_call(
        functools.partial(_mixb_kernel, slopes=slopes, n_cmp=n_cmp, n_slc=n_slc, nblk=nblk),
        out_shape=jax.ShapeDtypeStruct((b, s, B_Q), BF16),
        grid=(b, nblk),
        in_specs=[pl.BlockSpec((1, Q_BLOCK, B_Q), lambda bi, i: (bi, i, 0)),
                  pl.BlockSpec((1, s, LANES), lambda bi, i: (bi, 0, 0)),
                  pl.BlockSpec((1, s, LANES), lambda bi, i: (bi, 0, 0)),
                  pl.BlockSpec((1, n_pad, LANES), lambda bi, i: (bi, 0, 0)),
                  pl.BlockSpec((1, Q_BLOCK, LANES), lambda bi, i: (bi, i, 0)),
                  pl.BlockSpec((LANES, n_pad), lambda bi, i: (0, 0)),
                  pl.BlockSpec((n_chunks, LANES, SLC_CHUNK), lambda bi, i: (0, 0, 0)),
                  pl.BlockSpec((3, LANES, B_Q), lambda bi, i: (0, 0, 0)),
                  pl.BlockSpec((1, B_Q), lambda bi, i: (0, 0))],
        out_specs=pl.BlockSpec((1, Q_BLOCK, B_Q), lambda bi, i: (bi, i, 0)),
        scratch_shapes=[pltpu.VMEM((B_HEADS, Q_BLOCK, LANES), F32),
                        pltpu.VMEM((B_HEADS, Q_BLOCK, LANES), F32),
                        pltpu.VMEM((B_HEADS, Q_BLOCK, LANES), F32)],
        compiler_params=pltpu.CompilerParams(dimension_semantics=("arbitrary", "arbitrary"),
                                             vmem_limit_bytes=VMEM_LIMIT),
        name="mixer_b",
    )(qb, kvs, kvw, kvc, gate, mmap_t, expand, rep, g)


def _mixc_kernel(q_ref, k_ref, v_ref, tri_ref, g_ref, o_ref, acc_ref, carry_ref):
    i = pl.program_id(1)
    t0 = i * Q_BLOCK
    q_pairs = _stack_pairs(q_ref[0], C_HEADS)
    acc_ref[...] = jnp.zeros(acc_ref.shape, F32)
    carry_ref[...] = jnp.zeros(carry_ref.shape, F32)
    row = lax.broadcasted_iota(jnp.int32, (Q_BLOCK, KEY_CHUNK), 0)
    col = lax.broadcasted_iota(jnp.int32, (Q_BLOCK, KEY_CHUNK), 1)
    cr = col - row
    n_chunks = (t0 + Q_BLOCK + KEY_CHUNK - 1) // KEY_CHUNK

    def sweep(chunk_ids, masked):
        ks, vs, pasts = [], [], []
        for c in chunk_ids:
            k0 = pl.multiple_of(c * KEY_CHUNK, KEY_CHUNK)
            ks.append(k_ref[0, pl.ds(k0, KEY_CHUNK), :])
            vs.append(v_ref[0, pl.ds(k0, KEY_CHUNK), :])
            pasts.append(cr < (t0 - k0))
        z_all = []
        for k in ks:
            for p in range(C_HEADS // 2):
                zz = _dot_nt(q_pairs[p], k[:, p * LANES:(p + 1) * LANES])
                z_all += [zz[:Q_BLOCK], zz[Q_BLOCK:]]
        carries = [carry_ref[hd] for hd in range(C_HEADS)]
        log_w, parts_hi, parts_lo = [], [], []
        for ci in range(len(chunk_ids)):
            for hd in range(C_HEADS):
                z = z_all[ci * C_HEADS + hd]
                keep = -(jnp.maximum(z, 0.0) + jnp.log2(1.0 + jnp.exp2(-jnp.abs(z))))
                lk = jnp.where(pasts[ci], keep, 0.0) if masked else keep
                lk_hi, lk_lo = _split_bf16(lk)
                parts_hi.append(lk_hi)
                parts_lo.append(lk_lo)
                log_w.append(keep + z + _lane_tile(carries[hd], KEY_CHUNK))
                carries[hd] = carries[hd] + jnp.sum(lk, axis=-1, keepdims=True)
        for hd in range(C_HEADS):
            carry_ref[hd] = carries[hd]
        n_parts = len(parts_hi)
        tails = _dot(jnp.concatenate(parts_hi + parts_lo, axis=0), tri_ref[...])
        weights = []
        for j in range(n_parts):
            tail = tails[j * Q_BLOCK:(j + 1) * Q_BLOCK] + tails[(n_parts + j) * Q_BLOCK:(n_parts + j + 1) * Q_BLOCK]
            w = jnp.exp2(log_w[j] + tail)
            weights.append((jnp.where(pasts[j // C_HEADS], w, 0.0) if masked else w).astype(BF16))
        for p in range(C_HEADS // 2):
            w_pair = jnp.concatenate([jnp.concatenate(weights[ci * C_HEADS + 2 * p:ci * C_HEADS + 2 * p + 2], axis=0)
                                      for ci in range(len(chunk_ids))], axis=1)
            v_pair = jnp.concatenate([v[:, p * LANES:(p + 1) * LANES] for v in vs], axis=0)
            o = _dot(w_pair, v_pair)
            acc_ref[2 * p] = acc_ref[2 * p] + o[:Q_BLOCK]
            acc_ref[2 * p + 1] = acc_ref[2 * p + 1] + o[Q_BLOCK:]

    sweep([n_chunks - 1], True)
    n_pairs = lax.shift_right_logical(n_chunks - 1, 1)

    @pl.when(((n_chunks - 1) & 1) == 1)
    def _():
        sweep([n_chunks - 2], False)

    def earlier_pair(j, carry):
        c = 2 * (n_pairs - 1 - j)
        sweep([c + 1, c], False)
        return carry

    lax.fori_loop(0, n_pairs, earlier_pair, 0)
    lane = lax.broadcasted_iota(jnp.int32, (Q_BLOCK, LANES), 1)
    lo = lane < HEAD_DIM
    oc = jnp.concatenate([jnp.where(lo, acc_ref[2 * p], acc_ref[2 * p + 1]) for p in range(C_HEADS // 2)], axis=1)
    o_ref[0] = _rms(oc, g_ref[...]).astype(BF16)


def _mixer_c(qc, kc, vc, g):
    b, s, _ = qc.shape
    nblk = s // Q_BLOCK
    tri = jnp.asarray(np.arange(KEY_CHUNK)[:, None] > np.arange(KEY_CHUNK)[None, :], BF16)
    return pl.pallas_call(
        _mixc_kernel,
        out_shape=jax.ShapeDtypeStruct((b, s, C_W), BF16),
        grid=(b, nblk),
        in_specs=[pl.BlockSpec((1, Q_BLOCK, C_W), lambda bi, i: (bi, i, 0)),
                  pl.BlockSpec((1, s, C_W), lambda bi, i: (bi, 0, 0)),
                  pl.BlockSpec((1, s, C_W), lambda bi, i: (bi, 0, 0)),
                  pl.BlockSpec((KEY_CHUNK, KEY_CHUNK), lambda bi, i: (0, 0)),
                  pl.BlockSpec((1, C_W), lambda bi, i: (0, 0))],
        out_specs=pl.BlockSpec((1, Q_BLOCK, C_W), lambda bi, i: (bi, i, 0)),
        scratch_shapes=[pltpu.VMEM((C_HEADS, Q_BLOCK, LANES), F32),
                        pltpu.VMEM((C_HEADS, Q_BLOCK, LANES), F32)],
        compiler_params=pltpu.CompilerParams(dimension_semantics=("arbitrary", "arbitrary"),
                                             vmem_limit_bytes=VMEM_LIMIT),
        name="mixer_c",
    )(qc, kc, vc, tri, g)


def _outffn_kernel(x_ref, ma_ref, mb_ref, mc_ref, wo_ref, gf_ref, wg_ref, wu_ref, wd_ref, gl_ref, o_ref, *, final):
    x = (x_ref[...] + _dot(ma_ref[...], wo_ref[:A_Q, :]) + _dot(mb_ref[...], wo_ref[A_Q:A_Q + B_Q, :])
         + _dot(mc_ref[...], wo_ref[A_Q + B_Q:, :]))
    h = _rms(x, gf_ref[...]).astype(BF16)
    o_ref[...] = x
    for c in range(D_FF // FF_CHUNK):
        sl = slice(c * FF_CHUNK, (c + 1) * FF_CHUNK)
        gate = _dot(h, wg_ref[:, sl])
        up = _dot(h, wu_ref[:, sl])
        o_ref[...] += _dot((jax.nn.silu(gate) * up).astype(BF16), wd_ref[sl, :])
    if final:
        o_ref[...] = _rms(o_ref[...], gl_ref[...])


def _outffn(x2, ma, mb, mc, wo, gf, wg, wu, wd, gl, final):
    t = x2.shape[0]
    tm = TOKEN_TILE
    const = lambda i: (0, 0)
    tok = lambda i: (i, 0)
    return pl.pallas_call(
        functools.partial(_outffn_kernel, final=final),
        out_shape=jax.ShapeDtypeStruct((t, D_MODEL), F32),
        grid=(t // tm,),
        in_specs=[pl.BlockSpec((tm, D_MODEL), tok),
                  pl.BlockSpec((tm, A_Q), tok),
                  pl.BlockSpec((tm, B_Q), tok),
                  pl.BlockSpec((tm, C_W), tok),
                  pl.BlockSpec((MIX_WIDTH, D_MODEL), const),
                  pl.BlockSpec((1, D_MODEL), const),
                  pl.BlockSpec((D_MODEL, D_FF), const),
                  pl.BlockSpec((D_MODEL, D_FF), const),
                  pl.BlockSpec((D_FF, D_MODEL), const),
                  pl.BlockSpec((1, D_MODEL), const)],
        out_specs=pl.BlockSpec((tm, D_MODEL), tok),
        compiler_params=pltpu.CompilerParams(dimension_semantics=("arbitrary",), vmem_limit_bytes=VMEM_LIMIT),
        name="outproj_ffn",
    )(x2, ma, mb, mc, wo, gf, wg, wu, wd, gl)


def _regroup_w_in(w_in):
    sizes = (A_Q, HEAD_DIM, HEAD_DIM, B_Q, HEAD_DIM, HEAD_DIM, HEAD_DIM, HEAD_DIM, HEAD_DIM, HEAD_DIM,
             N_GATES, C_W, C_W, C_W)
    offs = np.concatenate([[0], np.cumsum(sizes)])
    qa, ka, va, qb, kcb, vcb, ksb, vsb, kwb, vwb, gb, qc, kc, vc = [w_in[:, offs[j]:offs[j + 1]]
                                                                     for j in range(len(sizes))]
    pad = jnp.zeros((w_in.shape[0], LANES - N_GATES), w_in.dtype)
    cols = [qa * SCALE, ka, va, qb * SCALE, ksb, vsb, kwb, vwb, kcb, vcb, gb, pad, qc * SCALE, kc, vc]
    return jnp.concatenate(cols, axis=1).astype(BF16)


def _layer(x2, b, s, w_in, w_out, g_attn, g_ffn, g_out_a, g_out_b, g_out_c, sinks, cmp_pos, cmp_w1, cmp_w2,
           w_gate, w_up, w_down, g_final, final):
    row = lambda v: v.reshape(1, -1)
    qa, kva, qb, kvs, kvw, kcr, vcr, gate, qc, kc, vc = _inproj(x2, row(g_attn), _regroup_w_in(w_in))
    r3 = lambda a: a.reshape(b, s, a.shape[-1])

    n_chunk = s // CMP_STRIDE
    half = CMP_LEN // 2
    flat = half * HEAD_DIM
    xk = kcr.reshape(b, n_chunk, flat)
    xv = vcr.reshape(b, n_chunk, flat)
    pos = cmp_pos.reshape(2, 2, 1, flat)
    w1 = cmp_w1.reshape(2, 2, flat, CMP_HIDDEN).astype(BF16)
    zeros = jnp.zeros((CMP_HIDDEN, HEAD_DIM), cmp_w2.dtype)
    w2 = jnp.stack([jnp.concatenate([cmp_w2[0], zeros], axis=1),
                    jnp.concatenate([zeros, cmp_w2[1]], axis=1)]).astype(BF16)
    kvc = _compress(xk, xv, pos, w1, w2)

    ma = _mixer_a(sinks, r3(qa), r3(kva), row(g_out_a))
    mb = _mixer_b(r3(qb), r3(kvs), r3(kvw), kvc, r3(gate), row(g_out_b))
    mc = _mixer_c(r3(qc), r3(kc), r3(vc), row(g_out_c))
    t = b * s
    return _outffn(x2, ma.reshape(t, A_Q), mb.reshape(t, B_Q), mc.reshape(t, C_W), w_out.astype(BF16),
                   row(g_ffn), w_gate.astype(BF16), w_up.astype(BF16), w_down.astype(BF16), row(g_final), final)


def kernel(x, w_in, w_out, g_attn, g_ffn, g_out_a, g_out_b, g_out_c, sinks, cmp_pos, cmp_w1, cmp_w2, w_gate, w_up,
           w_down, g_final):
    b, s, d = x.shape
    depth = w_in.shape[0]
    assert d == D_MODEL and s % KEY_CHUNK == 0 and (b * s) % TOKEN_TILE == 0
    assert s // Q_BLOCK >= (B_WINDOW + Q_BLOCK) // Q_BLOCK and (s // CMP_STRIDE) % LANES == 0
    x2 = x.reshape(b * s, d)
    for l in range(depth):
        x2 = _layer(x2, b, s, w_in[l], w_out[l], g_attn[l], g_ffn[l], g_out_a[l], g_out_b[l], g_out_c[l], sinks[l],
                    cmp_pos[l], cmp_w1[l], cmp_w2[l], w_gate[l], w_up[l], w_down[l], g_final, l == depth - 1)
    return x2.reshape(b, s, d)
```

```python
import functools

import jax
import jax.numpy as jnp
import numpy as np
from jax import lax
from jax.experimental import pallas as pl
from jax.experimental.pallas import tpu as pltpu

F32 = jnp.float32
BF16 = jnp.bfloat16

D_MODEL = 1024
HEAD_DIM = 64
LANES = 128
Q_BLOCK = 128
Q_WIDE = 256
A_HEADS = 8
A_WINDOW = 128
B_HEADS = 4
CMP_LEN = 32
CMP_STRIDE = 16
CMP_HIDDEN = 128
SLC_LEN = 64
SLC_TOP = 8
B_WINDOW = 512
C_HEADS = 4
A_Q = A_HEADS * HEAD_DIM
B_Q = B_HEADS * HEAD_DIM
C_W = C_HEADS * HEAD_DIM
MIX_WIDTH = A_Q + B_Q + C_W
D_FF = 2816
N_GATES = B_HEADS * 3
NEG_INF = -1e30
FORCE = 1e4
EPS = 1e-6
SCALE = HEAD_DIM ** -0.5
LOG2E = 1.4426950408889634

KEY_CHUNK = 256
SLC_CHUNK = 512
FF_CHUNK = 256
TOKEN_TILE = 512
VMEM_LIMIT = 56 * 1024 * 1024

_G_QA, _G_KVA, _G_QB, _G_KVS, _G_KVW, _G_KVC, _G_GATE, _G_QC, _G_KC, _G_VC, IN_PAD = (
    0, 512, 640, 896, 1024, 1152, 1280, 1408, 1664, 1920, 2176)


def _alibi_slopes():
    n = A_HEADS + B_HEADS
    sl = 2.0 ** (-8.0 * np.arange(1, n + 1) / n) * LOG2E
    return [float(v) for v in sl[:A_HEADS]], [float(v) for v in sl[A_HEADS:]]


def _dot(a, b):
    return jnp.dot(a, b, preferred_element_type=F32)


def _dot_nt(a, b):
    return lax.dot_general(a, b, (((1,), (1,)), ((), ())), preferred_element_type=F32)


def _split_bf16(x):
    hi = x.astype(BF16)
    lo = (x - hi.astype(F32)).astype(BF16)
    return hi, lo


def _rms(x, g):
    ms = jnp.mean(x * x, axis=-1, keepdims=True)
    return x * lax.rsqrt(ms + EPS) * g


def _lane_tile(x, width):
    return jnp.concatenate([x] * (width // LANES), axis=1)


def _swap_halves(x):
    return jnp.concatenate([x[:, HEAD_DIM:], x[:, :HEAD_DIM]], axis=1)


def _stack_heads(q, n_heads):
    rows = q.shape[0]
    lo = lax.broadcasted_iota(jnp.int32, (rows, LANES), 1) < HEAD_DIM
    tiles = []
    for p in range(n_heads // 2):
        pair = q[:, p * LANES:(p + 1) * LANES]
        zero = jnp.zeros_like(pair)
        tiles.append(jnp.where(lo, pair, zero))
        tiles.append(jnp.where(lo, _swap_halves(pair), zero))
    return jnp.concatenate(tiles, axis=0)


def _stack_pairs(q, n_heads):
    rows = q.shape[0]
    lo = lax.broadcasted_iota(jnp.int32, (rows, LANES), 1) < HEAD_DIM
    out = []
    for p in range(n_heads // 2):
        pair = q[:, p * LANES:(p + 1) * LANES]
        zero = jnp.zeros_like(pair)
        out.append(jnp.concatenate([jnp.where(lo, pair, zero), jnp.where(lo, zero, pair)], axis=0))
    return out


def _unstack_heads(o, n_heads):
    rows = o.shape[0] // n_heads
    lo = lax.broadcasted_iota(jnp.int32, (rows, LANES), 1) < HEAD_DIM
    pairs = []
    for p in range(n_heads // 2):
        even = o[(2 * p) * rows:(2 * p + 1) * rows]
        odd = o[(2 * p + 1) * rows:(2 * p + 2) * rows]
        pairs.append(jnp.where(lo, pltpu.roll(even, HEAD_DIM, 1), odd))
    return jnp.concatenate(pairs, axis=1)


def _inproj_kernel(x_ref, g_ref, w_ref, qa_ref, kva_ref, qb_ref, kvs_ref, kvw_ref, kcr_ref, vcr_ref,
                   gate_ref, qc_ref, kc_ref, vc_ref):
    h = _rms(x_ref[...], g_ref[...]).astype(BF16)

    def proj(lo, hi):
        return _dot(h, w_ref[:, lo:hi])

    qa_ref[...] = (proj(_G_QA, _G_KVA) * LOG2E).astype(BF16)
    kva_ref[...] = proj(_G_KVA, _G_QB).astype(BF16)
    qb_ref[...] = (proj(_G_QB, _G_KVS) * LOG2E).astype(BF16)
    kvs_ref[...] = proj(_G_KVS, _G_KVW).astype(BF16)
    kvw_ref[...] = proj(_G_KVW, _G_KVC).astype(BF16)
    kvc = proj(_G_KVC, _G_GATE)
    kcr_ref[...] = kvc[:, :HEAD_DIM]
    vcr_ref[...] = kvc[:, HEAD_DIM:]
    gate_ref[...] = proj(_G_GATE, _G_QC)
    qc_ref[...] = (proj(_G_QC, _G_KC) * LOG2E).astype(BF16)
    kc_ref[...] = proj(_G_KC, _G_VC).astype(BF16)
    vc_ref[...] = proj(_G_VC, IN_PAD).astype(BF16)


def _inproj(x2, g, w):
    t = x2.shape[0]
    tm = TOKEN_TILE
    widths = [(A_Q, BF16), (LANES, BF16), (B_Q, BF16), (LANES, BF16), (LANES, BF16), (HEAD_DIM, F32),
              (HEAD_DIM, F32), (LANES, F32), (C_W, BF16), (C_W, BF16), (C_W, BF16)]
    return pl.pallas_call(
        _inproj_kernel,
        out_shape=[jax.ShapeDtypeStruct((t, wd), dt) for wd, dt in widths],
        grid=(t // tm,),
        in_specs=[pl.BlockSpec((tm, D_MODEL), lambda i: (i, 0)),
                  pl.BlockSpec((1, D_MODEL), lambda i: (0, 0)),
                  pl.BlockSpec((D_MODEL, IN_PAD), lambda i: (0, 0))],
        out_specs=[pl.BlockSpec((tm, wd), lambda i: (i, 0)) for wd, _ in widths],
        compiler_params=pltpu.CompilerParams(dimension_semantics=("arbitrary",), vmem_limit_bytes=VMEM_LIMIT),
        name="inproj",
    )(x2, g, w)


def _compress_kernel(xk_ref, xv_ref, pos_ref, w1_ref, w2_ref, o_ref):
    n = xk_ref.shape[1]
    out = jnp.zeros((n, LANES), F32)
    for t, x_ref in enumerate((xk_ref, xv_ref)):
        x = x_ref[0]
        h_lo = _dot((x + pos_ref[t, 0]).astype(BF16), w1_ref[t, 0])
        h_hi = _dot((x + pos_ref[t, 1]).astype(BF16), w1_ref[t, 1])
        h = h_lo + pltpu.roll(h_hi, n - 1, 0)
        out = out + _dot(jax.nn.gelu(h).astype(BF16), w2_ref[t])
    o_ref[0] = out.astype(BF16)


def _compress(xk, xv, pos, w1, w2):
    b, n, wdt = xk.shape
    return pl.pallas_call(
        _compress_kernel,
        out_shape=jax.ShapeDtypeStruct((b, n, LANES), BF16),
        grid=(b,),
        in_specs=[pl.BlockSpec((1, n, wdt), lambda i: (i, 0, 0)),
                  pl.BlockSpec((1, n, wdt), lambda i: (i, 0, 0)),
                  pl.BlockSpec((2, 2, 1, wdt), lambda i: (0, 0, 0, 0)),
                  pl.BlockSpec((2, 2, wdt, CMP_HIDDEN), lambda i: (0, 0, 0, 0)),
                  pl.BlockSpec((2, CMP_HIDDEN, LANES), lambda i: (0, 0, 0))],
        out_specs=pl.BlockSpec((1, n, LANES), lambda i: (i, 0, 0)),
        compiler_params=pltpu.CompilerParams(dimension_semantics=("arbitrary",), vmem_limit_bytes=VMEM_LIMIT),
        name="compress",
    )(xk, xv, pos, w1, w2)


def _mixa_kernel(sink_ref, q_ref, kvp_ref, kvc_ref, g_ref, o_ref, *, slopes):
    i = pl.program_id(1)
    q = q_ref[0]
    kv = jnp.concatenate([kvp_ref[0], kvc_ref[0]], axis=0)
    span = 2 * Q_BLOCK
    row = lax.broadcasted_iota(jnp.int32, (Q_BLOCK, span), 0)
    col = lax.broadcasted_iota(jnp.int32, (Q_BLOCK, span), 1)
    dist = row + Q_BLOCK - col
    valid = (dist >= 0) & (dist < A_WINDOW) & ((col >= Q_BLOCK) | (i > 0))
    distf = dist.astype(F32)

    s_all = _dot_nt(_stack_heads(q, A_HEADS), kv)
    probs, inv_l = [], []
    for hd in range(A_HEADS):
        sink = sink_ref[hd] * LOG2E
        s = s_all[hd * Q_BLOCK:(hd + 1) * Q_BLOCK]
        s = jnp.where(valid, s - slopes[hd] * distf, NEG_INF)
        m = jnp.maximum(jnp.max(s, axis=-1, keepdims=True), sink)
        e = jnp.exp2(s - m)
        inv_l.append(1.0 / (jnp.sum(e, axis=-1, keepdims=True) + jnp.exp2(sink - m)))
        probs.append(e.astype(BF16))
    o_all = _dot(jnp.concatenate(probs, axis=0), kv)
    o_all = jnp.concatenate([o_all[hd * Q_BLOCK:(hd + 1) * Q_BLOCK] * inv_l[hd] for hd in range(A_HEADS)], axis=0)
    o_ref[0] = _rms(_unstack_heads(o_all, A_HEADS), g_ref[...]).astype(BF16)


def _mixer_a(sinks, qa, kva, g):
    b, s, _ = qa.shape
    nblk = s // Q_BLOCK
    slopes, _ = _alibi_slopes()
    return pl.pallas_call(
        functools.partial(_mixa_kernel, slopes=slopes),
        out_shape=jax.ShapeDtypeStruct((b, s, A_Q), BF16),
        grid=(b, nblk),
        in_specs=[pl.BlockSpec(memory_space=pltpu.SMEM),
                  pl.BlockSpec((1, Q_BLOCK, A_Q), lambda bi, i: (bi, i, 0)),
                  pl.BlockSpec((1, Q_BLOCK, LANES), lambda bi, i: (bi, jnp.maximum(i - 1, 0), 0)),
                  pl.BlockSpec((1, Q_BLOCK, LANES), lambda bi, i: (bi, i, 0)),
                  pl.BlockSpec((1, A_Q), lambda bi, i: (0, 0))],
        out_specs=pl.BlockSpec((1, Q_BLOCK, A_Q), lambda bi, i: (bi, i, 0)),
        compiler_params=pltpu.CompilerParams(dimension_semantics=("arbitrary", "arbitrary"),
                                             vmem_limit_bytes=VMEM_LIMIT),
        name="mixer_a",
    )(sinks, qa, kva, kva, g)


def _mixb_kernel(q_ref, kvs_ref, kvw_ref, kvc_ref, gate_ref, mmap_ref, exp_ref, rep_ref, g_ref, o_ref,
                 acc_ref, m_ref, l_ref, *, slopes, n_cmp, n_slc):
    i = pl.program_id(1)
    t0 = i * Q_WIDE
    q_all = _stack_heads(q_ref[0], B_HEADS)

    def head_rows(x, hd):
        return x[hd * Q_WIDE:(hd + 1) * Q_WIDE]

    kvc = kvc_ref[0]
    n_pad = kvc.shape[0]
    row_c = lax.broadcasted_iota(jnp.int32, (Q_WIDE, n_pad), 0)
    col_c = lax.broadcasted_iota(jnp.int32, (Q_WIDE, n_pad), 1)
    dist_c = (t0 + row_c) - (col_c * CMP_STRIDE + (CMP_LEN - 1))
    vis_c = (dist_c >= 0) & (col_c < n_cmp)
    dist_cf = dist_c.astype(F32)
    row1 = lax.broadcasted_iota(jnp.int32, (Q_WIDE, 1), 0)
    any_vis = ((t0 + row1) >= (CMP_LEN - 1)).astype(F32)
    s_all = _dot_nt(q_all, kvc)
    probs = []
    p_sum = jnp.zeros((Q_WIDE, n_pad), F32)
    for hd in range(B_HEADS):
        s = jnp.where(vis_c, head_rows(s_all, hd) - slopes[hd] * dist_cf, NEG_INF)
        m = jnp.max(s, axis=-1, keepdims=True)
        e = jnp.exp2(s - m)
        pr = e * (any_vis / jnp.sum(e, axis=-1, keepdims=True))
        p_sum = p_sum + pr
        probs.append(pr.astype(BF16))
    o_cmp = _unstack_heads(_dot(jnp.concatenate(probs, axis=0), kvc), B_HEADS)

    p_hi, p_lo = _split_bf16(p_sum)
    imp = (_dot_nt(mmap_ref[...], p_hi) + _dot_nt(mmap_ref[...], p_lo))[:n_slc]
    blk = lax.broadcasted_iota(jnp.int32, (n_slc, Q_WIDE), 0)
    tq = t0 + lax.broadcasted_iota(jnp.int32, (n_slc, Q_WIDE), 1)
    cur = tq // SLC_LEN
    valid_b = blk <= cur
    forced = (blk == 0) | (blk == cur) | (blk == cur - 1)
    val = jnp.where(forced, FORCE, jnp.where(valid_b, imp, -FORCE))
    rank = jnp.zeros((n_slc, Q_WIDE), F32)
    for j in range(n_slc):
        vj = val[j:j + 1, :]
        ahead = (vj > val) | ((vj == val) & (blk > j))
        rank = rank + ahead.astype(F32)
    sel = ((rank < SLC_TOP) & valid_b).astype(F32)
    sel = jnp.concatenate([sel, jnp.zeros((LANES - n_slc, Q_WIDE), F32)], axis=0)
    sel_q = sel.T.astype(BF16)

    wspan = B_WINDOW + Q_WIDE
    w0 = pl.multiple_of(jnp.clip(t0 - B_WINDOW, 0, kvw_ref.shape[1] - wspan), LANES)
    kvw = kvw_ref[0, pl.ds(w0, wspan), :]
    row_w = lax.broadcasted_iota(jnp.int32, (Q_WIDE, wspan), 0)
    col_w = lax.broadcasted_iota(jnp.int32, (Q_WIDE, wspan), 1)
    dist_w = (row_w - col_w) + (t0 - w0)
    valid_w = (dist_w >= 0) & (dist_w < B_WINDOW)
    dist_wf = dist_w.astype(F32)
    s_all = _dot_nt(q_all, kvw)
    probs, inv_l = [], []
    for hd in range(B_HEADS):
        s = jnp.where(valid_w, head_rows(s_all, hd) - slopes[hd] * dist_wf, NEG_INF)
        m = jnp.max(s, axis=-1, keepdims=True)
        e = jnp.exp2(s - m)
        inv_l.append(1.0 / jnp.sum(e, axis=-1, keepdims=True))
        probs.append(e.astype(BF16))
    o_all = _dot(jnp.concatenate(probs, axis=0), kvw)
    o_win = _unstack_heads(jnp.concatenate([head_rows(o_all, hd) * inv_l[hd] for hd in range(B_HEADS)], axis=0),
                           B_HEADS)

    g_hi, g_lo = _split_bf16(jax.nn.sigmoid(gate_ref[0]))
    gates = [_dot(g_hi, rep_ref[r]) + _dot(g_lo, rep_ref[r]) for r in range(3)]
    ob_rest = gates[0] * o_cmp + gates[2] * o_win

    acc_ref[...] = jnp.zeros(acc_ref.shape, F32)
    m_ref[...] = jnp.full(m_ref.shape, NEG_INF, F32)
    l_ref[...] = jnp.zeros(l_ref.shape, F32)
    def slc_chunk(c, width):
        k0 = pl.multiple_of(c * SLC_CHUNK, SLC_CHUNK)
        kv = kvs_ref[0, pl.ds(k0, width), :]
        row_k = lax.broadcasted_iota(jnp.int32, (Q_WIDE, width), 0)
        col_k = lax.broadcasted_iota(jnp.int32, (Q_WIDE, width), 1)
        dist = (row_k - col_k) + (t0 - k0)
        allowed = (_dot(sel_q, exp_ref[c][:, :width]) > 0.5) & (dist >= 0)
        distf = dist.astype(F32)
        s_all = _dot_nt(q_all, kv)
        probs, alphas = [], []
        for hd in range(B_HEADS):
            s = jnp.where(allowed, head_rows(s_all, hd) - slopes[hd] * distf, NEG_INF)
            m_old = m_ref[hd]
            m_new = jnp.maximum(m_old, jnp.max(s, axis=-1, keepdims=True))
            alpha = jnp.exp2(m_old - m_new)
            e = jnp.exp2(s - _lane_tile(m_new, width))
            l_ref[hd] = alpha * l_ref[hd] + jnp.sum(e, axis=-1, keepdims=True)
            m_ref[hd] = m_new
            probs.append(e.astype(BF16))
            alphas.append(alpha)
        o_all = _dot(jnp.concatenate(probs, axis=0), kv)
        for hd in range(B_HEADS):
            acc_ref[hd] = alphas[hd] * acc_ref[hd] + head_rows(o_all, hd)

    n_half = lax.shift_right_logical(t0 + Q_WIDE + SLC_CHUNK // 2 - 1, (SLC_CHUNK // 2).bit_length() - 1)
    n_full = lax.shift_right_logical(n_half, 1)

    def full_chunk(c, carry):
        slc_chunk(c, SLC_CHUNK)
        return carry

    lax.fori_loop(0, n_full, full_chunk, 0)

    @pl.when((n_half & 1) == 1)
    def _():
        slc_chunk(n_full, SLC_CHUNK // 2)

    o_slc = _unstack_heads(jnp.concatenate([acc_ref[hd] * (1.0 / l_ref[hd]) for hd in range(B_HEADS)], axis=0),
                           B_HEADS)

    ob = ob_rest + gates[1] * o_slc
    o_ref[0] = _rms(ob, g_ref[...]).astype(BF16)


def _mixer_b(qb, kvs, kvw, kvc, gate, g):
    b, s, _ = qb.shape
    n_pad = kvc.shape[1]
    n_cmp = s // CMP_STRIDE - 1
    n_slc = s // SLC_LEN
    n_chunks = s // SLC_CHUNK
    _, slopes = _alibi_slopes()
    cs = np.arange(n_pad)[None, :] * CMP_STRIDE
    ss = np.arange(LANES)[:, None] * SLC_LEN
    ov = np.maximum(0, np.minimum(cs + CMP_LEN, ss + SLC_LEN) - np.maximum(cs, ss)) / CMP_STRIDE
    ov = ov * (np.arange(n_pad)[None, :] < n_cmp) * (np.arange(LANES)[:, None] < n_slc)
    mmap_t = jnp.asarray(ov, BF16)
    key_blk = (np.arange(n_chunks)[:, None, None] * SLC_CHUNK + np.arange(SLC_CHUNK)[None, None, :]) // SLC_LEN
    expand = jnp.asarray(key_blk == np.arange(LANES)[None, :, None], BF16)
    rep = np.zeros((3, LANES, B_Q), np.float32)
    for r in range(3):
        for h in range(B_HEADS):
            rep[r, 3 * h + r, h * HEAD_DIM:(h + 1) * HEAD_DIM] = 1.0
    rep = jnp.asarray(rep, BF16)
    return pl.pallas_call(
        functools.partial(_mixb_kernel, slopes=slopes, n_cmp=n_cmp, n_slc=n_slc),
        out_shape=jax.ShapeDtypeStruct((b, s, B_Q), BF16),
        grid=(b, s // Q_WIDE),
        in_specs=[pl.BlockSpec((1, Q_WIDE, B_Q), lambda bi, i: (bi, i, 0)),
                  pl.BlockSpec((1, s, LANES), lambda bi, i: (bi, 0, 0)),
                  pl.BlockSpec((1, s, LANES), lambda bi, i: (bi, 0, 0)),
                  pl.BlockSpec((1, n_pad, LANES), lambda bi, i: (bi, 0, 0)),
                  pl.BlockSpec((1, Q_WIDE, LANES), lambda bi, i: (bi, i, 0)),
                  pl.BlockSpec((LANES, n_pad), lambda bi, i: (0, 0)),
                  pl.BlockSpec((n_chunks, LANES, SLC_CHUNK), lambda bi, i: (0, 0, 0)),
                  pl.BlockSpec((3, LANES, B_Q), lambda bi, i: (0, 0, 0)),
                  pl.BlockSpec((1, B_Q), lambda bi, i: (0, 0))],
        out_specs=pl.BlockSpec((1, Q_WIDE, B_Q), lambda bi, i: (bi, i, 0)),
        scratch_shapes=[pltpu.VMEM((B_HEADS, Q_WIDE, LANES), F32),
                        pltpu.VMEM((B_HEADS, Q_WIDE, LANES), F32),
                        pltpu.VMEM((B_HEADS, Q_WIDE, LANES), F32)],
        compiler_params=pltpu.CompilerParams(dimension_semantics=("arbitrary", "arbitrary"),
                                             vmem_limit_bytes=VMEM_LIMIT),
        name="mixer_b",
    )(qb, kvs, kvw, kvc, gate, mmap_t, expand, rep, g)


def _mixc_kernel(q_ref, k_ref, v_ref, tri_ref, g_ref, o_ref, acc_ref, carry_ref, z_ref, w_ref):
    i = pl.program_id(1)
    t0 = i * Q_WIDE
    n_pair = C_HEADS // 2
    q_pairs = _stack_pairs(q_ref[0], C_HEADS)
    acc_ref[...] = jnp.zeros(acc_ref.shape, F32)
    carry_ref[...] = jnp.zeros(carry_ref.shape, F32)
    row = lax.broadcasted_iota(jnp.int32, (Q_WIDE, KEY_CHUNK), 0)
    col = lax.broadcasted_iota(jnp.int32, (Q_WIDE, KEY_CHUNK), 1)
    cr = col - row
    n_chunks = (t0 + Q_WIDE + KEY_CHUNK - 1) // KEY_CHUNK

    def chunk_rows(ref, c):
        return ref[0, pl.ds(pl.multiple_of(c * KEY_CHUNK, KEY_CHUNK), KEY_CHUNK), :]

    def logits_to_scratch(c):
        k = chunk_rows(k_ref, c)
        for p in range(n_pair):
            z_ref[p] = _dot_nt(q_pairs[p], k[:, p * LANES:(p + 1) * LANES])

    def add_values(c):
        v = chunk_rows(v_ref, c)
        for p in range(n_pair):
            o = _dot(w_ref[p], v[:, p * LANES:(p + 1) * LANES])
            acc_ref[2 * p] = acc_ref[2 * p] + o[:Q_WIDE]
            acc_ref[2 * p + 1] = acc_ref[2 * p + 1] + o[Q_WIDE:]

    def sweep(c, diagonal):
        if not diagonal:
            add_values(c + 1)
        past = cr < (t0 - c * KEY_CHUNK)
        log_w, parts_hi, parts_lo = [], [], []
        for hd in range(C_HEADS):
            z = z_ref[hd // 2, (hd % 2) * Q_WIDE:(hd % 2 + 1) * Q_WIDE, :]
            keep = -(jnp.maximum(z, 0.0) + jnp.log2(1.0 + jnp.exp2(-jnp.abs(z))))
            lk = jnp.where(past, keep, 0.0) if diagonal else keep
            lk_hi, lk_lo = _split_bf16(lk)
            parts_hi.append(lk_hi)
            parts_lo.append(lk_lo)
            log_w.append(keep + z + _lane_tile(carry_ref[hd], KEY_CHUNK))
            carry_ref[hd] = carry_ref[hd] + jnp.sum(lk, axis=-1, keepdims=True)
        tails = _dot(jnp.concatenate(parts_hi + parts_lo, axis=0), tri_ref[...])
        logits_to_scratch(jnp.maximum(c - 1, 0))
        weights = []
        for hd in range(C_HEADS):
            tail = tails[hd * Q_WIDE:(hd + 1) * Q_WIDE] + tails[(C_HEADS + hd) * Q_WIDE:(C_HEADS + hd + 1) * Q_WIDE]
            w = jnp.exp2(log_w[hd] + tail)
            weights.append((jnp.where(past, w, 0.0) if diagonal else w).astype(BF16))
        for p in range(n_pair):
            w_ref[p] = jnp.concatenate(weights[2 * p:2 * p + 2], axis=0)

    logits_to_scratch(n_chunks - 1)
    sweep(n_chunks - 1, True)

    def earlier(j, carry):
        sweep(n_chunks - 1 - j, False)
        return carry

    lax.fori_loop(1, n_chunks, earlier, 0)
    add_values(0)
    lane = lax.broadcasted_iota(jnp.int32, (Q_WIDE, LANES), 1)
    lo = lane < HEAD_DIM
    oc = jnp.concatenate([jnp.where(lo, acc_ref[2 * p], acc_ref[2 * p + 1]) for p in range(n_pair)], axis=1)
    o_ref[0] = _rms(oc, g_ref[...]).astype(BF16)


def _mixer_c(qc, kc, vc, g):
    b, s, _ = qc.shape
    tri = jnp.asarray(np.arange(KEY_CHUNK)[:, None] > np.arange(KEY_CHUNK)[None, :], BF16)
    return pl.pallas_call(
        _mixc_kernel,
        out_shape=jax.ShapeDtypeStruct((b, s, C_W), BF16),
        grid=(b, s // Q_WIDE),
        in_specs=[pl.BlockSpec((1, Q_WIDE, C_W), lambda bi, i: (bi, i, 0)),
                  pl.BlockSpec((1, s, C_W), lambda bi, i: (bi, 0, 0)),
                  pl.BlockSpec((1, s, C_W), lambda bi, i: (bi, 0, 0)),
                  pl.BlockSpec((KEY_CHUNK, KEY_CHUNK), lambda bi, i: (0, 0)),
                  pl.BlockSpec((1, C_W), lambda bi, i: (0, 0))],
        out_specs=pl.BlockSpec((1, Q_WIDE, C_W), lambda bi, i: (bi, i, 0)),
        scratch_shapes=[pltpu.VMEM((C_HEADS, Q_WIDE, LANES), F32),
                        pltpu.VMEM((C_HEADS, Q_WIDE, LANES), F32),
                        pltpu.VMEM((C_HEADS // 2, 2 * Q_WIDE, KEY_CHUNK), F32),
                        pltpu.VMEM((C_HEADS // 2, 2 * Q_WIDE, KEY_CHUNK), BF16)],
        compiler_params=pltpu.CompilerParams(dimension_semantics=("arbitrary", "arbitrary"),
                                             vmem_limit_bytes=VMEM_LIMIT),
        name="mixer_c",
    )(qc, kc, vc, tri, g)


def _outffn_kernel(x_ref, ma_ref, mb_ref, mc_ref, wo_ref, gf_ref, wg_ref, wu_ref, wd_ref, gl_ref, o_ref, *, final):
    x = (x_ref[...] + _dot(ma_ref[...], wo_ref[:A_Q, :]) + _dot(mb_ref[...], wo_ref[A_Q:A_Q + B_Q, :])
         + _dot(mc_ref[...], wo_ref[A_Q + B_Q:, :]))
    h = _rms(x, gf_ref[...]).astype(BF16)
    o_ref[...] = x
    for c in range(D_FF // FF_CHUNK):
        sl = slice(c * FF_CHUNK, (c + 1) * FF_CHUNK)
        gate = _dot(h, wg_ref[:, sl])
        up = _dot(h, wu_ref[:, sl])
        o_ref[...] += _dot((jax.nn.silu(gate) * up).astype(BF16), wd_ref[sl, :])
    if final:
        o_ref[...] = _rms(o_ref[...], gl_ref[...])


def _outffn(x2, ma, mb, mc, wo, gf, wg, wu, wd, gl, final):
    t = x2.shape[0]
    tm = TOKEN_TILE
    const = lambda i: (0, 0)
    tok = lambda i: (i, 0)
    return pl.pallas_call(
        functools.partial(_outffn_kernel, final=final),
        out_shape=jax.ShapeDtypeStruct((t, D_MODEL), F32),
        grid=(t // tm,),
        in_specs=[pl.BlockSpec((tm, D_MODEL), tok),
                  pl.BlockSpec((tm, A_Q), tok),
                  pl.BlockSpec((tm, B_Q), tok),
                  pl.BlockSpec((tm, C_W), tok),
                  pl.BlockSpec((MIX_WIDTH, D_MODEL), const),
                  pl.BlockSpec((1, D_MODEL), const),
                  pl.BlockSpec((D_MODEL, D_FF), const),
                  pl.BlockSpec((D_MODEL, D_FF), const),
                  pl.BlockSpec((D_FF, D_MODEL), const),
                  pl.BlockSpec((1, D_MODEL), const)],
        out_specs=pl.BlockSpec((tm, D_MODEL), tok),
        compiler_params=pltpu.CompilerParams(dimension_semantics=("arbitrary",), vmem_limit_bytes=VMEM_LIMIT),
        name="outproj_ffn",
    )(x2, ma, mb, mc, wo, gf, wg, wu, wd, gl)


def _regroup_w_in(w_in):
    sizes = (A_Q, HEAD_DIM, HEAD_DIM, B_Q, HEAD_DIM, HEAD_DIM, HEAD_DIM, HEAD_DIM, HEAD_DIM, HEAD_DIM,
             N_GATES, C_W, C_W, C_W)
    offs = np.concatenate([[0], np.cumsum(sizes)])
    qa, ka, va, qb, kcb, vcb, ksb, vsb, kwb, vwb, gb, qc, kc, vc = [w_in[:, offs[j]:offs[j + 1]]
                                                                     for j in range(len(sizes))]
    pad = jnp.zeros((w_in.shape[0], LANES - N_GATES), w_in.dtype)
    cols = [qa * SCALE, ka, va, qb * SCALE, ksb, vsb, kwb, vwb, kcb, vcb, gb, pad, qc * SCALE, kc, vc]
    return jnp.concatenate(cols, axis=1).astype(BF16)


def _layer(x2, b, s, w_in, w_out, g_attn, g_ffn, g_out_a, g_out_b, g_out_c, sinks, cmp_pos, cmp_w1, cmp_w2,
           w_gate, w_up, w_down, g_final, final):
    row = lambda v: v.reshape(1, -1)
    qa, kva, qb, kvs, kvw, kcr, vcr, gate, qc, kc, vc = _inproj(x2, row(g_attn), _regroup_w_in(w_in))
    r3 = lambda a: a.reshape(b, s, a.shape[-1])

    n_chunk = s // CMP_STRIDE
    half = CMP_LEN // 2
    flat = half * HEAD_DIM
    xk = kcr.reshape(b, n_chunk, flat)
    xv = vcr.reshape(b, n_chunk, flat)
    pos = cmp_pos.reshape(2, 2, 1, flat)
    w1 = cmp_w1.reshape(2, 2, flat, CMP_HIDDEN).astype(BF16)
    zeros = jnp.zeros((CMP_HIDDEN, HEAD_DIM), cmp_w2.dtype)
    w2 = jnp.stack([jnp.concatenate([cmp_w2[0], zeros], axis=1),
                    jnp.concatenate([zeros, cmp_w2[1]], axis=1)]).astype(BF16)
    kvc = _compress(xk, xv, pos, w1, w2)

    ma = _mixer_a(sinks, r3(qa), r3(kva), row(g_out_a))
    mb = _mixer_b(r3(qb), r3(kvs), r3(kvw), kvc, r3(gate), row(g_out_b))
    mc = _mixer_c(r3(qc), r3(kc), r3(vc), row(g_out_c))
    t = b * s
    return _outffn(x2, ma.reshape(t, A_Q), mb.reshape(t, B_Q), mc.reshape(t, C_W), w_out.astype(BF16),
                   row(g_ffn), w_gate.astype(BF16), w_up.astype(BF16), w_down.astype(BF16), row(g_final), final)


def kernel(x, w_in, w_out, g_attn, g_ffn, g_out_a, g_out_b, g_out_c, sinks, cmp_pos, cmp_w1, cmp_w2, w_gate, w_up,
           w_down, g_final):
    b, s, d = x.shape
    depth = w_in.shape[0]
    assert d == D_MODEL and s % KEY_CHUNK == 0 and (b * s) % TOKEN_TILE == 0
    assert s >= B_WINDOW + Q_WIDE and s % SLC_CHUNK == 0 and (s // CMP_STRIDE) % LANES == 0
    x2 = x.reshape(b * s, d)
    for l in range(depth):
        x2 = _layer(x2, b, s, w_in[l], w_out[l], g_attn[l], g_ffn[l], g_out_a[l], g_out_b[l], g_out_c[l], sinks[l],
                    cmp_pos[l], cmp_w1[l], cmp_w2[l], w_gate[l], w_up[l], w_down[l], g_final, l == depth - 1)
    return x2.reshape(b, s, d)
```

```python
import functools

import jax
import jax.numpy as jnp
import numpy as np
from jax import lax
from jax.experimental import pallas as pl
from jax.experimental.pallas import tpu as pltpu

F32 = jnp.float32
BF16 = jnp.bfloat16

D_MODEL = 1024
HEAD_DIM = 64
LANES = 128
Q_BLOCK = 128
Q_WIDE = 256
A_HEADS = 8
A_WINDOW = 128
B_HEADS = 4
CMP_LEN = 32
CMP_STRIDE = 16
CMP_HIDDEN = 128
SLC_LEN = 64
SLC_TOP = 8
B_WINDOW = 512
C_HEADS = 4
A_Q = A_HEADS * HEAD_DIM
B_Q = B_HEADS * HEAD_DIM
C_W = C_HEADS * HEAD_DIM
MIX_WIDTH = A_Q + B_Q + C_W
D_FF = 2816
N_GATES = B_HEADS * 3
NEG_INF = -1e30
FORCE = 1e4
EPS = 1e-6
SCALE = HEAD_DIM ** -0.5
LOG2E = 1.4426950408889634

KEY_CHUNK = 256
SLC_CHUNK = 512
FF_CHUNK = 256
TOKEN_TILE = 512
VMEM_LIMIT = 56 * 1024 * 1024

_G_QA, _G_KVA, _G_QB, _G_KVS, _G_KVW, _G_KVC, _G_GATE, _G_QC, _G_KC, _G_VC, IN_PAD = (
    0, 512, 640, 896, 1024, 1152, 1280, 1408, 1664, 1920, 2176)


def _alibi_slopes():
    n = A_HEADS + B_HEADS
    sl = 2.0 ** (-8.0 * np.arange(1, n + 1) / n) * LOG2E
    return [float(v) for v in sl[:A_HEADS]], [float(v) for v in sl[A_HEADS:]]


def _round_bf16(x):
    bits = np.ascontiguousarray(x, np.float32).view(np.uint32).astype(np.uint64)
    bits = ((bits + 0x7FFF + ((bits >> 16) & 1)) >> 16) << 16
    return bits.astype(np.uint32).view(np.float32)


def _key_features(pos):
    pos = np.asarray(pos, np.int64)
    assert pos.max() < 2048
    f = np.zeros((len(pos), LANES), np.float32)
    for j in range(3):
        f[:, HEAD_DIM + 2 * j] = 256 * (pos // 256)
        f[:, HEAD_DIM + 2 * j + 1] = pos % 256
    return jnp.asarray(f, BF16)


def _query_features(slopes, rows):
    f = np.zeros((len(slopes) * rows, LANES), np.float32)
    for h, slope in enumerate(slopes):
        rest = np.float32(slope)
        for j in range(3):
            part = _round_bf16(rest)
            rest = np.float32(rest - part)
            f[h * rows:(h + 1) * rows, HEAD_DIM + 2 * j:HEAD_DIM + 2 * j + 2] = part
    return jnp.asarray(f, BF16)


def _dot(a, b):
    return jnp.dot(a, b, preferred_element_type=F32)


def _dot_nt(a, b):
    return lax.dot_general(a, b, (((1,), (1,)), ((), ())), preferred_element_type=F32)


def _split_bf16(x):
    hi = x.astype(BF16)
    lo = (x - hi.astype(F32)).astype(BF16)
    return hi, lo


def _rms(x, g):
    ms = jnp.mean(x * x, axis=-1, keepdims=True)
    return x * lax.rsqrt(ms + EPS) * g


def _lane_tile(x, width):
    return jnp.concatenate([x] * (width // LANES), axis=1)


def _swap_halves(x):
    return jnp.concatenate([x[:, HEAD_DIM:], x[:, :HEAD_DIM]], axis=1)


def _stack_heads(q, n_heads, feat):
    rows = q.shape[0]
    lo = lax.broadcasted_iota(jnp.int32, (rows, LANES), 1) < HEAD_DIM
    tiles = []
    for p in range(n_heads // 2):
        pair = q[:, p * LANES:(p + 1) * LANES]
        tiles.append(jnp.where(lo, pair, feat[(2 * p) * rows:(2 * p + 1) * rows]))
        tiles.append(jnp.where(lo, _swap_halves(pair), feat[(2 * p + 1) * rows:(2 * p + 2) * rows]))
    return jnp.concatenate(tiles, axis=0)


def _with_key_features(kv, feat):
    lo = lax.broadcasted_iota(jnp.int32, kv.shape, 1) < HEAD_DIM
    return jnp.where(lo, kv, feat)


def _stack_pairs(q, n_heads):
    rows = q.shape[0]
    lo = lax.broadcasted_iota(jnp.int32, (rows, LANES), 1) < HEAD_DIM
    out = []
    for p in range(n_heads // 2):
        pair = q[:, p * LANES:(p + 1) * LANES]
        zero = jnp.zeros_like(pair)
        out.append(jnp.concatenate([jnp.where(lo, pair, zero), jnp.where(lo, zero, pair)], axis=0))
    return out


def _unstack_heads(o, n_heads):
    rows = o.shape[0] // n_heads
    lo = lax.broadcasted_iota(jnp.int32, (rows, LANES), 1) < HEAD_DIM
    pairs = []
    for p in range(n_heads // 2):
        even = o[(2 * p) * rows:(2 * p + 1) * rows]
        odd = o[(2 * p + 1) * rows:(2 * p + 2) * rows]
        pairs.append(jnp.where(lo, pltpu.roll(even, HEAD_DIM, 1), odd))
    return jnp.concatenate(pairs, axis=1)


def _inproj_kernel(x_ref, g_ref, w_ref, qa_ref, kva_ref, qb_ref, kvs_ref, kvw_ref, kcr_ref, vcr_ref,
                   gate_ref, qc_ref, kc_ref, vc_ref):
    h = _rms(x_ref[...], g_ref[...]).astype(BF16)

    def proj(lo, hi):
        return _dot(h, w_ref[:, lo:hi])

    qa_ref[...] = (proj(_G_QA, _G_KVA) * LOG2E).astype(BF16)
    kva_ref[...] = proj(_G_KVA, _G_QB).astype(BF16)
    qb_ref[...] = (proj(_G_QB, _G_KVS) * LOG2E).astype(BF16)
    kvs_ref[...] = proj(_G_KVS, _G_KVW).astype(BF16)
    kvw_ref[...] = proj(_G_KVW, _G_KVC).astype(BF16)
    kvc = proj(_G_KVC, _G_GATE)
    kcr_ref[...] = kvc[:, :HEAD_DIM]
    vcr_ref[...] = kvc[:, HEAD_DIM:]
    gate_ref[...] = proj(_G_GATE, _G_QC)
    qc_ref[...] = (proj(_G_QC, _G_KC) * LOG2E).astype(BF16)
    kc_ref[...] = proj(_G_KC, _G_VC).astype(BF16)
    vc_ref[...] = proj(_G_VC, IN_PAD).astype(BF16)


def _inproj(x2, g, w):
    t = x2.shape[0]
    tm = TOKEN_TILE
    widths = [(A_Q, BF16), (LANES, BF16), (B_Q, BF16), (LANES, BF16), (LANES, BF16), (HEAD_DIM, F32),
              (HEAD_DIM, F32), (LANES, F32), (C_W, BF16), (C_W, BF16), (C_W, BF16)]
    return pl.pallas_call(
        _inproj_kernel,
        out_shape=[jax.ShapeDtypeStruct((t, wd), dt) for wd, dt in widths],
        grid=(t // tm,),
        in_specs=[pl.BlockSpec((tm, D_MODEL), lambda i: (i, 0)),
                  pl.BlockSpec((1, D_MODEL), lambda i: (0, 0)),
                  pl.BlockSpec((D_MODEL, IN_PAD), lambda i: (0, 0))],
        out_specs=[pl.BlockSpec((tm, wd), lambda i: (i, 0)) for wd, _ in widths],
        compiler_params=pltpu.CompilerParams(dimension_semantics=("arbitrary",), vmem_limit_bytes=VMEM_LIMIT),
        name="inproj",
    )(x2, g, w)


def _compress_kernel(xk_ref, xv_ref, pos_ref, w1_ref, w2_ref, o_ref):
    n = xk_ref.shape[1]
    out = jnp.zeros((n, LANES), F32)
    for t, x_ref in enumerate((xk_ref, xv_ref)):
        x = x_ref[0]
        h_lo = _dot((x + pos_ref[t, 0]).astype(BF16), w1_ref[t, 0])
        h_hi = _dot((x + pos_ref[t, 1]).astype(BF16), w1_ref[t, 1])
        h = h_lo + pltpu.roll(h_hi, n - 1, 0)
        out = out + _dot(jax.nn.gelu(h).astype(BF16), w2_ref[t])
    o_ref[0] = out.astype(BF16)


def _compress(xk, xv, pos, w1, w2):
    b, n, wdt = xk.shape
    return pl.pallas_call(
        _compress_kernel,
        out_shape=jax.ShapeDtypeStruct((b, n, LANES), BF16),
        grid=(b,),
        in_specs=[pl.BlockSpec((1, n, wdt), lambda i: (i, 0, 0)),
                  pl.BlockSpec((1, n, wdt), lambda i: (i, 0, 0)),
                  pl.BlockSpec((2, 2, 1, wdt), lambda i: (0, 0, 0, 0)),
                  pl.BlockSpec((2, 2, wdt, CMP_HIDDEN), lambda i: (0, 0, 0, 0)),
                  pl.BlockSpec((2, CMP_HIDDEN, LANES), lambda i: (0, 0, 0))],
        out_specs=pl.BlockSpec((1, n, LANES), lambda i: (i, 0, 0)),
        compiler_params=pltpu.CompilerParams(dimension_semantics=("arbitrary",), vmem_limit_bytes=VMEM_LIMIT),
        name="compress",
    )(xk, xv, pos, w1, w2)


def _mixa_kernel(sink_ref, q_ref, kvp_ref, kvc_ref, pfp_ref, pfc_ref, qf_ref, g_ref, o_ref, *, slopes):
    i = pl.program_id(1)
    span = 2 * Q_BLOCK
    row = lax.broadcasted_iota(jnp.int32, (Q_BLOCK, span), 0)
    col = lax.broadcasted_iota(jnp.int32, (Q_BLOCK, span), 1)
    dist = row + Q_BLOCK - col
    band = (dist >= 0) & (dist < A_WINDOW)
    kv_rows = jnp.concatenate([kvp_ref[0], kvc_ref[0]], axis=0)
    pf_rows = jnp.concatenate([pfp_ref[...], pfc_ref[...]], axis=0)

    for r in range(Q_WIDE // Q_BLOCK):
        q = q_ref[0, r * Q_BLOCK:(r + 1) * Q_BLOCK, :]
        kv = kv_rows[r * Q_BLOCK:r * Q_BLOCK + span]
        valid = band & ((col >= Q_BLOCK) | (i > 0)) if r == 0 else band
        t_row = (i * Q_WIDE + r * Q_BLOCK + lax.broadcasted_iota(jnp.int32, (Q_BLOCK, 1), 0)).astype(F32)
        kx = _with_key_features(kv, pf_rows[r * Q_BLOCK:r * Q_BLOCK + span])
        s_all = _dot_nt(_stack_heads(q, A_HEADS, qf_ref[...]), kx)
        probs, inv_l = [], []
        for hd in range(A_HEADS):
            sink = sink_ref[hd] * LOG2E + slopes[hd] * t_row
            s = jnp.where(valid, s_all[hd * Q_BLOCK:(hd + 1) * Q_BLOCK], NEG_INF)
            m = jnp.maximum(jnp.max(s, axis=-1, keepdims=True), sink)
            e = jnp.exp2(s - m)
            inv_l.append(1.0 / (jnp.sum(e, axis=-1, keepdims=True) + jnp.exp2(sink - m)))
            probs.append(e.astype(BF16))
        o_all = _dot(jnp.concatenate(probs, axis=0), kv)
        o_all = jnp.concatenate([o_all[hd * Q_BLOCK:(hd + 1) * Q_BLOCK] * inv_l[hd] for hd in range(A_HEADS)], axis=0)
        o_ref[0, r * Q_BLOCK:(r + 1) * Q_BLOCK, :] = _rms(_unstack_heads(o_all, A_HEADS), g_ref[...]).astype(BF16)


def _mixer_a(sinks, qa, kva, g):
    b, s, _ = qa.shape
    slopes, _ = _alibi_slopes()
    pos_feat = _key_features(np.arange(s))
    sub = Q_WIDE // Q_BLOCK
    prev_block = lambda i: jnp.maximum(sub * i - 1, 0)
    return pl.pallas_call(
        functools.partial(_mixa_kernel, slopes=slopes),
        out_shape=jax.ShapeDtypeStruct((b, s, A_Q), BF16),
        grid=(b, s // Q_WIDE),
        in_specs=[pl.BlockSpec(memory_space=pltpu.SMEM),
                  pl.BlockSpec((1, Q_WIDE, A_Q), lambda bi, i: (bi, i, 0)),
                  pl.BlockSpec((1, Q_BLOCK, LANES), lambda bi, i: (bi, prev_block(i), 0)),
                  pl.BlockSpec((1, Q_WIDE, LANES), lambda bi, i: (bi, i, 0)),
                  pl.BlockSpec((Q_BLOCK, LANES), lambda bi, i: (prev_block(i), 0)),
                  pl.BlockSpec((Q_WIDE, LANES), lambda bi, i: (i, 0)),
                  pl.BlockSpec((A_HEADS * Q_BLOCK, LANES), lambda bi, i: (0, 0)),
                  pl.BlockSpec((1, A_Q), lambda bi, i: (0, 0))],
        out_specs=pl.BlockSpec((1, Q_WIDE, A_Q), lambda bi, i: (bi, i, 0)),
        compiler_params=pltpu.CompilerParams(dimension_semantics=("arbitrary", "arbitrary"),
                                             vmem_limit_bytes=VMEM_LIMIT),
        name="mixer_a",
    )(sinks, qa, kva, kva, pos_feat, pos_feat, _query_features(slopes, Q_BLOCK), g)


def _mixb_kernel(q_ref, kvs_ref, kvw_ref, kvc_ref, gate_ref, pf_ref, cf_ref, qf_ref, mmap_ref, exp_ref, rep_ref,
                 g_ref, o_ref, acc_ref, m_ref, l_ref, *, n_cmp, n_slc):
    i = pl.program_id(1)
    t0 = i * Q_WIDE
    q_all = _stack_heads(q_ref[0], B_HEADS, qf_ref[...])

    def head_rows(x, hd, rows=Q_WIDE):
        return x[hd * rows:(hd + 1) * rows]

    kvc = kvc_ref[0]
    n_pad = kvc.shape[0]
    row_c = lax.broadcasted_iota(jnp.int32, (Q_WIDE, n_pad), 0)
    col_c = lax.broadcasted_iota(jnp.int32, (Q_WIDE, n_pad), 1)
    vis_c = ((t0 + row_c) >= (col_c * CMP_STRIDE + (CMP_LEN - 1))) & (col_c < n_cmp)
    row1 = lax.broadcasted_iota(jnp.int32, (Q_WIDE, 1), 0)
    any_vis = ((t0 + row1) >= (CMP_LEN - 1)).astype(F32)
    s_all = _dot_nt(q_all, _with_key_features(kvc, cf_ref[...]))
    probs = []
    p_sum = jnp.zeros((Q_WIDE, n_pad), F32)
    for hd in range(B_HEADS):
        s = jnp.where(vis_c, head_rows(s_all, hd), NEG_INF)
        m = jnp.max(s, axis=-1, keepdims=True)
        e = jnp.exp2(s - m)
        pr = e * (any_vis / jnp.sum(e, axis=-1, keepdims=True))
        p_sum = p_sum + pr
        probs.append(pr.astype(BF16))
    o_cmp = _unstack_heads(_dot(jnp.concatenate(probs, axis=0), kvc), B_HEADS)

    p_hi, p_lo = _split_bf16(p_sum)
    imp = (_dot_nt(mmap_ref[...], p_hi) + _dot_nt(mmap_ref[...], p_lo))[:n_slc]
    blk = lax.broadcasted_iota(jnp.int32, (n_slc, Q_WIDE), 0)
    tq = t0 + lax.broadcasted_iota(jnp.int32, (n_slc, Q_WIDE), 1)
    cur = tq // SLC_LEN
    valid_b = blk <= cur
    forced = (blk == 0) | (blk == cur) | (blk == cur - 1)
    val = jnp.where(forced, FORCE, jnp.where(valid_b, imp, -FORCE))
    rank = jnp.zeros((n_slc, Q_WIDE), F32)
    for j in range(n_slc):
        vj = val[j:j + 1, :]
        ahead = (vj > val) | ((vj == val) & (blk > j))
        rank = rank + ahead.astype(F32)
    sel = ((rank < SLC_TOP) & valid_b).astype(F32)
    sel = jnp.concatenate([sel, jnp.zeros((LANES - n_slc, Q_WIDE), F32)], axis=0)
    sel_q = sel.T.astype(BF16)

    wspan = B_WINDOW + Q_BLOCK
    row_w = lax.broadcasted_iota(jnp.int32, (Q_BLOCK, wspan), 0)
    col_w = lax.broadcasted_iota(jnp.int32, (Q_BLOCK, wspan), 1)
    n_sub = Q_WIDE // Q_BLOCK
    o_sub = [[None] * n_sub for _ in range(B_HEADS)]
    for r in range(n_sub):
        r0 = t0 + r * Q_BLOCK
        w0 = pl.multiple_of(jnp.clip(r0 - B_WINDOW, 0, kvw_ref.shape[1] - wspan), LANES)
        kvw = kvw_ref[0, pl.ds(w0, wspan), :]
        dist_w = (row_w - col_w) + (r0 - w0)
        valid_w = (dist_w >= 0) & (dist_w < B_WINDOW)
        q_sub = jnp.concatenate([head_rows(q_all, hd)[r * Q_BLOCK:(r + 1) * Q_BLOCK] for hd in range(B_HEADS)], axis=0)
        s_all = _dot_nt(q_sub, _with_key_features(kvw, pf_ref[pl.ds(w0, wspan), :]))
        probs, inv_l = [], []
        for hd in range(B_HEADS):
            s = jnp.where(valid_w, head_rows(s_all, hd, Q_BLOCK), NEG_INF)
            m = jnp.max(s, axis=-1, keepdims=True)
            e = jnp.exp2(s - m)
            inv_l.append(1.0 / jnp.sum(e, axis=-1, keepdims=True))
            probs.append(e.astype(BF16))
        o_all = _dot(jnp.concatenate(probs, axis=0), kvw)
        for hd in range(B_HEADS):
            o_sub[hd][r] = head_rows(o_all, hd, Q_BLOCK) * inv_l[hd]
    o_win = _unstack_heads(jnp.concatenate([o_sub[hd][r] for hd in range(B_HEADS) for r in range(n_sub)], axis=0),
                           B_HEADS)

    g_hi, g_lo = _split_bf16(jax.nn.sigmoid(gate_ref[0]))
    gates = [_dot(g_hi, rep_ref[r]) + _dot(g_lo, rep_ref[r]) for r in range(3)]
    ob_rest = gates[0] * o_cmp + gates[2] * o_win

    acc_ref[...] = jnp.zeros(acc_ref.shape, F32)
    m_ref[...] = jnp.full(m_ref.shape, NEG_INF, F32)
    l_ref[...] = jnp.zeros(l_ref.shape, F32)
    def slc_chunk(c, width):
        k0 = pl.multiple_of(c * SLC_CHUNK, SLC_CHUNK)
        kv = kvs_ref[0, pl.ds(k0, width), :]
        row_k = lax.broadcasted_iota(jnp.int32, (Q_WIDE, width), 0)
        col_k = lax.broadcasted_iota(jnp.int32, (Q_WIDE, width), 1)
        allowed = (_dot(sel_q, exp_ref[c][:, :width]) > 0.5) & ((col_k - row_k) <= (t0 - k0))
        s_all = _dot_nt(q_all, _with_key_features(kv, pf_ref[pl.ds(k0, width), :]))
        probs, alphas = [], []
        for hd in range(B_HEADS):
            s = jnp.where(allowed, head_rows(s_all, hd), NEG_INF)
            m_old = m_ref[hd]
            m_new = jnp.maximum(m_old, jnp.max(s, axis=-1, keepdims=True))
            alpha = jnp.exp2(m_old - m_new)
            e = jnp.exp2(s - _lane_tile(m_new, width))
            l_ref[hd] = alpha * l_ref[hd] + jnp.sum(e, axis=-1, keepdims=True)
            m_ref[hd] = m_new
            probs.append(e.astype(BF16))
            alphas.append(alpha)
        o_all = _dot(jnp.concatenate(probs, axis=0), kv)
        for hd in range(B_HEADS):
            acc_ref[hd] = alphas[hd] * acc_ref[hd] + head_rows(o_all, hd)

    n_half = lax.shift_right_logical(t0 + Q_WIDE + SLC_CHUNK // 2 - 1, (SLC_CHUNK // 2).bit_length() - 1)
    n_full = lax.shift_right_logical(n_half, 1)

    def full_chunk(c, carry):
        slc_chunk(c, SLC_CHUNK)
        return carry

    lax.fori_loop(0, n_full, full_chunk, 0)

    @pl.when((n_half & 1) == 1)
    def _():
        slc_chunk(n_full, SLC_CHUNK // 2)

    o_slc = _unstack_heads(jnp.concatenate([acc_ref[hd] * (1.0 / l_ref[hd]) for hd in range(B_HEADS)], axis=0),
                           B_HEADS)

    ob = ob_rest + gates[1] * o_slc
    o_ref[0] = _rms(ob, g_ref[...]).astype(BF16)


def _mixer_b(qb, kvs, kvw, kvc, gate, g):
    b, s, _ = qb.shape
    n_pad = kvc.shape[1]
    n_cmp = s // CMP_STRIDE - 1
    n_slc = s // SLC_LEN
    n_chunks = s // SLC_CHUNK
    _, slopes = _alibi_slopes()
    pos_feat = _key_features(np.arange(s))
    cmp_feat = _key_features(np.minimum(np.arange(n_pad), n_cmp - 1) * CMP_STRIDE + (CMP_LEN - 1))
    cs = np.arange(n_pad)[None, :] * CMP_STRIDE
    ss = np.arange(LANES)[:, None] * SLC_LEN
    ov = np.maximum(0, np.minimum(cs + CMP_LEN, ss + SLC_LEN) - np.maximum(cs, ss)) / CMP_STRIDE
    ov = ov * (np.arange(n_pad)[None, :] < n_cmp) * (np.arange(LANES)[:, None] < n_slc)
    mmap_t = jnp.asarray(ov, BF16)
    key_blk = (np.arange(n_chunks)[:, None, None] * SLC_CHUNK + np.arange(SLC_CHUNK)[None, None, :]) // SLC_LEN
    expand = jnp.asarray(key_blk == np.arange(LANES)[None, :, None], BF16)
    rep = np.zeros((3, LANES, B_Q), np.float32)
    for r in range(3):
        for h in range(B_HEADS):
            rep[r, 3 * h + r, h * HEAD_DIM:(h + 1) * HEAD_DIM] = 1.0
    rep = jnp.asarray(rep, BF16)
    return pl.pallas_call(
        functools.partial(_mixb_kernel, n_cmp=n_cmp, n_slc=n_slc),
        out_shape=jax.ShapeDtypeStruct((b, s, B_Q), BF16),
        grid=(b, s // Q_WIDE),
        in_specs=[pl.BlockSpec((1, Q_WIDE, B_Q), lambda bi, i: (bi, i, 0)),
                  pl.BlockSpec((1, s, LANES), lambda bi, i: (bi, 0, 0)),
                  pl.BlockSpec((1, s, LANES), lambda bi, i: (bi, 0, 0)),
                  pl.BlockSpec((1, n_pad, LANES), lambda bi, i: (bi, 0, 0)),
                  pl.BlockSpec((1, Q_WIDE, LANES), lambda bi, i: (bi, i, 0)),
                  pl.BlockSpec((s, LANES), lambda bi, i: (0, 0)),
                  pl.BlockSpec((n_pad, LANES), lambda bi, i: (0, 0)),
                  pl.BlockSpec((B_HEADS * Q_WIDE, LANES), lambda bi, i: (0, 0)),
                  pl.BlockSpec((LANES, n_pad), lambda bi, i: (0, 0)),
                  pl.BlockSpec((n_chunks, LANES, SLC_CHUNK), lambda bi, i: (0, 0, 0)),
                  pl.BlockSpec((3, LANES, B_Q), lambda bi, i: (0, 0, 0)),
                  pl.BlockSpec((1, B_Q), lambda bi, i: (0, 0))],
        out_specs=pl.BlockSpec((1, Q_WIDE, B_Q), lambda bi, i: (bi, i, 0)),
        scratch_shapes=[pltpu.VMEM((B_HEADS, Q_WIDE, LANES), F32),
                        pltpu.VMEM((B_HEADS, Q_WIDE, LANES), F32),
                        pltpu.VMEM((B_HEADS, Q_WIDE, LANES), F32)],
        compiler_params=pltpu.CompilerParams(dimension_semantics=("arbitrary", "arbitrary"),
                                             vmem_limit_bytes=VMEM_LIMIT),
        name="mixer_b",
    )(qb, kvs, kvw, kvc, gate, pos_feat, cmp_feat, _query_features(slopes, Q_WIDE), mmap_t, expand, rep, g)


def _mixc_kernel(q_ref, k_ref, v_ref, tri_ref, g_ref, o_ref, acc_ref, carry_ref, z_ref, w_ref):
    i = pl.program_id(1)
    t0 = i * Q_WIDE
    n_pair = C_HEADS // 2
    q_pairs = _stack_pairs(q_ref[0], C_HEADS)
    acc_ref[...] = jnp.zeros(acc_ref.shape, F32)
    carry_ref[...] = jnp.zeros(carry_ref.shape, F32)
    row = lax.broadcasted_iota(jnp.int32, (Q_WIDE, KEY_CHUNK), 0)
    col = lax.broadcasted_iota(jnp.int32, (Q_WIDE, KEY_CHUNK), 1)
    cr = col - row
    n_chunks = (t0 + Q_WIDE + KEY_CHUNK - 1) // KEY_CHUNK

    def chunk_rows(ref, c):
        return ref[0, pl.ds(pl.multiple_of(c * KEY_CHUNK, KEY_CHUNK), KEY_CHUNK), :]

    def logits_to_scratch(c):
        k = chunk_rows(k_ref, c)
        for p in range(n_pair):
            z_ref[p] = _dot_nt(q_pairs[p], k[:, p * LANES:(p + 1) * LANES])

    def add_values(c):
        v = chunk_rows(v_ref, c)
        for p in range(n_pair):
            o = _dot(w_ref[p], v[:, p * LANES:(p + 1) * LANES])
            acc_ref[2 * p] = acc_ref[2 * p] + o[:Q_WIDE]
            acc_ref[2 * p + 1] = acc_ref[2 * p + 1] + o[Q_WIDE:]

    def sweep(c, diagonal):
        if not diagonal:
            add_values(c + 1)
        past = cr < (t0 - c * KEY_CHUNK)
        log_w, parts_hi, parts_lo = [], [], []
        for hd in range(C_HEADS):
            z = z_ref[hd // 2, (hd % 2) * Q_WIDE:(hd % 2 + 1) * Q_WIDE, :]
            keep = -(jnp.maximum(z, 0.0) + jnp.log2(1.0 + jnp.exp2(-jnp.abs(z))))
            lk = jnp.where(past, keep, 0.0) if diagonal else keep
            lk_hi, lk_lo = _split_bf16(lk)
            parts_hi.append(lk_hi)
            parts_lo.append(lk_lo)
            log_w.append(keep + z + _lane_tile(carry_ref[hd], KEY_CHUNK))
            carry_ref[hd] = carry_ref[hd] + jnp.sum(lk, axis=-1, keepdims=True)
        tails = _dot(jnp.concatenate(parts_hi + parts_lo, axis=0), tri_ref[...])
        logits_to_scratch(jnp.maximum(c - 1, 0))
        weights = []
        for hd in range(C_HEADS):
            tail = tails[hd * Q_WIDE:(hd + 1) * Q_WIDE] + tails[(C_HEADS + hd) * Q_WIDE:(C_HEADS + hd + 1) * Q_WIDE]
            w = jnp.exp2(log_w[hd] + tail)
            weights.append((jnp.where(past, w, 0.0) if diagonal else w).astype(BF16))
        for p in range(n_pair):
            w_ref[p] = jnp.concatenate(weights[2 * p:2 * p + 2], axis=0)

    logits_to_scratch(n_chunks - 1)
    sweep(n_chunks - 1, True)

    def earlier(j, carry):
        sweep(n_chunks - 1 - j, False)
        return carry

    lax.fori_loop(1, n_chunks, earlier, 0)
    add_values(0)
    lane = lax.broadcasted_iota(jnp.int32, (Q_WIDE, LANES), 1)
    lo = lane < HEAD_DIM
    oc = jnp.concatenate([jnp.where(lo, acc_ref[2 * p], acc_ref[2 * p + 1]) for p in range(n_pair)], axis=1)
    o_ref[0] = _rms(oc, g_ref[...]).astype(BF16)


def _mixer_c(qc, kc, vc, g):
    b, s, _ = qc.shape
    tri = jnp.asarray(np.arange(KEY_CHUNK)[:, None] > np.arange(KEY_CHUNK)[None, :], BF16)
    return pl.pallas_call(
        _mixc_kernel,
        out_shape=jax.ShapeDtypeStruct((b, s, C_W), BF16),
        grid=(b, s // Q_WIDE),
        in_specs=[pl.BlockSpec((1, Q_WIDE, C_W), lambda bi, i: (bi, i, 0)),
                  pl.BlockSpec((1, s, C_W), lambda bi, i: (bi, 0, 0)),
                  pl.BlockSpec((1, s, C_W), lambda bi, i: (bi, 0, 0)),
                  pl.BlockSpec((KEY_CHUNK, KEY_CHUNK), lambda bi, i: (0, 0)),
                  pl.BlockSpec((1, C_W), lambda bi, i: (0, 0))],
        out_specs=pl.BlockSpec((1, Q_WIDE, C_W), lambda bi, i: (bi, i, 0)),
        scratch_shapes=[pltpu.VMEM((C_HEADS, Q_WIDE, LANES), F32),
                        pltpu.VMEM((C_HEADS, Q_WIDE, LANES), F32),
                        pltpu.VMEM((C_HEADS // 2, 2 * Q_WIDE, KEY_CHUNK), F32),
                        pltpu.VMEM((C_HEADS // 2, 2 * Q_WIDE, KEY_CHUNK), BF16)],
        compiler_params=pltpu.CompilerParams(dimension_semantics=("arbitrary", "arbitrary"),
                                             vmem_limit_bytes=VMEM_LIMIT),
        name="mixer_c",
    )(qc, kc, vc, tri, g)


def _outffn_kernel(x_ref, ma_ref, mb_ref, mc_ref, wo_ref, gf_ref, wg_ref, wu_ref, wd_ref, gl_ref, o_ref, *, final):
    x = (x_ref[...] + _dot(ma_ref[...], wo_ref[:A_Q, :]) + _dot(mb_ref[...], wo_ref[A_Q:A_Q + B_Q, :])
         + _dot(mc_ref[...], wo_ref[A_Q + B_Q:, :]))
    h = _rms(x, gf_ref[...]).astype(BF16)
    o_ref[...] = x
    for c in range(D_FF // FF_CHUNK):
        sl = slice(c * FF_CHUNK, (c + 1) * FF_CHUNK)
        gate = _dot(h, wg_ref[:, sl])
        up = _dot(h, wu_ref[:, sl])
        o_ref[...] += _dot((jax.nn.silu(gate) * up).astype(BF16), wd_ref[sl, :])
    if final:
        o_ref[...] = _rms(o_ref[...], gl_ref[...])


def _outffn(x2, ma, mb, mc, wo, gf, wg, wu, wd, gl, final):
    t = x2.shape[0]
    tm = TOKEN_TILE
    const = lambda i: (0, 0)
    tok = lambda i: (i, 0)
    return pl.pallas_call(
        functools.partial(_outffn_kernel, final=final),
        out_shape=jax.ShapeDtypeStruct((t, D_MODEL), F32),
        grid=(t // tm,),
        in_specs=[pl.BlockSpec((tm, D_MODEL), tok),
                  pl.BlockSpec((tm, A_Q), tok),
                  pl.BlockSpec((tm, B_Q), tok),
                  pl.BlockSpec((tm, C_W), tok),
                  pl.BlockSpec((MIX_WIDTH, D_MODEL), const),
                  pl.BlockSpec((1, D_MODEL), const),
                  pl.BlockSpec((D_MODEL, D_FF), const),
                  pl.BlockSpec((D_MODEL, D_FF), const),
                  pl.BlockSpec((D_FF, D_MODEL), const),
                  pl.BlockSpec((1, D_MODEL), const)],
        out_specs=pl.BlockSpec((tm, D_MODEL), tok),
        compiler_params=pltpu.CompilerParams(dimension_semantics=("arbitrary",), vmem_limit_bytes=VMEM_LIMIT),
        name="outproj_ffn",
    )(x2, ma, mb, mc, wo, gf, wg, wu, wd, gl)


def _regroup_w_in(w_in):
    sizes = (A_Q, HEAD_DIM, HEAD_DIM, B_Q, HEAD_DIM, HEAD_DIM, HEAD_DIM, HEAD_DIM, HEAD_DIM, HEAD_DIM,
             N_GATES, C_W, C_W, C_W)
    offs = np.concatenate([[0], np.cumsum(sizes)])
    qa, ka, va, qb, kcb, vcb, ksb, vsb, kwb, vwb, gb, qc, kc, vc = [w_in[:, offs[j]:offs[j + 1]]
                                                                     for j in range(len(sizes))]
    pad = jnp.zeros((w_in.shape[0], LANES - N_GATES), w_in.dtype)
    cols = [qa * SCALE, ka, va, qb * SCALE, ksb, vsb, kwb, vwb, kcb, vcb, gb, pad, qc * SCALE, kc, vc]
    return jnp.concatenate(cols, axis=1).astype(BF16)


def _layer(x2, b, s, w_in, w_out, g_attn, g_ffn, g_out_a, g_out_b, g_out_c, sinks, cmp_pos, cmp_w1, cmp_w2,
           w_gate, w_up, w_down, g_final, final):
    row = lambda v: v.reshape(1, -1)
    qa, kva, qb, kvs, kvw, kcr, vcr, gate, qc, kc, vc = _inproj(x2, row(g_attn), _regroup_w_in(w_in))
    r3 = lambda a: a.reshape(b, s, a.shape[-1])

    n_chunk = s // CMP_STRIDE
    half = CMP_LEN // 2
    flat = half * HEAD_DIM
    xk = kcr.reshape(b, n_chunk, flat)
    xv = vcr.reshape(b, n_chunk, flat)
    pos = cmp_pos.reshape(2, 2, 1, flat)
    w1 = cmp_w1.reshape(2, 2, flat, CMP_HIDDEN).astype(BF16)
    zeros = jnp.zeros((CMP_HIDDEN, HEAD_DIM), cmp_w2.dtype)
    w2 = jnp.stack([jnp.concatenate([cmp_w2[0], zeros], axis=1),
                    jnp.concatenate([zeros, cmp_w2[1]], axis=1)]).astype(BF16)
    kvc = _compress(xk, xv, pos, w1, w2)

    ma = _mixer_a(sinks, r3(qa), r3(kva), row(g_out_a))
    mb = _mixer_b(r3(qb), r3(kvs), r3(kvw), kvc, r3(gate), row(g_out_b))
    mc = _mixer_c(r3(qc), r3(kc), r3(vc), row(g_out_c))
    t = b * s
    return _outffn(x2, ma.reshape(t, A_Q), mb.reshape(t, B_Q), mc.reshape(t, C_W), w_out.astype(BF16),
                   row(g_ffn), w_gate.astype(BF16), w_up.astype(BF16), w_down.astype(BF16), row(g_final), final)


def kernel(x, w_in, w_out, g_attn, g_ffn, g_out_a, g_out_b, g_out_c, sinks, cmp_pos, cmp_w1, cmp_w2, w_gate, w_up,
           w_down, g_final):
    b, s, d = x.shape
    depth = w_in.shape[0]
    assert d == D_MODEL and s % KEY_CHUNK == 0 and (b * s) % TOKEN_TILE == 0
    assert s >= B_WINDOW + Q_WIDE and s % SLC_CHUNK == 0 and (s // CMP_STRIDE) % LANES == 0
    x2 = x.reshape(b * s, d)
    for l in range(depth):
        x2 = _layer(x2, b, s, w_in[l], w_out[l], g_attn[l], g_ffn[l], g_out_a[l], g_out_b[l], g_out_c[l], sinks[l],
                    cmp_pos[l], cmp_w1[l], cmp_w2[l], w_gate[l], w_up[l], w_down[l], g_final, l == depth - 1)
    return x2.reshape(b, s, d)
```

```python
import functools

import jax
import jax.numpy as jnp
import numpy as np
from jax import lax
from jax.experimental import pallas as pl
from jax.experimental.pallas import tpu as pltpu

F32 = jnp.float32
BF16 = jnp.bfloat16

D_MODEL = 1024
HEAD_DIM = 64
LANES = 128
Q_BLOCK = 128
Q_WIDE = 256
A_HEADS = 8
A_WINDOW = 128
B_HEADS = 4
CMP_LEN = 32
CMP_STRIDE = 16
CMP_HIDDEN = 128
SLC_LEN = 64
SLC_TOP = 8
B_WINDOW = 512
C_HEADS = 4
A_Q = A_HEADS * HEAD_DIM
B_Q = B_HEADS * HEAD_DIM
C_W = C_HEADS * HEAD_DIM
MIX_WIDTH = A_Q + B_Q + C_W
D_FF = 2816
N_GATES = B_HEADS * 3
NEG_INF = -1e30
FORCE = 1e4
EPS = 1e-6
SCALE = HEAD_DIM ** -0.5
LOG2E = 1.4426950408889634

KEY_CHUNK = 256
SLC_CHUNK = 512
FF_CHUNK = 256
TOKEN_TILE = 512
INPROJ_TILE = 1024
VMEM_LIMIT = 56 * 1024 * 1024

_G_QA, _G_KVA, _G_QB, _G_KVS, _G_KVW, _G_KVC, _G_GATE, _G_QC, _G_KC, _G_VC, IN_PAD = (
    0, 512, 640, 896, 1024, 1152, 1280, 1408, 1664, 1920, 2176)


def _alibi_slopes():
    n = A_HEADS + B_HEADS
    sl = 2.0 ** (-8.0 * np.arange(1, n + 1) / n) * LOG2E
    return [float(v) for v in sl[:A_HEADS]], [float(v) for v in sl[A_HEADS:]]


def _round_bf16(x):
    bits = np.ascontiguousarray(x, np.float32).view(np.uint32).astype(np.uint64)
    bits = ((bits + 0x7FFF + ((bits >> 16) & 1)) >> 16) << 16
    return bits.astype(np.uint32).view(np.float32)


def _key_features(pos):
    pos = np.asarray(pos, np.int64)
    assert pos.max() < 2048
    f = np.zeros((len(pos), LANES), np.float32)
    for j in range(3):
        f[:, HEAD_DIM + 2 * j] = 256 * (pos // 256)
        f[:, HEAD_DIM + 2 * j + 1] = pos % 256
    return jnp.asarray(f, BF16)


def _query_features(slopes, rows):
    f = np.zeros((len(slopes) * rows, LANES), np.float32)
    for h, slope in enumerate(slopes):
        rest = np.float32(slope)
        for j in range(3):
            part = _round_bf16(rest)
            rest = np.float32(rest - part)
            f[h * rows:(h + 1) * rows, HEAD_DIM + 2 * j:HEAD_DIM + 2 * j + 2] = part
    return jnp.asarray(f, BF16)


def _dot(a, b):
    return jnp.dot(a, b, preferred_element_type=F32)


def _dot_nt(a, b):
    return lax.dot_general(a, b, (((1,), (1,)), ((), ())), preferred_element_type=F32)


def _split_bf16(x):
    hi = x.astype(BF16)
    lo = (x - hi.astype(F32)).astype(BF16)
    return hi, lo


def _rms(x, g):
    ms = jnp.mean(x * x, axis=-1, keepdims=True)
    return x * lax.rsqrt(ms + EPS) * g


def _lane_tile(x, width):
    return jnp.concatenate([x] * (width // LANES), axis=1)


def _swap_halves(x):
    return jnp.concatenate([x[:, HEAD_DIM:], x[:, :HEAD_DIM]], axis=1)


def _stack_heads(q, n_heads, feat):
    rows = q.shape[0]
    lo = lax.broadcasted_iota(jnp.int32, (rows, LANES), 1) < HEAD_DIM
    tiles = []
    for p in range(n_heads // 2):
        pair = q[:, p * LANES:(p + 1) * LANES]
        tiles.append(jnp.where(lo, pair, feat[(2 * p) * rows:(2 * p + 1) * rows]))
        tiles.append(jnp.where(lo, _swap_halves(pair), feat[(2 * p + 1) * rows:(2 * p + 2) * rows]))
    return jnp.concatenate(tiles, axis=0)


def _with_key_features(kv, feat):
    lo = lax.broadcasted_iota(jnp.int32, kv.shape, 1) < HEAD_DIM
    return jnp.where(lo, kv, feat)


def _stack_pairs(q, n_heads):
    rows = q.shape[0]
    lo = lax.broadcasted_iota(jnp.int32, (rows, LANES), 1) < HEAD_DIM
    out = []
    for p in range(n_heads // 2):
        pair = q[:, p * LANES:(p + 1) * LANES]
        zero = jnp.zeros_like(pair)
        out.append(jnp.concatenate([jnp.where(lo, pair, zero), jnp.where(lo, zero, pair)], axis=0))
    return out


def _unstack_heads(o, n_heads):
    rows = o.shape[0] // n_heads
    lo = lax.broadcasted_iota(jnp.int32, (rows, LANES), 1) < HEAD_DIM
    pairs = []
    for p in range(n_heads // 2):
        even = o[(2 * p) * rows:(2 * p + 1) * rows]
        odd = o[(2 * p + 1) * rows:(2 * p + 2) * rows]
        pairs.append(jnp.where(lo, pltpu.roll(even, HEAD_DIM, 1), odd))
    return jnp.concatenate(pairs, axis=1)


def _inproj_kernel(x_ref, g_ref, w_ref, qa_ref, kva_ref, qb_ref, kvs_ref, kvw_ref, kcr_ref, vcr_ref,
                   gate_ref, qc_ref, kc_ref, vc_ref):
    h = _rms(x_ref[...], g_ref[...]).astype(BF16)

    def proj(lo, hi):
        return _dot(h, w_ref[:, lo:hi])

    qa_ref[...] = (proj(_G_QA, _G_KVA) * LOG2E).astype(BF16)
    kva_ref[...] = proj(_G_KVA, _G_QB).astype(BF16)
    qb_ref[...] = (proj(_G_QB, _G_KVS) * LOG2E).astype(BF16)
    kvs_ref[...] = proj(_G_KVS, _G_KVW).astype(BF16)
    kvw_ref[...] = proj(_G_KVW, _G_KVC).astype(BF16)
    kvc = proj(_G_KVC, _G_GATE)
    kcr_ref[...] = kvc[:, :HEAD_DIM]
    vcr_ref[...] = kvc[:, HEAD_DIM:]
    gate_ref[...] = proj(_G_GATE, _G_QC)
    qc_ref[...] = (proj(_G_QC, _G_KC) * LOG2E).astype(BF16)
    kc_ref[...] = proj(_G_KC, _G_VC).astype(BF16)
    vc_ref[...] = proj(_G_VC, IN_PAD).astype(BF16)


def _inproj(x2, g, w):
    t = x2.shape[0]
    tm = INPROJ_TILE
    widths = [(A_Q, BF16), (LANES, BF16), (B_Q, BF16), (LANES, BF16), (LANES, BF16), (HEAD_DIM, F32),
              (HEAD_DIM, F32), (LANES, F32), (C_W, BF16), (C_W, BF16), (C_W, BF16)]
    return pl.pallas_call(
        _inproj_kernel,
        out_shape=[jax.ShapeDtypeStruct((t, wd), dt) for wd, dt in widths],
        grid=(t // tm,),
        in_specs=[pl.BlockSpec((tm, D_MODEL), lambda i: (i, 0)),
                  pl.BlockSpec((1, D_MODEL), lambda i: (0, 0)),
                  pl.BlockSpec((D_MODEL, IN_PAD), lambda i: (0, 0))],
        out_specs=[pl.BlockSpec((tm, wd), lambda i: (i, 0)) for wd, _ in widths],
        compiler_params=pltpu.CompilerParams(dimension_semantics=("arbitrary",), vmem_limit_bytes=VMEM_LIMIT),
        name="inproj",
    )(x2, g, w)


def _compress_kernel(xk_ref, xv_ref, pos_ref, w1_ref, w2_ref, o_ref):
    n = xk_ref.shape[1]
    out = jnp.zeros((n, LANES), F32)
    for t, x_ref in enumerate((xk_ref, xv_ref)):
        x = x_ref[0]
        h_lo = _dot((x + pos_ref[t, 0]).astype(BF16), w1_ref[t, 0])
        h_hi = _dot((x + pos_ref[t, 1]).astype(BF16), w1_ref[t, 1])
        h = h_lo + pltpu.roll(h_hi, n - 1, 0)
        out = out + _dot(jax.nn.gelu(h).astype(BF16), w2_ref[t])
    o_ref[0] = out.astype(BF16)


def _compress(xk, xv, pos, w1, w2):
    b, n, wdt = xk.shape
    return pl.pallas_call(
        _compress_kernel,
        out_shape=jax.ShapeDtypeStruct((b, n, LANES), BF16),
        grid=(b,),
        in_specs=[pl.BlockSpec((1, n, wdt), lambda i: (i, 0, 0)),
                  pl.BlockSpec((1, n, wdt), lambda i: (i, 0, 0)),
                  pl.BlockSpec((2, 2, 1, wdt), lambda i: (0, 0, 0, 0)),
                  pl.BlockSpec((2, 2, wdt, CMP_HIDDEN), lambda i: (0, 0, 0, 0)),
                  pl.BlockSpec((2, CMP_HIDDEN, LANES), lambda i: (0, 0, 0))],
        out_specs=pl.BlockSpec((1, n, LANES), lambda i: (i, 0, 0)),
        compiler_params=pltpu.CompilerParams(dimension_semantics=("arbitrary",), vmem_limit_bytes=VMEM_LIMIT),
        name="compress",
    )(xk, xv, pos, w1, w2)


def _mixa_kernel(sink_ref, q_ref, kvp_ref, kvc_ref, pfp_ref, pfc_ref, qf_ref, g_ref, o_ref, *, slopes):
    i = pl.program_id(1)
    span = 2 * Q_BLOCK
    row = lax.broadcasted_iota(jnp.int32, (Q_BLOCK, span), 0)
    col = lax.broadcasted_iota(jnp.int32, (Q_BLOCK, span), 1)
    dist = row + Q_BLOCK - col
    band = (dist >= 0) & (dist < A_WINDOW)
    kv_rows = jnp.concatenate([kvp_ref[0], kvc_ref[0]], axis=0)
    pf_rows = jnp.concatenate([pfp_ref[...], pfc_ref[...]], axis=0)

    for r in range(Q_WIDE // Q_BLOCK):
        q = q_ref[0, r * Q_BLOCK:(r + 1) * Q_BLOCK, :]
        kv = kv_rows[r * Q_BLOCK:r * Q_BLOCK + span]
        valid = band & ((col >= Q_BLOCK) | (i > 0)) if r == 0 else band
        t_row = (i * Q_WIDE + r * Q_BLOCK + lax.broadcasted_iota(jnp.int32, (Q_BLOCK, 1), 0)).astype(F32)
        kx = _with_key_features(kv, pf_rows[r * Q_BLOCK:r * Q_BLOCK + span])
        s_all = _dot_nt(_stack_heads(q, A_HEADS, qf_ref[...]), kx)
        probs, inv_l = [], []
        for hd in range(A_HEADS):
            sink = sink_ref[hd] * LOG2E + slopes[hd] * t_row
            s = jnp.where(valid, s_all[hd * Q_BLOCK:(hd + 1) * Q_BLOCK], NEG_INF)
            m = jnp.maximum(jnp.max(s, axis=-1, keepdims=True), sink)
            e = jnp.exp2(s - m)
            inv_l.append(1.0 / (jnp.sum(e, axis=-1, keepdims=True) + jnp.exp2(sink - m)))
            probs.append(e.astype(BF16))
        o_all = _dot(jnp.concatenate(probs, axis=0), kv)
        o_all = jnp.concatenate([o_all[hd * Q_BLOCK:(hd + 1) * Q_BLOCK] * inv_l[hd] for hd in range(A_HEADS)], axis=0)
        o_ref[0, r * Q_BLOCK:(r + 1) * Q_BLOCK, :] = _rms(_unstack_heads(o_all, A_HEADS), g_ref[...]).astype(BF16)


def _mixer_a(sinks, qa, kva, g):
    b, s, _ = qa.shape
    slopes, _ = _alibi_slopes()
    pos_feat = _key_features(np.arange(s))
    sub = Q_WIDE // Q_BLOCK
    prev_block = lambda i: jnp.maximum(sub * i - 1, 0)
    return pl.pallas_call(
        functools.partial(_mixa_kernel, slopes=slopes),
        out_shape=jax.ShapeDtypeStruct((b, s, A_Q), BF16),
        grid=(b, s // Q_WIDE),
        in_specs=[pl.BlockSpec(memory_space=pltpu.SMEM),
                  pl.BlockSpec((1, Q_WIDE, A_Q), lambda bi, i: (bi, i, 0)),
                  pl.BlockSpec((1, Q_BLOCK, LANES), lambda bi, i: (bi, prev_block(i), 0)),
                  pl.BlockSpec((1, Q_WIDE, LANES), lambda bi, i: (bi, i, 0)),
                  pl.BlockSpec((Q_BLOCK, LANES), lambda bi, i: (prev_block(i), 0)),
                  pl.BlockSpec((Q_WIDE, LANES), lambda bi, i: (i, 0)),
                  pl.BlockSpec((A_HEADS * Q_BLOCK, LANES), lambda bi, i: (0, 0)),
                  pl.BlockSpec((1, A_Q), lambda bi, i: (0, 0))],
        out_specs=pl.BlockSpec((1, Q_WIDE, A_Q), lambda bi, i: (bi, i, 0)),
        compiler_params=pltpu.CompilerParams(dimension_semantics=("arbitrary", "arbitrary"),
                                             vmem_limit_bytes=VMEM_LIMIT),
        name="mixer_a",
    )(sinks, qa, kva, kva, pos_feat, pos_feat, _query_features(slopes, Q_BLOCK), g)


def _mixb_kernel(q_ref, kvs_ref, kvw_ref, kvc_ref, gate_ref, pf_ref, cf_ref, qf_ref, mmap_ref, exp_ref, rep_ref,
                 g_ref, o_ref, acc_ref, m_ref, l_ref, *, n_cmp, n_slc):
    i = pl.program_id(1)
    t0 = i * Q_WIDE
    q_all = _stack_heads(q_ref[0], B_HEADS, qf_ref[...])

    def head_rows(x, hd, rows=Q_WIDE):
        return x[hd * rows:(hd + 1) * rows]

    kvc = kvc_ref[0]
    n_pad = kvc.shape[0]
    row_c = lax.broadcasted_iota(jnp.int32, (Q_WIDE, n_pad), 0)
    col_c = lax.broadcasted_iota(jnp.int32, (Q_WIDE, n_pad), 1)
    vis_c = ((t0 + row_c) >= (col_c * CMP_STRIDE + (CMP_LEN - 1))) & (col_c < n_cmp)
    row1 = lax.broadcasted_iota(jnp.int32, (Q_WIDE, 1), 0)
    any_vis = ((t0 + row1) >= (CMP_LEN - 1)).astype(F32)
    s_all = _dot_nt(q_all, _with_key_features(kvc, cf_ref[...]))
    probs = []
    p_sum = jnp.zeros((Q_WIDE, n_pad), F32)
    for hd in range(B_HEADS):
        s = jnp.where(vis_c, head_rows(s_all, hd), NEG_INF)
        m = jnp.max(s, axis=-1, keepdims=True)
        e = jnp.exp2(s - m)
        pr = e * (any_vis / jnp.sum(e, axis=-1, keepdims=True))
        p_sum = p_sum + pr
        probs.append(pr.astype(BF16))
    o_cmp = _unstack_heads(_dot(jnp.concatenate(probs, axis=0), kvc), B_HEADS)

    p_hi, p_lo = _split_bf16(p_sum)
    imp = (_dot_nt(mmap_ref[...], p_hi) + _dot_nt(mmap_ref[...], p_lo))[:n_slc]
    blk = lax.broadcasted_iota(jnp.int32, (n_slc, Q_WIDE), 0)
    tq = t0 + lax.broadcasted_iota(jnp.int32, (n_slc, Q_WIDE), 1)
    cur = tq // SLC_LEN
    valid_b = blk <= cur
    forced = (blk == 0) | (blk == cur) | (blk == cur - 1)
    val = jnp.where(forced, FORCE, jnp.where(valid_b, imp, -FORCE))
    rank = jnp.zeros((n_slc, Q_WIDE), F32)
    for j in range(n_slc):
        vj = val[j:j + 1, :]
        ahead = (vj > val) | ((vj == val) & (blk > j))
        rank = rank + ahead.astype(F32)
    sel = ((rank < SLC_TOP) & valid_b).astype(F32)
    sel = jnp.concatenate([sel, jnp.zeros((LANES - n_slc, Q_WIDE), F32)], axis=0)
    sel_q = sel.T.astype(BF16)

    wspan = B_WINDOW + Q_BLOCK
    row_w = lax.broadcasted_iota(jnp.int32, (Q_BLOCK, wspan), 0)
    col_w = lax.broadcasted_iota(jnp.int32, (Q_BLOCK, wspan), 1)
    n_sub = Q_WIDE // Q_BLOCK
    o_sub = [[None] * n_sub for _ in range(B_HEADS)]
    for r in range(n_sub):
        r0 = t0 + r * Q_BLOCK
        w0 = pl.multiple_of(jnp.clip(r0 - B_WINDOW, 0, kvw_ref.shape[1] - wspan), LANES)
        kvw = kvw_ref[0, pl.ds(w0, wspan), :]
        dist_w = (row_w - col_w) + (r0 - w0)
        valid_w = (dist_w >= 0) & (dist_w < B_WINDOW)
        q_sub = jnp.concatenate([head_rows(q_all, hd)[r * Q_BLOCK:(r + 1) * Q_BLOCK] for hd in range(B_HEADS)], axis=0)
        s_all = _dot_nt(q_sub, _with_key_features(kvw, pf_ref[pl.ds(w0, wspan), :]))
        probs, inv_l = [], []
        for hd in range(B_HEADS):
            s = jnp.where(valid_w, head_rows(s_all, hd, Q_BLOCK), NEG_INF)
            m = jnp.max(s, axis=-1, keepdims=True)
            e = jnp.exp2(s - m)
            inv_l.append(1.0 / jnp.sum(e, axis=-1, keepdims=True))
            probs.append(e.astype(BF16))
        o_all = _dot(jnp.concatenate(probs, axis=0), kvw)
        for hd in range(B_HEADS):
            o_sub[hd][r] = head_rows(o_all, hd, Q_BLOCK) * inv_l[hd]
    o_win = _unstack_heads(jnp.concatenate([o_sub[hd][r] for hd in range(B_HEADS) for r in range(n_sub)], axis=0),
                           B_HEADS)

    g_hi, g_lo = _split_bf16(jax.nn.sigmoid(gate_ref[0]))
    gates = [_dot(g_hi, rep_ref[r]) + _dot(g_lo, rep_ref[r]) for r in range(3)]
    ob_rest = gates[0] * o_cmp + gates[2] * o_win

    acc_ref[...] = jnp.zeros(acc_ref.shape, F32)
    m_ref[...] = jnp.full(m_ref.shape, NEG_INF, F32)
    l_ref[...] = jnp.zeros(l_ref.shape, F32)
    def slc_chunk(c, width):
        k0 = pl.multiple_of(c * SLC_CHUNK, SLC_CHUNK)
        kv = kvs_ref[0, pl.ds(k0, width), :]
        row_k = lax.broadcasted_iota(jnp.int32, (Q_WIDE, width), 0)
        col_k = lax.broadcasted_iota(jnp.int32, (Q_WIDE, width), 1)
        allowed = (_dot(sel_q, exp_ref[c][:, :width]) > 0.5) & ((col_k - row_k) <= (t0 - k0))
        s_all = _dot_nt(q_all, _with_key_features(kv, pf_ref[pl.ds(k0, width), :]))
        probs, alphas = [], []
        for hd in range(B_HEADS):
            s = jnp.where(allowed, head_rows(s_all, hd), NEG_INF)
            m_old = m_ref[hd]
            m_new = jnp.maximum(m_old, jnp.max(s, axis=-1, keepdims=True))
            alpha = jnp.exp2(m_old - m_new)
            e = jnp.exp2(s - _lane_tile(m_new, width))
            l_ref[hd] = alpha * l_ref[hd] + jnp.sum(e, axis=-1, keepdims=True)
            m_ref[hd] = m_new
            probs.append(e.astype(BF16))
            alphas.append(alpha)
        o_all = _dot(jnp.concatenate(probs, axis=0), kv)
        for hd in range(B_HEADS):
            acc_ref[hd] = alphas[hd] * acc_ref[hd] + head_rows(o_all, hd)

    n_half = lax.shift_right_logical(t0 + Q_WIDE + SLC_CHUNK // 2 - 1, (SLC_CHUNK // 2).bit_length() - 1)
    n_full = lax.shift_right_logical(n_half, 1)

    def full_chunk(c, carry):
        slc_chunk(c, SLC_CHUNK)
        return carry

    lax.fori_loop(0, n_full, full_chunk, 0)

    @pl.when((n_half & 1) == 1)
    def _():
        slc_chunk(n_full, SLC_CHUNK // 2)

    o_slc = _unstack_heads(jnp.concatenate([acc_ref[hd] * (1.0 / l_ref[hd]) for hd in range(B_HEADS)], axis=0),
                           B_HEADS)

    ob = ob_rest + gates[1] * o_slc
    o_ref[0] = _rms(ob, g_ref[...]).astype(BF16)


def _mixer_b(qb, kvs, kvw, kvc, gate, g):
    b, s, _ = qb.shape
    n_pad = kvc.shape[1]
    n_cmp = s // CMP_STRIDE - 1
    n_slc = s // SLC_LEN
    n_chunks = s // SLC_CHUNK
    _, slopes = _alibi_slopes()
    pos_feat = _key_features(np.arange(s))
    cmp_feat = _key_features(np.minimum(np.arange(n_pad), n_cmp - 1) * CMP_STRIDE + (CMP_LEN - 1))
    cs = np.arange(n_pad)[None, :] * CMP_STRIDE
    ss = np.arange(LANES)[:, None] * SLC_LEN
    ov = np.maximum(0, np.minimum(cs + CMP_LEN, ss + SLC_LEN) - np.maximum(cs, ss)) / CMP_STRIDE
    ov = ov * (np.arange(n_pad)[None, :] < n_cmp) * (np.arange(LANES)[:, None] < n_slc)
    mmap_t = jnp.asarray(ov, BF16)
    key_blk = (np.arange(n_chunks)[:, None, None] * SLC_CHUNK + np.arange(SLC_CHUNK)[None, None, :]) // SLC_LEN
    expand = jnp.asarray(key_blk == np.arange(LANES)[None, :, None], BF16)
    rep = np.zeros((3, LANES, B_Q), np.float32)
    for r in range(3):
        for h in range(B_HEADS):
            rep[r, 3 * h + r, h * HEAD_DIM:(h + 1) * HEAD_DIM] = 1.0
    rep = jnp.asarray(rep, BF16)
    return pl.pallas_call(
        functools.partial(_mixb_kernel, n_cmp=n_cmp, n_slc=n_slc),
        out_shape=jax.ShapeDtypeStruct((b, s, B_Q), BF16),
        grid=(b, s // Q_WIDE),
        in_specs=[pl.BlockSpec((1, Q_WIDE, B_Q), lambda bi, i: (bi, i, 0)),
                  pl.BlockSpec((1, s, LANES), lambda bi, i: (bi, 0, 0)),
                  pl.BlockSpec((1, s, LANES), lambda bi, i: (bi, 0, 0)),
                  pl.BlockSpec((1, n_pad, LANES), lambda bi, i: (bi, 0, 0)),
                  pl.BlockSpec((1, Q_WIDE, LANES), lambda bi, i: (bi, i, 0)),
                  pl.BlockSpec((s, LANES), lambda bi, i: (0, 0)),
                  pl.BlockSpec((n_pad, LANES), lambda bi, i: (0, 0)),
                  pl.BlockSpec((B_HEADS * Q_WIDE, LANES), lambda bi, i: (0, 0)),
                  pl.BlockSpec((LANES, n_pad), lambda bi, i: (0, 0)),
                  pl.BlockSpec((n_chunks, LANES, SLC_CHUNK), lambda bi, i: (0, 0, 0)),
                  pl.BlockSpec((3, LANES, B_Q), lambda bi, i: (0, 0, 0)),
                  pl.BlockSpec((1, B_Q), lambda bi, i: (0, 0))],
        out_specs=pl.BlockSpec((1, Q_WIDE, B_Q), lambda bi, i: (bi, i, 0)),
        scratch_shapes=[pltpu.VMEM((B_HEADS, Q_WIDE, LANES), F32),
                        pltpu.VMEM((B_HEADS, Q_WIDE, LANES), F32),
                        pltpu.VMEM((B_HEADS, Q_WIDE, LANES), F32)],
        compiler_params=pltpu.CompilerParams(dimension_semantics=("arbitrary", "arbitrary"),
                                             vmem_limit_bytes=VMEM_LIMIT),
        name="mixer_b",
    )(qb, kvs, kvw, kvc, gate, pos_feat, cmp_feat, _query_features(slopes, Q_WIDE), mmap_t, expand, rep, g)


def _mixc_kernel(q_ref, k_ref, v_ref, tri_ref, g_ref, o_ref, acc_ref, carry_ref, z_ref, w_ref):
    i = pl.program_id(1)
    t0 = i * Q_WIDE
    n_pair = C_HEADS // 2
    q_pairs = _stack_pairs(q_ref[0], C_HEADS)
    acc_ref[...] = jnp.zeros(acc_ref.shape, F32)
    carry_ref[...] = jnp.zeros(carry_ref.shape, F32)
    row = lax.broadcasted_iota(jnp.int32, (Q_WIDE, KEY_CHUNK), 0)
    col = lax.broadcasted_iota(jnp.int32, (Q_WIDE, KEY_CHUNK), 1)
    cr = col - row
    n_chunks = (t0 + Q_WIDE + KEY_CHUNK - 1) // KEY_CHUNK

    def chunk_rows(ref, c):
        return ref[0, pl.ds(pl.multiple_of(c * KEY_CHUNK, KEY_CHUNK), KEY_CHUNK), :]

    def logits_to_scratch(c):
        k = chunk_rows(k_ref, c)
        for p in range(n_pair):
            z_ref[p] = _dot_nt(q_pairs[p], k[:, p * LANES:(p + 1) * LANES])

    def add_values(c):
        v = chunk_rows(v_ref, c)
        for p in range(n_pair):
            o = _dot(w_ref[p], v[:, p * LANES:(p + 1) * LANES])
            acc_ref[2 * p] = acc_ref[2 * p] + o[:Q_WIDE]
            acc_ref[2 * p + 1] = acc_ref[2 * p + 1] + o[Q_WIDE:]

    def sweep(c, diagonal):
        if not diagonal:
            add_values(c + 1)
        past = cr < (t0 - c * KEY_CHUNK)
        log_w, parts = [], []
        for hd in range(C_HEADS):
            z = z_ref[hd // 2, (hd % 2) * Q_WIDE:(hd % 2 + 1) * Q_WIDE, :]
            keep = -(jnp.maximum(z, 0.0) + jnp.log2(1.0 + jnp.exp2(-jnp.abs(z))))
            lk = jnp.where(past, keep, 0.0) if diagonal else keep
            parts.append(lk.astype(BF16))
            log_w.append(keep + z + _lane_tile(carry_ref[hd], KEY_CHUNK))
            carry_ref[hd] = carry_ref[hd] + jnp.sum(lk, axis=-1, keepdims=True)
        tails = _dot(jnp.concatenate(parts, axis=0), tri_ref[...])
        logits_to_scratch(jnp.maximum(c - 1, 0))
        weights = []
        for hd in range(C_HEADS):
            w = jnp.exp2(log_w[hd] + tails[hd * Q_WIDE:(hd + 1) * Q_WIDE])
            weights.append((jnp.where(past, w, 0.0) if diagonal else w).astype(BF16))
        for p in range(n_pair):
            w_ref[p] = jnp.concatenate(weights[2 * p:2 * p + 2], axis=0)

    logits_to_scratch(n_chunks - 1)
    sweep(n_chunks - 1, True)

    def earlier(j, carry):
        sweep(n_chunks - 1 - j, False)
        return carry

    lax.fori_loop(1, n_chunks, earlier, 0)
    add_values(0)
    lane = lax.broadcasted_iota(jnp.int32, (Q_WIDE, LANES), 1)
    lo = lane < HEAD_DIM
    oc = jnp.concatenate([jnp.where(lo, acc_ref[2 * p], acc_ref[2 * p + 1]) for p in range(n_pair)], axis=1)
    o_ref[0] = _rms(oc, g_ref[...]).astype(BF16)


def _mixer_c(qc, kc, vc, g):
    b, s, _ = qc.shape
    tri = jnp.asarray(np.arange(KEY_CHUNK)[:, None] > np.arange(KEY_CHUNK)[None, :], BF16)
    return pl.pallas_call(
        _mixc_kernel,
        out_shape=jax.ShapeDtypeStruct((b, s, C_W), BF16),
        grid=(b, s // Q_WIDE),
        in_specs=[pl.BlockSpec((1, Q_WIDE, C_W), lambda bi, i: (bi, i, 0)),
                  pl.BlockSpec((1, s, C_W), lambda bi, i: (bi, 0, 0)),
                  pl.BlockSpec((1, s, C_W), lambda bi, i: (bi, 0, 0)),
                  pl.BlockSpec((KEY_CHUNK, KEY_CHUNK), lambda bi, i: (0, 0)),
                  pl.BlockSpec((1, C_W), lambda bi, i: (0, 0))],
        out_specs=pl.BlockSpec((1, Q_WIDE, C_W), lambda bi, i: (bi, i, 0)),
        scratch_shapes=[pltpu.VMEM((C_HEADS, Q_WIDE, LANES), F32),
                        pltpu.VMEM((C_HEADS, Q_WIDE, LANES), F32),
                        pltpu.VMEM((C_HEADS // 2, 2 * Q_WIDE, KEY_CHUNK), F32),
                        pltpu.VMEM((C_HEADS // 2, 2 * Q_WIDE, KEY_CHUNK), BF16)],
        compiler_params=pltpu.CompilerParams(dimension_semantics=("arbitrary", "arbitrary"),
                                             vmem_limit_bytes=VMEM_LIMIT),
        name="mixer_c",
    )(qc, kc, vc, tri, g)


def _outffn_kernel(x_ref, ma_ref, mb_ref, mc_ref, wo_ref, gf_ref, wg_ref, wu_ref, wd_ref, gl_ref, o_ref, *, final):
    x = (x_ref[...] + _dot(ma_ref[...], wo_ref[:A_Q, :]) + _dot(mb_ref[...], wo_ref[A_Q:A_Q + B_Q, :])
         + _dot(mc_ref[...], wo_ref[A_Q + B_Q:, :]))
    h = _rms(x, gf_ref[...]).astype(BF16)
    o_ref[...] = x
    for c in range(D_FF // FF_CHUNK):
        sl = slice(c * FF_CHUNK, (c + 1) * FF_CHUNK)
        gate = _dot(h, wg_ref[:, sl])
        up = _dot(h, wu_ref[:, sl])
        o_ref[...] += _dot((jax.nn.silu(gate) * up).astype(BF16), wd_ref[sl, :])
    if final:
        o_ref[...] = _rms(o_ref[...], gl_ref[...])


def _outffn(x2, ma, mb, mc, wo, gf, wg, wu, wd, gl, final):
    t = x2.shape[0]
    tm = TOKEN_TILE
    const = lambda i: (0, 0)
    tok = lambda i: (i, 0)
    return pl.pallas_call(
        functools.partial(_outffn_kernel, final=final),
        out_shape=jax.ShapeDtypeStruct((t, D_MODEL), F32),
        grid=(t // tm,),
        in_specs=[pl.BlockSpec((tm, D_MODEL), tok),
                  pl.BlockSpec((tm, A_Q), tok),
                  pl.BlockSpec((tm, B_Q), tok),
                  pl.BlockSpec((tm, C_W), tok),
                  pl.BlockSpec((MIX_WIDTH, D_MODEL), const),
                  pl.BlockSpec((1, D_MODEL), const),
                  pl.BlockSpec((D_MODEL, D_FF), const),
                  pl.BlockSpec((D_MODEL, D_FF), const),
                  pl.BlockSpec((D_FF, D_MODEL), const),
                  pl.BlockSpec((1, D_MODEL), const)],
        out_specs=pl.BlockSpec((tm, D_MODEL), tok),
        compiler_params=pltpu.CompilerParams(dimension_semantics=("arbitrary",), vmem_limit_bytes=VMEM_LIMIT),
        name="outproj_ffn",
    )(x2, ma, mb, mc, wo, gf, wg, wu, wd, gl)


def _regroup_w_in(w_in):
    sizes = (A_Q, HEAD_DIM, HEAD_DIM, B_Q, HEAD_DIM, HEAD_DIM, HEAD_DIM, HEAD_DIM, HEAD_DIM, HEAD_DIM,
             N_GATES, C_W, C_W, C_W)
    offs = np.concatenate([[0], np.cumsum(sizes)])
    qa, ka, va, qb, kcb, vcb, ksb, vsb, kwb, vwb, gb, qc, kc, vc = [w_in[:, offs[j]:offs[j + 1]]
                                                                     for j in range(len(sizes))]
    pad = jnp.zeros((w_in.shape[0], LANES - N_GATES), w_in.dtype)
    cols = [qa * SCALE, ka, va, qb * SCALE, ksb, vsb, kwb, vwb, kcb, vcb, gb, pad, qc * SCALE, kc, vc]
    return jnp.concatenate(cols, axis=1).astype(BF16)


def _layer(x2, b, s, w_in, w_out, g_attn, g_ffn, g_out_a, g_out_b, g_out_c, sinks, cmp_pos, cmp_w1, cmp_w2,
           w_gate, w_up, w_down, g_final, final):
    row = lambda v: v.reshape(1, -1)
    qa, kva, qb, kvs, kvw, kcr, vcr, gate, qc, kc, vc = _inproj(x2, row(g_attn), _regroup_w_in(w_in))
    r3 = lambda a: a.reshape(b, s, a.shape[-1])

    n_chunk = s // CMP_STRIDE
    half = CMP_LEN // 2
    flat = half * HEAD_DIM
    xk = kcr.reshape(b, n_chunk, flat)
    xv = vcr.reshape(b, n_chunk, flat)
    pos = cmp_pos.reshape(2, 2, 1, flat)
    w1 = cmp_w1.reshape(2, 2, flat, CMP_HIDDEN).astype(BF16)
    zeros = jnp.zeros((CMP_HIDDEN, HEAD_DIM), cmp_w2.dtype)
    w2 = jnp.stack([jnp.concatenate([cmp_w2[0], zeros], axis=1),
                    jnp.concatenate([zeros, cmp_w2[1]], axis=1)]).astype(BF16)
    kvc = _compress(xk, xv, pos, w1, w2)

    ma = _mixer_a(sinks, r3(qa), r3(kva), row(g_out_a))
    mb = _mixer_b(r3(qb), r3(kvs), r3(kvw), kvc, r3(gate), row(g_out_b))
    mc = _mixer_c(r3(qc), r3(kc), r3(vc), row(g_out_c))
    t = b * s
    return _outffn(x2, ma.reshape(t, A_Q), mb.reshape(t, B_Q), mc.reshape(t, C_W), w_out.astype(BF16),
                   row(g_ffn), w_gate.astype(BF16), w_up.astype(BF16), w_down.astype(BF16), row(g_final), final)


def kernel(x, w_in, w_out, g_attn, g_ffn, g_out_a, g_out_b, g_out_c, sinks, cmp_pos, cmp_w1, cmp_w2, w_gate, w_up,
           w_down, g_final):
    b, s, d = x.shape
    depth = w_in.shape[0]
    assert d == D_MODEL and s % KEY_CHUNK == 0 and (b * s) % TOKEN_TILE == 0 and (b * s) % INPROJ_TILE == 0
    assert s >= B_WINDOW + Q_WIDE and s % SLC_CHUNK == 0 and (s // CMP_STRIDE) % LANES == 0
    x2 = x.reshape(b * s, d)
    for l in range(depth):
        x2 = _layer(x2, b, s, w_in[l], w_out[l], g_attn[l], g_ffn[l], g_out_a[l], g_out_b[l], g_out_c[l], sinks[l],
                    cmp_pos[l], cmp_w1[l], cmp_w2[l], w_gate[l], w_up[l], w_down[l], g_final, l == depth - 1)
    return x2.reshape(b, s, d)
```

```python
import functools

import jax
import jax.numpy as jnp
import numpy as np
from jax import lax
from jax.experimental import pallas as pl
from jax.experimental.pallas import tpu as pltpu

F32 = jnp.float32
BF16 = jnp.bfloat16

D_MODEL = 1024
HEAD_DIM = 64
LANES = 128
Q_BLOCK = 128
Q_WIDE = 256
A_HEADS = 8
A_WINDOW = 128
B_HEADS = 4
CMP_LEN = 32
CMP_STRIDE = 16
CMP_HIDDEN = 128
SLC_LEN = 64
SLC_TOP = 8
B_WINDOW = 512
C_HEADS = 4
A_Q = A_HEADS * HEAD_DIM
B_Q = B_HEADS * HEAD_DIM
C_W = C_HEADS * HEAD_DIM
MIX_WIDTH = A_Q + B_Q + C_W
D_FF = 2816
N_GATES = B_HEADS * 3
NEG_INF = -1e30
FORCE = 1e4
EPS = 1e-6
SCALE = HEAD_DIM ** -0.5
LOG2E = 1.4426950408889634

KEY_CHUNK = 256
SLC_CHUNK = 512
FF_CHUNK = 256
TOKEN_TILE = 512
INPROJ_TILE = 1024
VMEM_LIMIT = 56 * 1024 * 1024

_G_QA, _G_KVA, _G_QB, _G_KVS, _G_KVW, _G_KVC, _G_GATE, _G_QC, _G_KC, _G_VC, IN_PAD = (
    0, 512, 640, 896, 1024, 1152, 1280, 1408, 1664, 1920, 2176)


def _alibi_slopes():
    n = A_HEADS + B_HEADS
    sl = 2.0 ** (-8.0 * np.arange(1, n + 1) / n) * LOG2E
    return [float(v) for v in sl[:A_HEADS]], [float(v) for v in sl[A_HEADS:]]


def _round_bf16(x):
    bits = np.ascontiguousarray(x, np.float32).view(np.uint32).astype(np.uint64)
    bits = ((bits + 0x7FFF + ((bits >> 16) & 1)) >> 16) << 16
    return bits.astype(np.uint32).view(np.float32)


def _key_features(pos):
    pos = np.asarray(pos, np.int64)
    assert pos.max() < 2048
    f = np.zeros((len(pos), LANES), np.float32)
    for j in range(3):
        f[:, HEAD_DIM + 2 * j] = 256 * (pos // 256)
        f[:, HEAD_DIM + 2 * j + 1] = pos % 256
    return jnp.asarray(f, BF16)


def _query_features(slopes, rows):
    f = np.zeros((len(slopes) * rows, LANES), np.float32)
    for h, slope in enumerate(slopes):
        rest = np.float32(slope)
        for j in range(3):
            part = _round_bf16(rest)
            rest = np.float32(rest - part)
            f[h * rows:(h + 1) * rows, HEAD_DIM + 2 * j:HEAD_DIM + 2 * j + 2] = part
    return jnp.asarray(f, BF16)


def _dot(a, b):
    return jnp.dot(a, b, preferred_element_type=F32)


def _dot_nt(a, b):
    return lax.dot_general(a, b, (((1,), (1,)), ((), ())), preferred_element_type=F32)


def _split_bf16(x):
    hi = x.astype(BF16)
    lo = (x - hi.astype(F32)).astype(BF16)
    return hi, lo


def _rms(x, g):
    ms = jnp.mean(x * x, axis=-1, keepdims=True)
    return x * lax.rsqrt(ms + EPS) * g


def _lane_tile(x, width):
    return jnp.concatenate([x] * (width // LANES), axis=1)


def _swap_halves(x):
    return jnp.concatenate([x[:, HEAD_DIM:], x[:, :HEAD_DIM]], axis=1)


def _stack_heads(q, n_heads, feat):
    rows = q.shape[0]
    lo = lax.broadcasted_iota(jnp.int32, (rows, LANES), 1) < HEAD_DIM
    tiles = []
    for p in range(n_heads // 2):
        pair = q[:, p * LANES:(p + 1) * LANES]
        tiles.append(jnp.where(lo, pair, feat[(2 * p) * rows:(2 * p + 1) * rows]))
        tiles.append(jnp.where(lo, _swap_halves(pair), feat[(2 * p + 1) * rows:(2 * p + 2) * rows]))
    return jnp.concatenate(tiles, axis=0)


def _with_key_features(kv, feat):
    lo = lax.broadcasted_iota(jnp.int32, kv.shape, 1) < HEAD_DIM
    return jnp.where(lo, kv, feat)


def _stack_pairs(q, n_heads):
    rows = q.shape[0]
    lo = lax.broadcasted_iota(jnp.int32, (rows, LANES), 1) < HEAD_DIM
    out = []
    for p in range(n_heads // 2):
        pair = q[:, p * LANES:(p + 1) * LANES]
        zero = jnp.zeros_like(pair)
        out.append(jnp.concatenate([jnp.where(lo, pair, zero), jnp.where(lo, zero, pair)], axis=0))
    return out


def _with_ones_for_keys(kv):
    lo = lax.broadcasted_iota(jnp.int32, kv.shape, 1) < HEAD_DIM
    return jnp.where(lo, jnp.ones_like(kv), kv)


def _unstack_normalized(o, n_heads):
    rows = o.shape[0] // n_heads
    lo = lax.broadcasted_iota(jnp.int32, (rows, LANES), 1) < HEAD_DIM
    pairs = []
    for p in range(n_heads // 2):
        even = o[(2 * p) * rows:(2 * p + 1) * rows]
        odd = o[(2 * p + 1) * rows:(2 * p + 2) * rows]
        pairs.append(jnp.where(lo, pltpu.roll(even, HEAD_DIM, 1) / even, odd / pltpu.roll(odd, HEAD_DIM, 1)))
    return jnp.concatenate(pairs, axis=1)


def _unstack_heads(o, n_heads):
    rows = o.shape[0] // n_heads
    lo = lax.broadcasted_iota(jnp.int32, (rows, LANES), 1) < HEAD_DIM
    pairs = []
    for p in range(n_heads // 2):
        even = o[(2 * p) * rows:(2 * p + 1) * rows]
        odd = o[(2 * p + 1) * rows:(2 * p + 2) * rows]
        pairs.append(jnp.where(lo, pltpu.roll(even, HEAD_DIM, 1), odd))
    return jnp.concatenate(pairs, axis=1)


def _inproj_kernel(x_ref, g_ref, w_ref, qa_ref, kva_ref, qb_ref, kvs_ref, kvw_ref, kcr_ref, vcr_ref,
                   gate_ref, qc_ref, kc_ref, vc_ref):
    h = _rms(x_ref[...], g_ref[...]).astype(BF16)

    def proj(lo, hi):
        return _dot(h, w_ref[:, lo:hi])

    qa_ref[...] = (proj(_G_QA, _G_KVA) * LOG2E).astype(BF16)
    kva_ref[...] = proj(_G_KVA, _G_QB).astype(BF16)
    qb_ref[...] = (proj(_G_QB, _G_KVS) * LOG2E).astype(BF16)
    kvs_ref[...] = proj(_G_KVS, _G_KVW).astype(BF16)
    kvw_ref[...] = proj(_G_KVW, _G_KVC).astype(BF16)
    kvc = proj(_G_KVC, _G_GATE)
    kcr_ref[...] = kvc[:, :HEAD_DIM]
    vcr_ref[...] = kvc[:, HEAD_DIM:]
    gate_ref[...] = proj(_G_GATE, _G_QC)
    qc_ref[...] = (proj(_G_QC, _G_KC) * LOG2E).astype(BF16)
    kc_ref[...] = proj(_G_KC, _G_VC).astype(BF16)
    vc_ref[...] = proj(_G_VC, IN_PAD).astype(BF16)


def _inproj(x2, g, w):
    t = x2.shape[0]
    tm = INPROJ_TILE
    widths = [(A_Q, BF16), (LANES, BF16), (B_Q, BF16), (LANES, BF16), (LANES, BF16), (HEAD_DIM, F32),
              (HEAD_DIM, F32), (LANES, F32), (C_W, BF16), (C_W, BF16), (C_W, BF16)]
    return pl.pallas_call(
        _inproj_kernel,
        out_shape=[jax.ShapeDtypeStruct((t, wd), dt) for wd, dt in widths],
        grid=(t // tm,),
        in_specs=[pl.BlockSpec((tm, D_MODEL), lambda i: (i, 0)),
                  pl.BlockSpec((1, D_MODEL), lambda i: (0, 0)),
                  pl.BlockSpec((D_MODEL, IN_PAD), lambda i: (0, 0))],
        out_specs=[pl.BlockSpec((tm, wd), lambda i: (i, 0)) for wd, _ in widths],
        compiler_params=pltpu.CompilerParams(dimension_semantics=("arbitrary",), vmem_limit_bytes=VMEM_LIMIT),
        name="inproj",
    )(x2, g, w)


def _compress_kernel(xk_ref, xv_ref, pos_ref, w1_ref, w2_ref, o_ref):
    n = xk_ref.shape[1]
    out = jnp.zeros((n, LANES), F32)
    for t, x_ref in enumerate((xk_ref, xv_ref)):
        x = x_ref[0]
        h_lo = _dot((x + pos_ref[t, 0]).astype(BF16), w1_ref[t, 0])
        h_hi = _dot((x + pos_ref[t, 1]).astype(BF16), w1_ref[t, 1])
        h = h_lo + pltpu.roll(h_hi, n - 1, 0)
        out = out + _dot(jax.nn.gelu(h).astype(BF16), w2_ref[t])
    o_ref[0] = out.astype(BF16)


def _compress(xk, xv, pos, w1, w2):
    b, n, wdt = xk.shape
    return pl.pallas_call(
        _compress_kernel,
        out_shape=jax.ShapeDtypeStruct((b, n, LANES), BF16),
        grid=(b,),
        in_specs=[pl.BlockSpec((1, n, wdt), lambda i: (i, 0, 0)),
                  pl.BlockSpec((1, n, wdt), lambda i: (i, 0, 0)),
                  pl.BlockSpec((2, 2, 1, wdt), lambda i: (0, 0, 0, 0)),
                  pl.BlockSpec((2, 2, wdt, CMP_HIDDEN), lambda i: (0, 0, 0, 0)),
                  pl.BlockSpec((2, CMP_HIDDEN, LANES), lambda i: (0, 0, 0))],
        out_specs=pl.BlockSpec((1, n, LANES), lambda i: (i, 0, 0)),
        compiler_params=pltpu.CompilerParams(dimension_semantics=("arbitrary",), vmem_limit_bytes=VMEM_LIMIT),
        name="compress",
    )(xk, xv, pos, w1, w2)


def _mixa_kernel(sink_ref, q_ref, kvp_ref, kvc_ref, pfp_ref, pfc_ref, qf_ref, g_ref, o_ref, *, slopes):
    i = pl.program_id(1)
    span = 2 * Q_BLOCK
    row = lax.broadcasted_iota(jnp.int32, (Q_BLOCK, span), 0)
    col = lax.broadcasted_iota(jnp.int32, (Q_BLOCK, span), 1)
    dist = row + Q_BLOCK - col
    band = (dist >= 0) & (dist < A_WINDOW)
    kv_rows = jnp.concatenate([kvp_ref[0], kvc_ref[0]], axis=0)
    pf_rows = jnp.concatenate([pfp_ref[...], pfc_ref[...]], axis=0)

    for r in range(Q_WIDE // Q_BLOCK):
        q = q_ref[0, r * Q_BLOCK:(r + 1) * Q_BLOCK, :]
        kv = kv_rows[r * Q_BLOCK:r * Q_BLOCK + span]
        valid = band & ((col >= Q_BLOCK) | (i > 0)) if r == 0 else band
        t_row = (i * Q_WIDE + r * Q_BLOCK + lax.broadcasted_iota(jnp.int32, (Q_BLOCK, 1), 0)).astype(F32)
        kx = _with_key_features(kv, pf_rows[r * Q_BLOCK:r * Q_BLOCK + span])
        s_all = _dot_nt(_stack_heads(q, A_HEADS, qf_ref[...]), kx)
        probs, inv_l = [], []
        for hd in range(A_HEADS):
            sink = sink_ref[hd] * LOG2E + slopes[hd] * t_row
            s = jnp.where(valid, s_all[hd * Q_BLOCK:(hd + 1) * Q_BLOCK], NEG_INF)
            m = jnp.maximum(jnp.max(s, axis=-1, keepdims=True), sink)
            e = jnp.exp2(s - m)
            inv_l.append(1.0 / (jnp.sum(e, axis=-1, keepdims=True) + jnp.exp2(sink - m)))
            probs.append(e.astype(BF16))
        o_all = _dot(jnp.concatenate(probs, axis=0), kv)
        o_all = jnp.concatenate([o_all[hd * Q_BLOCK:(hd + 1) * Q_BLOCK] * inv_l[hd] for hd in range(A_HEADS)], axis=0)
        o_ref[0, r * Q_BLOCK:(r + 1) * Q_BLOCK, :] = _rms(_unstack_heads(o_all, A_HEADS), g_ref[...]).astype(BF16)


def _mixer_a(sinks, qa, kva, g):
    b, s, _ = qa.shape
    slopes, _ = _alibi_slopes()
    pos_feat = _key_features(np.arange(s))
    sub = Q_WIDE // Q_BLOCK
    prev_block = lambda i: jnp.maximum(sub * i - 1, 0)
    return pl.pallas_call(
        functools.partial(_mixa_kernel, slopes=slopes),
        out_shape=jax.ShapeDtypeStruct((b, s, A_Q), BF16),
        grid=(b, s // Q_WIDE),
        in_specs=[pl.BlockSpec(memory_space=pltpu.SMEM),
                  pl.BlockSpec((1, Q_WIDE, A_Q), lambda bi, i: (bi, i, 0)),
                  pl.BlockSpec((1, Q_BLOCK, LANES), lambda bi, i: (bi, prev_block(i), 0)),
                  pl.BlockSpec((1, Q_WIDE, LANES), lambda bi, i: (bi, i, 0)),
                  pl.BlockSpec((Q_BLOCK, LANES), lambda bi, i: (prev_block(i), 0)),
                  pl.BlockSpec((Q_WIDE, LANES), lambda bi, i: (i, 0)),
                  pl.BlockSpec((A_HEADS * Q_BLOCK, LANES), lambda bi, i: (0, 0)),
                  pl.BlockSpec((1, A_Q), lambda bi, i: (0, 0))],
        out_specs=pl.BlockSpec((1, Q_WIDE, A_Q), lambda bi, i: (bi, i, 0)),
        compiler_params=pltpu.CompilerParams(dimension_semantics=("arbitrary", "arbitrary"),
                                             vmem_limit_bytes=VMEM_LIMIT),
        name="mixer_a",
    )(sinks, qa, kva, kva, pos_feat, pos_feat, _query_features(slopes, Q_BLOCK), g)


def _mixb_kernel(q_ref, kvs_ref, kvw_ref, kvc_ref, gate_ref, pf_ref, cf_ref, qf_ref, mmap_ref, exp_ref, rep_ref,
                 g_ref, o_ref, acc_ref, m_ref, *, n_cmp, n_slc):
    i = pl.program_id(1)
    t0 = i * Q_WIDE
    q_all = _stack_heads(q_ref[0], B_HEADS, qf_ref[...])

    def head_rows(x, hd, rows=Q_WIDE):
        return x[hd * rows:(hd + 1) * rows]

    kvc = kvc_ref[0]
    n_pad = kvc.shape[0]
    s_cmp = _dot_nt(q_all, _with_key_features(kvc, cf_ref[...]))
    wspan = B_WINDOW + Q_BLOCK
    row_w = lax.broadcasted_iota(jnp.int32, (Q_BLOCK, wspan), 0)
    col_w = lax.broadcasted_iota(jnp.int32, (Q_BLOCK, wspan), 1)
    n_sub = Q_WIDE // Q_BLOCK
    kvw, valid_w, s_win = [], [], []
    for r in range(n_sub):
        r0 = t0 + r * Q_BLOCK
        w0 = pl.multiple_of(jnp.clip(r0 - B_WINDOW, 0, kvw_ref.shape[1] - wspan), LANES)
        kvw.append(kvw_ref[0, pl.ds(w0, wspan), :])
        dist_w = (row_w - col_w) + (r0 - w0)
        valid_w.append((dist_w >= 0) & (dist_w < B_WINDOW))
        q_sub = jnp.concatenate([head_rows(q_all, hd)[r * Q_BLOCK:(r + 1) * Q_BLOCK] for hd in range(B_HEADS)], axis=0)
        s_win.append(_dot_nt(q_sub, _with_key_features(kvw[r], pf_ref[pl.ds(w0, wspan), :])))

    g_hi, g_lo = _split_bf16(jax.nn.sigmoid(gate_ref[0]))
    gates = [_dot(g_hi, rep_ref[r]) + _dot(g_lo, rep_ref[r]) for r in range(3)]

    row_c = lax.broadcasted_iota(jnp.int32, (Q_WIDE, n_pad), 0)
    col_c = lax.broadcasted_iota(jnp.int32, (Q_WIDE, n_pad), 1)
    vis_c = ((t0 + row_c) >= (col_c * CMP_STRIDE + (CMP_LEN - 1))) & (col_c < n_cmp)
    row1 = lax.broadcasted_iota(jnp.int32, (Q_WIDE, 1), 0)
    any_vis = ((t0 + row1) >= (CMP_LEN - 1)).astype(F32)
    probs = []
    p_sum = jnp.zeros((Q_WIDE, n_pad), F32)
    for hd in range(B_HEADS):
        s = jnp.where(vis_c, head_rows(s_cmp, hd), NEG_INF)
        m = jnp.max(s, axis=-1, keepdims=True)
        e = jnp.exp2(s - m)
        pr = e * (any_vis / jnp.sum(e, axis=-1, keepdims=True))
        p_sum = p_sum + pr
        probs.append(pr.astype(BF16))
    o_cmp = _unstack_heads(_dot(jnp.concatenate(probs, axis=0), kvc), B_HEADS)

    p_hi, p_lo = _split_bf16(p_sum)
    imp = (_dot_nt(mmap_ref[...], p_hi) + _dot_nt(mmap_ref[...], p_lo))[:n_slc]
    blk = lax.broadcasted_iota(jnp.int32, (n_slc, Q_WIDE), 0)
    tq = t0 + lax.broadcasted_iota(jnp.int32, (n_slc, Q_WIDE), 1)
    cur = tq // SLC_LEN
    valid_b = blk <= cur
    forced = (blk == 0) | (blk == cur) | (blk == cur - 1)
    val = jnp.where(forced, FORCE, jnp.where(valid_b, imp, -FORCE))
    rank = jnp.zeros((n_slc, Q_WIDE), F32)
    for j in range(n_slc):
        vj = val[j:j + 1, :]
        ahead = (vj > val) | ((vj == val) & (blk > j))
        rank = rank + ahead.astype(F32)
    sel = ((rank < SLC_TOP) & valid_b).astype(F32)
    sel = jnp.concatenate([sel, jnp.zeros((LANES - n_slc, Q_WIDE), F32)], axis=0)
    sel_q = sel.T.astype(BF16)

    o_sub = [[None] * n_sub for _ in range(B_HEADS)]
    for r in range(n_sub):
        probs = []
        for hd in range(B_HEADS):
            s = jnp.where(valid_w[r], head_rows(s_win[r], hd, Q_BLOCK), NEG_INF)
            probs.append(jnp.exp2(s - jnp.max(s, axis=-1, keepdims=True)).astype(BF16))
        o_all = _dot(jnp.concatenate(probs, axis=0), _with_ones_for_keys(kvw[r]))
        for hd in range(B_HEADS):
            o_sub[hd][r] = head_rows(o_all, hd, Q_BLOCK)
    o_win = _unstack_normalized(jnp.concatenate([o_sub[hd][r] for hd in range(B_HEADS) for r in range(n_sub)], axis=0),
                                B_HEADS)
    ob_rest = gates[0] * o_cmp + gates[2] * o_win

    acc_ref[...] = jnp.zeros(acc_ref.shape, F32)
    m_ref[...] = jnp.full(m_ref.shape, NEG_INF, F32)

    def slc_chunk(c, width):
        k0 = pl.multiple_of(c * SLC_CHUNK, SLC_CHUNK)
        kv = kvs_ref[0, pl.ds(k0, width), :]
        row_k = lax.broadcasted_iota(jnp.int32, (Q_WIDE, width), 0)
        col_k = lax.broadcasted_iota(jnp.int32, (Q_WIDE, width), 1)
        allowed = (_dot(sel_q, exp_ref[c][:, :width]) > 0.5) & ((col_k - row_k) <= (t0 - k0))
        s_all = _dot_nt(q_all, _with_key_features(kv, pf_ref[pl.ds(k0, width), :]))
        probs, alphas = [], []
        for hd in range(B_HEADS):
            s = jnp.where(allowed, head_rows(s_all, hd), NEG_INF)
            m_old = m_ref[hd]
            m_new = jnp.maximum(m_old, jnp.max(s, axis=-1, keepdims=True))
            alpha = jnp.exp2(m_old - m_new)
            m_ref[hd] = m_new
            probs.append(jnp.exp2(s - _lane_tile(m_new, width)).astype(BF16))
            alphas.append(alpha)
        o_all = _dot(jnp.concatenate(probs, axis=0), _with_ones_for_keys(kv))
        for hd in range(B_HEADS):
            acc_ref[hd] = alphas[hd] * acc_ref[hd] + head_rows(o_all, hd)

    n_half = lax.shift_right_logical(t0 + Q_WIDE + SLC_CHUNK // 2 - 1, (SLC_CHUNK // 2).bit_length() - 1)
    n_full = lax.shift_right_logical(n_half, 1)

    def full_chunk(c, carry):
        slc_chunk(c, SLC_CHUNK)
        return carry

    lax.fori_loop(0, n_full, full_chunk, 0)

    @pl.when((n_half & 1) == 1)
    def _():
        slc_chunk(n_full, SLC_CHUNK // 2)

    o_slc = _unstack_normalized(jnp.concatenate([acc_ref[hd] for hd in range(B_HEADS)], axis=0), B_HEADS)

    ob = ob_rest + gates[1] * o_slc
    o_ref[0] = _rms(ob, g_ref[...]).astype(BF16)


def _mixer_b(qb, kvs, kvw, kvc, gate, g):
    b, s, _ = qb.shape
    n_pad = kvc.shape[1]
    n_cmp = s // CMP_STRIDE - 1
    n_slc = s // SLC_LEN
    n_chunks = s // SLC_CHUNK
    _, slopes = _alibi_slopes()
    pos_feat = _key_features(np.arange(s))
    cmp_feat = _key_features(np.minimum(np.arange(n_pad), n_cmp - 1) * CMP_STRIDE + (CMP_LEN - 1))
    cs = np.arange(n_pad)[None, :] * CMP_STRIDE
    ss = np.arange(LANES)[:, None] * SLC_LEN
    ov = np.maximum(0, np.minimum(cs + CMP_LEN, ss + SLC_LEN) - np.maximum(cs, ss)) / CMP_STRIDE
    ov = ov * (np.arange(n_pad)[None, :] < n_cmp) * (np.arange(LANES)[:, None] < n_slc)
    mmap_t = jnp.asarray(ov, BF16)
    key_blk = (np.arange(n_chunks)[:, None, None] * SLC_CHUNK + np.arange(SLC_CHUNK)[None, None, :]) // SLC_LEN
    expand = jnp.asarray(key_blk == np.arange(LANES)[None, :, None], BF16)
    rep = np.zeros((3, LANES, B_Q), np.float32)
    for r in range(3):
        for h in range(B_HEADS):
            rep[r, 3 * h + r, h * HEAD_DIM:(h + 1) * HEAD_DIM] = 1.0
    rep = jnp.asarray(rep, BF16)
    return pl.pallas_call(
        functools.partial(_mixb_kernel, n_cmp=n_cmp, n_slc=n_slc),
        out_shape=jax.ShapeDtypeStruct((b, s, B_Q), BF16),
        grid=(b, s // Q_WIDE),
        in_specs=[pl.BlockSpec((1, Q_WIDE, B_Q), lambda bi, i: (bi, i, 0)),
                  pl.BlockSpec((1, s, LANES), lambda bi, i: (bi, 0, 0)),
                  pl.BlockSpec((1, s, LANES), lambda bi, i: (bi, 0, 0)),
                  pl.BlockSpec((1, n_pad, LANES), lambda bi, i: (bi, 0, 0)),
                  pl.BlockSpec((1, Q_WIDE, LANES), lambda bi, i: (bi, i, 0)),
                  pl.BlockSpec((s, LANES), lambda bi, i: (0, 0)),
                  pl.BlockSpec((n_pad, LANES), lambda bi, i: (0, 0)),
                  pl.BlockSpec((B_HEADS * Q_WIDE, LANES), lambda bi, i: (0, 0)),
                  pl.BlockSpec((LANES, n_pad), lambda bi, i: (0, 0)),
                  pl.BlockSpec((n_chunks, LANES, SLC_CHUNK), lambda bi, i: (0, 0, 0)),
                  pl.BlockSpec((3, LANES, B_Q), lambda bi, i: (0, 0, 0)),
                  pl.BlockSpec((1, B_Q), lambda bi, i: (0, 0))],
        out_specs=pl.BlockSpec((1, Q_WIDE, B_Q), lambda bi, i: (bi, i, 0)),
        scratch_shapes=[pltpu.VMEM((B_HEADS, Q_WIDE, LANES), F32),
                        pltpu.VMEM((B_HEADS, Q_WIDE, LANES), F32)],
        compiler_params=pltpu.CompilerParams(dimension_semantics=("arbitrary", "arbitrary"),
                                             vmem_limit_bytes=VMEM_LIMIT),
        name="mixer_b",
    )(qb, kvs, kvw, kvc, gate, pos_feat, cmp_feat, _query_features(slopes, Q_WIDE), mmap_t, expand, rep, g)


def _mixc_kernel(q_ref, k_ref, v_ref, tri_ref, g_ref, o_ref, acc_ref, carry_ref, z_ref, w_ref):
    i = pl.program_id(1)
    t0 = i * Q_WIDE
    n_pair = C_HEADS // 2
    q_pairs = _stack_pairs(q_ref[0], C_HEADS)
    acc_ref[...] = jnp.zeros(acc_ref.shape, F32)
    carry_ref[...] = jnp.zeros(carry_ref.shape, F32)
    row = lax.broadcasted_iota(jnp.int32, (Q_WIDE, KEY_CHUNK), 0)
    col = lax.broadcasted_iota(jnp.int32, (Q_WIDE, KEY_CHUNK), 1)
    cr = col - row
    n_chunks = (t0 + Q_WIDE + KEY_CHUNK - 1) // KEY_CHUNK

    def chunk_rows(ref, c):
        return ref[0, pl.ds(pl.multiple_of(c * KEY_CHUNK, KEY_CHUNK), KEY_CHUNK), :]

    def logits_to_scratch(c):
        k = chunk_rows(k_ref, c)
        for p in range(n_pair):
            z_ref[p] = _dot_nt(q_pairs[p], k[:, p * LANES:(p + 1) * LANES])

    def add_values(c):
        v = chunk_rows(v_ref, c)
        for p in range(n_pair):
            o = _dot(w_ref[p], v[:, p * LANES:(p + 1) * LANES])
            acc_ref[2 * p] = acc_ref[2 * p] + o[:Q_WIDE]
            acc_ref[2 * p + 1] = acc_ref[2 * p + 1] + o[Q_WIDE:]

    def sweep(c, diagonal):
        if not diagonal:
            add_values(c + 1)
        past = cr < (t0 - c * KEY_CHUNK)
        log_w, parts = [], []
        for hd in range(C_HEADS):
            z = z_ref[hd // 2, (hd % 2) * Q_WIDE:(hd % 2 + 1) * Q_WIDE, :]
            keep = -(jnp.maximum(z, 0.0) + jnp.log2(1.0 + jnp.exp2(-jnp.abs(z))))
            lk = jnp.where(past, keep, 0.0) if diagonal else keep
            parts.append(lk.astype(BF16))
            log_w.append(keep + z + _lane_tile(carry_ref[hd], KEY_CHUNK))
            carry_ref[hd] = carry_ref[hd] + jnp.sum(lk, axis=-1, keepdims=True)
        tails = _dot(jnp.concatenate(parts, axis=0), tri_ref[...])
        logits_to_scratch(jnp.maximum(c - 1, 0))
        weights = []
        for hd in range(C_HEADS):
            w = jnp.exp2(log_w[hd] + tails[hd * Q_WIDE:(hd + 1) * Q_WIDE])
            weights.append((jnp.where(past, w, 0.0) if diagonal else w).astype(BF16))
        for p in range(n_pair):
            w_ref[p] = jnp.concatenate(weights[2 * p:2 * p + 2], axis=0)

    logits_to_scratch(n_chunks - 1)
    sweep(n_chunks - 1, True)

    def earlier(j, carry):
        sweep(n_chunks - 1 - j, False)
        return carry

    lax.fori_loop(1, n_chunks, earlier, 0)
    add_values(0)
    lane = lax.broadcasted_iota(jnp.int32, (Q_WIDE, LANES), 1)
    lo = lane < HEAD_DIM
    oc = jnp.concatenate([jnp.where(lo, acc_ref[2 * p], acc_ref[2 * p + 1]) for p in range(n_pair)], axis=1)
    o_ref[0] = _rms(oc, g_ref[...]).astype(BF16)


def _mixer_c(qc, kc, vc, g):
    b, s, _ = qc.shape
    tri = jnp.asarray(np.arange(KEY_CHUNK)[:, None] > np.arange(KEY_CHUNK)[None, :], BF16)
    return pl.pallas_call(
        _mixc_kernel,
        out_shape=jax.ShapeDtypeStruct((b, s, C_W), BF16),
        grid=(b, s // Q_WIDE),
        in_specs=[pl.BlockSpec((1, Q_WIDE, C_W), lambda bi, i: (bi, i, 0)),
                  pl.BlockSpec((1, s, C_W), lambda bi, i: (bi, 0, 0)),
                  pl.BlockSpec((1, s, C_W), lambda bi, i: (bi, 0, 0)),
                  pl.BlockSpec((KEY_CHUNK, KEY_CHUNK), lambda bi, i: (0, 0)),
                  pl.BlockSpec((1, C_W), lambda bi, i: (0, 0))],
        out_specs=pl.BlockSpec((1, Q_WIDE, C_W), lambda bi, i: (bi, i, 0)),
        scratch_shapes=[pltpu.VMEM((C_HEADS, Q_WIDE, LANES), F32),
                        pltpu.VMEM((C_HEADS, Q_WIDE, LANES), F32),
                        pltpu.VMEM((C_HEADS // 2, 2 * Q_WIDE, KEY_CHUNK), F32),
                        pltpu.VMEM((C_HEADS // 2, 2 * Q_WIDE, KEY_CHUNK), BF16)],
        compiler_params=pltpu.CompilerParams(dimension_semantics=("arbitrary", "arbitrary"),
                                             vmem_limit_bytes=VMEM_LIMIT),
        name="mixer_c",
    )(qc, kc, vc, tri, g)


def _outffn_kernel(x_ref, ma_ref, mb_ref, mc_ref, wo_ref, gf_ref, wg_ref, wu_ref, wd_ref, gl_ref, o_ref, *, final):
    x = (x_ref[...] + _dot(ma_ref[...], wo_ref[:A_Q, :]) + _dot(mb_ref[...], wo_ref[A_Q:A_Q + B_Q, :])
         + _dot(mc_ref[...], wo_ref[A_Q + B_Q:, :]))
    h = _rms(x, gf_ref[...]).astype(BF16)
    o_ref[...] = x
    for c in range(D_FF // FF_CHUNK):
        sl = slice(c * FF_CHUNK, (c + 1) * FF_CHUNK)
        gate = _dot(h, wg_ref[:, sl])
        up = _dot(h, wu_ref[:, sl])
        o_ref[...] += _dot((jax.nn.silu(gate) * up).astype(BF16), wd_ref[sl, :])
    if final:
        o_ref[...] = _rms(o_ref[...], gl_ref[...])


def _outffn(x2, ma, mb, mc, wo, gf, wg, wu, wd, gl, final):
    t = x2.shape[0]
    tm = TOKEN_TILE
    const = lambda i: (0, 0)
    tok = lambda i: (i, 0)
    return pl.pallas_call(
        functools.partial(_outffn_kernel, final=final),
        out_shape=jax.ShapeDtypeStruct((t, D_MODEL), F32),
        grid=(t // tm,),
        in_specs=[pl.BlockSpec((tm, D_MODEL), tok),
                  pl.BlockSpec((tm, A_Q), tok),
                  pl.BlockSpec((tm, B_Q), tok),
                  pl.BlockSpec((tm, C_W), tok),
                  pl.BlockSpec((MIX_WIDTH, D_MODEL), const),
                  pl.BlockSpec((1, D_MODEL), const),
                  pl.BlockSpec((D_MODEL, D_FF), const),
                  pl.BlockSpec((D_MODEL, D_FF), const),
                  pl.BlockSpec((D_FF, D_MODEL), const),
                  pl.BlockSpec((1, D_MODEL), const)],
        out_specs=pl.BlockSpec((tm, D_MODEL), tok),
        compiler_params=pltpu.CompilerParams(dimension_semantics=("arbitrary",), vmem_limit_bytes=VMEM_LIMIT),
        name="outproj_ffn",
    )(x2, ma, mb, mc, wo, gf, wg, wu, wd, gl)


def _regroup_w_in(w_in):
    sizes = (A_Q, HEAD_DIM, HEAD_DIM, B_Q, HEAD_DIM, HEAD_DIM, HEAD_DIM, HEAD_DIM, HEAD_DIM, HEAD_DIM,
             N_GATES, C_W, C_W, C_W)
    offs = np.concatenate([[0], np.cumsum(sizes)])
    qa, ka, va, qb, kcb, vcb, ksb, vsb, kwb, vwb, gb, qc, kc, vc = [w_in[:, offs[j]:offs[j + 1]]
                                                                     for j in range(len(sizes))]
    pad = jnp.zeros((w_in.shape[0], LANES - N_GATES), w_in.dtype)
    cols = [qa * SCALE, ka, va, qb * SCALE, ksb, vsb, kwb, vwb, kcb, vcb, gb, pad, qc * SCALE, kc, vc]
    return jnp.concatenate(cols, axis=1).astype(BF16)


def _layer(x2, b, s, w_in, w_out, g_attn, g_ffn, g_out_a, g_out_b, g_out_c, sinks, cmp_pos, cmp_w1, cmp_w2,
           w_gate, w_up, w_down, g_final, final):
    row = lambda v: v.reshape(1, -1)
    qa, kva, qb, kvs, kvw, kcr, vcr, gate, qc, kc, vc = _inproj(x2, row(g_attn), _regroup_w_in(w_in))
    r3 = lambda a: a.reshape(b, s, a.shape[-1])

    n_chunk = s // CMP_STRIDE
    half = CMP_LEN // 2
    flat = half * HEAD_DIM
    xk = kcr.reshape(b, n_chunk, flat)
    xv = vcr.reshape(b, n_chunk, flat)
    pos = cmp_pos.reshape(2, 2, 1, flat)
    w1 = cmp_w1.reshape(2, 2, flat, CMP_HIDDEN).astype(BF16)
    zeros = jnp.zeros((CMP_HIDDEN, HEAD_DIM), cmp_w2.dtype)
    w2 = jnp.stack([jnp.concatenate([cmp_w2[0], zeros], axis=1),
                    jnp.concatenate([zeros, cmp_w2[1]], axis=1)]).astype(BF16)
    kvc = _compress(xk, xv, pos, w1, w2)

    ma = _mixer_a(sinks, r3(qa), r3(kva), row(g_out_a))
    mb = _mixer_b(r3(qb), r3(kvs), r3(kvw), kvc, r3(gate), row(g_out_b))
    mc = _mixer_c(r3(qc), r3(kc), r3(vc), row(g_out_c))
    t = b * s
    return _outffn(x2, ma.reshape(t, A_Q), mb.reshape(t, B_Q), mc.reshape(t, C_W), w_out.astype(BF16),
                   row(g_ffn), w_gate.astype(BF16), w_up.astype(BF16), w_down.astype(BF16), row(g_final), final)


def kernel(x, w_in, w_out, g_attn, g_ffn, g_out_a, g_out_b, g_out_c, sinks, cmp_pos, cmp_w1, cmp_w2, w_gate, w_up,
           w_down, g_final):
    b, s, d = x.shape
    depth = w_in.shape[0]
    assert d == D_MODEL and s % KEY_CHUNK == 0 and (b * s) % TOKEN_TILE == 0 and (b * s) % INPROJ_TILE == 0
    assert s >= B_WINDOW + Q_WIDE and s % SLC_CHUNK == 0 and (s // CMP_STRIDE) % LANES == 0
    x2 = x.reshape(b * s, d)
    for l in range(depth):
        x2 = _layer(x2, b, s, w_in[l], w_out[l], g_attn[l], g_ffn[l], g_out_a[l], g_out_b[l], g_out_c[l], sinks[l],
                    cmp_pos[l], cmp_w1[l], cmp_w2[l], w_gate[l], w_up[l], w_down[l], g_final, l == depth - 1)
    return x2.reshape(b, s, d)
```

```python
import functools

import jax
import jax.numpy as jnp
import numpy as np
from jax import lax
from jax.experimental import pallas as pl
from jax.experimental.pallas import tpu as pltpu

F32 = jnp.float32
BF16 = jnp.bfloat16

D_MODEL = 1024
HEAD_DIM = 64
LANES = 128
Q_BLOCK = 128
Q_WIDE = 256
A_HEADS = 8
A_WINDOW = 128
B_HEADS = 4
CMP_LEN = 32
CMP_STRIDE = 16
CMP_HIDDEN = 128
SLC_LEN = 64
SLC_TOP = 8
B_WINDOW = 512
C_HEADS = 4
A_Q = A_HEADS * HEAD_DIM
B_Q = B_HEADS * HEAD_DIM
C_W = C_HEADS * HEAD_DIM
MIX_WIDTH = A_Q + B_Q + C_W
D_FF = 2816
N_GATES = B_HEADS * 3
NEG_INF = -1e30
FORCE = 1e4
EPS = 1e-6
SCALE = HEAD_DIM ** -0.5
LOG2E = 1.4426950408889634

KEY_CHUNK = 256
SLC_CHUNK = 512
FF_CHUNK = 256
TOKEN_TILE = 512
INPROJ_TILE = 1024
VMEM_LIMIT = 56 * 1024 * 1024

_G_QA, _G_KVA, _G_QB, _G_KVS, _G_KVW, _G_KVC, _G_GATE, _G_QC, _G_KC, _G_VC, IN_PAD = (
    0, 512, 640, 896, 1024, 1152, 1280, 1408, 1664, 1920, 2176)


def _alibi_slopes():
    n = A_HEADS + B_HEADS
    sl = 2.0 ** (-8.0 * np.arange(1, n + 1) / n) * LOG2E
    return [float(v) for v in sl[:A_HEADS]], [float(v) for v in sl[A_HEADS:]]


def _round_bf16(x):
    bits = np.ascontiguousarray(x, np.float32).view(np.uint32).astype(np.uint64)
    bits = ((bits + 0x7FFF + ((bits >> 16) & 1)) >> 16) << 16
    return bits.astype(np.uint32).view(np.float32)


def _key_features(pos):
    pos = np.asarray(pos, np.int64)
    assert pos.max() < 2048
    f = np.zeros((len(pos), LANES), np.float32)
    for j in range(3):
        f[:, HEAD_DIM + 2 * j] = 256 * (pos // 256)
        f[:, HEAD_DIM + 2 * j + 1] = pos % 256
    return jnp.asarray(f, BF16)


def _query_features(slopes, rows):
    f = np.zeros((len(slopes) * rows, LANES), np.float32)
    for h, slope in enumerate(slopes):
        rest = np.float32(slope)
        for j in range(3):
            part = _round_bf16(rest)
            rest = np.float32(rest - part)
            f[h * rows:(h + 1) * rows, HEAD_DIM + 2 * j:HEAD_DIM + 2 * j + 2] = part
    return jnp.asarray(f, BF16)


def _dot(a, b):
    return jnp.dot(a, b, preferred_element_type=F32)


def _dot_nt(a, b):
    return lax.dot_general(a, b, (((1,), (1,)), ((), ())), preferred_element_type=F32)


def _split_bf16(x):
    hi = x.astype(BF16)
    lo = (x - hi.astype(F32)).astype(BF16)
    return hi, lo


def _rms(x, g):
    ms = jnp.mean(x * x, axis=-1, keepdims=True)
    return x * lax.rsqrt(ms + EPS) * g


def _lane_tile(x, width):
    return jnp.concatenate([x] * (width // LANES), axis=1)


def _swap_halves(x):
    return jnp.concatenate([x[:, HEAD_DIM:], x[:, :HEAD_DIM]], axis=1)


def _stack_heads(q, n_heads, feat):
    rows = q.shape[0]
    lo = lax.broadcasted_iota(jnp.int32, (rows, LANES), 1) < HEAD_DIM
    tiles = []
    for p in range(n_heads // 2):
        pair = q[:, p * LANES:(p + 1) * LANES]
        tiles.append(jnp.where(lo, pair, feat[(2 * p) * rows:(2 * p + 1) * rows]))
        tiles.append(jnp.where(lo, _swap_halves(pair), feat[(2 * p + 1) * rows:(2 * p + 2) * rows]))
    return jnp.concatenate(tiles, axis=0)


def _with_key_features(kv, feat):
    lo = lax.broadcasted_iota(jnp.int32, kv.shape, 1) < HEAD_DIM
    return jnp.where(lo, kv, feat)


def _stack_pairs(q, n_heads):
    rows = q.shape[0]
    lo = lax.broadcasted_iota(jnp.int32, (rows, LANES), 1) < HEAD_DIM
    out = []
    for p in range(n_heads // 2):
        pair = q[:, p * LANES:(p + 1) * LANES]
        zero = jnp.zeros_like(pair)
        out.append(jnp.concatenate([jnp.where(lo, pair, zero), jnp.where(lo, zero, pair)], axis=0))
    return out


def _with_ones_for_keys(kv):
    lo = lax.broadcasted_iota(jnp.int32, kv.shape, 1) < HEAD_DIM
    return jnp.where(lo, jnp.ones_like(kv), kv)


def _unstack_normalized(o, n_heads):
    rows = o.shape[0] // n_heads
    lo = lax.broadcasted_iota(jnp.int32, (rows, LANES), 1) < HEAD_DIM
    pairs = []
    for p in range(n_heads // 2):
        even = o[(2 * p) * rows:(2 * p + 1) * rows]
        odd = o[(2 * p + 1) * rows:(2 * p + 2) * rows]
        pairs.append(jnp.where(lo, pltpu.roll(even, HEAD_DIM, 1) / even, odd / pltpu.roll(odd, HEAD_DIM, 1)))
    return jnp.concatenate(pairs, axis=1)


def _unstack_heads(o, n_heads):
    rows = o.shape[0] // n_heads
    lo = lax.broadcasted_iota(jnp.int32, (rows, LANES), 1) < HEAD_DIM
    pairs = []
    for p in range(n_heads // 2):
        even = o[(2 * p) * rows:(2 * p + 1) * rows]
        odd = o[(2 * p + 1) * rows:(2 * p + 2) * rows]
        pairs.append(jnp.where(lo, pltpu.roll(even, HEAD_DIM, 1), odd))
    return jnp.concatenate(pairs, axis=1)


def _inproj_kernel(x_ref, g_ref, w_ref, qa_ref, kva_ref, qb_ref, kvs_ref, kvw_ref, kvr_ref,
                   gate_ref, qc_ref, kc_ref, vc_ref):
    h = _rms(x_ref[...], g_ref[...]).astype(BF16)

    def proj(lo, hi):
        return _dot(h, w_ref[:, lo:hi])

    qa_ref[...] = (proj(_G_QA, _G_KVA) * LOG2E).astype(BF16)
    kva_ref[...] = proj(_G_KVA, _G_QB).astype(BF16)
    qb_ref[...] = (proj(_G_QB, _G_KVS) * LOG2E).astype(BF16)
    kvs_ref[...] = proj(_G_KVS, _G_KVW).astype(BF16)
    kvw_ref[...] = proj(_G_KVW, _G_KVC).astype(BF16)
    kvr_ref[...] = proj(_G_KVC, _G_GATE)
    gate_ref[...] = proj(_G_GATE, _G_QC)
    qc_ref[...] = (proj(_G_QC, _G_KC) * LOG2E).astype(BF16)
    kc_ref[...] = proj(_G_KC, _G_VC).astype(BF16)
    vc_ref[...] = proj(_G_VC, IN_PAD).astype(BF16)


def _inproj(x2, g, w):
    t = x2.shape[0]
    tm = INPROJ_TILE
    widths = [(A_Q, BF16), (LANES, BF16), (B_Q, BF16), (LANES, BF16), (LANES, BF16), (LANES, F32),
              (LANES, F32), (C_W, BF16), (C_W, BF16), (C_W, BF16)]
    return pl.pallas_call(
        _inproj_kernel,
        out_shape=[jax.ShapeDtypeStruct((t, wd), dt) for wd, dt in widths],
        grid=(t // tm,),
        in_specs=[pl.BlockSpec((tm, D_MODEL), lambda i: (i, 0)),
                  pl.BlockSpec((1, D_MODEL), lambda i: (0, 0)),
                  pl.BlockSpec((D_MODEL, IN_PAD), lambda i: (0, 0))],
        out_specs=[pl.BlockSpec((tm, wd), lambda i: (i, 0)) for wd, _ in widths],
        compiler_params=pltpu.CompilerParams(dimension_semantics=("arbitrary",), vmem_limit_bytes=VMEM_LIMIT),
        name="inproj",
    )(x2, g, w)


def _compress_kernel(x_ref, pos_ref, w1_ref, w2_ref, o_ref):
    half = CMP_LEN // 2
    n = x_ref.shape[1] // CMP_STRIDE
    h_lo = jnp.zeros((n, 2 * CMP_HIDDEN), F32)
    h_hi = jnp.zeros((n, 2 * CMP_HIDDEN), F32)
    for l in range(half):
        x = x_ref[0, pl.ds(l, n, stride=CMP_STRIDE), :]
        h_lo = h_lo + _dot((x + pos_ref[0, l]).astype(BF16), w1_ref[0, l])
        h_hi = h_hi + _dot((x + pos_ref[1, l]).astype(BF16), w1_ref[1, l])
    h = h_lo + pltpu.roll(h_hi, n - 1, 0)
    o_ref[0] = _dot(jax.nn.gelu(h).astype(BF16), w2_ref[...]).astype(BF16)


def _compress(x, pos, w1, w2):
    b, s, _ = x.shape
    n = s // CMP_STRIDE
    half = CMP_LEN // 2
    return pl.pallas_call(
        _compress_kernel,
        out_shape=jax.ShapeDtypeStruct((b, n, LANES), BF16),
        grid=(b,),
        in_specs=[pl.BlockSpec((1, s, LANES), lambda i: (i, 0, 0)),
                  pl.BlockSpec((2, half, 1, LANES), lambda i: (0, 0, 0, 0)),
                  pl.BlockSpec((2, half, LANES, 2 * CMP_HIDDEN), lambda i: (0, 0, 0, 0)),
                  pl.BlockSpec((2 * CMP_HIDDEN, LANES), lambda i: (0, 0))],
        out_specs=pl.BlockSpec((1, n, LANES), lambda i: (i, 0, 0)),
        compiler_params=pltpu.CompilerParams(dimension_semantics=("arbitrary",), vmem_limit_bytes=VMEM_LIMIT),
        name="compress",
    )(x, pos, w1, w2)


def _mixa_kernel(sink_ref, q_ref, kvp_ref, kvc_ref, pfp_ref, pfc_ref, qf_ref, g_ref, o_ref, *, slopes):
    i = pl.program_id(1)
    span = 2 * Q_BLOCK
    row = lax.broadcasted_iota(jnp.int32, (Q_BLOCK, span), 0)
    col = lax.broadcasted_iota(jnp.int32, (Q_BLOCK, span), 1)
    dist = row + Q_BLOCK - col
    band = (dist >= 0) & (dist < A_WINDOW)
    kv_rows = jnp.concatenate([kvp_ref[0], kvc_ref[0]], axis=0)
    pf_rows = jnp.concatenate([pfp_ref[...], pfc_ref[...]], axis=0)

    for r in range(Q_WIDE // Q_BLOCK):
        q = q_ref[0, r * Q_BLOCK:(r + 1) * Q_BLOCK, :]
        kv = kv_rows[r * Q_BLOCK:r * Q_BLOCK + span]
        valid = band & ((col >= Q_BLOCK) | (i > 0)) if r == 0 else band
        t_row = (i * Q_WIDE + r * Q_BLOCK + lax.broadcasted_iota(jnp.int32, (Q_BLOCK, 1), 0)).astype(F32)
        kx = _with_key_features(kv, pf_rows[r * Q_BLOCK:r * Q_BLOCK + span])
        s_all = _dot_nt(_stack_heads(q, A_HEADS, qf_ref[...]), kx)
        probs, inv_l = [], []
        for hd in range(A_HEADS):
            sink = sink_ref[hd] * LOG2E + slopes[hd] * t_row
            s = jnp.where(valid, s_all[hd * Q_BLOCK:(hd + 1) * Q_BLOCK], NEG_INF)
            m = jnp.maximum(jnp.max(s, axis=-1, keepdims=True), sink)
            e = jnp.exp2(s - m)
            inv_l.append(1.0 / (jnp.sum(e, axis=-1, keepdims=True) + jnp.exp2(sink - m)))
            probs.append(e.astype(BF16))
        o_all = _dot(jnp.concatenate(probs, axis=0), kv)
        o_all = jnp.concatenate([o_all[hd * Q_BLOCK:(hd + 1) * Q_BLOCK] * inv_l[hd] for hd in range(A_HEADS)], axis=0)
        o_ref[0, r * Q_BLOCK:(r + 1) * Q_BLOCK, :] = _rms(_unstack_heads(o_all, A_HEADS), g_ref[...]).astype(BF16)


def _mixer_a(sinks, qa, kva, g):
    b, s, _ = qa.shape
    slopes, _ = _alibi_slopes()
    pos_feat = _key_features(np.arange(s))
    sub = Q_WIDE // Q_BLOCK
    prev_block = lambda i: jnp.maximum(sub * i - 1, 0)
    return pl.pallas_call(
        functools.partial(_mixa_kernel, slopes=slopes),
        out_shape=jax.ShapeDtypeStruct((b, s, A_Q), BF16),
        grid=(b, s // Q_WIDE),
        in_specs=[pl.BlockSpec(memory_space=pltpu.SMEM),
                  pl.BlockSpec((1, Q_WIDE, A_Q), lambda bi, i: (bi, i, 0)),
                  pl.BlockSpec((1, Q_BLOCK, LANES), lambda bi, i: (bi, prev_block(i), 0)),
                  pl.BlockSpec((1, Q_WIDE, LANES), lambda bi, i: (bi, i, 0)),
                  pl.BlockSpec((Q_BLOCK, LANES), lambda bi, i: (prev_block(i), 0)),
                  pl.BlockSpec((Q_WIDE, LANES), lambda bi, i: (i, 0)),
                  pl.BlockSpec((A_HEADS * Q_BLOCK, LANES), lambda bi, i: (0, 0)),
                  pl.BlockSpec((1, A_Q), lambda bi, i: (0, 0))],
        out_specs=pl.BlockSpec((1, Q_WIDE, A_Q), lambda bi, i: (bi, i, 0)),
        compiler_params=pltpu.CompilerParams(dimension_semantics=("arbitrary", "arbitrary"),
                                             vmem_limit_bytes=VMEM_LIMIT),
        name="mixer_a",
    )(sinks, qa, kva, kva, pos_feat, pos_feat, _query_features(slopes, Q_BLOCK), g)


def _mixb_kernel(q_ref, kvs_ref, kvw_ref, kvc_ref, gate_ref, pf_ref, cf_ref, qf_ref, mmap_ref, exp_ref, rep_ref,
                 g_ref, o_ref, acc_ref, m_ref, *, n_cmp, n_slc):
    i = pl.program_id(1)
    t0 = i * Q_WIDE
    q_all = _stack_heads(q_ref[0], B_HEADS, qf_ref[...])

    def head_rows(x, hd, rows=Q_WIDE):
        return x[hd * rows:(hd + 1) * rows]

    kvc = kvc_ref[0]
    n_pad = kvc.shape[0]
    s_cmp = _dot_nt(q_all, _with_key_features(kvc, cf_ref[...]))
    wspan = B_WINDOW + Q_BLOCK
    row_w = lax.broadcasted_iota(jnp.int32, (Q_BLOCK, wspan), 0)
    col_w = lax.broadcasted_iota(jnp.int32, (Q_BLOCK, wspan), 1)
    n_sub = Q_WIDE // Q_BLOCK
    kvw, valid_w, s_win = [], [], []
    for r in range(n_sub):
        r0 = t0 + r * Q_BLOCK
        w0 = pl.multiple_of(jnp.clip(r0 - B_WINDOW, 0, kvw_ref.shape[1] - wspan), LANES)
        kvw.append(kvw_ref[0, pl.ds(w0, wspan), :])
        dist_w = (row_w - col_w) + (r0 - w0)
        valid_w.append((dist_w >= 0) & (dist_w < B_WINDOW))
        q_sub = jnp.concatenate([head_rows(q_all, hd)[r * Q_BLOCK:(r + 1) * Q_BLOCK] for hd in range(B_HEADS)], axis=0)
        s_win.append(_dot_nt(q_sub, _with_key_features(kvw[r], pf_ref[pl.ds(w0, wspan), :])))

    sig = jax.nn.sigmoid(gate_ref[0]).astype(BF16)
    gates = [_dot(sig, rep_ref[r]) for r in range(3)]

    row_c = lax.broadcasted_iota(jnp.int32, (Q_WIDE, n_pad), 0)
    col_c = lax.broadcasted_iota(jnp.int32, (Q_WIDE, n_pad), 1)
    vis_c = ((t0 + row_c) >= (col_c * CMP_STRIDE + (CMP_LEN - 1))) & (col_c < n_cmp)
    row1 = lax.broadcasted_iota(jnp.int32, (Q_WIDE, 1), 0)
    any_vis = ((t0 + row1) >= (CMP_LEN - 1)).astype(F32)
    probs = []
    p_sum = jnp.zeros((Q_WIDE, n_pad), F32)
    for hd in range(B_HEADS):
        s = jnp.where(vis_c, head_rows(s_cmp, hd), NEG_INF)
        m = jnp.max(s, axis=-1, keepdims=True)
        e = jnp.exp2(s - m)
        pr = e * (any_vis / jnp.sum(e, axis=-1, keepdims=True))
        p_sum = p_sum + pr
        probs.append(pr.astype(BF16))
    o_cmp = _unstack_heads(_dot(jnp.concatenate(probs, axis=0), kvc), B_HEADS)

    p_hi, p_lo = _split_bf16(p_sum)
    imp = (_dot_nt(mmap_ref[...], p_hi) + _dot_nt(mmap_ref[...], p_lo))[:n_slc]
    blk = lax.broadcasted_iota(jnp.int32, (n_slc, Q_WIDE), 0)
    tq = t0 + lax.broadcasted_iota(jnp.int32, (n_slc, Q_WIDE), 1)
    cur = tq // SLC_LEN
    valid_b = blk <= cur
    forced = (blk == 0) | (blk == cur) | (blk == cur - 1)
    val = jnp.where(forced, FORCE, jnp.where(valid_b, imp, -FORCE))
    rank = jnp.zeros((n_slc, Q_WIDE), F32)
    for j in range(n_slc):
        vj = val[j:j + 1, :]
        ahead = (vj > val) | ((vj == val) & (blk > j))
        rank = rank + ahead.astype(F32)
    sel = ((rank < SLC_TOP) & valid_b).astype(F32)
    sel = jnp.concatenate([sel, jnp.zeros((LANES - n_slc, Q_WIDE), F32)], axis=0)
    sel_q = sel.T.astype(BF16)

    o_sub = [[None] * n_sub for _ in range(B_HEADS)]
    for r in range(n_sub):
        probs = []
        for hd in range(B_HEADS):
            s = jnp.where(valid_w[r], head_rows(s_win[r], hd, Q_BLOCK), NEG_INF)
            probs.append(jnp.exp2(s - jnp.max(s, axis=-1, keepdims=True)).astype(BF16))
        o_all = _dot(jnp.concatenate(probs, axis=0), _with_ones_for_keys(kvw[r]))
        for hd in range(B_HEADS):
            o_sub[hd][r] = head_rows(o_all, hd, Q_BLOCK)
    o_win = _unstack_normalized(jnp.concatenate([o_sub[hd][r] for hd in range(B_HEADS) for r in range(n_sub)], axis=0),
                                B_HEADS)
    ob_rest = gates[0] * o_cmp + gates[2] * o_win

    acc_ref[...] = jnp.zeros(acc_ref.shape, F32)
    m_ref[...] = jnp.full(m_ref.shape, NEG_INF, F32)

    def slc_chunk(c, width):
        k0 = pl.multiple_of(c * SLC_CHUNK, SLC_CHUNK)
        kv = kvs_ref[0, pl.ds(k0, width), :]
        row_k = lax.broadcasted_iota(jnp.int32, (Q_WIDE, width), 0)
        col_k = lax.broadcasted_iota(jnp.int32, (Q_WIDE, width), 1)
        allowed = (_dot(sel_q, exp_ref[c][:, :width]) > 0.5) & ((col_k - row_k) <= (t0 - k0))
        s_all = _dot_nt(q_all, _with_key_features(kv, pf_ref[pl.ds(k0, width), :]))
        probs, alphas = [], []
        for hd in range(B_HEADS):
            s = jnp.where(allowed, head_rows(s_all, hd), NEG_INF)
            m_old = m_ref[hd]
            m_new = jnp.maximum(m_old, jnp.max(s, axis=-1, keepdims=True))
            alpha = jnp.exp2(m_old - m_new)
            m_ref[hd] = m_new
            probs.append(jnp.exp2(s - _lane_tile(m_new, width)).astype(BF16))
            alphas.append(alpha)
        o_all = _dot(jnp.concatenate(probs, axis=0), _with_ones_for_keys(kv))
        for hd in range(B_HEADS):
            acc_ref[hd] = alphas[hd] * acc_ref[hd] + head_rows(o_all, hd)

    n_half = lax.shift_right_logical(t0 + Q_WIDE + SLC_CHUNK // 2 - 1, (SLC_CHUNK // 2).bit_length() - 1)
    n_full = lax.shift_right_logical(n_half, 1)

    def full_chunk(c, carry):
        slc_chunk(c, SLC_CHUNK)
        return carry

    lax.fori_loop(0, n_full, full_chunk, 0)

    @pl.when((n_half & 1) == 1)
    def _():
        slc_chunk(n_full, SLC_CHUNK // 2)

    o_slc = _unstack_normalized(jnp.concatenate([acc_ref[hd] for hd in range(B_HEADS)], axis=0), B_HEADS)

    ob = ob_rest + gates[1] * o_slc
    o_ref[0] = _rms(ob, g_ref[...]).astype(BF16)


def _mixer_b(qb, kvs, kvw, kvc, gate, g):
    b, s, _ = qb.shape
    n_pad = kvc.shape[1]
    n_cmp = s // CMP_STRIDE - 1
    n_slc = s // SLC_LEN
    n_chunks = s // SLC_CHUNK
    _, slopes = _alibi_slopes()
    pos_feat = _key_features(np.arange(s))
    cmp_feat = _key_features(np.minimum(np.arange(n_pad), n_cmp - 1) * CMP_STRIDE + (CMP_LEN - 1))
    cs = np.arange(n_pad)[None, :] * CMP_STRIDE
    ss = np.arange(LANES)[:, None] * SLC_LEN
    ov = np.maximum(0, np.minimum(cs + CMP_LEN, ss + SLC_LEN) - np.maximum(cs, ss)) / CMP_STRIDE
    ov = ov * (np.arange(n_pad)[None, :] < n_cmp) * (np.arange(LANES)[:, None] < n_slc)
    mmap_t = jnp.asarray(ov, BF16)
    key_blk = (np.arange(n_chunks)[:, None, None] * SLC_CHUNK + np.arange(SLC_CHUNK)[None, None, :]) // SLC_LEN
    expand = jnp.asarray(key_blk == np.arange(LANES)[None, :, None], BF16)
    rep = np.zeros((3, LANES, B_Q), np.float32)
    for r in range(3):
        for h in range(B_HEADS):
            rep[r, 3 * h + r, h * HEAD_DIM:(h + 1) * HEAD_DIM] = 1.0
    rep = jnp.asarray(rep, BF16)
    return pl.pallas_call(
        functools.partial(_mixb_kernel, n_cmp=n_cmp, n_slc=n_slc),
        out_shape=jax.ShapeDtypeStruct((b, s, B_Q), BF16),
        grid=(b, s // Q_WIDE),
        in_specs=[pl.BlockSpec((1, Q_WIDE, B_Q), lambda bi, i: (bi, i, 0)),
                  pl.BlockSpec((1, s, LANES), lambda bi, i: (bi, 0, 0)),
                  pl.BlockSpec((1, s, LANES), lambda bi, i: (bi, 0, 0)),
                  pl.BlockSpec((1, n_pad, LANES), lambda bi, i: (bi, 0, 0)),
                  pl.BlockSpec((1, Q_WIDE, LANES), lambda bi, i: (bi, i, 0)),
                  pl.BlockSpec((s, LANES), lambda bi, i: (0, 0)),
                  pl.BlockSpec((n_pad, LANES), lambda bi, i: (0, 0)),
                  pl.BlockSpec((B_HEADS * Q_WIDE, LANES), lambda bi, i: (0, 0)),
                  pl.BlockSpec((LANES, n_pad), lambda bi, i: (0, 0)),
                  pl.BlockSpec((n_chunks, LANES, SLC_CHUNK), lambda bi, i: (0, 0, 0)),
                  pl.BlockSpec((3, LANES, B_Q), lambda bi, i: (0, 0, 0)),
                  pl.BlockSpec((1, B_Q), lambda bi, i: (0, 0))],
        out_specs=pl.BlockSpec((1, Q_WIDE, B_Q), lambda bi, i: (bi, i, 0)),
        scratch_shapes=[pltpu.VMEM((B_HEADS, Q_WIDE, LANES), F32),
                        pltpu.VMEM((B_HEADS, Q_WIDE, LANES), F32)],
        compiler_params=pltpu.CompilerParams(dimension_semantics=("arbitrary", "arbitrary"),
                                             vmem_limit_bytes=VMEM_LIMIT),
        name="mixer_b",
    )(qb, kvs, kvw, kvc, gate, pos_feat, cmp_feat, _query_features(slopes, Q_WIDE), mmap_t, expand, rep, g)


def _mixc_kernel(q_ref, k_ref, v_ref, tri_ref, g_ref, o_ref, acc_ref, carry_ref, z_ref, w_ref):
    i = pl.program_id(1)
    t0 = i * Q_WIDE
    n_pair = C_HEADS // 2
    q_pairs = _stack_pairs(q_ref[0], C_HEADS)
    acc_ref[...] = jnp.zeros(acc_ref.shape, F32)
    carry_ref[...] = jnp.zeros(carry_ref.shape, F32)
    row = lax.broadcasted_iota(jnp.int32, (Q_WIDE, KEY_CHUNK), 0)
    col = lax.broadcasted_iota(jnp.int32, (Q_WIDE, KEY_CHUNK), 1)
    cr = col - row
    n_chunks = (t0 + Q_WIDE + KEY_CHUNK - 1) // KEY_CHUNK

    def chunk_rows(ref, c):
        return ref[0, pl.ds(pl.multiple_of(c * KEY_CHUNK, KEY_CHUNK), KEY_CHUNK), :]

    def logits_to_scratch(c):
        k = chunk_rows(k_ref, c)
        for p in range(n_pair):
            z_ref[p] = _dot_nt(q_pairs[p], k[:, p * LANES:(p + 1) * LANES])

    def add_values(c):
        v = chunk_rows(v_ref, c)
        for p in range(n_pair):
            o = _dot(w_ref[p], v[:, p * LANES:(p + 1) * LANES])
            acc_ref[2 * p] = acc_ref[2 * p] + o[:Q_WIDE]
            acc_ref[2 * p + 1] = acc_ref[2 * p + 1] + o[Q_WIDE:]

    def sweep(c, diagonal):
        if not diagonal:
            add_values(c + 1)
        past = cr < (t0 - c * KEY_CHUNK)
        log_w, parts = [], []
        for hd in range(C_HEADS):
            z = z_ref[hd // 2, (hd % 2) * Q_WIDE:(hd % 2 + 1) * Q_WIDE, :]
            neg_abs = lax.bitcast_convert_type(lax.bitcast_convert_type(z, jnp.uint32) | jnp.uint32(0x80000000), F32)
            drop = jnp.maximum(z, 0.0) + jnp.log2(1.0 + jnp.exp2(neg_abs))
            dk = jnp.where(past, drop, 0.0) if diagonal else drop
            parts.append(dk.astype(BF16))
            log_w.append(z - drop - _lane_tile(carry_ref[hd], KEY_CHUNK))
            carry_ref[hd] = carry_ref[hd] + jnp.sum(dk, axis=-1, keepdims=True)
        tails = _dot(jnp.concatenate(parts, axis=0), tri_ref[...])
        logits_to_scratch(jnp.maximum(c - 1, 0))
        weights = []
        for hd in range(C_HEADS):
            w = jnp.exp2(log_w[hd] - tails[hd * Q_WIDE:(hd + 1) * Q_WIDE])
            weights.append((jnp.where(past, w, 0.0) if diagonal else w).astype(BF16))
        for p in range(n_pair):
            w_ref[p] = jnp.concatenate(weights[2 * p:2 * p + 2], axis=0)

    logits_to_scratch(n_chunks - 1)
    sweep(n_chunks - 1, True)

    def earlier(j, carry):
        sweep(n_chunks - 1 - j, False)
        return carry

    lax.fori_loop(1, n_chunks, earlier, 0)
    add_values(0)
    lane = lax.broadcasted_iota(jnp.int32, (Q_WIDE, LANES), 1)
    lo = lane < HEAD_DIM
    oc = jnp.concatenate([jnp.where(lo, acc_ref[2 * p], acc_ref[2 * p + 1]) for p in range(n_pair)], axis=1)
    o_ref[0] = _rms(oc, g_ref[...]).astype(BF16)


def _mixer_c(qc, kc, vc, g):
    b, s, _ = qc.shape
    tri = jnp.asarray(np.arange(KEY_CHUNK)[:, None] > np.arange(KEY_CHUNK)[None, :], BF16)
    return pl.pallas_call(
        _mixc_kernel,
        out_shape=jax.ShapeDtypeStruct((b, s, C_W), BF16),
        grid=(b, s // Q_WIDE),
        in_specs=[pl.BlockSpec((1, Q_WIDE, C_W), lambda bi, i: (bi, i, 0)),
                  pl.BlockSpec((1, s, C_W), lambda bi, i: (bi, 0, 0)),
                  pl.BlockSpec((1, s, C_W), lambda bi, i: (bi, 0, 0)),
                  pl.BlockSpec((KEY_CHUNK, KEY_CHUNK), lambda bi, i: (0, 0)),
                  pl.BlockSpec((1, C_W), lambda bi, i: (0, 0))],
        out_specs=pl.BlockSpec((1, Q_WIDE, C_W), lambda bi, i: (bi, i, 0)),
        scratch_shapes=[pltpu.VMEM((C_HEADS, Q_WIDE, LANES), F32),
                        pltpu.VMEM((C_HEADS, Q_WIDE, LANES), F32),
                        pltpu.VMEM((C_HEADS // 2, 2 * Q_WIDE, KEY_CHUNK), F32),
                        pltpu.VMEM((C_HEADS // 2, 2 * Q_WIDE, KEY_CHUNK), BF16)],
        compiler_params=pltpu.CompilerParams(dimension_semantics=("arbitrary", "arbitrary"),
                                             vmem_limit_bytes=VMEM_LIMIT),
        name="mixer_c",
    )(qc, kc, vc, tri, g)


def _outffn_kernel(x_ref, ma_ref, mb_ref, mc_ref, wo_ref, gf_ref, wg_ref, wu_ref, wd_ref, gl_ref, o_ref, *, final):
    x = (x_ref[...] + _dot(ma_ref[...], wo_ref[:A_Q, :]) + _dot(mb_ref[...], wo_ref[A_Q:A_Q + B_Q, :])
         + _dot(mc_ref[...], wo_ref[A_Q + B_Q:, :]))
    h = _rms(x, gf_ref[...]).astype(BF16)
    o_ref[...] = x
    for c in range(D_FF // FF_CHUNK):
        sl = slice(c * FF_CHUNK, (c + 1) * FF_CHUNK)
        gate = _dot(h, wg_ref[:, sl])
        up = _dot(h, wu_ref[:, sl])
        o_ref[...] += _dot((jax.nn.silu(gate) * up).astype(BF16), wd_ref[sl, :])
    if final:
        o_ref[...] = _rms(o_ref[...], gl_ref[...])


def _outffn(x2, ma, mb, mc, wo, gf, wg, wu, wd, gl, final):
    t = x2.shape[0]
    tm = TOKEN_TILE
    const = lambda i: (0, 0)
    tok = lambda i: (i, 0)
    return pl.pallas_call(
        functools.partial(_outffn_kernel, final=final),
        out_shape=jax.ShapeDtypeStruct((t, D_MODEL), F32),
        grid=(t // tm,),
        in_specs=[pl.BlockSpec((tm, D_MODEL), tok),
                  pl.BlockSpec((tm, A_Q), tok),
                  pl.BlockSpec((tm, B_Q), tok),
                  pl.BlockSpec((tm, C_W), tok),
                  pl.BlockSpec((MIX_WIDTH, D_MODEL), const),
                  pl.BlockSpec((1, D_MODEL), const),
                  pl.BlockSpec((D_MODEL, D_FF), const),
                  pl.BlockSpec((D_MODEL, D_FF), const),
                  pl.BlockSpec((D_FF, D_MODEL), const),
                  pl.BlockSpec((1, D_MODEL), const)],
        out_specs=pl.BlockSpec((tm, D_MODEL), tok),
        compiler_params=pltpu.CompilerParams(dimension_semantics=("arbitrary",), vmem_limit_bytes=VMEM_LIMIT),
        name="outproj_ffn",
    )(x2, ma, mb, mc, wo, gf, wg, wu, wd, gl)


def _regroup_w_in(w_in):
    sizes = (A_Q, HEAD_DIM, HEAD_DIM, B_Q, HEAD_DIM, HEAD_DIM, HEAD_DIM, HEAD_DIM, HEAD_DIM, HEAD_DIM,
             N_GATES, C_W, C_W, C_W)
    offs = np.concatenate([[0], np.cumsum(sizes)])
    qa, ka, va, qb, kcb, vcb, ksb, vsb, kwb, vwb, gb, qc, kc, vc = [w_in[:, offs[j]:offs[j + 1]]
                                                                     for j in range(len(sizes))]
    pad = jnp.zeros((w_in.shape[0], LANES - N_GATES), w_in.dtype)
    cols = [qa * SCALE, ka, va, qb * SCALE, ksb, vsb, kwb, vwb, kcb, vcb, gb, pad, qc * SCALE, kc, vc]
    return jnp.concatenate(cols, axis=1).astype(BF16)


def _layer(x2, b, s, w_in, w_out, g_attn, g_ffn, g_out_a, g_out_b, g_out_c, sinks, cmp_pos, cmp_w1, cmp_w2,
           w_gate, w_up, w_down, g_final, final):
    row = lambda v: v.reshape(1, -1)
    qa, kva, qb, kvs, kvw, kvr, gate, qc, kc, vc = _inproj(x2, row(g_attn), _regroup_w_in(w_in))
    r3 = lambda a: a.reshape(b, s, a.shape[-1])

    half = CMP_LEN // 2
    pos = jnp.concatenate([cmp_pos[0], cmp_pos[1]], axis=-1).reshape(2, half, 1, LANES)
    z1 = jnp.zeros_like(cmp_w1[0])
    w1 = jnp.concatenate([jnp.concatenate([cmp_w1[0], z1], axis=-1), jnp.concatenate([z1, cmp_w1[1]], axis=-1)], axis=1)
    w1 = w1.reshape(2, half, LANES, 2 * CMP_HIDDEN).astype(BF16)
    z2 = jnp.zeros_like(cmp_w2[0])
    w2 = jnp.concatenate([jnp.concatenate([cmp_w2[0], z2], axis=1), jnp.concatenate([z2, cmp_w2[1]], axis=1)],
                         axis=0).astype(BF16)
    kvc = _compress(r3(kvr), pos, w1, w2)

    ma = _mixer_a(sinks, r3(qa), r3(kva), row(g_out_a))
    mb = _mixer_b(r3(qb), r3(kvs), r3(kvw), kvc, r3(gate), row(g_out_b))
    mc = _mixer_c(r3(qc), r3(kc), r3(vc), row(g_out_c))
    t = b * s
    return _outffn(x2, ma.reshape(t, A_Q), mb.reshape(t, B_Q), mc.reshape(t, C_W), w_out.astype(BF16),
                   row(g_ffn), w_gate.astype(BF16), w_up.astype(BF16), w_down.astype(BF16), row(g_final), final)


def kernel(x, w_in, w_out, g_attn, g_ffn, g_out_a, g_out_b, g_out_c, sinks, cmp_pos, cmp_w1, cmp_w2, w_gate, w_up,
           w_down, g_final):
    b, s, d = x.shape
    depth = w_in.shape[0]
    assert d == D_MODEL and s % KEY_CHUNK == 0 and (b * s) % TOKEN_TILE == 0 and (b * s) % INPROJ_TILE == 0
    assert s >= B_WINDOW + Q_WIDE and s % SLC_CHUNK == 0 and (s // CMP_STRIDE) % LANES == 0
    x2 = x.reshape(b * s, d)
    for l in range(depth):
        x2 = _layer(x2, b, s, w_in[l], w_out[l], g_attn[l], g_ffn[l], g_out_a[l], g_out_b[l], g_out_c[l], sinks[l],
                    cmp_pos[l], cmp_w1[l], cmp_w2[l], w_gate[l], w_up[l], w_down[l], g_final, l == depth - 1)
    return x2.reshape(b, s, d)
```

```python
import functools

import jax
import jax.numpy as jnp
import numpy as np
from jax import lax
from jax.experimental import pallas as pl
from jax.experimental.pallas import tpu as pltpu

F32 = jnp.float32
BF16 = jnp.bfloat16

D_MODEL = 1024
HEAD_DIM = 64
LANES = 128
Q_BLOCK = 128
Q_WIDE = 256
A_TILE = 256
C_BATCH = 4
A_HEADS = 8
A_WINDOW = 128
B_HEADS = 4
CMP_LEN = 32
CMP_STRIDE = 16
CMP_HIDDEN = 128
SLC_LEN = 64
SLC_TOP = 8
B_WINDOW = 512
C_HEADS = 4
A_Q = A_HEADS * HEAD_DIM
B_Q = B_HEADS * HEAD_DIM
C_W = C_HEADS * HEAD_DIM
MIX_WIDTH = A_Q + B_Q + C_W
D_FF = 2816
N_GATES = B_HEADS * 3
NEG_INF = -1e30
FORCE = 1e4
EPS = 1e-6
SCALE = HEAD_DIM ** -0.5
LOG2E = 1.4426950408889634

KEY_CHUNK = 256
SLC_CHUNK = 512
FF_CHUNK = 256
TOKEN_TILE = 512
INPROJ_TILE = 1024
VMEM_LIMIT = 56 * 1024 * 1024

_G_QA, _G_KVA, _G_QB, _G_KVS, _G_KVW, _G_KVC, _G_GATE, _G_QC, _G_KC, _G_VC, IN_PAD = (
    0, 512, 640, 896, 1024, 1152, 1280, 1408, 1664, 1920, 2176)


def _alibi_slopes():
    n = A_HEADS + B_HEADS
    sl = 2.0 ** (-8.0 * np.arange(1, n + 1) / n) * LOG2E
    return [float(v) for v in sl[:A_HEADS]], [float(v) for v in sl[A_HEADS:]]


def _round_bf16(x):
    bits = np.ascontiguousarray(x, np.float32).view(np.uint32).astype(np.uint64)
    bits = ((bits + 0x7FFF + ((bits >> 16) & 1)) >> 16) << 16
    return bits.astype(np.uint32).view(np.float32)


def _key_features(pos):
    pos = np.asarray(pos, np.int64)
    assert pos.max() < 2048
    f = np.zeros((len(pos), LANES), np.float32)
    for j in range(3):
        f[:, HEAD_DIM + 2 * j] = 256 * (pos // 256)
        f[:, HEAD_DIM + 2 * j + 1] = pos % 256
    return jnp.asarray(f, BF16)


def _query_features(slopes, rows):
    f = np.zeros((len(slopes) * rows, LANES), np.float32)
    for h, slope in enumerate(slopes):
        rest = np.float32(slope)
        for j in range(3):
            part = _round_bf16(rest)
            rest = np.float32(rest - part)
            f[h * rows:(h + 1) * rows, HEAD_DIM + 2 * j:HEAD_DIM + 2 * j + 2] = part
    return jnp.asarray(f, BF16)


def _dot(a, b):
    return jnp.dot(a, b, preferred_element_type=F32)


def _dot_nt(a, b):
    return lax.dot_general(a, b, (((1,), (1,)), ((), ())), preferred_element_type=F32)


def _split_bf16(x):
    hi = x.astype(BF16)
    lo = (x - hi.astype(F32)).astype(BF16)
    return hi, lo


def _rms(x, g):
    ms = jnp.mean(x * x, axis=-1, keepdims=True)
    return x * lax.rsqrt(ms + EPS) * g


def _lane_tile(x, width):
    return jnp.concatenate([x] * (width // LANES), axis=1)


def _swap_halves(x):
    return jnp.concatenate([x[:, HEAD_DIM:], x[:, :HEAD_DIM]], axis=1)


def _stack_heads(q, n_heads, feat):
    rows = q.shape[0]
    lo = lax.broadcasted_iota(jnp.int32, (rows, LANES), 1) < HEAD_DIM
    tiles = []
    for p in range(n_heads // 2):
        pair = q[:, p * LANES:(p + 1) * LANES]
        tiles.append(jnp.where(lo, pair, feat[(2 * p) * rows:(2 * p + 1) * rows]))
        tiles.append(jnp.where(lo, _swap_halves(pair), feat[(2 * p + 1) * rows:(2 * p + 2) * rows]))
    return jnp.concatenate(tiles, axis=0)


def _with_key_features(kv, feat):
    lo = lax.broadcasted_iota(jnp.int32, kv.shape, 1) < HEAD_DIM
    return jnp.where(lo, kv, feat)


def _stack_pairs(q, n_heads):
    rows = q.shape[0]
    lo = lax.broadcasted_iota(jnp.int32, (rows, LANES), 1) < HEAD_DIM
    out = []
    for p in range(n_heads // 2):
        pair = q[:, p * LANES:(p + 1) * LANES]
        zero = jnp.zeros_like(pair)
        out.append(jnp.concatenate([jnp.where(lo, pair, zero), jnp.where(lo, zero, pair)], axis=0))
    return out


def _with_ones_for_keys(kv):
    lo = lax.broadcasted_iota(jnp.int32, kv.shape, 1) < HEAD_DIM
    return jnp.where(lo, jnp.ones_like(kv), kv)


def _unstack_normalized(o, n_heads):
    rows = o.shape[0] // n_heads
    lo = lax.broadcasted_iota(jnp.int32, (rows, LANES), 1) < HEAD_DIM
    pairs = []
    for p in range(n_heads // 2):
        even = o[(2 * p) * rows:(2 * p + 1) * rows]
        odd = o[(2 * p + 1) * rows:(2 * p + 2) * rows]
        pairs.append(jnp.where(lo, pltpu.roll(even, HEAD_DIM, 1) / even, odd / pltpu.roll(odd, HEAD_DIM, 1)))
    return jnp.concatenate(pairs, axis=1)


def _unstack_heads(o, n_heads):
    rows = o.shape[0] // n_heads
    lo = lax.broadcasted_iota(jnp.int32, (rows, LANES), 1) < HEAD_DIM
    pairs = []
    for p in range(n_heads // 2):
        even = o[(2 * p) * rows:(2 * p + 1) * rows]
        odd = o[(2 * p + 1) * rows:(2 * p + 2) * rows]
        pairs.append(jnp.where(lo, pltpu.roll(even, HEAD_DIM, 1), odd))
    return jnp.concatenate(pairs, axis=1)


def _inproj_kernel(x_ref, g_ref, w_ref, qa_ref, kva_ref, qb_ref, kvs_ref, kvw_ref, kvr_ref,
                   gate_ref, qc_ref, kc_ref, vc_ref):
    h = _rms(x_ref[...], g_ref[...]).astype(BF16)

    def proj(lo, hi):
        return _dot(h, w_ref[:, lo:hi])

    qa_ref[...] = (proj(_G_QA, _G_KVA) * LOG2E).astype(BF16)
    kva_ref[...] = proj(_G_KVA, _G_QB).astype(BF16)
    qb_ref[...] = (proj(_G_QB, _G_KVS) * LOG2E).astype(BF16)
    kvs_ref[...] = proj(_G_KVS, _G_KVW).astype(BF16)
    kvw_ref[...] = proj(_G_KVW, _G_KVC).astype(BF16)
    kvr_ref[...] = proj(_G_KVC, _G_GATE)
    gate_ref[...] = proj(_G_GATE, _G_QC)
    qc_ref[...] = (proj(_G_QC, _G_KC) * LOG2E).astype(BF16)
    kc_ref[...] = proj(_G_KC, _G_VC).astype(BF16)
    vc_ref[...] = proj(_G_VC, IN_PAD).astype(BF16)


def _inproj(x2, g, w):
    t = x2.shape[0]
    tm = INPROJ_TILE
    widths = [(A_Q, BF16), (LANES, BF16), (B_Q, BF16), (LANES, BF16), (LANES, BF16), (LANES, F32),
              (LANES, F32), (C_W, BF16), (C_W, BF16), (C_W, BF16)]
    return pl.pallas_call(
        _inproj_kernel,
        out_shape=[jax.ShapeDtypeStruct((t, wd), dt) for wd, dt in widths],
        grid=(t // tm,),
        in_specs=[pl.BlockSpec((tm, D_MODEL), lambda i: (i, 0)),
                  pl.BlockSpec((1, D_MODEL), lambda i: (0, 0)),
                  pl.BlockSpec((D_MODEL, IN_PAD), lambda i: (0, 0))],
        out_specs=[pl.BlockSpec((tm, wd), lambda i: (i, 0)) for wd, _ in widths],
        compiler_params=pltpu.CompilerParams(dimension_semantics=("arbitrary",), vmem_limit_bytes=VMEM_LIMIT),
        name="inproj",
    )(x2, g, w)


def _compress_kernel(x_ref, pos_ref, w1_ref, w2_ref, o_ref):
    half = CMP_LEN // 2
    n = x_ref.shape[1] // CMP_STRIDE
    h_lo = jnp.zeros((n, 2 * CMP_HIDDEN), F32)
    h_hi = jnp.zeros((n, 2 * CMP_HIDDEN), F32)
    for l in range(half):
        x = x_ref[0, pl.ds(l, n, stride=CMP_STRIDE), :]
        h_lo = h_lo + _dot((x + pos_ref[0, l]).astype(BF16), w1_ref[0, l])
        h_hi = h_hi + _dot((x + pos_ref[1, l]).astype(BF16), w1_ref[1, l])
    h = h_lo + pltpu.roll(h_hi, n - 1, 0)
    o_ref[0] = _dot(jax.nn.gelu(h).astype(BF16), w2_ref[...]).astype(BF16)


def _compress(x, pos, w1, w2):
    b, s, _ = x.shape
    n = s // CMP_STRIDE
    half = CMP_LEN // 2
    return pl.pallas_call(
        _compress_kernel,
        out_shape=jax.ShapeDtypeStruct((b, n, LANES), BF16),
        grid=(b,),
        in_specs=[pl.BlockSpec((1, s, LANES), lambda i: (i, 0, 0)),
                  pl.BlockSpec((2, half, 1, LANES), lambda i: (0, 0, 0, 0)),
                  pl.BlockSpec((2, half, LANES, 2 * CMP_HIDDEN), lambda i: (0, 0, 0, 0)),
                  pl.BlockSpec((2 * CMP_HIDDEN, LANES), lambda i: (0, 0))],
        out_specs=pl.BlockSpec((1, n, LANES), lambda i: (i, 0, 0)),
        compiler_params=pltpu.CompilerParams(dimension_semantics=("arbitrary",), vmem_limit_bytes=VMEM_LIMIT),
        name="compress",
    )(x, pos, w1, w2)


def _mixa_kernel(sink_ref, q_ref, kvp_ref, kvc_ref, pfp_ref, pfc_ref, qf_ref, g_ref, o_ref, *, slopes):
    i = pl.program_id(1)
    span = 2 * Q_BLOCK
    row = lax.broadcasted_iota(jnp.int32, (Q_BLOCK, span), 0)
    col = lax.broadcasted_iota(jnp.int32, (Q_BLOCK, span), 1)
    dist = row + Q_BLOCK - col
    band = (dist >= 0) & (dist < A_WINDOW)
    kv_rows = jnp.concatenate([kvp_ref[0], kvc_ref[0]], axis=0)
    pf_rows = jnp.concatenate([pfp_ref[...], pfc_ref[...]], axis=0)

    n_sub = A_TILE // Q_BLOCK
    kvs, scores = [], []
    for r in range(n_sub):
        q = q_ref[0, r * Q_BLOCK:(r + 1) * Q_BLOCK, :]
        kvs.append(kv_rows[r * Q_BLOCK:r * Q_BLOCK + span])
        kx = _with_key_features(kvs[r], pf_rows[r * Q_BLOCK:r * Q_BLOCK + span])
        scores.append(_dot_nt(_stack_heads(q, A_HEADS, qf_ref[...]), kx))
    for r in range(n_sub):
        kv, s_all = kvs[r], scores[r]
        valid = band & ((col >= Q_BLOCK) | (i > 0)) if r == 0 else band
        t_row = (i * A_TILE + r * Q_BLOCK + lax.broadcasted_iota(jnp.int32, (Q_BLOCK, 1), 0)).astype(F32)
        probs, inv_l = [], []
        for hd in range(A_HEADS):
            sink = sink_ref[hd] * LOG2E + slopes[hd] * t_row
            s = jnp.where(valid, s_all[hd * Q_BLOCK:(hd + 1) * Q_BLOCK], NEG_INF)
            m = jnp.maximum(jnp.max(s, axis=-1, keepdims=True), sink)
            e = jnp.exp2(s - m)
            inv_l.append(1.0 / (jnp.sum(e, axis=-1, keepdims=True) + jnp.exp2(sink - m)))
            probs.append(e.astype(BF16))
        o_all = _dot(jnp.concatenate(probs, axis=0), kv)
        o_all = jnp.concatenate([o_all[hd * Q_BLOCK:(hd + 1) * Q_BLOCK] * inv_l[hd] for hd in range(A_HEADS)], axis=0)
        o_ref[0, r * Q_BLOCK:(r + 1) * Q_BLOCK, :] = _rms(_unstack_heads(o_all, A_HEADS), g_ref[...]).astype(BF16)


def _mixer_a(sinks, qa, kva, g):
    b, s, _ = qa.shape
    slopes, _ = _alibi_slopes()
    pos_feat = _key_features(np.arange(s))
    sub = A_TILE // Q_BLOCK
    prev_block = lambda i: jnp.maximum(sub * i - 1, 0)
    return pl.pallas_call(
        functools.partial(_mixa_kernel, slopes=slopes),
        out_shape=jax.ShapeDtypeStruct((b, s, A_Q), BF16),
        grid=(b, s // A_TILE),
        in_specs=[pl.BlockSpec(memory_space=pltpu.SMEM),
                  pl.BlockSpec((1, A_TILE, A_Q), lambda bi, i: (bi, i, 0)),
                  pl.BlockSpec((1, Q_BLOCK, LANES), lambda bi, i: (bi, prev_block(i), 0)),
                  pl.BlockSpec((1, A_TILE, LANES), lambda bi, i: (bi, i, 0)),
                  pl.BlockSpec((Q_BLOCK, LANES), lambda bi, i: (prev_block(i), 0)),
                  pl.BlockSpec((A_TILE, LANES), lambda bi, i: (i, 0)),
                  pl.BlockSpec((A_HEADS * Q_BLOCK, LANES), lambda bi, i: (0, 0)),
                  pl.BlockSpec((1, A_Q), lambda bi, i: (0, 0))],
        out_specs=pl.BlockSpec((1, A_TILE, A_Q), lambda bi, i: (bi, i, 0)),
        compiler_params=pltpu.CompilerParams(dimension_semantics=("arbitrary", "arbitrary"),
                                             vmem_limit_bytes=VMEM_LIMIT),
        name="mixer_a",
    )(sinks, qa, kva, kva, pos_feat, pos_feat, _query_features(slopes, Q_BLOCK), g)


def _mixb_kernel(q_ref, kvs_ref, kvw_ref, kvc_ref, gate_ref, pf_ref, cf_ref, qf_ref, mmap_ref, exp_ref, rep_ref,
                 g_ref, o_ref, acc_ref, m_ref, *, n_cmp, n_slc):
    i = pl.program_id(1)
    t0 = i * Q_WIDE
    q_all = _stack_heads(q_ref[0], B_HEADS, qf_ref[...])

    def head_rows(x, hd, rows=Q_WIDE):
        return x[hd * rows:(hd + 1) * rows]

    kvc = kvc_ref[0]
    n_pad = kvc.shape[0]
    s_cmp = _dot_nt(q_all, _with_key_features(kvc, cf_ref[...]))
    wspan = B_WINDOW + Q_BLOCK
    row_w = lax.broadcasted_iota(jnp.int32, (Q_BLOCK, wspan), 0)
    col_w = lax.broadcasted_iota(jnp.int32, (Q_BLOCK, wspan), 1)
    n_sub = Q_WIDE // Q_BLOCK
    kvw, valid_w, s_win = [], [], []
    for r in range(n_sub):
        r0 = t0 + r * Q_BLOCK
        w0 = pl.multiple_of(jnp.clip(r0 - B_WINDOW, 0, kvw_ref.shape[1] - wspan), LANES)
        kvw.append(kvw_ref[0, pl.ds(w0, wspan), :])
        dist_w = (row_w - col_w) + (r0 - w0)
        valid_w.append((dist_w >= 0) & (dist_w < B_WINDOW))
        q_sub = jnp.concatenate([head_rows(q_all, hd)[r * Q_BLOCK:(r + 1) * Q_BLOCK] for hd in range(B_HEADS)], axis=0)
        s_win.append(_dot_nt(q_sub, _with_key_features(kvw[r], pf_ref[pl.ds(w0, wspan), :])))

    sig = jax.nn.sigmoid(gate_ref[0]).astype(BF16)
    gates = [_dot(sig, rep_ref[r]) for r in range(3)]

    row_c = lax.broadcasted_iota(jnp.int32, (Q_WIDE, n_pad), 0)
    col_c = lax.broadcasted_iota(jnp.int32, (Q_WIDE, n_pad), 1)
    vis_c = ((t0 + row_c) >= (col_c * CMP_STRIDE + (CMP_LEN - 1))) & (col_c < n_cmp)
    row1 = lax.broadcasted_iota(jnp.int32, (Q_WIDE, 1), 0)
    any_vis = ((t0 + row1) >= (CMP_LEN - 1)).astype(F32)
    probs = []
    p_sum = jnp.zeros((Q_WIDE, n_pad), F32)
    for hd in range(B_HEADS):
        s = jnp.where(vis_c, head_rows(s_cmp, hd), NEG_INF)
        m = jnp.max(s, axis=-1, keepdims=True)
        e = jnp.exp2(s - m)
        pr = e * (any_vis / jnp.sum(e, axis=-1, keepdims=True))
        p_sum = p_sum + pr
        probs.append(pr.astype(BF16))
    o_cmp = _unstack_heads(_dot(jnp.concatenate(probs, axis=0), kvc), B_HEADS)

    p_hi, p_lo = _split_bf16(p_sum)
    imp = (_dot_nt(mmap_ref[...], p_hi) + _dot_nt(mmap_ref[...], p_lo))[:n_slc]
    blk = lax.broadcasted_iota(jnp.int32, (n_slc, Q_WIDE), 0)
    tq = t0 + lax.broadcasted_iota(jnp.int32, (n_slc, Q_WIDE), 1)
    cur = tq // SLC_LEN
    valid_b = blk <= cur
    forced = (blk == 0) | (blk == cur) | (blk == cur - 1)
    val = jnp.where(forced, FORCE, jnp.where(valid_b, imp, -FORCE))
    rank = jnp.zeros((n_slc, Q_WIDE), F32)
    for j in range(n_slc):
        vj = val[j:j + 1, :]
        ahead = (vj > val) | ((vj == val) & (blk > j))
        rank = rank + ahead.astype(F32)
    sel = ((rank < SLC_TOP) & valid_b).astype(F32)
    sel = jnp.concatenate([sel, jnp.zeros((LANES - n_slc, Q_WIDE), F32)], axis=0)
    sel_q = sel.T.astype(BF16)

    o_sub = [[None] * n_sub for _ in range(B_HEADS)]
    for r in range(n_sub):
        probs = []
        for hd in range(B_HEADS):
            s = jnp.where(valid_w[r], head_rows(s_win[r], hd, Q_BLOCK), NEG_INF)
            probs.append(jnp.exp2(s - jnp.max(s, axis=-1, keepdims=True)).astype(BF16))
        o_all = _dot(jnp.concatenate(probs, axis=0), _with_ones_for_keys(kvw[r]))
        for hd in range(B_HEADS):
            o_sub[hd][r] = head_rows(o_all, hd, Q_BLOCK)
    o_win = _unstack_normalized(jnp.concatenate([o_sub[hd][r] for hd in range(B_HEADS) for r in range(n_sub)], axis=0),
                                B_HEADS)
    ob_rest = gates[0] * o_cmp + gates[2] * o_win

    acc_ref[...] = jnp.zeros(acc_ref.shape, F32)
    m_ref[...] = jnp.full(m_ref.shape, NEG_INF, F32)

    def slc_chunk(c, width):
        k0 = pl.multiple_of(c * SLC_CHUNK, SLC_CHUNK)
        kv = kvs_ref[0, pl.ds(k0, width), :]
        row_k = lax.broadcasted_iota(jnp.int32, (Q_WIDE, width), 0)
        col_k = lax.broadcasted_iota(jnp.int32, (Q_WIDE, width), 1)
        allowed = (_dot(sel_q, exp_ref[c][:, :width]) > 0.5) & ((col_k - row_k) <= (t0 - k0))
        s_all = _dot_nt(q_all, _with_key_features(kv, pf_ref[pl.ds(k0, width), :]))
        probs, alphas = [], []
        for hd in range(B_HEADS):
            s = jnp.where(allowed, head_rows(s_all, hd), NEG_INF)
            m_old = m_ref[hd]
            m_new = jnp.maximum(m_old, jnp.max(s, axis=-1, keepdims=True))
            alpha = jnp.exp2(m_old - m_new)
            m_ref[hd] = m_new
            probs.append(jnp.exp2(s - _lane_tile(m_new, width)).astype(BF16))
            alphas.append(alpha)
        o_all = _dot(jnp.concatenate(probs, axis=0), _with_ones_for_keys(kv))
        for hd in range(B_HEADS):
            acc_ref[hd] = alphas[hd] * acc_ref[hd] + head_rows(o_all, hd)

    n_half = lax.shift_right_logical(t0 + Q_WIDE + SLC_CHUNK // 2 - 1, (SLC_CHUNK // 2).bit_length() - 1)
    n_full = lax.shift_right_logical(n_half, 1)

    def full_chunk(c, carry):
        slc_chunk(c, SLC_CHUNK)
        return carry

    lax.fori_loop(0, n_full, full_chunk, 0)

    @pl.when((n_half & 1) == 1)
    def _():
        slc_chunk(n_full, SLC_CHUNK // 2)

    o_slc = _unstack_normalized(jnp.concatenate([acc_ref[hd] for hd in range(B_HEADS)], axis=0), B_HEADS)

    ob = ob_rest + gates[1] * o_slc
    o_ref[0] = _rms(ob, g_ref[...]).astype(BF16)


def _mixer_b(qb, kvs, kvw, kvc, gate, g):
    b, s, _ = qb.shape
    n_pad = kvc.shape[1]
    n_cmp = s // CMP_STRIDE - 1
    n_slc = s // SLC_LEN
    n_chunks = s // SLC_CHUNK
    _, slopes = _alibi_slopes()
    pos_feat = _key_features(np.arange(s))
    cmp_feat = _key_features(np.minimum(np.arange(n_pad), n_cmp - 1) * CMP_STRIDE + (CMP_LEN - 1))
    cs = np.arange(n_pad)[None, :] * CMP_STRIDE
    ss = np.arange(LANES)[:, None] * SLC_LEN
    ov = np.maximum(0, np.minimum(cs + CMP_LEN, ss + SLC_LEN) - np.maximum(cs, ss)) / CMP_STRIDE
    ov = ov * (np.arange(n_pad)[None, :] < n_cmp) * (np.arange(LANES)[:, None] < n_slc)
    mmap_t = jnp.asarray(ov, BF16)
    key_blk = (np.arange(n_chunks)[:, None, None] * SLC_CHUNK + np.arange(SLC_CHUNK)[None, None, :]) // SLC_LEN
    expand = jnp.asarray(key_blk == np.arange(LANES)[None, :, None], BF16)
    rep = np.zeros((3, LANES, B_Q), np.float32)
    for r in range(3):
        for h in range(B_HEADS):
            rep[r, 3 * h + r, h * HEAD_DIM:(h + 1) * HEAD_DIM] = 1.0
    rep = jnp.asarray(rep, BF16)
    return pl.pallas_call(
        functools.partial(_mixb_kernel, n_cmp=n_cmp, n_slc=n_slc),
        out_shape=jax.ShapeDtypeStruct((b, s, B_Q), BF16),
        grid=(b, s // Q_WIDE),
        in_specs=[pl.BlockSpec((1, Q_WIDE, B_Q), lambda bi, i: (bi, i, 0)),
                  pl.BlockSpec((1, s, LANES), lambda bi, i: (bi, 0, 0)),
                  pl.BlockSpec((1, s, LANES), lambda bi, i: (bi, 0, 0)),
                  pl.BlockSpec((1, n_pad, LANES), lambda bi, i: (bi, 0, 0)),
                  pl.BlockSpec((1, Q_WIDE, LANES), lambda bi, i: (bi, i, 0)),
                  pl.BlockSpec((s, LANES), lambda bi, i: (0, 0)),
                  pl.BlockSpec((n_pad, LANES), lambda bi, i: (0, 0)),
                  pl.BlockSpec((B_HEADS * Q_WIDE, LANES), lambda bi, i: (0, 0)),
                  pl.BlockSpec((LANES, n_pad), lambda bi, i: (0, 0)),
                  pl.BlockSpec((n_chunks, LANES, SLC_CHUNK), lambda bi, i: (0, 0, 0)),
                  pl.BlockSpec((3, LANES, B_Q), lambda bi, i: (0, 0, 0)),
                  pl.BlockSpec((1, B_Q), lambda bi, i: (0, 0))],
        out_specs=pl.BlockSpec((1, Q_WIDE, B_Q), lambda bi, i: (bi, i, 0)),
        scratch_shapes=[pltpu.VMEM((B_HEADS, Q_WIDE, LANES), F32),
                        pltpu.VMEM((B_HEADS, Q_WIDE, LANES), F32)],
        compiler_params=pltpu.CompilerParams(dimension_semantics=("arbitrary", "arbitrary"),
                                             vmem_limit_bytes=VMEM_LIMIT),
        name="mixer_b",
    )(qb, kvs, kvw, kvc, gate, pos_feat, cmp_feat, _query_features(slopes, Q_WIDE), mmap_t, expand, rep, g)


def _mixc_kernel(q_ref, k_ref, v_ref, tri_ref, g_ref, o_ref, acc_ref, carry_ref, z_ref, w_ref):
    i = pl.program_id(1)
    t0 = i * Q_WIDE
    nb = q_ref.shape[0]
    n_pair = C_HEADS // 2
    q_pairs = [qp for bb in range(nb) for qp in _stack_pairs(q_ref[bb], C_HEADS)]
    acc_ref[...] = jnp.zeros(acc_ref.shape, F32)
    carry_ref[...] = jnp.zeros(carry_ref.shape, F32)
    row = lax.broadcasted_iota(jnp.int32, (Q_WIDE, KEY_CHUNK), 0)
    col = lax.broadcasted_iota(jnp.int32, (Q_WIDE, KEY_CHUNK), 1)
    cr = col - row
    n_chunks = (t0 + Q_WIDE + KEY_CHUNK - 1) // KEY_CHUNK

    def chunk_rows(ref, bb, c):
        return ref[bb, pl.ds(pl.multiple_of(c * KEY_CHUNK, KEY_CHUNK), KEY_CHUNK), :]

    def logits_to_scratch(c):
        for bb in range(nb):
            k = chunk_rows(k_ref, bb, c)
            for p in range(n_pair):
                z_ref[bb * n_pair + p] = _dot_nt(q_pairs[bb * n_pair + p], k[:, p * LANES:(p + 1) * LANES])

    def add_values(c):
        for bb in range(nb):
            v = chunk_rows(v_ref, bb, c)
            for p in range(n_pair):
                o = _dot(w_ref[bb * n_pair + p], v[:, p * LANES:(p + 1) * LANES])
                g = bb * C_HEADS + 2 * p
                acc_ref[g] = acc_ref[g] + o[:Q_WIDE]
                acc_ref[g + 1] = acc_ref[g + 1] + o[Q_WIDE:]

    def sweep(c, diagonal):
        if not diagonal:
            add_values(c + 1)
        past = cr < (t0 - c * KEY_CHUNK)
        n_all = nb * C_HEADS
        log_w, parts = [], []
        for g in range(n_all):
            z = z_ref[g // 2, (g % 2) * Q_WIDE:(g % 2 + 1) * Q_WIDE, :]
            neg_abs = lax.bitcast_convert_type(lax.bitcast_convert_type(z, jnp.uint32) | jnp.uint32(0x80000000), F32)
            drop = jnp.maximum(z, 0.0) + jnp.log2(1.0 + jnp.exp2(neg_abs))
            dk = jnp.where(past, drop, 0.0) if diagonal else drop
            parts.append(dk.astype(BF16))
            log_w.append(z - drop - _lane_tile(carry_ref[g], KEY_CHUNK))
            carry_ref[g] = carry_ref[g] + jnp.sum(dk, axis=-1, keepdims=True)
        tails = _dot(jnp.concatenate(parts, axis=0), tri_ref[...])
        logits_to_scratch(jnp.maximum(c - 1, 0))
        weights = []
        for g in range(n_all):
            w = jnp.exp2(log_w[g] - tails[g * Q_WIDE:(g + 1) * Q_WIDE])
            weights.append((jnp.where(past, w, 0.0) if diagonal else w).astype(BF16))
        for u in range(nb * n_pair):
            w_ref[u] = jnp.concatenate(weights[2 * u:2 * u + 2], axis=0)

    logits_to_scratch(n_chunks - 1)
    sweep(n_chunks - 1, True)

    def earlier(j, carry):
        sweep(n_chunks - 1 - j, False)
        return carry

    lax.fori_loop(1, n_chunks, earlier, 0)
    add_values(0)
    lane = lax.broadcasted_iota(jnp.int32, (Q_WIDE, LANES), 1)
    lo = lane < HEAD_DIM
    for bb in range(nb):
        oc = jnp.concatenate([jnp.where(lo, acc_ref[bb * C_HEADS + 2 * p], acc_ref[bb * C_HEADS + 2 * p + 1])
                              for p in range(n_pair)], axis=1)
        o_ref[bb] = _rms(oc, g_ref[...]).astype(BF16)


def _mixer_c(qc, kc, vc, g):
    b, s, _ = qc.shape
    tri = jnp.asarray(np.arange(KEY_CHUNK)[:, None] > np.arange(KEY_CHUNK)[None, :], BF16)
    nb = C_BATCH if b % C_BATCH == 0 else 1
    return pl.pallas_call(
        _mixc_kernel,
        out_shape=jax.ShapeDtypeStruct((b, s, C_W), BF16),
        grid=(b // nb, s // Q_WIDE),
        in_specs=[pl.BlockSpec((nb, Q_WIDE, C_W), lambda bi, i: (bi, i, 0)),
                  pl.BlockSpec((nb, s, C_W), lambda bi, i: (bi, 0, 0)),
                  pl.BlockSpec((nb, s, C_W), lambda bi, i: (bi, 0, 0)),
                  pl.BlockSpec((KEY_CHUNK, KEY_CHUNK), lambda bi, i: (0, 0)),
                  pl.BlockSpec((1, C_W), lambda bi, i: (0, 0))],
        out_specs=pl.BlockSpec((nb, Q_WIDE, C_W), lambda bi, i: (bi, i, 0)),
        scratch_shapes=[pltpu.VMEM((nb * C_HEADS, Q_WIDE, LANES), F32),
                        pltpu.VMEM((nb * C_HEADS, Q_WIDE, LANES), F32),
                        pltpu.VMEM((nb * C_HEADS // 2, 2 * Q_WIDE, KEY_CHUNK), F32),
                        pltpu.VMEM((nb * C_HEADS // 2, 2 * Q_WIDE, KEY_CHUNK), BF16)],
        compiler_params=pltpu.CompilerParams(dimension_semantics=("arbitrary", "arbitrary"),
                                             vmem_limit_bytes=VMEM_LIMIT),
        name="mixer_c",
    )(qc, kc, vc, tri, g)


def _outffn_kernel(x_ref, ma_ref, mb_ref, mc_ref, wo_ref, gf_ref, wg_ref, wu_ref, wd_ref, gl_ref, o_ref, *, final):
    x = (x_ref[...] + _dot(ma_ref[...], wo_ref[:A_Q, :]) + _dot(mb_ref[...], wo_ref[A_Q:A_Q + B_Q, :])
         + _dot(mc_ref[...], wo_ref[A_Q + B_Q:, :]))
    h = _rms(x, gf_ref[...]).astype(BF16)
    o_ref[...] = x
    for c in range(D_FF // FF_CHUNK):
        sl = slice(c * FF_CHUNK, (c + 1) * FF_CHUNK)
        gate = _dot(h, wg_ref[:, sl])
        up = _dot(h, wu_ref[:, sl])
        o_ref[...] += _dot((jax.nn.silu(gate) * up).astype(BF16), wd_ref[sl, :])
    if final:
        o_ref[...] = _rms(o_ref[...], gl_ref[...])


def _outffn(x2, ma, mb, mc, wo, gf, wg, wu, wd, gl, final):
    t = x2.shape[0]
    tm = TOKEN_TILE
    const = lambda i: (0, 0)
    tok = lambda i: (i, 0)
    return pl.pallas_call(
        functools.partial(_outffn_kernel, final=final),
        out_shape=jax.ShapeDtypeStruct((t, D_MODEL), F32),
        grid=(t // tm,),
        in_specs=[pl.BlockSpec((tm, D_MODEL), tok),
                  pl.BlockSpec((tm, A_Q), tok),
                  pl.BlockSpec((tm, B_Q), tok),
                  pl.BlockSpec((tm, C_W), tok),
                  pl.BlockSpec((MIX_WIDTH, D_MODEL), const),
                  pl.BlockSpec((1, D_MODEL), const),
                  pl.BlockSpec((D_MODEL, D_FF), const),
                  pl.BlockSpec((D_MODEL, D_FF), const),
                  pl.BlockSpec((D_FF, D_MODEL), const),
                  pl.BlockSpec((1, D_MODEL), const)],
        out_specs=pl.BlockSpec((tm, D_MODEL), tok),
        compiler_params=pltpu.CompilerParams(dimension_semantics=("arbitrary",), vmem_limit_bytes=VMEM_LIMIT),
        name="outproj_ffn",
    )(x2, ma, mb, mc, wo, gf, wg, wu, wd, gl)


def _regroup_w_in(w_in):
    sizes = (A_Q, HEAD_DIM, HEAD_DIM, B_Q, HEAD_DIM, HEAD_DIM, HEAD_DIM, HEAD_DIM, HEAD_DIM, HEAD_DIM,
             N_GATES, C_W, C_W, C_W)
    offs = np.concatenate([[0], np.cumsum(sizes)])
    qa, ka, va, qb, kcb, vcb, ksb, vsb, kwb, vwb, gb, qc, kc, vc = [w_in[:, offs[j]:offs[j + 1]]
                                                                     for j in range(len(sizes))]
    pad = jnp.zeros((w_in.shape[0], LANES - N_GATES), w_in.dtype)
    cols = [qa * SCALE, ka, va, qb * SCALE, ksb, vsb, kwb, vwb, kcb, vcb, gb, pad, qc * SCALE, kc, vc]
    return jnp.concatenate(cols, axis=1).astype(BF16)


def _layer(x2, b, s, w_in, w_out, g_attn, g_ffn, g_out_a, g_out_b, g_out_c, sinks, cmp_pos, cmp_w1, cmp_w2,
           w_gate, w_up, w_down, g_final, final):
    row = lambda v: v.reshape(1, -1)
    qa, kva, qb, kvs, kvw, kvr, gate, qc, kc, vc = _inproj(x2, row(g_attn), _regroup_w_in(w_in))
    r3 = lambda a: a.reshape(b, s, a.shape[-1])

    half = CMP_LEN // 2
    pos = jnp.concatenate([cmp_pos[0], cmp_pos[1]], axis=-1).reshape(2, half, 1, LANES)
    z1 = jnp.zeros_like(cmp_w1[0])
    w1 = jnp.concatenate([jnp.concatenate([cmp_w1[0], z1], axis=-1), jnp.concatenate([z1, cmp_w1[1]], axis=-1)], axis=1)
    w1 = w1.reshape(2, half, LANES, 2 * CMP_HIDDEN).astype(BF16)
    z2 = jnp.zeros_like(cmp_w2[0])
    w2 = jnp.concatenate([jnp.concatenate([cmp_w2[0], z2], axis=1), jnp.concatenate([z2, cmp_w2[1]], axis=1)],
                         axis=0).astype(BF16)
    kvc = _compress(r3(kvr), pos, w1, w2)

    ma = _mixer_a(sinks, r3(qa), r3(kva), row(g_out_a))
    mb = _mixer_b(r3(qb), r3(kvs), r3(kvw), kvc, r3(gate), row(g_out_b))
    mc = _mixer_c(r3(qc), r3(kc), r3(vc), row(g_out_c))
    t = b * s
    return _outffn(x2, ma.reshape(t, A_Q), mb.reshape(t, B_Q), mc.reshape(t, C_W), w_out.astype(BF16),
                   row(g_ffn), w_gate.astype(BF16), w_up.astype(BF16), w_down.astype(BF16), row(g_final), final)


def kernel(x, w_in, w_out, g_attn, g_ffn, g_out_a, g_out_b, g_out_c, sinks, cmp_pos, cmp_w1, cmp_w2, w_gate, w_up,
           w_down, g_final):
    b, s, d = x.shape
    depth = w_in.shape[0]
    assert d == D_MODEL and s % KEY_CHUNK == 0 and (b * s) % TOKEN_TILE == 0 and (b * s) % INPROJ_TILE == 0
    assert s >= B_WINDOW + Q_WIDE and s % SLC_CHUNK == 0 and s % A_TILE == 0 and (s // CMP_STRIDE) % LANES == 0
    x2 = x.reshape(b * s, d)
    for l in range(depth):
        x2 = _layer(x2, b, s, w_in[l], w_out[l], g_attn[l], g_ffn[l], g_out_a[l], g_out_b[l], g_out_c[l], sinks[l],
                    cmp_pos[l], cmp_w1[l], cmp_w2[l], w_gate[l], w_up[l], w_down[l], g_final, l == depth - 1)
    return x2.reshape(b, s, d)
```

```python
import functools

import jax
import jax.numpy as jnp
import numpy as np
from jax import lax
from jax.experimental import pallas as pl
from jax.experimental.pallas import tpu as pltpu

F32 = jnp.float32
BF16 = jnp.bfloat16

D_MODEL = 1024
HEAD_DIM = 64
LANES = 128
Q_BLOCK = 128
Q_WIDE = 256
A_TILE = 256
B_BATCH = 4
C_BATCH = 4
A_HEADS = 8
A_WINDOW = 128
B_HEADS = 4
CMP_LEN = 32
CMP_STRIDE = 16
CMP_HIDDEN = 128
SLC_LEN = 64
SLC_TOP = 8
B_WINDOW = 512
C_HEADS = 4
A_Q = A_HEADS * HEAD_DIM
B_Q = B_HEADS * HEAD_DIM
C_W = C_HEADS * HEAD_DIM
MIX_WIDTH = A_Q + B_Q + C_W
D_FF = 2816
N_GATES = B_HEADS * 3
NEG_INF = -1e30
FORCE = 1e4
EPS = 1e-6
SCALE = HEAD_DIM ** -0.5
LOG2E = 1.4426950408889634

KEY_CHUNK = 256
SLC_CHUNK = 512
FF_CHUNK = 256
TOKEN_TILE = 512
INPROJ_TILE = 1024
VMEM_LIMIT = 56 * 1024 * 1024

_G_QA, _G_KVA, _G_QB, _G_KVS, _G_KVW, _G_KVC, _G_GATE, _G_QC, _G_KC, _G_VC, IN_PAD = (
    0, 512, 640, 896, 1024, 1152, 1280, 1408, 1664, 1920, 2176)


def _alibi_slopes():
    n = A_HEADS + B_HEADS
    sl = 2.0 ** (-8.0 * np.arange(1, n + 1) / n) * LOG2E
    return [float(v) for v in sl[:A_HEADS]], [float(v) for v in sl[A_HEADS:]]


def _round_bf16(x):
    bits = np.ascontiguousarray(x, np.float32).view(np.uint32).astype(np.uint64)
    bits = ((bits + 0x7FFF + ((bits >> 16) & 1)) >> 16) << 16
    return bits.astype(np.uint32).view(np.float32)


def _key_features(pos):
    pos = np.asarray(pos, np.int64)
    assert pos.max() < 2048
    f = np.zeros((len(pos), LANES), np.float32)
    for j in range(3):
        f[:, HEAD_DIM + 2 * j] = 256 * (pos // 256)
        f[:, HEAD_DIM + 2 * j + 1] = pos % 256
    return jnp.asarray(f, BF16)


def _query_features(slopes, rows):
    f = np.zeros((len(slopes) * rows, LANES), np.float32)
    for h, slope in enumerate(slopes):
        rest = np.float32(slope)
        for j in range(3):
            part = _round_bf16(rest)
            rest = np.float32(rest - part)
            f[h * rows:(h + 1) * rows, HEAD_DIM + 2 * j:HEAD_DIM + 2 * j + 2] = part
    return jnp.asarray(f, BF16)


def _dot(a, b):
    return jnp.dot(a, b, preferred_element_type=F32)


def _dot_nt(a, b):
    return lax.dot_general(a, b, (((1,), (1,)), ((), ())), preferred_element_type=F32)


def _split_bf16(x):
    hi = x.astype(BF16)
    lo = (x - hi.astype(F32)).astype(BF16)
    return hi, lo


def _rms(x, g):
    ms = jnp.mean(x * x, axis=-1, keepdims=True)
    return x * lax.rsqrt(ms + EPS) * g


def _lane_tile(x, width):
    return jnp.concatenate([x] * (width // LANES), axis=1)


def _swap_halves(x):
    return jnp.concatenate([x[:, HEAD_DIM:], x[:, :HEAD_DIM]], axis=1)


def _stack_heads(q, n_heads, feat):
    rows = q.shape[0]
    lo = lax.broadcasted_iota(jnp.int32, (rows, LANES), 1) < HEAD_DIM
    tiles = []
    for p in range(n_heads // 2):
        pair = q[:, p * LANES:(p + 1) * LANES]
        tiles.append(jnp.where(lo, pair, feat[(2 * p) * rows:(2 * p + 1) * rows]))
        tiles.append(jnp.where(lo, _swap_halves(pair), feat[(2 * p + 1) * rows:(2 * p + 2) * rows]))
    return jnp.concatenate(tiles, axis=0)


def _with_key_features(kv, feat):
    lo = lax.broadcasted_iota(jnp.int32, kv.shape, 1) < HEAD_DIM
    return jnp.where(lo, kv, feat)


def _stack_pairs(q, n_heads):
    rows = q.shape[0]
    lo = lax.broadcasted_iota(jnp.int32, (rows, LANES), 1) < HEAD_DIM
    out = []
    for p in range(n_heads // 2):
        pair = q[:, p * LANES:(p + 1) * LANES]
        zero = jnp.zeros_like(pair)
        out.append(jnp.concatenate([jnp.where(lo, pair, zero), jnp.where(lo, zero, pair)], axis=0))
    return out


def _with_ones_for_keys(kv):
    lo = lax.broadcasted_iota(jnp.int32, kv.shape, 1) < HEAD_DIM
    return jnp.where(lo, jnp.ones_like(kv), kv)


def _unstack_normalized(o, n_heads):
    rows = o.shape[0] // n_heads
    lo = lax.broadcasted_iota(jnp.int32, (rows, LANES), 1) < HEAD_DIM
    pairs = []
    for p in range(n_heads // 2):
        even = o[(2 * p) * rows:(2 * p + 1) * rows]
        odd = o[(2 * p + 1) * rows:(2 * p + 2) * rows]
        pairs.append(jnp.where(lo, pltpu.roll(even, HEAD_DIM, 1) / even, odd / pltpu.roll(odd, HEAD_DIM, 1)))
    return jnp.concatenate(pairs, axis=1)


def _unstack_heads(o, n_heads):
    rows = o.shape[0] // n_heads
    lo = lax.broadcasted_iota(jnp.int32, (rows, LANES), 1) < HEAD_DIM
    pairs = []
    for p in range(n_heads // 2):
        even = o[(2 * p) * rows:(2 * p + 1) * rows]
        odd = o[(2 * p + 1) * rows:(2 * p + 2) * rows]
        pairs.append(jnp.where(lo, pltpu.roll(even, HEAD_DIM, 1), odd))
    return jnp.concatenate(pairs, axis=1)


def _inproj_kernel(x_ref, g_ref, w_ref, qa_ref, kva_ref, qb_ref, kvs_ref, kvw_ref, kvr_ref,
                   gate_ref, qc_ref, kc_ref, vc_ref):
    h = _rms(x_ref[...], g_ref[...]).astype(BF16)

    def proj(lo, hi):
        return _dot(h, w_ref[:, lo:hi])

    qa_ref[...] = (proj(_G_QA, _G_KVA) * LOG2E).astype(BF16)
    kva_ref[...] = proj(_G_KVA, _G_QB).astype(BF16)
    qb_ref[...] = (proj(_G_QB, _G_KVS) * LOG2E).astype(BF16)
    kvs_ref[...] = proj(_G_KVS, _G_KVW).astype(BF16)
    kvw_ref[...] = proj(_G_KVW, _G_KVC).astype(BF16)
    kvr_ref[...] = proj(_G_KVC, _G_GATE)
    gate_ref[...] = proj(_G_GATE, _G_QC)
    qc_ref[...] = (proj(_G_QC, _G_KC) * LOG2E).astype(BF16)
    kc_ref[...] = proj(_G_KC, _G_VC).astype(BF16)
    vc_ref[...] = proj(_G_VC, IN_PAD).astype(BF16)


def _inproj(x2, g, w):
    t = x2.shape[0]
    tm = INPROJ_TILE
    widths = [(A_Q, BF16), (LANES, BF16), (B_Q, BF16), (LANES, BF16), (LANES, BF16), (LANES, F32),
              (LANES, F32), (C_W, BF16), (C_W, BF16), (C_W, BF16)]
    return pl.pallas_call(
        _inproj_kernel,
        out_shape=[jax.ShapeDtypeStruct((t, wd), dt) for wd, dt in widths],
        grid=(t // tm,),
        in_specs=[pl.BlockSpec((tm, D_MODEL), lambda i: (i, 0)),
                  pl.BlockSpec((1, D_MODEL), lambda i: (0, 0)),
                  pl.BlockSpec((D_MODEL, IN_PAD), lambda i: (0, 0))],
        out_specs=[pl.BlockSpec((tm, wd), lambda i: (i, 0)) for wd, _ in widths],
        compiler_params=pltpu.CompilerParams(dimension_semantics=("arbitrary",), vmem_limit_bytes=VMEM_LIMIT),
        name="inproj",
    )(x2, g, w)


def _compress_kernel(x_ref, pos_ref, w1_ref, w2_ref, o_ref):
    half = CMP_LEN // 2
    n = x_ref.shape[1] // CMP_STRIDE
    h_lo = jnp.zeros((n, 2 * CMP_HIDDEN), F32)
    h_hi = jnp.zeros((n, 2 * CMP_HIDDEN), F32)
    for l in range(half):
        x = x_ref[0, pl.ds(l, n, stride=CMP_STRIDE), :]
        h_lo = h_lo + _dot((x + pos_ref[0, l]).astype(BF16), w1_ref[0, l])
        h_hi = h_hi + _dot((x + pos_ref[1, l]).astype(BF16), w1_ref[1, l])
    h = h_lo + pltpu.roll(h_hi, n - 1, 0)
    o_ref[0] = _dot(jax.nn.gelu(h).astype(BF16), w2_ref[...]).astype(BF16)


def _compress(x, pos, w1, w2):
    b, s, _ = x.shape
    n = s // CMP_STRIDE
    half = CMP_LEN // 2
    return pl.pallas_call(
        _compress_kernel,
        out_shape=jax.ShapeDtypeStruct((b, n, LANES), BF16),
        grid=(b,),
        in_specs=[pl.BlockSpec((1, s, LANES), lambda i: (i, 0, 0)),
                  pl.BlockSpec((2, half, 1, LANES), lambda i: (0, 0, 0, 0)),
                  pl.BlockSpec((2, half, LANES, 2 * CMP_HIDDEN), lambda i: (0, 0, 0, 0)),
                  pl.BlockSpec((2 * CMP_HIDDEN, LANES), lambda i: (0, 0))],
        out_specs=pl.BlockSpec((1, n, LANES), lambda i: (i, 0, 0)),
        compiler_params=pltpu.CompilerParams(dimension_semantics=("arbitrary",), vmem_limit_bytes=VMEM_LIMIT),
        name="compress",
    )(x, pos, w1, w2)


def _mixa_kernel(sink_ref, q_ref, kvp_ref, kvc_ref, pfp_ref, pfc_ref, qf_ref, g_ref, o_ref, *, slopes):
    i = pl.program_id(1)
    span = 2 * Q_BLOCK
    row = lax.broadcasted_iota(jnp.int32, (Q_BLOCK, span), 0)
    col = lax.broadcasted_iota(jnp.int32, (Q_BLOCK, span), 1)
    dist = row + Q_BLOCK - col
    band = (dist >= 0) & (dist < A_WINDOW)
    kv_rows = jnp.concatenate([kvp_ref[0], kvc_ref[0]], axis=0)
    pf_rows = jnp.concatenate([pfp_ref[...], pfc_ref[...]], axis=0)

    n_sub = A_TILE // Q_BLOCK
    kvs, scores = [], []
    for r in range(n_sub):
        q = q_ref[0, r * Q_BLOCK:(r + 1) * Q_BLOCK, :]
        kvs.append(kv_rows[r * Q_BLOCK:r * Q_BLOCK + span])
        kx = _with_key_features(kvs[r], pf_rows[r * Q_BLOCK:r * Q_BLOCK + span])
        scores.append(_dot_nt(_stack_heads(q, A_HEADS, qf_ref[...]), kx))
    for r in range(n_sub):
        kv, s_all = kvs[r], scores[r]
        valid = band & ((col >= Q_BLOCK) | (i > 0)) if r == 0 else band
        t_row = (i * A_TILE + r * Q_BLOCK + lax.broadcasted_iota(jnp.int32, (Q_BLOCK, 1), 0)).astype(F32)
        probs, inv_l = [], []
        for hd in range(A_HEADS):
            sink = sink_ref[hd] * LOG2E + slopes[hd] * t_row
            s = jnp.where(valid, s_all[hd * Q_BLOCK:(hd + 1) * Q_BLOCK], NEG_INF)
            m = jnp.maximum(jnp.max(s, axis=-1, keepdims=True), sink)
            e = jnp.exp2(s - m)
            inv_l.append(1.0 / (jnp.sum(e, axis=-1, keepdims=True) + jnp.exp2(sink - m)))
            probs.append(e.astype(BF16))
        o_all = _dot(jnp.concatenate(probs, axis=0), kv)
        o_all = jnp.concatenate([o_all[hd * Q_BLOCK:(hd + 1) * Q_BLOCK] * inv_l[hd] for hd in range(A_HEADS)], axis=0)
        o_ref[0, r * Q_BLOCK:(r + 1) * Q_BLOCK, :] = _rms(_unstack_heads(o_all, A_HEADS), g_ref[...]).astype(BF16)


def _mixer_a(sinks, qa, kva, g):
    b, s, _ = qa.shape
    slopes, _ = _alibi_slopes()
    pos_feat = _key_features(np.arange(s))
    sub = A_TILE // Q_BLOCK
    prev_block = lambda i: jnp.maximum(sub * i - 1, 0)
    return pl.pallas_call(
        functools.partial(_mixa_kernel, slopes=slopes),
        out_shape=jax.ShapeDtypeStruct((b, s, A_Q), BF16),
        grid=(b, s // A_TILE),
        in_specs=[pl.BlockSpec(memory_space=pltpu.SMEM),
                  pl.BlockSpec((1, A_TILE, A_Q), lambda bi, i: (bi, i, 0)),
                  pl.BlockSpec((1, Q_BLOCK, LANES), lambda bi, i: (bi, prev_block(i), 0)),
                  pl.BlockSpec((1, A_TILE, LANES), lambda bi, i: (bi, i, 0)),
                  pl.BlockSpec((Q_BLOCK, LANES), lambda bi, i: (prev_block(i), 0)),
                  pl.BlockSpec((A_TILE, LANES), lambda bi, i: (i, 0)),
                  pl.BlockSpec((A_HEADS * Q_BLOCK, LANES), lambda bi, i: (0, 0)),
                  pl.BlockSpec((1, A_Q), lambda bi, i: (0, 0))],
        out_specs=pl.BlockSpec((1, A_TILE, A_Q), lambda bi, i: (bi, i, 0)),
        compiler_params=pltpu.CompilerParams(dimension_semantics=("arbitrary", "arbitrary"),
                                             vmem_limit_bytes=VMEM_LIMIT),
        name="mixer_a",
    )(sinks, qa, kva, kva, pos_feat, pos_feat, _query_features(slopes, Q_BLOCK), g)


def _mixb_kernel(q_ref, kvs_ref, kvw_ref, kvc_ref, gate_ref, pf_ref, cf_ref, qf_ref, mmap_ref, exp_ref, rep_ref,
                 g_ref, o_ref, acc_ref, m_ref, *, n_cmp, n_slc):
    i = pl.program_id(1)
    t0 = i * Q_WIDE
    nb = q_ref.shape[0]
    q_all = [_stack_heads(q_ref[bb], B_HEADS, qf_ref[...]) for bb in range(nb)]

    def head_rows(x, hd, rows=Q_WIDE):
        return x[hd * rows:(hd + 1) * rows]

    kvc = [kvc_ref[bb] for bb in range(nb)]
    n_pad = kvc[0].shape[0]
    s_cmp = [_dot_nt(q_all[bb], _with_key_features(kvc[bb], cf_ref[...])) for bb in range(nb)]
    wspan = B_WINDOW + Q_BLOCK
    row_w = lax.broadcasted_iota(jnp.int32, (Q_BLOCK, wspan), 0)
    col_w = lax.broadcasted_iota(jnp.int32, (Q_BLOCK, wspan), 1)
    n_sub = Q_WIDE // Q_BLOCK
    valid_w = []
    kvw = [[None] * n_sub for _ in range(nb)]
    s_win = [[None] * n_sub for _ in range(nb)]
    for r in range(n_sub):
        r0 = t0 + r * Q_BLOCK
        w0 = pl.multiple_of(jnp.clip(r0 - B_WINDOW, 0, kvw_ref.shape[1] - wspan), LANES)
        dist_w = (row_w - col_w) + (r0 - w0)
        valid_w.append((dist_w >= 0) & (dist_w < B_WINDOW))
        for bb in range(nb):
            kvw[bb][r] = kvw_ref[bb, pl.ds(w0, wspan), :]
            q_sub = jnp.concatenate([head_rows(q_all[bb], hd)[r * Q_BLOCK:(r + 1) * Q_BLOCK] for hd in range(B_HEADS)],
                                    axis=0)
            s_win[bb][r] = _dot_nt(q_sub, _with_key_features(kvw[bb][r], pf_ref[pl.ds(w0, wspan), :]))

    gates = []
    for bb in range(nb):
        sig = jax.nn.sigmoid(gate_ref[bb]).astype(BF16)
        gates.append([_dot(sig, rep_ref[r]) for r in range(3)])

    row_c = lax.broadcasted_iota(jnp.int32, (Q_WIDE, n_pad), 0)
    col_c = lax.broadcasted_iota(jnp.int32, (Q_WIDE, n_pad), 1)
    vis_c = ((t0 + row_c) >= (col_c * CMP_STRIDE + (CMP_LEN - 1))) & (col_c < n_cmp)
    row1 = lax.broadcasted_iota(jnp.int32, (Q_WIDE, 1), 0)
    any_vis = ((t0 + row1) >= (CMP_LEN - 1)).astype(F32)
    blk = lax.broadcasted_iota(jnp.int32, (n_slc, Q_WIDE), 0)
    tq = t0 + lax.broadcasted_iota(jnp.int32, (n_slc, Q_WIDE), 1)
    cur = tq // SLC_LEN
    valid_b = blk <= cur
    forced = (blk == 0) | (blk == cur) | (blk == cur - 1)
    o_cmp, sel_q = [], []
    for bb in range(nb):
        probs = []
        p_sum = jnp.zeros((Q_WIDE, n_pad), F32)
        for hd in range(B_HEADS):
            s = jnp.where(vis_c, head_rows(s_cmp[bb], hd), NEG_INF)
            m = jnp.max(s, axis=-1, keepdims=True)
            e = jnp.exp2(s - m)
            pr = e * (any_vis / jnp.sum(e, axis=-1, keepdims=True))
            p_sum = p_sum + pr
            probs.append(pr.astype(BF16))
        o_cmp.append(_unstack_heads(_dot(jnp.concatenate(probs, axis=0), kvc[bb]), B_HEADS))
        p_hi, p_lo = _split_bf16(p_sum)
        imp = (_dot_nt(mmap_ref[...], p_hi) + _dot_nt(mmap_ref[...], p_lo))[:n_slc]
        val = jnp.where(forced, FORCE, jnp.where(valid_b, imp, -FORCE))
        rank = jnp.zeros((n_slc, Q_WIDE), F32)
        for j in range(n_slc):
            vj = val[j:j + 1, :]
            ahead = (vj > val) | ((vj == val) & (blk > j))
            rank = rank + ahead.astype(F32)
        sel = ((rank < SLC_TOP) & valid_b).astype(F32)
        sel = jnp.concatenate([sel, jnp.zeros((LANES - n_slc, Q_WIDE), F32)], axis=0)
        sel_q.append(sel.T.astype(BF16))

    ob_rest = []
    for bb in range(nb):
        o_sub = [[None] * n_sub for _ in range(B_HEADS)]
        for r in range(n_sub):
            probs = []
            for hd in range(B_HEADS):
                s = jnp.where(valid_w[r], head_rows(s_win[bb][r], hd, Q_BLOCK), NEG_INF)
                probs.append(jnp.exp2(s - jnp.max(s, axis=-1, keepdims=True)).astype(BF16))
            o_all = _dot(jnp.concatenate(probs, axis=0), _with_ones_for_keys(kvw[bb][r]))
            for hd in range(B_HEADS):
                o_sub[hd][r] = head_rows(o_all, hd, Q_BLOCK)
        o_win = _unstack_normalized(
            jnp.concatenate([o_sub[hd][r] for hd in range(B_HEADS) for r in range(n_sub)], axis=0), B_HEADS)
        ob_rest.append(gates[bb][0] * o_cmp[bb] + gates[bb][2] * o_win)

    acc_ref[...] = jnp.zeros(acc_ref.shape, F32)
    m_ref[...] = jnp.full(m_ref.shape, NEG_INF, F32)

    def slc_chunk(c, width):
        k0 = pl.multiple_of(c * SLC_CHUNK, SLC_CHUNK)
        row_k = lax.broadcasted_iota(jnp.int32, (Q_WIDE, width), 0)
        col_k = lax.broadcasted_iota(jnp.int32, (Q_WIDE, width), 1)
        causal = (col_k - row_k) <= (t0 - k0)
        feat = pf_ref[pl.ds(k0, width), :]
        for bb in range(nb):
            kv = kvs_ref[bb, pl.ds(k0, width), :]
            allowed = (_dot(sel_q[bb], exp_ref[c][:, :width]) > 0.5) & causal
            s_all = _dot_nt(q_all[bb], _with_key_features(kv, feat))
            probs, alphas = [], []
            for hd in range(B_HEADS):
                g = bb * B_HEADS + hd
                s = jnp.where(allowed, head_rows(s_all, hd), NEG_INF)
                m_old = m_ref[g]
                m_new = jnp.maximum(m_old, jnp.max(s, axis=-1, keepdims=True))
                alpha = jnp.exp2(m_old - m_new)
                m_ref[g] = m_new
                probs.append(jnp.exp2(s - _lane_tile(m_new, width)).astype(BF16))
                alphas.append(alpha)
            o_all = _dot(jnp.concatenate(probs, axis=0), _with_ones_for_keys(kv))
            for hd in range(B_HEADS):
                g = bb * B_HEADS + hd
                acc_ref[g] = alphas[hd] * acc_ref[g] + head_rows(o_all, hd)

    n_half = lax.shift_right_logical(t0 + Q_WIDE + SLC_CHUNK // 2 - 1, (SLC_CHUNK // 2).bit_length() - 1)
    n_full = lax.shift_right_logical(n_half, 1)

    def full_chunk(c, carry):
        slc_chunk(c, SLC_CHUNK)
        return carry

    lax.fori_loop(0, n_full, full_chunk, 0)

    @pl.when((n_half & 1) == 1)
    def _():
        slc_chunk(n_full, SLC_CHUNK // 2)

    for bb in range(nb):
        o_slc = _unstack_normalized(
            jnp.concatenate([acc_ref[bb * B_HEADS + hd] for hd in range(B_HEADS)], axis=0), B_HEADS)
        ob = ob_rest[bb] + gates[bb][1] * o_slc
        o_ref[bb] = _rms(ob, g_ref[...]).astype(BF16)


def _mixer_b(qb, kvs, kvw, kvc, gate, g):
    b, s, _ = qb.shape
    n_pad = kvc.shape[1]
    n_cmp = s // CMP_STRIDE - 1
    n_slc = s // SLC_LEN
    n_chunks = s // SLC_CHUNK
    _, slopes = _alibi_slopes()
    pos_feat = _key_features(np.arange(s))
    cmp_feat = _key_features(np.minimum(np.arange(n_pad), n_cmp - 1) * CMP_STRIDE + (CMP_LEN - 1))
    cs = np.arange(n_pad)[None, :] * CMP_STRIDE
    ss = np.arange(LANES)[:, None] * SLC_LEN
    ov = np.maximum(0, np.minimum(cs + CMP_LEN, ss + SLC_LEN) - np.maximum(cs, ss)) / CMP_STRIDE
    ov = ov * (np.arange(n_pad)[None, :] < n_cmp) * (np.arange(LANES)[:, None] < n_slc)
    mmap_t = jnp.asarray(ov, BF16)
    key_blk = (np.arange(n_chunks)[:, None, None] * SLC_CHUNK + np.arange(SLC_CHUNK)[None, None, :]) // SLC_LEN
    expand = jnp.asarray(key_blk == np.arange(LANES)[None, :, None], BF16)
    rep = np.zeros((3, LANES, B_Q), np.float32)
    for r in range(3):
        for h in range(B_HEADS):
            rep[r, 3 * h + r, h * HEAD_DIM:(h + 1) * HEAD_DIM] = 1.0
    rep = jnp.asarray(rep, BF16)
    nb = B_BATCH if b % B_BATCH == 0 else 1
    return pl.pallas_call(
        functools.partial(_mixb_kernel, n_cmp=n_cmp, n_slc=n_slc),
        out_shape=jax.ShapeDtypeStruct((b, s, B_Q), BF16),
        grid=(b // nb, s // Q_WIDE),
        in_specs=[pl.BlockSpec((nb, Q_WIDE, B_Q), lambda bi, i: (bi, i, 0)),
                  pl.BlockSpec((nb, s, LANES), lambda bi, i: (bi, 0, 0)),
                  pl.BlockSpec((nb, s, LANES), lambda bi, i: (bi, 0, 0)),
                  pl.BlockSpec((nb, n_pad, LANES), lambda bi, i: (bi, 0, 0)),
                  pl.BlockSpec((nb, Q_WIDE, LANES), lambda bi, i: (bi, i, 0)),
                  pl.BlockSpec((s, LANES), lambda bi, i: (0, 0)),
                  pl.BlockSpec((n_pad, LANES), lambda bi, i: (0, 0)),
                  pl.BlockSpec((B_HEADS * Q_WIDE, LANES), lambda bi, i: (0, 0)),
                  pl.BlockSpec((LANES, n_pad), lambda bi, i: (0, 0)),
                  pl.BlockSpec((n_chunks, LANES, SLC_CHUNK), lambda bi, i: (0, 0, 0)),
                  pl.BlockSpec((3, LANES, B_Q), lambda bi, i: (0, 0, 0)),
                  pl.BlockSpec((1, B_Q), lambda bi, i: (0, 0))],
        out_specs=pl.BlockSpec((nb, Q_WIDE, B_Q), lambda bi, i: (bi, i, 0)),
        scratch_shapes=[pltpu.VMEM((nb * B_HEADS, Q_WIDE, LANES), F32),
                        pltpu.VMEM((nb * B_HEADS, Q_WIDE, LANES), F32)],
        compiler_params=pltpu.CompilerParams(dimension_semantics=("arbitrary", "arbitrary"),
                                             vmem_limit_bytes=VMEM_LIMIT),
        name="mixer_b",
    )(qb, kvs, kvw, kvc, gate, pos_feat, cmp_feat, _query_features(slopes, Q_WIDE), mmap_t, expand, rep, g)


def _mixc_kernel(q_ref, k_ref, v_ref, tri_ref, g_ref, o_ref, acc_ref, carry_ref, z_ref, w_ref):
    i = pl.program_id(1)
    t0 = i * Q_WIDE
    nb = q_ref.shape[0]
    n_pair = C_HEADS // 2
    q_pairs = [qp for bb in range(nb) for qp in _stack_pairs(q_ref[bb], C_HEADS)]
    acc_ref[...] = jnp.zeros(acc_ref.shape, F32)
    carry_ref[...] = jnp.zeros(carry_ref.shape, F32)
    row = lax.broadcasted_iota(jnp.int32, (Q_WIDE, KEY_CHUNK), 0)
    col = lax.broadcasted_iota(jnp.int32, (Q_WIDE, KEY_CHUNK), 1)
    cr = col - row
    n_chunks = (t0 + Q_WIDE + KEY_CHUNK - 1) // KEY_CHUNK

    def chunk_rows(ref, bb, c):
        return ref[bb, pl.ds(pl.multiple_of(c * KEY_CHUNK, KEY_CHUNK), KEY_CHUNK), :]

    def logits_to_scratch(c):
        for bb in range(nb):
            k = chunk_rows(k_ref, bb, c)
            for p in range(n_pair):
                z_ref[bb * n_pair + p] = _dot_nt(q_pairs[bb * n_pair + p], k[:, p * LANES:(p + 1) * LANES])

    def add_values(c):
        for bb in range(nb):
            v = chunk_rows(v_ref, bb, c)
            for p in range(n_pair):
                o = _dot(w_ref[bb * n_pair + p], v[:, p * LANES:(p + 1) * LANES])
                g = bb * C_HEADS + 2 * p
                acc_ref[g] = acc_ref[g] + o[:Q_WIDE]
                acc_ref[g + 1] = acc_ref[g + 1] + o[Q_WIDE:]

    def sweep(c, diagonal):
        if not diagonal:
            add_values(c + 1)
        past = cr < (t0 - c * KEY_CHUNK)
        n_all = nb * C_HEADS
        log_w, parts = [], []
        for g in range(n_all):
            z = z_ref[g // 2, (g % 2) * Q_WIDE:(g % 2 + 1) * Q_WIDE, :]
            neg_abs = lax.bitcast_convert_type(lax.bitcast_convert_type(z, jnp.uint32) | jnp.uint32(0x80000000), F32)
            drop = jnp.maximum(z, 0.0) + jnp.log2(1.0 + jnp.exp2(neg_abs))
            dk = jnp.where(past, drop, 0.0) if diagonal else drop
            parts.append(dk.astype(BF16))
            log_w.append(z - drop - _lane_tile(carry_ref[g], KEY_CHUNK))
            carry_ref[g] = carry_ref[g] + jnp.sum(dk, axis=-1, keepdims=True)
        tails = _dot(jnp.concatenate(parts, axis=0), tri_ref[...])
        logits_to_scratch(jnp.maximum(c - 1, 0))
        weights = []
        for g in range(n_all):
            w = jnp.exp2(log_w[g] - tails[g * Q_WIDE:(g + 1) * Q_WIDE])
            weights.append((jnp.where(past, w, 0.0) if diagonal else w).astype(BF16))
        for u in range(nb * n_pair):
            w_ref[u] = jnp.concatenate(weights[2 * u:2 * u + 2], axis=0)

    logits_to_scratch(n_chunks - 1)
    sweep(n_chunks - 1, True)

    def earlier(j, carry):
        sweep(n_chunks - 1 - j, False)
        return carry

    lax.fori_loop(1, n_chunks, earlier, 0)
    add_values(0)
    lane = lax.broadcasted_iota(jnp.int32, (Q_WIDE, LANES), 1)
    lo = lane < HEAD_DIM
    for bb in range(nb):
        oc = jnp.concatenate([jnp.where(lo, acc_ref[bb * C_HEADS + 2 * p], acc_ref[bb * C_HEADS + 2 * p + 1])
                              for p in range(n_pair)], axis=1)
        o_ref[bb] = _rms(oc, g_ref[...]).astype(BF16)


def _mixer_c(qc, kc, vc, g):
    b, s, _ = qc.shape
    tri = jnp.asarray(np.arange(KEY_CHUNK)[:, None] > np.arange(KEY_CHUNK)[None, :], BF16)
    nb = C_BATCH if b % C_BATCH == 0 else 1
    return pl.pallas_call(
        _mixc_kernel,
        out_shape=jax.ShapeDtypeStruct((b, s, C_W), BF16),
        grid=(b // nb, s // Q_WIDE),
        in_specs=[pl.BlockSpec((nb, Q_WIDE, C_W), lambda bi, i: (bi, i, 0)),
                  pl.BlockSpec((nb, s, C_W), lambda bi, i: (bi, 0, 0)),
                  pl.BlockSpec((nb, s, C_W), lambda bi, i: (bi, 0, 0)),
                  pl.BlockSpec((KEY_CHUNK, KEY_CHUNK), lambda bi, i: (0, 0)),
                  pl.BlockSpec((1, C_W), lambda bi, i: (0, 0))],
        out_specs=pl.BlockSpec((nb, Q_WIDE, C_W), lambda bi, i: (bi, i, 0)),
        scratch_shapes=[pltpu.VMEM((nb * C_HEADS, Q_WIDE, LANES), F32),
                        pltpu.VMEM((nb * C_HEADS, Q_WIDE, LANES), F32),
                        pltpu.VMEM((nb * C_HEADS // 2, 2 * Q_WIDE, KEY_CHUNK), F32),
                        pltpu.VMEM((nb * C_HEADS // 2, 2 * Q_WIDE, KEY_CHUNK), BF16)],
        compiler_params=pltpu.CompilerParams(dimension_semantics=("arbitrary", "arbitrary"),
                                             vmem_limit_bytes=VMEM_LIMIT),
        name="mixer_c",
    )(qc, kc, vc, tri, g)


def _outffn_kernel(x_ref, ma_ref, mb_ref, mc_ref, wo_ref, gf_ref, wg_ref, wu_ref, wd_ref, gl_ref, o_ref, *, final):
    x = (x_ref[...] + _dot(ma_ref[...], wo_ref[:A_Q, :]) + _dot(mb_ref[...], wo_ref[A_Q:A_Q + B_Q, :])
         + _dot(mc_ref[...], wo_ref[A_Q + B_Q:, :]))
    h = _rms(x, gf_ref[...]).astype(BF16)
    o_ref[...] = x
    for c in range(D_FF // FF_CHUNK):
        sl = slice(c * FF_CHUNK, (c + 1) * FF_CHUNK)
        gate = _dot(h, wg_ref[:, sl])
        up = _dot(h, wu_ref[:, sl])
        o_ref[...] += _dot((jax.nn.silu(gate) * up).astype(BF16), wd_ref[sl, :])
    if final:
        o_ref[...] = _rms(o_ref[...], gl_ref[...])


def _outffn(x2, ma, mb, mc, wo, gf, wg, wu, wd, gl, final):
    t = x2.shape[0]
    tm = TOKEN_TILE
    const = lambda i: (0, 0)
    tok = lambda i: (i, 0)
    return pl.pallas_call(
        functools.partial(_outffn_kernel, final=final),
        out_shape=jax.ShapeDtypeStruct((t, D_MODEL), F32),
        grid=(t // tm,),
        in_specs=[pl.BlockSpec((tm, D_MODEL), tok),
                  pl.BlockSpec((tm, A_Q), tok),
                  pl.BlockSpec((tm, B_Q), tok),
                  pl.BlockSpec((tm, C_W), tok),
                  pl.BlockSpec((MIX_WIDTH, D_MODEL), const),
                  pl.BlockSpec((1, D_MODEL), const),
                  pl.BlockSpec((D_MODEL, D_FF), const),
                  pl.BlockSpec((D_MODEL, D_FF), const),
                  pl.BlockSpec((D_FF, D_MODEL), const),
                  pl.BlockSpec((1, D_MODEL), const)],
        out_specs=pl.BlockSpec((tm, D_MODEL), tok),
        compiler_params=pltpu.CompilerParams(dimension_semantics=("arbitrary",), vmem_limit_bytes=VMEM_LIMIT),
        name="outproj_ffn",
    )(x2, ma, mb, mc, wo, gf, wg, wu, wd, gl)


def _regroup_w_in(w_in):
    sizes = (A_Q, HEAD_DIM, HEAD_DIM, B_Q, HEAD_DIM, HEAD_DIM, HEAD_DIM, HEAD_DIM, HEAD_DIM, HEAD_DIM,
             N_GATES, C_W, C_W, C_W)
    offs = np.concatenate([[0], np.cumsum(sizes)])
    qa, ka, va, qb, kcb, vcb, ksb, vsb, kwb, vwb, gb, qc, kc, vc = [w_in[:, offs[j]:offs[j + 1]]
                                                                     for j in range(len(sizes))]
    pad = jnp.zeros((w_in.shape[0], LANES - N_GATES), w_in.dtype)
    cols = [qa * SCALE, ka, va, qb * SCALE, ksb, vsb, kwb, vwb, kcb, vcb, gb, pad, qc * SCALE, kc, vc]
    return jnp.concatenate(cols, axis=1).astype(BF16)


def _layer(x2, b, s, w_in, w_out, g_attn, g_ffn, g_out_a, g_out_b, g_out_c, sinks, cmp_pos, cmp_w1, cmp_w2,
           w_gate, w_up, w_down, g_final, final):
    row = lambda v: v.reshape(1, -1)
    qa, kva, qb, kvs, kvw, kvr, gate, qc, kc, vc = _inproj(x2, row(g_attn), _regroup_w_in(w_in))
    r3 = lambda a: a.reshape(b, s, a.shape[-1])

    half = CMP_LEN // 2
    pos = jnp.concatenate([cmp_pos[0], cmp_pos[1]], axis=-1).reshape(2, half, 1, LANES)
    z1 = jnp.zeros_like(cmp_w1[0])
    w1 = jnp.concatenate([jnp.concatenate([cmp_w1[0], z1], axis=-1), jnp.concatenate([z1, cmp_w1[1]], axis=-1)], axis=1)
    w1 = w1.reshape(2, half, LANES, 2 * CMP_HIDDEN).astype(BF16)
    z2 = jnp.zeros_like(cmp_w2[0])
    w2 = jnp.concatenate([jnp.concatenate([cmp_w2[0], z2], axis=1), jnp.concatenate([z2, cmp_w2[1]], axis=1)],
                         axis=0).astype(BF16)
    kvc = _compress(r3(kvr), pos, w1, w2)

    ma = _mixer_a(sinks, r3(qa), r3(kva), row(g_out_a))
    mb = _mixer_b(r3(qb), r3(kvs), r3(kvw), kvc, r3(gate), row(g_out_b))
    mc = _mixer_c(r3(qc), r3(kc), r3(vc), row(g_out_c))
    t = b * s
    return _outffn(x2, ma.reshape(t, A_Q), mb.reshape(t, B_Q), mc.reshape(t, C_W), w_out.astype(BF16),
                   row(g_ffn), w_gate.astype(BF16), w_up.astype(BF16), w_down.astype(BF16), row(g_final), final)


def kernel(x, w_in, w_out, g_attn, g_ffn, g_out_a, g_out_b, g_out_c, sinks, cmp_pos, cmp_w1, cmp_w2, w_gate, w_up,
           w_down, g_final):
    b, s, d = x.shape
    depth = w_in.shape[0]
    assert d == D_MODEL and s % KEY_CHUNK == 0 and (b * s) % TOKEN_TILE == 0 and (b * s) % INPROJ_TILE == 0
    assert s >= B_WINDOW + Q_WIDE and s % SLC_CHUNK == 0 and s % A_TILE == 0 and (s // CMP_STRIDE) % LANES == 0
    x2 = x.reshape(b * s, d)
    for l in range(depth):
        x2 = _layer(x2, b, s, w_in[l], w_out[l], g_attn[l], g_ffn[l], g_out_a[l], g_out_b[l], g_out_c[l], sinks[l],
                    cmp_pos[l], cmp_w1[l], cmp_w2[l], w_gate[l], w_up[l], w_down[l], g_final, l == depth - 1)
    return x2.reshape(b, s, d)
```

```python
import functools

import jax
import jax.numpy as jnp
import numpy as np
from jax import lax
from jax.experimental import pallas as pl
from jax.experimental.pallas import tpu as pltpu

F32 = jnp.float32
BF16 = jnp.bfloat16

D_MODEL = 1024
HEAD_DIM = 64
LANES = 128
Q_BLOCK = 128
Q_WIDE = 256
A_TILE = 256
B_BATCH = 4
C_BATCH = 4
A_HEADS = 8
A_WINDOW = 128
B_HEADS = 4
CMP_LEN = 32
CMP_STRIDE = 16
CMP_HIDDEN = 128
SLC_LEN = 64
SLC_TOP = 8
B_WINDOW = 512
C_HEADS = 4
A_Q = A_HEADS * HEAD_DIM
B_Q = B_HEADS * HEAD_DIM
C_W = C_HEADS * HEAD_DIM
MIX_WIDTH = A_Q + B_Q + C_W
D_FF = 2816
N_GATES = B_HEADS * 3
NEG_INF = -1e30
MASK_BIG = 2.0 ** 100
SEL_LANE = 72
FORCE = 1e4
EPS = 1e-6
SCALE = HEAD_DIM ** -0.5
LOG2E = 1.4426950408889634

KEY_CHUNK = 256
SLC_CHUNK = 512
FF_CHUNK = 256
TOKEN_TILE = 512
INPROJ_TILE = 1024
VMEM_LIMIT = 56 * 1024 * 1024

_G_QA, _G_KVA, _G_QB, _G_KVS, _G_KVW, _G_KVC, _G_GATE, _G_QC, _G_KC, _G_VC, IN_PAD = (
    0, 512, 640, 896, 1024, 1152, 1280, 1408, 1664, 1920, 2176)


def _alibi_slopes():
    n = A_HEADS + B_HEADS
    sl = 2.0 ** (-8.0 * np.arange(1, n + 1) / n) * LOG2E
    return [float(v) for v in sl[:A_HEADS]], [float(v) for v in sl[A_HEADS:]]


def _round_bf16(x):
    bits = np.ascontiguousarray(x, np.float32).view(np.uint32).astype(np.uint64)
    bits = ((bits + 0x7FFF + ((bits >> 16) & 1)) >> 16) << 16
    return bits.astype(np.uint32).view(np.float32)


def _key_features(pos, block_len=None):
    pos = np.asarray(pos, np.int64)
    assert pos.max() < 2048
    f = np.zeros((len(pos), LANES), np.float32)
    for j in range(3):
        f[:, HEAD_DIM + 2 * j] = 256 * (pos // 256)
        f[:, HEAD_DIM + 2 * j + 1] = pos % 256
    if block_len is not None:
        assert SEL_LANE + pos.max() // block_len < LANES
        f[np.arange(len(pos)), SEL_LANE + pos // block_len] = -MASK_BIG
    return jnp.asarray(f, BF16)


def _query_features(slopes, rows):
    f = np.zeros((len(slopes) * rows, LANES), np.float32)
    for h, slope in enumerate(slopes):
        rest = np.float32(slope)
        for j in range(3):
            part = _round_bf16(rest)
            rest = np.float32(rest - part)
            f[h * rows:(h + 1) * rows, HEAD_DIM + 2 * j:HEAD_DIM + 2 * j + 2] = part
    return jnp.asarray(f, BF16)


def _dot(a, b):
    return jnp.dot(a, b, preferred_element_type=F32)


def _dot_nt(a, b):
    return lax.dot_general(a, b, (((1,), (1,)), ((), ())), preferred_element_type=F32)


def _split_bf16(x):
    hi = x.astype(BF16)
    lo = (x - hi.astype(F32)).astype(BF16)
    return hi, lo


def _rms(x, g):
    ms = jnp.mean(x * x, axis=-1, keepdims=True)
    return x * lax.rsqrt(ms + EPS) * g


def _lane_tile(x, width):
    return jnp.concatenate([x] * (width // LANES), axis=1)


def _swap_halves(x):
    return jnp.concatenate([x[:, HEAD_DIM:], x[:, :HEAD_DIM]], axis=1)


def _stack_heads(q, n_heads, feat):
    rows = q.shape[0]
    lo = lax.broadcasted_iota(jnp.int32, (rows, LANES), 1) < HEAD_DIM
    tiles = []
    for p in range(n_heads // 2):
        pair = q[:, p * LANES:(p + 1) * LANES]
        tiles.append(jnp.where(lo, pair, feat[(2 * p) * rows:(2 * p + 1) * rows]))
        tiles.append(jnp.where(lo, _swap_halves(pair), feat[(2 * p + 1) * rows:(2 * p + 2) * rows]))
    return jnp.concatenate(tiles, axis=0)


def _with_key_features(kv, feat):
    lo = lax.broadcasted_iota(jnp.int32, kv.shape, 1) < HEAD_DIM
    return jnp.where(lo, kv, feat)


def _stack_pairs(q, n_heads):
    rows = q.shape[0]
    lo = lax.broadcasted_iota(jnp.int32, (rows, LANES), 1) < HEAD_DIM
    out = []
    for p in range(n_heads // 2):
        pair = q[:, p * LANES:(p + 1) * LANES]
        zero = jnp.zeros_like(pair)
        out.append(jnp.concatenate([jnp.where(lo, pair, zero), jnp.where(lo, zero, pair)], axis=0))
    return out


def _with_ones_for_keys(kv):
    lo = lax.broadcasted_iota(jnp.int32, kv.shape, 1) < HEAD_DIM
    return jnp.where(lo, jnp.ones_like(kv), kv)


def _unstack_normalized(o, n_heads):
    rows = o.shape[0] // n_heads
    lo = lax.broadcasted_iota(jnp.int32, (rows, LANES), 1) < HEAD_DIM
    pairs = []
    for p in range(n_heads // 2):
        even = o[(2 * p) * rows:(2 * p + 1) * rows]
        odd = o[(2 * p + 1) * rows:(2 * p + 2) * rows]
        pairs.append(jnp.where(lo, pltpu.roll(even, HEAD_DIM, 1) / even, odd / pltpu.roll(odd, HEAD_DIM, 1)))
    return jnp.concatenate(pairs, axis=1)


def _unstack_heads(o, n_heads):
    rows = o.shape[0] // n_heads
    lo = lax.broadcasted_iota(jnp.int32, (rows, LANES), 1) < HEAD_DIM
    pairs = []
    for p in range(n_heads // 2):
        even = o[(2 * p) * rows:(2 * p + 1) * rows]
        odd = o[(2 * p + 1) * rows:(2 * p + 2) * rows]
        pairs.append(jnp.where(lo, pltpu.roll(even, HEAD_DIM, 1), odd))
    return jnp.concatenate(pairs, axis=1)


def _inproj_kernel(x_ref, g_ref, w_ref, qa_ref, kva_ref, qb_ref, kvs_ref, kvw_ref, kvr_ref,
                   gate_ref, qc_ref, kc_ref, vc_ref):
    h = _rms(x_ref[...], g_ref[...]).astype(BF16)

    def proj(lo, hi):
        return _dot(h, w_ref[:, lo:hi])

    qa_ref[...] = (proj(_G_QA, _G_KVA) * LOG2E).astype(BF16)
    kva_ref[...] = proj(_G_KVA, _G_QB).astype(BF16)
    qb_ref[...] = (proj(_G_QB, _G_KVS) * LOG2E).astype(BF16)
    kvs_ref[...] = proj(_G_KVS, _G_KVW).astype(BF16)
    kvw_ref[...] = proj(_G_KVW, _G_KVC).astype(BF16)
    kvr_ref[...] = proj(_G_KVC, _G_GATE)
    gate_ref[...] = proj(_G_GATE, _G_QC)
    qc_ref[...] = (proj(_G_QC, _G_KC) * LOG2E).astype(BF16)
    kc_ref[...] = proj(_G_KC, _G_VC).astype(BF16)
    vc_ref[...] = proj(_G_VC, IN_PAD).astype(BF16)


def _inproj(x2, g, w):
    t = x2.shape[0]
    tm = INPROJ_TILE
    widths = [(A_Q, BF16), (LANES, BF16), (B_Q, BF16), (LANES, BF16), (LANES, BF16), (LANES, F32),
              (LANES, F32), (C_W, BF16), (C_W, BF16), (C_W, BF16)]
    return pl.pallas_call(
        _inproj_kernel,
        out_shape=[jax.ShapeDtypeStruct((t, wd), dt) for wd, dt in widths],
        grid=(t // tm,),
        in_specs=[pl.BlockSpec((tm, D_MODEL), lambda i: (i, 0)),
                  pl.BlockSpec((1, D_MODEL), lambda i: (0, 0)),
                  pl.BlockSpec((D_MODEL, IN_PAD), lambda i: (0, 0))],
        out_specs=[pl.BlockSpec((tm, wd), lambda i: (i, 0)) for wd, _ in widths],
        compiler_params=pltpu.CompilerParams(dimension_semantics=("arbitrary",), vmem_limit_bytes=VMEM_LIMIT),
        name="inproj",
    )(x2, g, w)


def _compress_kernel(x_ref, pos_ref, w1_ref, w2_ref, o_ref):
    half = CMP_LEN // 2
    n = x_ref.shape[1] // CMP_STRIDE
    h_lo = jnp.zeros((n, 2 * CMP_HIDDEN), F32)
    h_hi = jnp.zeros((n, 2 * CMP_HIDDEN), F32)
    for l in range(half):
        x = x_ref[0, pl.ds(l, n, stride=CMP_STRIDE), :]
        h_lo = h_lo + _dot((x + pos_ref[0, l]).astype(BF16), w1_ref[0, l])
        h_hi = h_hi + _dot((x + pos_ref[1, l]).astype(BF16), w1_ref[1, l])
    h = h_lo + pltpu.roll(h_hi, n - 1, 0)
    o_ref[0] = _dot(jax.nn.gelu(h).astype(BF16), w2_ref[...]).astype(BF16)


def _compress(x, pos, w1, w2):
    b, s, _ = x.shape
    n = s // CMP_STRIDE
    half = CMP_LEN // 2
    return pl.pallas_call(
        _compress_kernel,
        out_shape=jax.ShapeDtypeStruct((b, n, LANES), BF16),
        grid=(b,),
        in_specs=[pl.BlockSpec((1, s, LANES), lambda i: (i, 0, 0)),
                  pl.BlockSpec((2, half, 1, LANES), lambda i: (0, 0, 0, 0)),
                  pl.BlockSpec((2, half, LANES, 2 * CMP_HIDDEN), lambda i: (0, 0, 0, 0)),
                  pl.BlockSpec((2 * CMP_HIDDEN, LANES), lambda i: (0, 0))],
        out_specs=pl.BlockSpec((1, n, LANES), lambda i: (i, 0, 0)),
        compiler_params=pltpu.CompilerParams(dimension_semantics=("arbitrary",), vmem_limit_bytes=VMEM_LIMIT),
        name="compress",
    )(x, pos, w1, w2)


def _mixa_kernel(sink_ref, q_ref, kvp_ref, kvc_ref, pfp_ref, pfc_ref, qf_ref, g_ref, o_ref, *, slopes):
    i = pl.program_id(1)
    span = 2 * Q_BLOCK
    row = lax.broadcasted_iota(jnp.int32, (Q_BLOCK, span), 0)
    col = lax.broadcasted_iota(jnp.int32, (Q_BLOCK, span), 1)
    dist = row + Q_BLOCK - col
    band = (dist >= 0) & (dist < A_WINDOW)
    kv_rows = jnp.concatenate([kvp_ref[0], kvc_ref[0]], axis=0)
    pf_rows = jnp.concatenate([pfp_ref[...], pfc_ref[...]], axis=0)

    n_sub = A_TILE // Q_BLOCK
    kvs, scores = [], []
    for r in range(n_sub):
        q = q_ref[0, r * Q_BLOCK:(r + 1) * Q_BLOCK, :]
        kvs.append(kv_rows[r * Q_BLOCK:r * Q_BLOCK + span])
        kx = _with_key_features(kvs[r], pf_rows[r * Q_BLOCK:r * Q_BLOCK + span])
        scores.append(_dot_nt(_stack_heads(q, A_HEADS, qf_ref[...]), kx))
    for r in range(n_sub):
        kv, s_all = kvs[r], scores[r]
        valid = band & ((col >= Q_BLOCK) | (i > 0)) if r == 0 else band
        t_row = (i * A_TILE + r * Q_BLOCK + lax.broadcasted_iota(jnp.int32, (Q_BLOCK, 1), 0)).astype(F32)
        probs, inv_l = [], []
        for hd in range(A_HEADS):
            sink = sink_ref[hd] * LOG2E + slopes[hd] * t_row
            s = jnp.where(valid, s_all[hd * Q_BLOCK:(hd + 1) * Q_BLOCK], NEG_INF)
            m = jnp.maximum(jnp.max(s, axis=-1, keepdims=True), sink)
            e = jnp.exp2(s - m)
            inv_l.append(1.0 / (jnp.sum(e, axis=-1, keepdims=True) + jnp.exp2(sink - m)))
            probs.append(e.astype(BF16))
        o_all = _dot(jnp.concatenate(probs, axis=0), kv)
        o_all = jnp.concatenate([o_all[hd * Q_BLOCK:(hd + 1) * Q_BLOCK] * inv_l[hd] for hd in range(A_HEADS)], axis=0)
        o_ref[0, r * Q_BLOCK:(r + 1) * Q_BLOCK, :] = _rms(_unstack_heads(o_all, A_HEADS), g_ref[...]).astype(BF16)


def _mixer_a(sinks, qa, kva, g):
    b, s, _ = qa.shape
    slopes, _ = _alibi_slopes()
    pos_feat = _key_features(np.arange(s))
    sub = A_TILE // Q_BLOCK
    prev_block = lambda i: jnp.maximum(sub * i - 1, 0)
    return pl.pallas_call(
        functools.partial(_mixa_kernel, slopes=slopes),
        out_shape=jax.ShapeDtypeStruct((b, s, A_Q), BF16),
        grid=(b, s // A_TILE),
        in_specs=[pl.BlockSpec(memory_space=pltpu.SMEM),
                  pl.BlockSpec((1, A_TILE, A_Q), lambda bi, i: (bi, i, 0)),
                  pl.BlockSpec((1, Q_BLOCK, LANES), lambda bi, i: (bi, prev_block(i), 0)),
                  pl.BlockSpec((1, A_TILE, LANES), lambda bi, i: (bi, i, 0)),
                  pl.BlockSpec((Q_BLOCK, LANES), lambda bi, i: (prev_block(i), 0)),
                  pl.BlockSpec((A_TILE, LANES), lambda bi, i: (i, 0)),
                  pl.BlockSpec((A_HEADS * Q_BLOCK, LANES), lambda bi, i: (0, 0)),
                  pl.BlockSpec((1, A_Q), lambda bi, i: (0, 0))],
        out_specs=pl.BlockSpec((1, A_TILE, A_Q), lambda bi, i: (bi, i, 0)),
        compiler_params=pltpu.CompilerParams(dimension_semantics=("arbitrary", "arbitrary"),
                                             vmem_limit_bytes=VMEM_LIMIT),
        name="mixer_a",
    )(sinks, qa, kva, kva, pos_feat, pos_feat, _query_features(slopes, Q_BLOCK), g)


def _mixb_kernel(q_ref, kvs_ref, kvw_ref, kvc_ref, gate_ref, pf_ref, sf_ref, cf_ref, qf_ref, mmap_ref, rep_ref,
                 g_ref, o_ref, acc_ref, m_ref, *, n_cmp, n_slc):
    i = pl.program_id(1)
    t0 = i * Q_WIDE
    nb = q_ref.shape[0]
    q_all = [_stack_heads(q_ref[bb], B_HEADS, qf_ref[...]) for bb in range(nb)]

    def head_rows(x, hd, rows=Q_WIDE):
        return x[hd * rows:(hd + 1) * rows]

    kvc = [kvc_ref[bb] for bb in range(nb)]
    n_pad = kvc[0].shape[0]
    s_cmp = [_dot_nt(q_all[bb], _with_key_features(kvc[bb], cf_ref[...])) for bb in range(nb)]
    wspan = B_WINDOW + Q_BLOCK
    row_w = lax.broadcasted_iota(jnp.int32, (Q_BLOCK, wspan), 0)
    col_w = lax.broadcasted_iota(jnp.int32, (Q_BLOCK, wspan), 1)
    n_sub = Q_WIDE // Q_BLOCK
    valid_w = []
    kvw = [[None] * n_sub for _ in range(nb)]
    s_win = [[None] * n_sub for _ in range(nb)]
    for r in range(n_sub):
        r0 = t0 + r * Q_BLOCK
        w0 = pl.multiple_of(jnp.clip(r0 - B_WINDOW, 0, kvw_ref.shape[1] - wspan), LANES)
        dist_w = (row_w - col_w) + (r0 - w0)
        valid_w.append((dist_w >= 0) & (dist_w < B_WINDOW))
        for bb in range(nb):
            kvw[bb][r] = kvw_ref[bb, pl.ds(w0, wspan), :]
            q_sub = jnp.concatenate([head_rows(q_all[bb], hd)[r * Q_BLOCK:(r + 1) * Q_BLOCK] for hd in range(B_HEADS)],
                                    axis=0)
            s_win[bb][r] = _dot_nt(q_sub, _with_key_features(kvw[bb][r], pf_ref[pl.ds(w0, wspan), :]))

    gates = []
    for bb in range(nb):
        sig = jax.nn.sigmoid(gate_ref[bb]).astype(BF16)
        gates.append([_dot(sig, rep_ref[r]) for r in range(3)])

    row_c = lax.broadcasted_iota(jnp.int32, (Q_WIDE, n_pad), 0)
    col_c = lax.broadcasted_iota(jnp.int32, (Q_WIDE, n_pad), 1)
    vis_c = ((t0 + row_c) >= (col_c * CMP_STRIDE + (CMP_LEN - 1))) & (col_c < n_cmp)
    row1 = lax.broadcasted_iota(jnp.int32, (Q_WIDE, 1), 0)
    any_vis = ((t0 + row1) >= (CMP_LEN - 1)).astype(F32)
    blk = lax.broadcasted_iota(jnp.int32, (n_slc, Q_WIDE), 0)
    tq = t0 + lax.broadcasted_iota(jnp.int32, (n_slc, Q_WIDE), 1)
    cur = tq // SLC_LEN
    valid_b = blk <= cur
    forced = (blk == 0) | (blk == cur) | (blk == cur - 1)
    lane_q = lax.broadcasted_iota(jnp.int32, (B_HEADS * Q_WIDE, LANES), 1)
    sel_lanes = (lane_q >= SEL_LANE) & (lane_q < SEL_LANE + n_slc)
    o_cmp, q_slc = [], []
    for bb in range(nb):
        probs = []
        p_sum = jnp.zeros((Q_WIDE, n_pad), F32)
        for hd in range(B_HEADS):
            s = jnp.where(vis_c, head_rows(s_cmp[bb], hd), NEG_INF)
            m = jnp.max(s, axis=-1, keepdims=True)
            e = jnp.exp2(s - m)
            pr = e * (any_vis / jnp.sum(e, axis=-1, keepdims=True))
            p_sum = p_sum + pr
            probs.append(pr.astype(BF16))
        o_cmp.append(_unstack_heads(_dot(jnp.concatenate(probs, axis=0), kvc[bb]), B_HEADS))
        p_hi, p_lo = _split_bf16(p_sum)
        imp = (_dot_nt(mmap_ref[...], p_hi) + _dot_nt(mmap_ref[...], p_lo))[:n_slc]
        val = jnp.where(forced, FORCE, jnp.where(valid_b, imp, -FORCE))
        rank = jnp.zeros((n_slc, Q_WIDE), F32)
        for j in range(n_slc):
            vj = val[j:j + 1, :]
            ahead = (vj > val) | ((vj == val) & (blk > j))
            rank = rank + ahead.astype(F32)
        unsel = 1.0 - ((rank < SLC_TOP) & valid_b).astype(F32)
        unsel = jnp.concatenate([jnp.zeros((SEL_LANE, Q_WIDE), F32), unsel,
                                 jnp.zeros((LANES - SEL_LANE - n_slc, Q_WIDE), F32)], axis=0)
        unsel_q = unsel.T.astype(BF16)
        q_slc.append(jnp.where(sel_lanes, jnp.concatenate([unsel_q] * B_HEADS, axis=0), q_all[bb]))

    ob_rest = []
    for bb in range(nb):
        o_sub = [[None] * n_sub for _ in range(B_HEADS)]
        for r in range(n_sub):
            probs = []
            for hd in range(B_HEADS):
                s = jnp.where(valid_w[r], head_rows(s_win[bb][r], hd, Q_BLOCK), NEG_INF)
                probs.append(jnp.exp2(s - jnp.max(s, axis=-1, keepdims=True)).astype(BF16))
            o_all = _dot(jnp.concatenate(probs, axis=0), _with_ones_for_keys(kvw[bb][r]))
            for hd in range(B_HEADS):
                o_sub[hd][r] = head_rows(o_all, hd, Q_BLOCK)
        o_win = _unstack_normalized(
            jnp.concatenate([o_sub[hd][r] for hd in range(B_HEADS) for r in range(n_sub)], axis=0), B_HEADS)
        ob_rest.append(gates[bb][0] * o_cmp[bb] + gates[bb][2] * o_win)

    acc_ref[...] = jnp.zeros(acc_ref.shape, F32)
    m_ref[...] = jnp.full(m_ref.shape, NEG_INF, F32)

    def slc_chunk(c, width):
        k0 = pl.multiple_of(c * SLC_CHUNK, SLC_CHUNK)
        row_k = lax.broadcasted_iota(jnp.int32, (Q_WIDE, width), 0)
        col_k = lax.broadcasted_iota(jnp.int32, (Q_WIDE, width), 1)
        causal = (col_k - row_k) <= (t0 - k0)
        feat = sf_ref[pl.ds(k0, width), :]
        for bb in range(nb):
            kv = kvs_ref[bb, pl.ds(k0, width), :]
            s_all = _dot_nt(q_slc[bb], _with_key_features(kv, feat))
            probs, alphas = [], []
            for hd in range(B_HEADS):
                g = bb * B_HEADS + hd
                s = jnp.where(causal, head_rows(s_all, hd), NEG_INF)
                m_old = m_ref[g]
                m_new = jnp.maximum(m_old, jnp.max(s, axis=-1, keepdims=True))
                alpha = jnp.exp2(m_old - m_new)
                m_ref[g] = m_new
                probs.append(jnp.exp2(s - _lane_tile(m_new, width)).astype(BF16))
                alphas.append(alpha)
            o_all = _dot(jnp.concatenate(probs, axis=0), _with_ones_for_keys(kv))
            for hd in range(B_HEADS):
                g = bb * B_HEADS + hd
                acc_ref[g] = alphas[hd] * acc_ref[g] + head_rows(o_all, hd)

    n_half = lax.shift_right_logical(t0 + Q_WIDE + SLC_CHUNK // 2 - 1, (SLC_CHUNK // 2).bit_length() - 1)
    n_full = lax.shift_right_logical(n_half, 1)

    def full_chunk(c, carry):
        slc_chunk(c, SLC_CHUNK)
        return carry

    lax.fori_loop(0, n_full, full_chunk, 0)

    @pl.when((n_half & 1) == 1)
    def _():
        slc_chunk(n_full, SLC_CHUNK // 2)

    for bb in range(nb):
        o_slc = _unstack_normalized(
            jnp.concatenate([acc_ref[bb * B_HEADS + hd] for hd in range(B_HEADS)], axis=0), B_HEADS)
        ob = ob_rest[bb] + gates[bb][1] * o_slc
        o_ref[bb] = _rms(ob, g_ref[...]).astype(BF16)


def _mixer_b(qb, kvs, kvw, kvc, gate, g):
    b, s, _ = qb.shape
    n_pad = kvc.shape[1]
    n_cmp = s // CMP_STRIDE - 1
    n_slc = s // SLC_LEN
    _, slopes = _alibi_slopes()
    pos_feat = _key_features(np.arange(s))
    slc_feat = _key_features(np.arange(s), SLC_LEN)
    cmp_feat = _key_features(np.minimum(np.arange(n_pad), n_cmp - 1) * CMP_STRIDE + (CMP_LEN - 1))
    cs = np.arange(n_pad)[None, :] * CMP_STRIDE
    ss = np.arange(LANES)[:, None] * SLC_LEN
    ov = np.maximum(0, np.minimum(cs + CMP_LEN, ss + SLC_LEN) - np.maximum(cs, ss)) / CMP_STRIDE
    ov = ov * (np.arange(n_pad)[None, :] < n_cmp) * (np.arange(LANES)[:, None] < n_slc)
    mmap_t = jnp.asarray(ov, BF16)
    rep = np.zeros((3, LANES, B_Q), np.float32)
    for r in range(3):
        for h in range(B_HEADS):
            rep[r, 3 * h + r, h * HEAD_DIM:(h + 1) * HEAD_DIM] = 1.0
    rep = jnp.asarray(rep, BF16)
    nb = B_BATCH if b % B_BATCH == 0 else 1
    return pl.pallas_call(
        functools.partial(_mixb_kernel, n_cmp=n_cmp, n_slc=n_slc),
        out_shape=jax.ShapeDtypeStruct((b, s, B_Q), BF16),
        grid=(b // nb, s // Q_WIDE),
        in_specs=[pl.BlockSpec((nb, Q_WIDE, B_Q), lambda bi, i: (bi, i, 0)),
                  pl.BlockSpec((nb, s, LANES), lambda bi, i: (bi, 0, 0)),
                  pl.BlockSpec((nb, s, LANES), lambda bi, i: (bi, 0, 0)),
                  pl.BlockSpec((nb, n_pad, LANES), lambda bi, i: (bi, 0, 0)),
                  pl.BlockSpec((nb, Q_WIDE, LANES), lambda bi, i: (bi, i, 0)),
                  pl.BlockSpec((s, LANES), lambda bi, i: (0, 0)),
                  pl.BlockSpec((s, LANES), lambda bi, i: (0, 0)),
                  pl.BlockSpec((n_pad, LANES), lambda bi, i: (0, 0)),
                  pl.BlockSpec((B_HEADS * Q_WIDE, LANES), lambda bi, i: (0, 0)),
                  pl.BlockSpec((LANES, n_pad), lambda bi, i: (0, 0)),
                  pl.BlockSpec((3, LANES, B_Q), lambda bi, i: (0, 0, 0)),
                  pl.BlockSpec((1, B_Q), lambda bi, i: (0, 0))],
        out_specs=pl.BlockSpec((nb, Q_WIDE, B_Q), lambda bi, i: (bi, i, 0)),
        scratch_shapes=[pltpu.VMEM((nb * B_HEADS, Q_WIDE, LANES), F32),
                        pltpu.VMEM((nb * B_HEADS, Q_WIDE, LANES), F32)],
        compiler_params=pltpu.CompilerParams(dimension_semantics=("arbitrary", "arbitrary"),
                                             vmem_limit_bytes=VMEM_LIMIT),
        name="mixer_b",
    )(qb, kvs, kvw, kvc, gate, pos_feat, slc_feat, cmp_feat, _query_features(slopes, Q_WIDE), mmap_t, rep, g)


def _mixc_kernel(q_ref, k_ref, v_ref, tri_ref, g_ref, o_ref, acc_ref, carry_ref, z_ref, w_ref):
    i = pl.program_id(1)
    t0 = i * Q_WIDE
    nb = q_ref.shape[0]
    n_pair = C_HEADS // 2
    q_pairs = [qp for bb in range(nb) for qp in _stack_pairs(q_ref[bb], C_HEADS)]
    acc_ref[...] = jnp.zeros(acc_ref.shape, F32)
    carry_ref[...] = jnp.zeros(carry_ref.shape, F32)
    row = lax.broadcasted_iota(jnp.int32, (Q_WIDE, KEY_CHUNK), 0)
    col = lax.broadcasted_iota(jnp.int32, (Q_WIDE, KEY_CHUNK), 1)
    cr = col - row
    n_chunks = (t0 + Q_WIDE + KEY_CHUNK - 1) // KEY_CHUNK

    def chunk_rows(ref, bb, c):
        return ref[bb, pl.ds(pl.multiple_of(c * KEY_CHUNK, KEY_CHUNK), KEY_CHUNK), :]

    def logits_to_scratch(c):
        for bb in range(nb):
            k = chunk_rows(k_ref, bb, c)
            for p in range(n_pair):
                z_ref[bb * n_pair + p] = _dot_nt(q_pairs[bb * n_pair + p], k[:, p * LANES:(p + 1) * LANES])

    def add_values(c):
        for bb in range(nb):
            v = chunk_rows(v_ref, bb, c)
            for p in range(n_pair):
                o = _dot(w_ref[bb * n_pair + p], v[:, p * LANES:(p + 1) * LANES])
                g = bb * C_HEADS + 2 * p
                acc_ref[g] = acc_ref[g] + o[:Q_WIDE]
                acc_ref[g + 1] = acc_ref[g + 1] + o[Q_WIDE:]

    def sweep(c, diagonal):
        if not diagonal:
            add_values(c + 1)
        past = cr < (t0 - c * KEY_CHUNK)
        n_all = nb * C_HEADS
        log_w, parts = [], []
        for g in range(n_all):
            z = z_ref[g // 2, (g % 2) * Q_WIDE:(g % 2 + 1) * Q_WIDE, :]
            neg_abs = lax.bitcast_convert_type(lax.bitcast_convert_type(z, jnp.uint32) | jnp.uint32(0x80000000), F32)
            drop = jnp.maximum(z, 0.0) + jnp.log2(1.0 + jnp.exp2(neg_abs))
            dk = jnp.where(past, drop, 0.0) if diagonal else drop
            parts.append(dk.astype(BF16))
            log_w.append(z - drop - _lane_tile(carry_ref[g], KEY_CHUNK))
            carry_ref[g] = carry_ref[g] + jnp.sum(dk, axis=-1, keepdims=True)
        tails = _dot(jnp.concatenate(parts, axis=0), tri_ref[...])
        logits_to_scratch(jnp.maximum(c - 1, 0))
        weights = []
        for g in range(n_all):
            w = jnp.exp2(log_w[g] - tails[g * Q_WIDE:(g + 1) * Q_WIDE])
            weights.append((jnp.where(past, w, 0.0) if diagonal else w).astype(BF16))
        for u in range(nb * n_pair):
            w_ref[u] = jnp.concatenate(weights[2 * u:2 * u + 2], axis=0)

    logits_to_scratch(n_chunks - 1)
    sweep(n_chunks - 1, True)

    def earlier(j, carry):
        sweep(n_chunks - 1 - j, False)
        return carry

    lax.fori_loop(1, n_chunks, earlier, 0)
    add_values(0)
    lane = lax.broadcasted_iota(jnp.int32, (Q_WIDE, LANES), 1)
    lo = lane < HEAD_DIM
    for bb in range(nb):
        oc = jnp.concatenate([jnp.where(lo, acc_ref[bb * C_HEADS + 2 * p], acc_ref[bb * C_HEADS + 2 * p + 1])
                              for p in range(n_pair)], axis=1)
        o_ref[bb] = _rms(oc, g_ref[...]).astype(BF16)


def _mixer_c(qc, kc, vc, g):
    b, s, _ = qc.shape
    tri = jnp.asarray(np.arange(KEY_CHUNK)[:, None] > np.arange(KEY_CHUNK)[None, :], BF16)
    nb = C_BATCH if b % C_BATCH == 0 else 1
    return pl.pallas_call(
        _mixc_kernel,
        out_shape=jax.ShapeDtypeStruct((b, s, C_W), BF16),
        grid=(b // nb, s // Q_WIDE),
        in_specs=[pl.BlockSpec((nb, Q_WIDE, C_W), lambda bi, i: (bi, i, 0)),
                  pl.BlockSpec((nb, s, C_W), lambda bi, i: (bi, 0, 0)),
                  pl.BlockSpec((nb, s, C_W), lambda bi, i: (bi, 0, 0)),
                  pl.BlockSpec((KEY_CHUNK, KEY_CHUNK), lambda bi, i: (0, 0)),
                  pl.BlockSpec((1, C_W), lambda bi, i: (0, 0))],
        out_specs=pl.BlockSpec((nb, Q_WIDE, C_W), lambda bi, i: (bi, i, 0)),
        scratch_shapes=[pltpu.VMEM((nb * C_HEADS, Q_WIDE, LANES), F32),
                        pltpu.VMEM((nb * C_HEADS, Q_WIDE, LANES), F32),
                        pltpu.VMEM((nb * C_HEADS // 2, 2 * Q_WIDE, KEY_CHUNK), F32),
                        pltpu.VMEM((nb * C_HEADS // 2, 2 * Q_WIDE, KEY_CHUNK), BF16)],
        compiler_params=pltpu.CompilerParams(dimension_semantics=("arbitrary", "arbitrary"),
                                             vmem_limit_bytes=VMEM_LIMIT),
        name="mixer_c",
    )(qc, kc, vc, tri, g)


def _outffn_kernel(x_ref, ma_ref, mb_ref, mc_ref, wo_ref, gf_ref, wg_ref, wu_ref, wd_ref, gl_ref, o_ref, *, final):
    x = (x_ref[...] + _dot(ma_ref[...], wo_ref[:A_Q, :]) + _dot(mb_ref[...], wo_ref[A_Q:A_Q + B_Q, :])
         + _dot(mc_ref[...], wo_ref[A_Q + B_Q:, :]))
    h = _rms(x, gf_ref[...]).astype(BF16)
    o_ref[...] = x
    for c in range(D_FF // FF_CHUNK):
        sl = slice(c * FF_CHUNK, (c + 1) * FF_CHUNK)
        gate = _dot(h, wg_ref[:, sl])
        up = _dot(h, wu_ref[:, sl])
        o_ref[...] += _dot((jax.nn.silu(gate) * up).astype(BF16), wd_ref[sl, :])
    if final:
        o_ref[...] = _rms(o_ref[...], gl_ref[...])


def _outffn(x2, ma, mb, mc, wo, gf, wg, wu, wd, gl, final):
    t = x2.shape[0]
    tm = TOKEN_TILE
    const = lambda i: (0, 0)
    tok = lambda i: (i, 0)
    return pl.pallas_call(
        functools.partial(_outffn_kernel, final=final),
        out_shape=jax.ShapeDtypeStruct((t, D_MODEL), F32),
        grid=(t // tm,),
        in_specs=[pl.BlockSpec((tm, D_MODEL), tok),
                  pl.BlockSpec((tm, A_Q), tok),
                  pl.BlockSpec((tm, B_Q), tok),
                  pl.BlockSpec((tm, C_W), tok),
                  pl.BlockSpec((MIX_WIDTH, D_MODEL), const),
                  pl.BlockSpec((1, D_MODEL), const),
                  pl.BlockSpec((D_MODEL, D_FF), const),
                  pl.BlockSpec((D_MODEL, D_FF), const),
                  pl.BlockSpec((D_FF, D_MODEL), const),
                  pl.BlockSpec((1, D_MODEL), const)],
        out_specs=pl.BlockSpec((tm, D_MODEL), tok),
        compiler_params=pltpu.CompilerParams(dimension_semantics=("arbitrary",), vmem_limit_bytes=VMEM_LIMIT),
        name="outproj_ffn",
    )(x2, ma, mb, mc, wo, gf, wg, wu, wd, gl)


def _regroup_w_in(w_in):
    sizes = (A_Q, HEAD_DIM, HEAD_DIM, B_Q, HEAD_DIM, HEAD_DIM, HEAD_DIM, HEAD_DIM, HEAD_DIM, HEAD_DIM,
             N_GATES, C_W, C_W, C_W)
    offs = np.concatenate([[0], np.cumsum(sizes)])
    qa, ka, va, qb, kcb, vcb, ksb, vsb, kwb, vwb, gb, qc, kc, vc = [w_in[:, offs[j]:offs[j + 1]]
                                                                     for j in range(len(sizes))]
    pad = jnp.zeros((w_in.shape[0], LANES - N_GATES), w_in.dtype)
    cols = [qa * SCALE, ka, va, qb * SCALE, ksb, vsb, kwb, vwb, kcb, vcb, gb, pad, qc * SCALE, kc, vc]
    return jnp.concatenate(cols, axis=1).astype(BF16)


def _layer(x2, b, s, w_in, w_out, g_attn, g_ffn, g_out_a, g_out_b, g_out_c, sinks, cmp_pos, cmp_w1, cmp_w2,
           w_gate, w_up, w_down, g_final, final):
    row = lambda v: v.reshape(1, -1)
    qa, kva, qb, kvs, kvw, kvr, gate, qc, kc, vc = _inproj(x2, row(g_attn), _regroup_w_in(w_in))
    r3 = lambda a: a.reshape(b, s, a.shape[-1])

    half = CMP_LEN // 2
    pos = jnp.concatenate([cmp_pos[0], cmp_pos[1]], axis=-1).reshape(2, half, 1, LANES)
    z1 = jnp.zeros_like(cmp_w1[0])
    w1 = jnp.concatenate([jnp.concatenate([cmp_w1[0], z1], axis=-1), jnp.concatenate([z1, cmp_w1[1]], axis=-1)], axis=1)
    w1 = w1.reshape(2, half, LANES, 2 * CMP_HIDDEN).astype(BF16)
    z2 = jnp.zeros_like(cmp_w2[0])
    w2 = jnp.concatenate([jnp.concatenate([cmp_w2[0], z2], axis=1), jnp.concatenate([z2, cmp_w2[1]], axis=1)],
                         axis=0).astype(BF16)
    kvc = _compress(r3(kvr), pos, w1, w2)

    ma = _mixer_a(sinks, r3(qa), r3(kva), row(g_out_a))
    mb = _mixer_b(r3(qb), r3(kvs), r3(kvw), kvc, r3(gate), row(g_out_b))
    mc = _mixer_c(r3(qc), r3(kc), r3(vc), row(g_out_c))
    t = b * s
    return _outffn(x2, ma.reshape(t, A_Q), mb.reshape(t, B_Q), mc.reshape(t, C_W), w_out.astype(BF16),
                   row(g_ffn), w_gate.astype(BF16), w_up.astype(BF16), w_down.astype(BF16), row(g_final), final)


def kernel(x, w_in, w_out, g_attn, g_ffn, g_out_a, g_out_b, g_out_c, sinks, cmp_pos, cmp_w1, cmp_w2, w_gate, w_up,
           w_down, g_final):
    b, s, d = x.shape
    depth = w_in.shape[0]
    assert d == D_MODEL and s % KEY_CHUNK == 0 and (b * s) % TOKEN_TILE == 0 and (b * s) % INPROJ_TILE == 0
    assert s >= B_WINDOW + Q_WIDE and s % SLC_CHUNK == 0 and s % A_TILE == 0 and (s // CMP_STRIDE) % LANES == 0
    x2 = x.reshape(b * s, d)
    for l in range(depth):
        x2 = _layer(x2, b, s, w_in[l], w_out[l], g_attn[l], g_ffn[l], g_out_a[l], g_out_b[l], g_out_c[l], sinks[l],
                    cmp_pos[l], cmp_w1[l], cmp_w2[l], w_gate[l], w_up[l], w_down[l], g_final, l == depth - 1)
    return x2.reshape(b, s, d)
```

```python
import functools

import jax
import jax.numpy as jnp
import numpy as np
from jax import lax
from jax.experimental import pallas as pl
from jax.experimental.pallas import tpu as pltpu

F32 = jnp.float32
BF16 = jnp.bfloat16

D_MODEL = 1024
HEAD_DIM = 64
LANES = 128
Q_BLOCK = 128
Q_WIDE = 256
A_TILE = 256
B_BATCH = 4
C_BATCH = 4
A_HEADS = 8
A_WINDOW = 128
B_HEADS = 4
CMP_LEN = 32
CMP_STRIDE = 16
CMP_HIDDEN = 128
SLC_LEN = 64
SLC_TOP = 8
B_WINDOW = 512
C_HEADS = 4
A_Q = A_HEADS * HEAD_DIM
B_Q = B_HEADS * HEAD_DIM
C_W = C_HEADS * HEAD_DIM
MIX_WIDTH = A_Q + B_Q + C_W
D_FF = 2816
N_GATES = B_HEADS * 3
NEG_INF = -1e30
MASK_BIG = 2.0 ** 100
SEL_LANE = 72
FORCE = 1e4
EPS = 1e-6
SCALE = HEAD_DIM ** -0.5
LOG2E = 1.4426950408889634

KEY_CHUNK = 256
SLC_CHUNK = 512
FF_CHUNK = 256
TOKEN_TILE = 1024
INPROJ_TILE = 1024
VMEM_LIMIT = 56 * 1024 * 1024

_G_QA, _G_KVA, _G_QB, _G_KVS, _G_KVW, _G_KVC, _G_GATE, _G_QC, _G_KC, _G_VC, IN_PAD = (
    0, 512, 640, 896, 1024, 1152, 1280, 1408, 1664, 1920, 2176)


def _alibi_slopes():
    n = A_HEADS + B_HEADS
    sl = 2.0 ** (-8.0 * np.arange(1, n + 1) / n) * LOG2E
    return [float(v) for v in sl[:A_HEADS]], [float(v) for v in sl[A_HEADS:]]


def _round_bf16(x):
    bits = np.ascontiguousarray(x, np.float32).view(np.uint32).astype(np.uint64)
    bits = ((bits + 0x7FFF + ((bits >> 16) & 1)) >> 16) << 16
    return bits.astype(np.uint32).view(np.float32)


def _key_features(pos, block_len=None):
    pos = np.asarray(pos, np.int64)
    assert pos.max() < 2048
    f = np.zeros((len(pos), LANES), np.float32)
    for j in range(3):
        f[:, HEAD_DIM + 2 * j] = 256 * (pos // 256)
        f[:, HEAD_DIM + 2 * j + 1] = pos % 256
    if block_len is not None:
        assert SEL_LANE + pos.max() // block_len < LANES
        f[np.arange(len(pos)), SEL_LANE + pos // block_len] = -MASK_BIG
    return jnp.asarray(f, BF16)


def _query_features(slopes, rows):
    f = np.zeros((len(slopes) * rows, LANES), np.float32)
    for h, slope in enumerate(slopes):
        rest = np.float32(slope)
        for j in range(3):
            part = _round_bf16(rest)
            rest = np.float32(rest - part)
            f[h * rows:(h + 1) * rows, HEAD_DIM + 2 * j:HEAD_DIM + 2 * j + 2] = part
    return jnp.asarray(f, BF16)


def _dot(a, b):
    return jnp.dot(a, b, preferred_element_type=F32)


def _dot_nt(a, b):
    return lax.dot_general(a, b, (((1,), (1,)), ((), ())), preferred_element_type=F32)


def _split_bf16(x):
    hi = x.astype(BF16)
    lo = (x - hi.astype(F32)).astype(BF16)
    return hi, lo


def _rms(x, g):
    ms = jnp.mean(x * x, axis=-1, keepdims=True)
    return x * lax.rsqrt(ms + EPS) * g


def _lane_tile(x, width):
    return jnp.concatenate([x] * (width // LANES), axis=1)


def _swap_halves(x):
    return jnp.concatenate([x[:, HEAD_DIM:], x[:, :HEAD_DIM]], axis=1)


def _stack_heads(q, n_heads, feat):
    rows = q.shape[0]
    lo = lax.broadcasted_iota(jnp.int32, (rows, LANES), 1) < HEAD_DIM
    tiles = []
    for p in range(n_heads // 2):
        pair = q[:, p * LANES:(p + 1) * LANES]
        tiles.append(jnp.where(lo, pair, feat[(2 * p) * rows:(2 * p + 1) * rows]))
        tiles.append(jnp.where(lo, _swap_halves(pair), feat[(2 * p + 1) * rows:(2 * p + 2) * rows]))
    return jnp.concatenate(tiles, axis=0)


def _with_key_features(kv, feat):
    lo = lax.broadcasted_iota(jnp.int32, kv.shape, 1) < HEAD_DIM
    return jnp.where(lo, kv, feat)


def _stack_pairs(q, n_heads):
    rows = q.shape[0]
    lo = lax.broadcasted_iota(jnp.int32, (rows, LANES), 1) < HEAD_DIM
    out = []
    for p in range(n_heads // 2):
        pair = q[:, p * LANES:(p + 1) * LANES]
        zero = jnp.zeros_like(pair)
        out.append(jnp.concatenate([jnp.where(lo, pair, zero), jnp.where(lo, zero, pair)], axis=0))
    return out


def _with_ones_for_keys(kv):
    lo = lax.broadcasted_iota(jnp.int32, kv.shape, 1) < HEAD_DIM
    return jnp.where(lo, jnp.ones_like(kv), kv)


def _unstack_normalized(o, n_heads):
    rows = o.shape[0] // n_heads
    lo = lax.broadcasted_iota(jnp.int32, (rows, LANES), 1) < HEAD_DIM
    pairs = []
    for p in range(n_heads // 2):
        even = o[(2 * p) * rows:(2 * p + 1) * rows]
        odd = o[(2 * p + 1) * rows:(2 * p + 2) * rows]
        pairs.append(jnp.where(lo, pltpu.roll(even, HEAD_DIM, 1) / even, odd / pltpu.roll(odd, HEAD_DIM, 1)))
    return jnp.concatenate(pairs, axis=1)


def _unstack_heads(o, n_heads):
    rows = o.shape[0] // n_heads
    lo = lax.broadcasted_iota(jnp.int32, (rows, LANES), 1) < HEAD_DIM
    pairs = []
    for p in range(n_heads // 2):
        even = o[(2 * p) * rows:(2 * p + 1) * rows]
        odd = o[(2 * p + 1) * rows:(2 * p + 2) * rows]
        pairs.append(jnp.where(lo, pltpu.roll(even, HEAD_DIM, 1), odd))
    return jnp.concatenate(pairs, axis=1)


def _inproj_kernel(x_ref, g_ref, w_ref, qa_ref, kva_ref, qb_ref, kvs_ref, kvw_ref, kvr_ref,
                   gate_ref, qc_ref, kc_ref, vc_ref):
    h = _rms(x_ref[...], g_ref[...]).astype(BF16)

    def proj(lo, hi):
        return _dot(h, w_ref[:, lo:hi])

    qa_ref[...] = (proj(_G_QA, _G_KVA) * LOG2E).astype(BF16)
    kva_ref[...] = proj(_G_KVA, _G_QB).astype(BF16)
    qb_ref[...] = (proj(_G_QB, _G_KVS) * LOG2E).astype(BF16)
    kvs_ref[...] = proj(_G_KVS, _G_KVW).astype(BF16)
    kvw_ref[...] = proj(_G_KVW, _G_KVC).astype(BF16)
    kvr_ref[...] = proj(_G_KVC, _G_GATE)
    gate_ref[...] = proj(_G_GATE, _G_QC)
    qc_ref[...] = (proj(_G_QC, _G_KC) * LOG2E).astype(BF16)
    kc_ref[...] = proj(_G_KC, _G_VC).astype(BF16)
    vc_ref[...] = proj(_G_VC, IN_PAD).astype(BF16)


def _inproj(x2, g, w):
    t = x2.shape[0]
    tm = INPROJ_TILE
    widths = [(A_Q, BF16), (LANES, BF16), (B_Q, BF16), (LANES, BF16), (LANES, BF16), (LANES, F32),
              (LANES, F32), (C_W, BF16), (C_W, BF16), (C_W, BF16)]
    return pl.pallas_call(
        _inproj_kernel,
        out_shape=[jax.ShapeDtypeStruct((t, wd), dt) for wd, dt in widths],
        grid=(t // tm,),
        in_specs=[pl.BlockSpec((tm, D_MODEL), lambda i: (i, 0)),
                  pl.BlockSpec((1, D_MODEL), lambda i: (0, 0)),
                  pl.BlockSpec((D_MODEL, IN_PAD), lambda i: (0, 0))],
        out_specs=[pl.BlockSpec((tm, wd), lambda i: (i, 0)) for wd, _ in widths],
        compiler_params=pltpu.CompilerParams(dimension_semantics=("arbitrary",), vmem_limit_bytes=VMEM_LIMIT),
        name="inproj",
    )(x2, g, w)


def _compress_kernel(x_ref, pos_ref, w1_ref, w2_ref, o_ref):
    half = CMP_LEN // 2
    n = x_ref.shape[1] // CMP_STRIDE
    h_lo = jnp.zeros((n, 2 * CMP_HIDDEN), F32)
    h_hi = jnp.zeros((n, 2 * CMP_HIDDEN), F32)
    for l in range(half):
        x = x_ref[0, pl.ds(l, n, stride=CMP_STRIDE), :]
        h_lo = h_lo + _dot((x + pos_ref[0, l]).astype(BF16), w1_ref[0, l])
        h_hi = h_hi + _dot((x + pos_ref[1, l]).astype(BF16), w1_ref[1, l])
    h = h_lo + pltpu.roll(h_hi, n - 1, 0)
    o_ref[0] = _dot(jax.nn.gelu(h).astype(BF16), w2_ref[...]).astype(BF16)


def _compress(x, pos, w1, w2):
    b, s, _ = x.shape
    n = s // CMP_STRIDE
    half = CMP_LEN // 2
    return pl.pallas_call(
        _compress_kernel,
        out_shape=jax.ShapeDtypeStruct((b, n, LANES), BF16),
        grid=(b,),
        in_specs=[pl.BlockSpec((1, s, LANES), lambda i: (i, 0, 0)),
                  pl.BlockSpec((2, half, 1, LANES), lambda i: (0, 0, 0, 0)),
                  pl.BlockSpec((2, half, LANES, 2 * CMP_HIDDEN), lambda i: (0, 0, 0, 0)),
                  pl.BlockSpec((2 * CMP_HIDDEN, LANES), lambda i: (0, 0))],
        out_specs=pl.BlockSpec((1, n, LANES), lambda i: (i, 0, 0)),
        compiler_params=pltpu.CompilerParams(dimension_semantics=("arbitrary",), vmem_limit_bytes=VMEM_LIMIT),
        name="compress",
    )(x, pos, w1, w2)


def _mixa_kernel(sink_ref, q_ref, kvp_ref, kvc_ref, pfp_ref, pfc_ref, qf_ref, g_ref, o_ref, *, slopes):
    i = pl.program_id(1)
    span = 2 * Q_BLOCK
    row = lax.broadcasted_iota(jnp.int32, (Q_BLOCK, span), 0)
    col = lax.broadcasted_iota(jnp.int32, (Q_BLOCK, span), 1)
    dist = row + Q_BLOCK - col
    band = (dist >= 0) & (dist < A_WINDOW)
    kv_rows = jnp.concatenate([kvp_ref[0], kvc_ref[0]], axis=0)
    pf_rows = jnp.concatenate([pfp_ref[...], pfc_ref[...]], axis=0)

    n_sub = A_TILE // Q_BLOCK
    kvs, scores = [], []
    for r in range(n_sub):
        q = q_ref[0, r * Q_BLOCK:(r + 1) * Q_BLOCK, :]
        kvs.append(kv_rows[r * Q_BLOCK:r * Q_BLOCK + span])
        kx = _with_key_features(kvs[r], pf_rows[r * Q_BLOCK:r * Q_BLOCK + span])
        scores.append(_dot_nt(_stack_heads(q, A_HEADS, qf_ref[...]), kx))
    for r in range(n_sub):
        kv, s_all = kvs[r], scores[r]
        valid = band & ((col >= Q_BLOCK) | (i > 0)) if r == 0 else band
        t_row = (i * A_TILE + r * Q_BLOCK + lax.broadcasted_iota(jnp.int32, (Q_BLOCK, 1), 0)).astype(F32)
        probs, inv_l = [], []
        for hd in range(A_HEADS):
            sink = sink_ref[hd] * LOG2E + slopes[hd] * t_row
            s = jnp.where(valid, s_all[hd * Q_BLOCK:(hd + 1) * Q_BLOCK], NEG_INF)
            m = jnp.maximum(jnp.max(s, axis=-1, keepdims=True), sink)
            e = jnp.exp2(s - m)
            inv_l.append(1.0 / (jnp.sum(e, axis=-1, keepdims=True) + jnp.exp2(sink - m)))
            probs.append(e.astype(BF16))
        o_all = _dot(jnp.concatenate(probs, axis=0), kv)
        o_all = jnp.concatenate([o_all[hd * Q_BLOCK:(hd + 1) * Q_BLOCK] * inv_l[hd] for hd in range(A_HEADS)], axis=0)
        o_ref[0, r * Q_BLOCK:(r + 1) * Q_BLOCK, :] = _rms(_unstack_heads(o_all, A_HEADS), g_ref[...]).astype(BF16)


def _mixer_a(sinks, qa, kva, g):
    b, s, _ = qa.shape
    slopes, _ = _alibi_slopes()
    pos_feat = _key_features(np.arange(s))
    sub = A_TILE // Q_BLOCK
    prev_block = lambda i: jnp.maximum(sub * i - 1, 0)
    return pl.pallas_call(
        functools.partial(_mixa_kernel, slopes=slopes),
        out_shape=jax.ShapeDtypeStruct((b, s, A_Q), BF16),
        grid=(b, s // A_TILE),
        in_specs=[pl.BlockSpec(memory_space=pltpu.SMEM),
                  pl.BlockSpec((1, A_TILE, A_Q), lambda bi, i: (bi, i, 0)),
                  pl.BlockSpec((1, Q_BLOCK, LANES), lambda bi, i: (bi, prev_block(i), 0)),
                  pl.BlockSpec((1, A_TILE, LANES), lambda bi, i: (bi, i, 0)),
                  pl.BlockSpec((Q_BLOCK, LANES), lambda bi, i: (prev_block(i), 0)),
                  pl.BlockSpec((A_TILE, LANES), lambda bi, i: (i, 0)),
                  pl.BlockSpec((A_HEADS * Q_BLOCK, LANES), lambda bi, i: (0, 0)),
                  pl.BlockSpec((1, A_Q), lambda bi, i: (0, 0))],
        out_specs=pl.BlockSpec((1, A_TILE, A_Q), lambda bi, i: (bi, i, 0)),
        compiler_params=pltpu.CompilerParams(dimension_semantics=("arbitrary", "arbitrary"),
                                             vmem_limit_bytes=VMEM_LIMIT),
        name="mixer_a",
    )(sinks, qa, kva, kva, pos_feat, pos_feat, _query_features(slopes, Q_BLOCK), g)


def _mixb_kernel(q_ref, kvs_ref, kvw_ref, kvc_ref, gate_ref, pf_ref, sf_ref, cf_ref, qf_ref, mmap_ref, rep_ref,
                 g_ref, o_ref, acc_ref, m_ref, *, n_cmp, n_slc):
    i = pl.program_id(1)
    t0 = i * Q_WIDE
    nb = q_ref.shape[0]
    q_all = [_stack_heads(q_ref[bb], B_HEADS, qf_ref[...]) for bb in range(nb)]

    def head_rows(x, hd, rows=Q_WIDE):
        return x[hd * rows:(hd + 1) * rows]

    kvc = [kvc_ref[bb] for bb in range(nb)]
    n_pad = kvc[0].shape[0]
    s_cmp = [_dot_nt(q_all[bb], _with_key_features(kvc[bb], cf_ref[...])) for bb in range(nb)]
    wspan = B_WINDOW + Q_BLOCK
    row_w = lax.broadcasted_iota(jnp.int32, (Q_BLOCK, wspan), 0)
    col_w = lax.broadcasted_iota(jnp.int32, (Q_BLOCK, wspan), 1)
    n_sub = Q_WIDE // Q_BLOCK
    valid_w = []
    kvw = [[None] * n_sub for _ in range(nb)]
    s_win = [[None] * n_sub for _ in range(nb)]
    for r in range(n_sub):
        r0 = t0 + r * Q_BLOCK
        w0 = pl.multiple_of(jnp.clip(r0 - B_WINDOW, 0, kvw_ref.shape[1] - wspan), LANES)
        dist_w = (row_w - col_w) + (r0 - w0)
        valid_w.append((dist_w >= 0) & (dist_w < B_WINDOW))
        for bb in range(nb):
            kvw[bb][r] = kvw_ref[bb, pl.ds(w0, wspan), :]
            q_sub = jnp.concatenate([head_rows(q_all[bb], hd)[r * Q_BLOCK:(r + 1) * Q_BLOCK] for hd in range(B_HEADS)],
                                    axis=0)
            s_win[bb][r] = _dot_nt(q_sub, _with_key_features(kvw[bb][r], pf_ref[pl.ds(w0, wspan), :]))

    gates = []
    for bb in range(nb):
        sig = jax.nn.sigmoid(gate_ref[bb]).astype(BF16)
        gates.append([_dot(sig, rep_ref[r]) for r in range(3)])

    row_c = lax.broadcasted_iota(jnp.int32, (Q_WIDE, n_pad), 0)
    col_c = lax.broadcasted_iota(jnp.int32, (Q_WIDE, n_pad), 1)
    vis_c = ((t0 + row_c) >= (col_c * CMP_STRIDE + (CMP_LEN - 1))) & (col_c < n_cmp)
    row1 = lax.broadcasted_iota(jnp.int32, (Q_WIDE, 1), 0)
    any_vis = ((t0 + row1) >= (CMP_LEN - 1)).astype(F32)
    blk = lax.broadcasted_iota(jnp.int32, (n_slc, Q_WIDE), 0)
    tq = t0 + lax.broadcasted_iota(jnp.int32, (n_slc, Q_WIDE), 1)
    cur = tq // SLC_LEN
    valid_b = blk <= cur
    forced = (blk == 0) | (blk == cur) | (blk == cur - 1)
    lane_q = lax.broadcasted_iota(jnp.int32, (B_HEADS * Q_WIDE, LANES), 1)
    sel_lanes = (lane_q >= SEL_LANE) & (lane_q < SEL_LANE + n_slc)
    o_cmp, q_slc = [], []
    for bb in range(nb):
        probs = []
        p_sum = jnp.zeros((Q_WIDE, n_pad), F32)
        for hd in range(B_HEADS):
            s = jnp.where(vis_c, head_rows(s_cmp[bb], hd), NEG_INF)
            m = jnp.max(s, axis=-1, keepdims=True)
            e = jnp.exp2(s - m)
            pr = e * (any_vis / jnp.sum(e, axis=-1, keepdims=True))
            p_sum = p_sum + pr
            probs.append(pr.astype(BF16))
        o_cmp.append(_unstack_heads(_dot(jnp.concatenate(probs, axis=0), kvc[bb]), B_HEADS))
        p_hi, p_lo = _split_bf16(p_sum)
        imp = (_dot_nt(mmap_ref[...], p_hi) + _dot_nt(mmap_ref[...], p_lo))[:n_slc]
        val = jnp.where(forced, FORCE, jnp.where(valid_b, imp, -FORCE))
        rank = jnp.zeros((n_slc, Q_WIDE), F32)
        for j in range(n_slc):
            vj = val[j:j + 1, :]
            ahead = (vj > val) | ((vj == val) & (blk > j))
            rank = rank + ahead.astype(F32)
        unsel = 1.0 - ((rank < SLC_TOP) & valid_b).astype(F32)
        unsel = jnp.concatenate([jnp.zeros((SEL_LANE, Q_WIDE), F32), unsel,
                                 jnp.zeros((LANES - SEL_LANE - n_slc, Q_WIDE), F32)], axis=0)
        unsel_q = unsel.T.astype(BF16)
        q_slc.append(jnp.where(sel_lanes, jnp.concatenate([unsel_q] * B_HEADS, axis=0), q_all[bb]))

    ob_rest = []
    for bb in range(nb):
        o_sub = [[None] * n_sub for _ in range(B_HEADS)]
        for r in range(n_sub):
            probs = []
            for hd in range(B_HEADS):
                s = jnp.where(valid_w[r], head_rows(s_win[bb][r], hd, Q_BLOCK), NEG_INF)
                probs.append(jnp.exp2(s - jnp.max(s, axis=-1, keepdims=True)).astype(BF16))
            o_all = _dot(jnp.concatenate(probs, axis=0), _with_ones_for_keys(kvw[bb][r]))
            for hd in range(B_HEADS):
                o_sub[hd][r] = head_rows(o_all, hd, Q_BLOCK)
        o_win = _unstack_normalized(
            jnp.concatenate([o_sub[hd][r] for hd in range(B_HEADS) for r in range(n_sub)], axis=0), B_HEADS)
        ob_rest.append(gates[bb][0] * o_cmp[bb] + gates[bb][2] * o_win)

    acc_ref[...] = jnp.zeros(acc_ref.shape, F32)
    m_ref[...] = jnp.full(m_ref.shape, NEG_INF, F32)

    def slc_chunk(c, width):
        k0 = pl.multiple_of(c * SLC_CHUNK, SLC_CHUNK)
        row_k = lax.broadcasted_iota(jnp.int32, (Q_WIDE, width), 0)
        col_k = lax.broadcasted_iota(jnp.int32, (Q_WIDE, width), 1)
        causal = (col_k - row_k) <= (t0 - k0)
        feat = sf_ref[pl.ds(k0, width), :]
        for bb in range(nb):
            kv = kvs_ref[bb, pl.ds(k0, width), :]
            s_all = _dot_nt(q_slc[bb], _with_key_features(kv, feat))
            probs, alphas = [], []
            for hd in range(B_HEADS):
                g = bb * B_HEADS + hd
                s = jnp.where(causal, head_rows(s_all, hd), NEG_INF)
                m_old = m_ref[g]
                m_new = jnp.maximum(m_old, jnp.max(s, axis=-1, keepdims=True))
                alpha = jnp.exp2(m_old - m_new)
                m_ref[g] = m_new
                probs.append(jnp.exp2(s - _lane_tile(m_new, width)).astype(BF16))
                alphas.append(alpha)
            o_all = _dot(jnp.concatenate(probs, axis=0), _with_ones_for_keys(kv))
            for hd in range(B_HEADS):
                g = bb * B_HEADS + hd
                acc_ref[g] = alphas[hd] * acc_ref[g] + head_rows(o_all, hd)

    n_half = lax.shift_right_logical(t0 + Q_WIDE + SLC_CHUNK // 2 - 1, (SLC_CHUNK // 2).bit_length() - 1)
    n_full = lax.shift_right_logical(n_half, 1)

    def full_chunk(c, carry):
        slc_chunk(c, SLC_CHUNK)
        return carry

    lax.fori_loop(0, n_full, full_chunk, 0)

    @pl.when((n_half & 1) == 1)
    def _():
        slc_chunk(n_full, SLC_CHUNK // 2)

    for bb in range(nb):
        o_slc = _unstack_normalized(
            jnp.concatenate([acc_ref[bb * B_HEADS + hd] for hd in range(B_HEADS)], axis=0), B_HEADS)
        ob = ob_rest[bb] + gates[bb][1] * o_slc
        o_ref[bb] = _rms(ob, g_ref[...]).astype(BF16)


def _mixer_b(qb, kvs, kvw, kvc, gate, g):
    b, s, _ = qb.shape
    n_pad = kvc.shape[1]
    n_cmp = s // CMP_STRIDE - 1
    n_slc = s // SLC_LEN
    _, slopes = _alibi_slopes()
    pos_feat = _key_features(np.arange(s))
    slc_feat = _key_features(np.arange(s), SLC_LEN)
    cmp_feat = _key_features(np.minimum(np.arange(n_pad), n_cmp - 1) * CMP_STRIDE + (CMP_LEN - 1))
    cs = np.arange(n_pad)[None, :] * CMP_STRIDE
    ss = np.arange(LANES)[:, None] * SLC_LEN
    ov = np.maximum(0, np.minimum(cs + CMP_LEN, ss + SLC_LEN) - np.maximum(cs, ss)) / CMP_STRIDE
    ov = ov * (np.arange(n_pad)[None, :] < n_cmp) * (np.arange(LANES)[:, None] < n_slc)
    mmap_t = jnp.asarray(ov, BF16)
    rep = np.zeros((3, LANES, B_Q), np.float32)
    for r in range(3):
        for h in range(B_HEADS):
            rep[r, 3 * h + r, h * HEAD_DIM:(h + 1) * HEAD_DIM] = 1.0
    rep = jnp.asarray(rep, BF16)
    nb = B_BATCH if b % B_BATCH == 0 else 1
    return pl.pallas_call(
        functools.partial(_mixb_kernel, n_cmp=n_cmp, n_slc=n_slc),
        out_shape=jax.ShapeDtypeStruct((b, s, B_Q), BF16),
        grid=(b // nb, s // Q_WIDE),
        in_specs=[pl.BlockSpec((nb, Q_WIDE, B_Q), lambda bi, i: (bi, i, 0)),
                  pl.BlockSpec((nb, s, LANES), lambda bi, i: (bi, 0, 0)),
                  pl.BlockSpec((nb, s, LANES), lambda bi, i: (bi, 0, 0)),
                  pl.BlockSpec((nb, n_pad, LANES), lambda bi, i: (bi, 0, 0)),
                  pl.BlockSpec((nb, Q_WIDE, LANES), lambda bi, i: (bi, i, 0)),
                  pl.BlockSpec((s, LANES), lambda bi, i: (0, 0)),
                  pl.BlockSpec((s, LANES), lambda bi, i: (0, 0)),
                  pl.BlockSpec((n_pad, LANES), lambda bi, i: (0, 0)),
                  pl.BlockSpec((B_HEADS * Q_WIDE, LANES), lambda bi, i: (0, 0)),
                  pl.BlockSpec((LANES, n_pad), lambda bi, i: (0, 0)),
                  pl.BlockSpec((3, LANES, B_Q), lambda bi, i: (0, 0, 0)),
                  pl.BlockSpec((1, B_Q), lambda bi, i: (0, 0))],
        out_specs=pl.BlockSpec((nb, Q_WIDE, B_Q), lambda bi, i: (bi, i, 0)),
        scratch_shapes=[pltpu.VMEM((nb * B_HEADS, Q_WIDE, LANES), F32),
                        pltpu.VMEM((nb * B_HEADS, Q_WIDE, LANES), F32)],
        compiler_params=pltpu.CompilerParams(dimension_semantics=("arbitrary", "arbitrary"),
                                             vmem_limit_bytes=VMEM_LIMIT),
        name="mixer_b",
    )(qb, kvs, kvw, kvc, gate, pos_feat, slc_feat, cmp_feat, _query_features(slopes, Q_WIDE), mmap_t, rep, g)


def _mixc_kernel(q_ref, k_ref, v_ref, tri_ref, g_ref, o_ref, acc_ref, carry_ref, z_ref, w_ref):
    i = pl.program_id(1)
    t0 = i * Q_WIDE
    nb = q_ref.shape[0]
    n_pair = C_HEADS // 2
    q_pairs = [qp for bb in range(nb) for qp in _stack_pairs(q_ref[bb], C_HEADS)]
    acc_ref[...] = jnp.zeros(acc_ref.shape, F32)
    carry_ref[...] = jnp.zeros(carry_ref.shape, F32)
    row = lax.broadcasted_iota(jnp.int32, (Q_WIDE, KEY_CHUNK), 0)
    col = lax.broadcasted_iota(jnp.int32, (Q_WIDE, KEY_CHUNK), 1)
    cr = col - row
    n_chunks = (t0 + Q_WIDE + KEY_CHUNK - 1) // KEY_CHUNK

    def chunk_rows(ref, bb, c):
        return ref[bb, pl.ds(pl.multiple_of(c * KEY_CHUNK, KEY_CHUNK), KEY_CHUNK), :]

    def logits_to_scratch(c):
        for bb in range(nb):
            k = chunk_rows(k_ref, bb, c)
            for p in range(n_pair):
                z_ref[bb * n_pair + p] = _dot_nt(q_pairs[bb * n_pair + p], k[:, p * LANES:(p + 1) * LANES])

    def add_values(c):
        for bb in range(nb):
            v = chunk_rows(v_ref, bb, c)
            for p in range(n_pair):
                o = _dot(w_ref[bb * n_pair + p], v[:, p * LANES:(p + 1) * LANES])
                g = bb * C_HEADS + 2 * p
                acc_ref[g] = acc_ref[g] + o[:Q_WIDE]
                acc_ref[g + 1] = acc_ref[g + 1] + o[Q_WIDE:]

    def sweep(c, diagonal):
        if not diagonal:
            add_values(c + 1)
        past = cr < (t0 - c * KEY_CHUNK)
        n_all = nb * C_HEADS
        log_w, parts = [], []
        for g in range(n_all):
            z = z_ref[g // 2, (g % 2) * Q_WIDE:(g % 2 + 1) * Q_WIDE, :]
            neg_abs = lax.bitcast_convert_type(lax.bitcast_convert_type(z, jnp.uint32) | jnp.uint32(0x80000000), F32)
            drop = jnp.maximum(z, 0.0) + jnp.log2(1.0 + jnp.exp2(neg_abs))
            dk = jnp.where(past, drop, 0.0) if diagonal else drop
            parts.append(dk.astype(BF16))
            log_w.append(z - drop - _lane_tile(carry_ref[g], KEY_CHUNK))
            carry_ref[g] = carry_ref[g] + jnp.sum(dk, axis=-1, keepdims=True)
        tails = _dot(jnp.concatenate(parts, axis=0), tri_ref[...])
        logits_to_scratch(jnp.maximum(c - 1, 0))
        weights = []
        for g in range(n_all):
            w = jnp.exp2(log_w[g] - tails[g * Q_WIDE:(g + 1) * Q_WIDE])
            weights.append((jnp.where(past, w, 0.0) if diagonal else w).astype(BF16))
        for u in range(nb * n_pair):
            w_ref[u] = jnp.concatenate(weights[2 * u:2 * u + 2], axis=0)

    logits_to_scratch(n_chunks - 1)
    sweep(n_chunks - 1, True)

    def earlier(j, carry):
        sweep(n_chunks - 1 - j, False)
        return carry

    lax.fori_loop(1, n_chunks, earlier, 0)
    add_values(0)
    lane = lax.broadcasted_iota(jnp.int32, (Q_WIDE, LANES), 1)
    lo = lane < HEAD_DIM
    for bb in range(nb):
        oc = jnp.concatenate([jnp.where(lo, acc_ref[bb * C_HEADS + 2 * p], acc_ref[bb * C_HEADS + 2 * p + 1])
                              for p in range(n_pair)], axis=1)
        o_ref[bb] = _rms(oc, g_ref[...]).astype(BF16)


def _mixer_c(qc, kc, vc, g):
    b, s, _ = qc.shape
    tri = jnp.asarray(np.arange(KEY_CHUNK)[:, None] > np.arange(KEY_CHUNK)[None, :], BF16)
    nb = C_BATCH if b % C_BATCH == 0 else 1
    return pl.pallas_call(
        _mixc_kernel,
        out_shape=jax.ShapeDtypeStruct((b, s, C_W), BF16),
        grid=(b // nb, s // Q_WIDE),
        in_specs=[pl.BlockSpec((nb, Q_WIDE, C_W), lambda bi, i: (bi, i, 0)),
                  pl.BlockSpec((nb, s, C_W), lambda bi, i: (bi, 0, 0)),
                  pl.BlockSpec((nb, s, C_W), lambda bi, i: (bi, 0, 0)),
                  pl.BlockSpec((KEY_CHUNK, KEY_CHUNK), lambda bi, i: (0, 0)),
                  pl.BlockSpec((1, C_W), lambda bi, i: (0, 0))],
        out_specs=pl.BlockSpec((nb, Q_WIDE, C_W), lambda bi, i: (bi, i, 0)),
        scratch_shapes=[pltpu.VMEM((nb * C_HEADS, Q_WIDE, LANES), F32),
                        pltpu.VMEM((nb * C_HEADS, Q_WIDE, LANES), F32),
                        pltpu.VMEM((nb * C_HEADS // 2, 2 * Q_WIDE, KEY_CHUNK), F32),
                        pltpu.VMEM((nb * C_HEADS // 2, 2 * Q_WIDE, KEY_CHUNK), BF16)],
        compiler_params=pltpu.CompilerParams(dimension_semantics=("arbitrary", "arbitrary"),
                                             vmem_limit_bytes=VMEM_LIMIT),
        name="mixer_c",
    )(qc, kc, vc, tri, g)


def _outffn_kernel(x_ref, ma_ref, mb_ref, mc_ref, wo_ref, gf_ref, wg_ref, wu_ref, wd_ref, gl_ref, o_ref, *, final):
    x = (x_ref[...] + _dot(ma_ref[...], wo_ref[:A_Q, :]) + _dot(mb_ref[...], wo_ref[A_Q:A_Q + B_Q, :])
         + _dot(mc_ref[...], wo_ref[A_Q + B_Q:, :]))
    h = _rms(x, gf_ref[...]).astype(BF16)
    o_ref[...] = x
    for c in range(D_FF // FF_CHUNK):
        sl = slice(c * FF_CHUNK, (c + 1) * FF_CHUNK)
        gate = _dot(h, wg_ref[:, sl])
        up = _dot(h, wu_ref[:, sl])
        o_ref[...] += _dot((jax.nn.silu(gate) * up).astype(BF16), wd_ref[sl, :])
    if final:
        o_ref[...] = _rms(o_ref[...], gl_ref[...])


def _outffn(x2, ma, mb, mc, wo, gf, wg, wu, wd, gl, final):
    t = x2.shape[0]
    tm = TOKEN_TILE
    const = lambda i: (0, 0)
    tok = lambda i: (i, 0)
    once = pl.Buffered(1)
    return pl.pallas_call(
        functools.partial(_outffn_kernel, final=final),
        out_shape=jax.ShapeDtypeStruct((t, D_MODEL), F32),
        grid=(t // tm,),
        in_specs=[pl.BlockSpec((tm, D_MODEL), tok),
                  pl.BlockSpec((tm, A_Q), tok),
                  pl.BlockSpec((tm, B_Q), tok),
                  pl.BlockSpec((tm, C_W), tok),
                  pl.BlockSpec((MIX_WIDTH, D_MODEL), const, pipeline_mode=once),
                  pl.BlockSpec((1, D_MODEL), const),
                  pl.BlockSpec((D_MODEL, D_FF), const, pipeline_mode=once),
                  pl.BlockSpec((D_MODEL, D_FF), const, pipeline_mode=once),
                  pl.BlockSpec((D_FF, D_MODEL), const, pipeline_mode=once),
                  pl.BlockSpec((1, D_MODEL), const)],
        out_specs=pl.BlockSpec((tm, D_MODEL), tok),
        compiler_params=pltpu.CompilerParams(dimension_semantics=("arbitrary",), vmem_limit_bytes=VMEM_LIMIT),
        name="outproj_ffn",
    )(x2, ma, mb, mc, wo, gf, wg, wu, wd, gl)


def _regroup_w_in(w_in):
    sizes = (A_Q, HEAD_DIM, HEAD_DIM, B_Q, HEAD_DIM, HEAD_DIM, HEAD_DIM, HEAD_DIM, HEAD_DIM, HEAD_DIM,
             N_GATES, C_W, C_W, C_W)
    offs = np.concatenate([[0], np.cumsum(sizes)])
    qa, ka, va, qb, kcb, vcb, ksb, vsb, kwb, vwb, gb, qc, kc, vc = [w_in[:, offs[j]:offs[j + 1]]
                                                                     for j in range(len(sizes))]
    pad = jnp.zeros((w_in.shape[0], LANES - N_GATES), w_in.dtype)
    cols = [qa * SCALE, ka, va, qb * SCALE, ksb, vsb, kwb, vwb, kcb, vcb, gb, pad, qc * SCALE, kc, vc]
    return jnp.concatenate(cols, axis=1).astype(BF16)


def _layer(x2, b, s, w_in, w_out, g_attn, g_ffn, g_out_a, g_out_b, g_out_c, sinks, cmp_pos, cmp_w1, cmp_w2,
           w_gate, w_up, w_down, g_final, final):
    row = lambda v: v.reshape(1, -1)
    qa, kva, qb, kvs, kvw, kvr, gate, qc, kc, vc = _inproj(x2, row(g_attn), _regroup_w_in(w_in))
    r3 = lambda a: a.reshape(b, s, a.shape[-1])

    half = CMP_LEN // 2
    pos = jnp.concatenate([cmp_pos[0], cmp_pos[1]], axis=-1).reshape(2, half, 1, LANES)
    z1 = jnp.zeros_like(cmp_w1[0])
    w1 = jnp.concatenate([jnp.concatenate([cmp_w1[0], z1], axis=-1), jnp.concatenate([z1, cmp_w1[1]], axis=-1)], axis=1)
    w1 = w1.reshape(2, half, LANES, 2 * CMP_HIDDEN).astype(BF16)
    z2 = jnp.zeros_like(cmp_w2[0])
    w2 = jnp.concatenate([jnp.concatenate([cmp_w2[0], z2], axis=1), jnp.concatenate([z2, cmp_w2[1]], axis=1)],
                         axis=0).astype(BF16)
    kvc = _compress(r3(kvr), pos, w1, w2)

    ma = _mixer_a(sinks, r3(qa), r3(kva), row(g_out_a))
    mb = _mixer_b(r3(qb), r3(kvs), r3(kvw), kvc, r3(gate), row(g_out_b))
    mc = _mixer_c(r3(qc), r3(kc), r3(vc), row(g_out_c))
    t = b * s
    return _outffn(x2, ma.reshape(t, A_Q), mb.reshape(t, B_Q), mc.reshape(t, C_W), w_out.astype(BF16),
                   row(g_ffn), w_gate.astype(BF16), w_up.astype(BF16), w_down.astype(BF16), row(g_final), final)


def kernel(x, w_in, w_out, g_attn, g_ffn, g_out_a, g_out_b, g_out_c, sinks, cmp_pos, cmp_w1, cmp_w2, w_gate, w_up,
           w_down, g_final):
    b, s, d = x.shape
    depth = w_in.shape[0]
    assert d == D_MODEL and s % KEY_CHUNK == 0 and (b * s) % TOKEN_TILE == 0 and (b * s) % INPROJ_TILE == 0
    assert s >= B_WINDOW + Q_WIDE and s % SLC_CHUNK == 0 and s % A_TILE == 0 and (s // CMP_STRIDE) % LANES == 0
    x2 = x.reshape(b * s, d)
    for l in range(depth):
        x2 = _layer(x2, b, s, w_in[l], w_out[l], g_attn[l], g_ffn[l], g_out_a[l], g_out_b[l], g_out_c[l], sinks[l],
                    cmp_pos[l], cmp_w1[l], cmp_w2[l], w_gate[l], w_up[l], w_down[l], g_final, l == depth - 1)
    return x2.reshape(b, s, d)
```

```python
import functools

import jax
import jax.numpy as jnp
import numpy as np
from jax import lax
from jax.experimental import pallas as pl
from jax.experimental.pallas import tpu as pltpu

F32 = jnp.float32
BF16 = jnp.bfloat16

D_MODEL = 1024
HEAD_DIM = 64
LANES = 128
Q_BLOCK = 128
Q_WIDE = 256
A_TILE = 256
B_BATCH = 4
C_BATCH = 4
A_HEADS = 8
A_WINDOW = 128
B_HEADS = 4
CMP_LEN = 32
CMP_STRIDE = 16
CMP_HIDDEN = 128
SLC_LEN = 64
SLC_TOP = 8
B_WINDOW = 512
C_HEADS = 4
A_Q = A_HEADS * HEAD_DIM
B_Q = B_HEADS * HEAD_DIM
C_W = C_HEADS * HEAD_DIM
MIX_WIDTH = A_Q + B_Q + C_W
D_FF = 2816
N_GATES = B_HEADS * 3
NEG_INF = -1e30
MASK_BIG = 2.0 ** 100
SEL_LANE = 72
FORCE = 1e4
EPS = 1e-6
SCALE = HEAD_DIM ** -0.5
LOG2E = 1.4426950408889634

KEY_CHUNK = 256
SLC_CHUNK = 512
FF_CHUNK = 256
TOKEN_TILE = 512
INPROJ_TILE = 1024
VMEM_LIMIT = 56 * 1024 * 1024

_G_QA, _G_KVA, _G_QB, _G_KVS, _G_KVW, _G_KVC, _G_GATE, _G_QC, _G_KC, _G_VC, IN_PAD = (
    0, 512, 640, 896, 1024, 1152, 1280, 1408, 1664, 1920, 2176)


def _alibi_slopes():
    n = A_HEADS + B_HEADS
    sl = 2.0 ** (-8.0 * np.arange(1, n + 1) / n) * LOG2E
    return [float(v) for v in sl[:A_HEADS]], [float(v) for v in sl[A_HEADS:]]


def _round_bf16(x):
    bits = np.ascontiguousarray(x, np.float32).view(np.uint32).astype(np.uint64)
    bits = ((bits + 0x7FFF + ((bits >> 16) & 1)) >> 16) << 16
    return bits.astype(np.uint32).view(np.float32)


def _key_features(pos, block_len=None):
    pos = np.asarray(pos, np.int64)
    assert pos.max() < 2048
    f = np.zeros((len(pos), LANES), np.float32)
    for j in range(3):
        f[:, HEAD_DIM + 2 * j] = 256 * (pos // 256)
        f[:, HEAD_DIM + 2 * j + 1] = pos % 256
    if block_len is not None:
        assert SEL_LANE + pos.max() // block_len < LANES
        f[np.arange(len(pos)), SEL_LANE + pos // block_len] = -MASK_BIG
    return jnp.asarray(f, BF16)


def _query_features(slopes, rows):
    f = np.zeros((len(slopes) * rows, LANES), np.float32)
    for h, slope in enumerate(slopes):
        rest = np.float32(slope)
        for j in range(3):
            part = _round_bf16(rest)
            rest = np.float32(rest - part)
            f[h * rows:(h + 1) * rows, HEAD_DIM + 2 * j:HEAD_DIM + 2 * j + 2] = part
    return jnp.asarray(f, BF16)


def _dot(a, b):
    return jnp.dot(a, b, preferred_element_type=F32)


def _dot_nt(a, b):
    return lax.dot_general(a, b, (((1,), (1,)), ((), ())), preferred_element_type=F32)


def _split_bf16(x):
    hi = x.astype(BF16)
    lo = (x - hi.astype(F32)).astype(BF16)
    return hi, lo


def _rms(x, g):
    ms = jnp.mean(x * x, axis=-1, keepdims=True)
    return x * lax.rsqrt(ms + EPS) * g


def _lane_tile(x, width):
    return jnp.concatenate([x] * (width // LANES), axis=1)


def _swap_halves(x):
    return jnp.concatenate([x[:, HEAD_DIM:], x[:, :HEAD_DIM]], axis=1)


def _stack_heads(q, n_heads, feat):
    rows = q.shape[0]
    lo = lax.broadcasted_iota(jnp.int32, (rows, LANES), 1) < HEAD_DIM
    tiles = []
    for p in range(n_heads // 2):
        pair = q[:, p * LANES:(p + 1) * LANES]
        tiles.append(jnp.where(lo, pair, feat[(2 * p) * rows:(2 * p + 1) * rows]))
        tiles.append(jnp.where(lo, _swap_halves(pair), feat[(2 * p + 1) * rows:(2 * p + 2) * rows]))
    return jnp.concatenate(tiles, axis=0)


def _with_key_features(kv, feat):
    lo = lax.broadcasted_iota(jnp.int32, kv.shape, 1) < HEAD_DIM
    return jnp.where(lo, kv, feat)


def _stack_pairs(q, n_heads):
    rows = q.shape[0]
    lo = lax.broadcasted_iota(jnp.int32, (rows, LANES), 1) < HEAD_DIM
    out = []
    for p in range(n_heads // 2):
        pair = q[:, p * LANES:(p + 1) * LANES]
        zero = jnp.zeros_like(pair)
        out.append(jnp.concatenate([jnp.where(lo, pair, zero), jnp.where(lo, zero, pair)], axis=0))
    return out


def _with_ones_for_keys(kv):
    lo = lax.broadcasted_iota(jnp.int32, kv.shape, 1) < HEAD_DIM
    return jnp.where(lo, jnp.ones_like(kv), kv)


def _unstack_normalized(o, n_heads):
    rows = o.shape[0] // n_heads
    lo = lax.broadcasted_iota(jnp.int32, (rows, LANES), 1) < HEAD_DIM
    pairs = []
    for p in range(n_heads // 2):
        even = o[(2 * p) * rows:(2 * p + 1) * rows]
        odd = o[(2 * p + 1) * rows:(2 * p + 2) * rows]
        pairs.append(jnp.where(lo, pltpu.roll(even, HEAD_DIM, 1) / even, odd / pltpu.roll(odd, HEAD_DIM, 1)))
    return jnp.concatenate(pairs, axis=1)


def _unstack_heads(o, n_heads):
    rows = o.shape[0] // n_heads
    lo = lax.broadcasted_iota(jnp.int32, (rows, LANES), 1) < HEAD_DIM
    pairs = []
    for p in range(n_heads // 2):
        even = o[(2 * p) * rows:(2 * p + 1) * rows]
        odd = o[(2 * p + 1) * rows:(2 * p + 2) * rows]
        pairs.append(jnp.where(lo, pltpu.roll(even, HEAD_DIM, 1), odd))
    return jnp.concatenate(pairs, axis=1)


def _inproj_kernel(x_ref, g_ref, w_ref, qa_ref, kva_ref, qb_ref, kvs_ref, kvw_ref, kvr_ref,
                   gate_ref, qc_ref, kc_ref, vc_ref):
    h = _rms(x_ref[...], g_ref[...]).astype(BF16)

    def proj(lo, hi):
        return _dot(h, w_ref[:, lo:hi])

    qa_ref[...] = (proj(_G_QA, _G_KVA) * LOG2E).astype(BF16)
    kva_ref[...] = proj(_G_KVA, _G_QB).astype(BF16)
    qb_ref[...] = (proj(_G_QB, _G_KVS) * LOG2E).astype(BF16)
    kvs_ref[...] = proj(_G_KVS, _G_KVW).astype(BF16)
    kvw_ref[...] = proj(_G_KVW, _G_KVC).astype(BF16)
    kvr_ref[...] = proj(_G_KVC, _G_GATE)
    gate_ref[...] = proj(_G_GATE, _G_QC)
    qc_ref[...] = (proj(_G_QC, _G_KC) * LOG2E).astype(BF16)
    kc_ref[...] = proj(_G_KC, _G_VC).astype(BF16)
    vc_ref[...] = proj(_G_VC, IN_PAD).astype(BF16)


def _inproj(x2, g, w):
    t = x2.shape[0]
    tm = INPROJ_TILE
    widths = [(A_Q, BF16), (LANES, BF16), (B_Q, BF16), (LANES, BF16), (LANES, BF16), (LANES, F32),
              (LANES, F32), (C_W, BF16), (C_W, BF16), (C_W, BF16)]
    return pl.pallas_call(
        _inproj_kernel,
        out_shape=[jax.ShapeDtypeStruct((t, wd), dt) for wd, dt in widths],
        grid=(t // tm,),
        in_specs=[pl.BlockSpec((tm, D_MODEL), lambda i: (i, 0)),
                  pl.BlockSpec((1, D_MODEL), lambda i: (0, 0)),
                  pl.BlockSpec((D_MODEL, IN_PAD), lambda i: (0, 0))],
        out_specs=[pl.BlockSpec((tm, wd), lambda i: (i, 0)) for wd, _ in widths],
        compiler_params=pltpu.CompilerParams(dimension_semantics=("arbitrary",), vmem_limit_bytes=VMEM_LIMIT),
        name="inproj",
    )(x2, g, w)


def _compress_kernel(x_ref, pos_ref, w1_ref, w2_ref, o_ref):
    half = CMP_LEN // 2
    n = x_ref.shape[1] // CMP_STRIDE
    h_lo = jnp.zeros((n, 2 * CMP_HIDDEN), F32)
    h_hi = jnp.zeros((n, 2 * CMP_HIDDEN), F32)
    for l in range(half):
        x = x_ref[0, pl.ds(l, n, stride=CMP_STRIDE), :]
        h_lo = h_lo + _dot((x + pos_ref[0, l]).astype(BF16), w1_ref[0, l])
        h_hi = h_hi + _dot((x + pos_ref[1, l]).astype(BF16), w1_ref[1, l])
    h = h_lo + pltpu.roll(h_hi, n - 1, 0)
    o_ref[0] = _dot(jax.nn.gelu(h).astype(BF16), w2_ref[...]).astype(BF16)


def _compress(x, pos, w1, w2):
    b, s, _ = x.shape
    n = s // CMP_STRIDE
    half = CMP_LEN // 2
    return pl.pallas_call(
        _compress_kernel,
        out_shape=jax.ShapeDtypeStruct((b, n, LANES), BF16),
        grid=(b,),
        in_specs=[pl.BlockSpec((1, s, LANES), lambda i: (i, 0, 0)),
                  pl.BlockSpec((2, half, 1, LANES), lambda i: (0, 0, 0, 0)),
                  pl.BlockSpec((2, half, LANES, 2 * CMP_HIDDEN), lambda i: (0, 0, 0, 0)),
                  pl.BlockSpec((2 * CMP_HIDDEN, LANES), lambda i: (0, 0))],
        out_specs=pl.BlockSpec((1, n, LANES), lambda i: (i, 0, 0)),
        compiler_params=pltpu.CompilerParams(dimension_semantics=("arbitrary",), vmem_limit_bytes=VMEM_LIMIT),
        name="compress",
    )(x, pos, w1, w2)


def _mixa_kernel(sink_ref, q_ref, kvp_ref, kvc_ref, pfp_ref, pfc_ref, qf_ref, g_ref, o_ref, *, slopes):
    i = pl.program_id(1)
    span = 2 * Q_BLOCK
    row = lax.broadcasted_iota(jnp.int32, (Q_BLOCK, span), 0)
    col = lax.broadcasted_iota(jnp.int32, (Q_BLOCK, span), 1)
    dist = row + Q_BLOCK - col
    band = (dist >= 0) & (dist < A_WINDOW)
    kv_rows = jnp.concatenate([kvp_ref[0], kvc_ref[0]], axis=0)
    pf_rows = jnp.concatenate([pfp_ref[...], pfc_ref[...]], axis=0)

    n_sub = A_TILE // Q_BLOCK
    kvs, scores = [], []
    for r in range(n_sub):
        q = q_ref[0, r * Q_BLOCK:(r + 1) * Q_BLOCK, :]
        kvs.append(kv_rows[r * Q_BLOCK:r * Q_BLOCK + span])
        kx = _with_key_features(kvs[r], pf_rows[r * Q_BLOCK:r * Q_BLOCK + span])
        scores.append(_dot_nt(_stack_heads(q, A_HEADS, qf_ref[...]), kx))
    for r in range(n_sub):
        kv, s_all = kvs[r], scores[r]
        valid = band & ((col >= Q_BLOCK) | (i > 0)) if r == 0 else band
        t_row = (i * A_TILE + r * Q_BLOCK + lax.broadcasted_iota(jnp.int32, (Q_BLOCK, 1), 0)).astype(F32)
        probs, inv_l = [], []
        for hd in range(A_HEADS):
            sink = sink_ref[hd] * LOG2E + slopes[hd] * t_row
            s = jnp.where(valid, s_all[hd * Q_BLOCK:(hd + 1) * Q_BLOCK], NEG_INF)
            m = jnp.maximum(jnp.max(s, axis=-1, keepdims=True), sink)
            e = jnp.exp2(s - m)
            inv_l.append(1.0 / (jnp.sum(e, axis=-1, keepdims=True) + jnp.exp2(sink - m)))
            probs.append(e.astype(BF16))
        o_all = _dot(jnp.concatenate(probs, axis=0), kv)
        o_all = jnp.concatenate([o_all[hd * Q_BLOCK:(hd + 1) * Q_BLOCK] * inv_l[hd] for hd in range(A_HEADS)], axis=0)
        o_ref[0, r * Q_BLOCK:(r + 1) * Q_BLOCK, :] = _rms(_unstack_heads(o_all, A_HEADS), g_ref[...]).astype(BF16)


def _mixer_a(sinks, qa, kva, g):
    b, s, _ = qa.shape
    slopes, _ = _alibi_slopes()
    pos_feat = _key_features(np.arange(s))
    sub = A_TILE // Q_BLOCK
    prev_block = lambda i: jnp.maximum(sub * i - 1, 0)
    return pl.pallas_call(
        functools.partial(_mixa_kernel, slopes=slopes),
        out_shape=jax.ShapeDtypeStruct((b, s, A_Q), BF16),
        grid=(b, s // A_TILE),
        in_specs=[pl.BlockSpec(memory_space=pltpu.SMEM),
                  pl.BlockSpec((1, A_TILE, A_Q), lambda bi, i: (bi, i, 0)),
                  pl.BlockSpec((1, Q_BLOCK, LANES), lambda bi, i: (bi, prev_block(i), 0)),
                  pl.BlockSpec((1, A_TILE, LANES), lambda bi, i: (bi, i, 0)),
                  pl.BlockSpec((Q_BLOCK, LANES), lambda bi, i: (prev_block(i), 0)),
                  pl.BlockSpec((A_TILE, LANES), lambda bi, i: (i, 0)),
                  pl.BlockSpec((A_HEADS * Q_BLOCK, LANES), lambda bi, i: (0, 0)),
                  pl.BlockSpec((1, A_Q), lambda bi, i: (0, 0))],
        out_specs=pl.BlockSpec((1, A_TILE, A_Q), lambda bi, i: (bi, i, 0)),
        compiler_params=pltpu.CompilerParams(dimension_semantics=("arbitrary", "arbitrary"),
                                             vmem_limit_bytes=VMEM_LIMIT),
        name="mixer_a",
    )(sinks, qa, kva, kva, pos_feat, pos_feat, _query_features(slopes, Q_BLOCK), g)


def _mixb_kernel(q_ref, kvs_ref, kvw_ref, kvc_ref, gate_ref, pf_ref, sf_ref, cf_ref, qf_ref, mmap_ref, rep_ref,
                 g_ref, o_ref, acc_ref, m_ref, *, n_cmp, n_slc):
    i = pl.program_id(1)
    t0 = i * Q_WIDE
    nb = q_ref.shape[0]
    q_all = [_stack_heads(q_ref[bb], B_HEADS, qf_ref[...]) for bb in range(nb)]

    def head_rows(x, hd, rows=Q_WIDE):
        return x[hd * rows:(hd + 1) * rows]

    kvc = [kvc_ref[bb] for bb in range(nb)]
    n_pad = kvc[0].shape[0]
    s_cmp = [_dot_nt(q_all[bb], _with_key_features(kvc[bb], cf_ref[...])) for bb in range(nb)]
    wspan = B_WINDOW + Q_BLOCK
    row_w = lax.broadcasted_iota(jnp.int32, (Q_BLOCK, wspan), 0)
    col_w = lax.broadcasted_iota(jnp.int32, (Q_BLOCK, wspan), 1)
    n_sub = Q_WIDE // Q_BLOCK
    valid_w = []
    kvw = [[None] * n_sub for _ in range(nb)]
    s_win = [[None] * n_sub for _ in range(nb)]
    for r in range(n_sub):
        r0 = t0 + r * Q_BLOCK
        w0 = pl.multiple_of(jnp.clip(r0 - B_WINDOW, 0, kvw_ref.shape[1] - wspan), LANES)
        dist_w = (row_w - col_w) + (r0 - w0)
        valid_w.append((dist_w >= 0) & (dist_w < B_WINDOW))
        for bb in range(nb):
            kvw[bb][r] = kvw_ref[bb, pl.ds(w0, wspan), :]
            q_sub = jnp.concatenate([head_rows(q_all[bb], hd)[r * Q_BLOCK:(r + 1) * Q_BLOCK] for hd in range(B_HEADS)],
                                    axis=0)
            s_win[bb][r] = _dot_nt(q_sub, _with_key_features(kvw[bb][r], pf_ref[pl.ds(w0, wspan), :]))

    gates = []
    for bb in range(nb):
        sig = jax.nn.sigmoid(gate_ref[bb]).astype(BF16)
        gates.append([_dot(sig, rep_ref[r]) for r in range(3)])

    row_c = lax.broadcasted_iota(jnp.int32, (Q_WIDE, n_pad), 0)
    col_c = lax.broadcasted_iota(jnp.int32, (Q_WIDE, n_pad), 1)
    vis_c = ((t0 + row_c) >= (col_c * CMP_STRIDE + (CMP_LEN - 1))) & (col_c < n_cmp)
    row1 = lax.broadcasted_iota(jnp.int32, (Q_WIDE, 1), 0)
    any_vis = ((t0 + row1) >= (CMP_LEN - 1)).astype(F32)
    blk = lax.broadcasted_iota(jnp.int32, (n_slc, Q_WIDE), 0)
    tq = t0 + lax.broadcasted_iota(jnp.int32, (n_slc, Q_WIDE), 1)
    cur = tq // SLC_LEN
    valid_b = blk <= cur
    forced = (blk == 0) | (blk == cur) | (blk == cur - 1)
    lane_q = lax.broadcasted_iota(jnp.int32, (B_HEADS * Q_WIDE, LANES), 1)
    sel_lanes = (lane_q >= SEL_LANE) & (lane_q < SEL_LANE + n_slc)
    o_cmp, q_slc = [], []
    for bb in range(nb):
        probs = []
        p_sum = jnp.zeros((Q_WIDE, n_pad), F32)
        for hd in range(B_HEADS):
            s = jnp.where(vis_c, head_rows(s_cmp[bb], hd), NEG_INF)
            m = jnp.max(s, axis=-1, keepdims=True)
            e = jnp.exp2(s - m)
            pr = e * (any_vis / jnp.sum(e, axis=-1, keepdims=True))
            p_sum = p_sum + pr
            probs.append(pr.astype(BF16))
        o_cmp.append(_unstack_heads(_dot(jnp.concatenate(probs, axis=0), kvc[bb]), B_HEADS))
        p_hi, p_lo = _split_bf16(p_sum)
        imp = (_dot_nt(mmap_ref[...], p_hi) + _dot_nt(mmap_ref[...], p_lo))[:n_slc]
        val = jnp.where(forced, FORCE, jnp.where(valid_b, imp, -FORCE))
        rank = jnp.zeros((n_slc, Q_WIDE), F32)
        for j in range(n_slc):
            vj = val[j:j + 1, :]
            ahead = (vj > val) | ((vj == val) & (blk > j))
            rank = rank + ahead.astype(F32)
        unsel = 1.0 - ((rank < SLC_TOP) & valid_b).astype(F32)
        unsel = jnp.concatenate([jnp.zeros((SEL_LANE, Q_WIDE), F32), unsel,
                                 jnp.zeros((LANES - SEL_LANE - n_slc, Q_WIDE), F32)], axis=0)
        unsel_q = unsel.T.astype(BF16)
        q_slc.append(jnp.where(sel_lanes, jnp.concatenate([unsel_q] * B_HEADS, axis=0), q_all[bb]))

    ob_rest = []
    for bb in range(nb):
        o_sub = [[None] * n_sub for _ in range(B_HEADS)]
        for r in range(n_sub):
            probs = []
            for hd in range(B_HEADS):
                s = jnp.where(valid_w[r], head_rows(s_win[bb][r], hd, Q_BLOCK), NEG_INF)
                probs.append(jnp.exp2(s - jnp.max(s, axis=-1, keepdims=True)).astype(BF16))
            o_all = _dot(jnp.concatenate(probs, axis=0), _with_ones_for_keys(kvw[bb][r]))
            for hd in range(B_HEADS):
                o_sub[hd][r] = head_rows(o_all, hd, Q_BLOCK)
        o_win = _unstack_normalized(
            jnp.concatenate([o_sub[hd][r] for hd in range(B_HEADS) for r in range(n_sub)], axis=0), B_HEADS)
        ob_rest.append(gates[bb][0] * o_cmp[bb] + gates[bb][2] * o_win)

    acc_ref[...] = jnp.zeros(acc_ref.shape, F32)
    m_ref[...] = jnp.full(m_ref.shape, NEG_INF, F32)

    def slc_chunk(c, width):
        k0 = pl.multiple_of(c * SLC_CHUNK, SLC_CHUNK)
        row_k = lax.broadcasted_iota(jnp.int32, (Q_WIDE, width), 0)
        col_k = lax.broadcasted_iota(jnp.int32, (Q_WIDE, width), 1)
        causal = (col_k - row_k) <= (t0 - k0)
        feat = sf_ref[pl.ds(k0, width), :]
        for bb in range(nb):
            kv = kvs_ref[bb, pl.ds(k0, width), :]
            s_all = _dot_nt(q_slc[bb], _with_key_features(kv, feat))
            probs, alphas = [], []
            for hd in range(B_HEADS):
                g = bb * B_HEADS + hd
                s = jnp.where(causal, head_rows(s_all, hd), NEG_INF)
                m_old = m_ref[g]
                m_new = jnp.maximum(m_old, jnp.max(s, axis=-1, keepdims=True))
                alpha = jnp.exp2(m_old - m_new)
                m_ref[g] = m_new
                probs.append(jnp.exp2(s - _lane_tile(m_new, width)).astype(BF16))
                alphas.append(alpha)
            o_all = _dot(jnp.concatenate(probs, axis=0), _with_ones_for_keys(kv))
            for hd in range(B_HEADS):
                g = bb * B_HEADS + hd
                acc_ref[g] = alphas[hd] * acc_ref[g] + head_rows(o_all, hd)

    n_half = lax.shift_right_logical(t0 + Q_WIDE + SLC_CHUNK // 2 - 1, (SLC_CHUNK // 2).bit_length() - 1)
    n_full = lax.shift_right_logical(n_half, 1)

    def full_chunk(c, carry):
        slc_chunk(c, SLC_CHUNK)
        return carry

    lax.fori_loop(0, n_full, full_chunk, 0)

    @pl.when((n_half & 1) == 1)
    def _():
        slc_chunk(n_full, SLC_CHUNK // 2)

    for bb in range(nb):
        o_slc = _unstack_normalized(
            jnp.concatenate([acc_ref[bb * B_HEADS + hd] for hd in range(B_HEADS)], axis=0), B_HEADS)
        ob = ob_rest[bb] + gates[bb][1] * o_slc
        o_ref[bb] = _rms(ob, g_ref[...]).astype(BF16)


def _mixer_b(qb, kvs, kvw, kvc, gate, g):
    b, s, _ = qb.shape
    n_pad = kvc.shape[1]
    n_cmp = s // CMP_STRIDE - 1
    n_slc = s // SLC_LEN
    _, slopes = _alibi_slopes()
    pos_feat = _key_features(np.arange(s))
    slc_feat = _key_features(np.arange(s), SLC_LEN)
    cmp_feat = _key_features(np.minimum(np.arange(n_pad), n_cmp - 1) * CMP_STRIDE + (CMP_LEN - 1))
    cs = np.arange(n_pad)[None, :] * CMP_STRIDE
    ss = np.arange(LANES)[:, None] * SLC_LEN
    ov = np.maximum(0, np.minimum(cs + CMP_LEN, ss + SLC_LEN) - np.maximum(cs, ss)) / CMP_STRIDE
    ov = ov * (np.arange(n_pad)[None, :] < n_cmp) * (np.arange(LANES)[:, None] < n_slc)
    mmap_t = jnp.asarray(ov, BF16)
    rep = np.zeros((3, LANES, B_Q), np.float32)
    for r in range(3):
        for h in range(B_HEADS):
            rep[r, 3 * h + r, h * HEAD_DIM:(h + 1) * HEAD_DIM] = 1.0
    rep = jnp.asarray(rep, BF16)
    nb = B_BATCH if b % B_BATCH == 0 else 1
    return pl.pallas_call(
        functools.partial(_mixb_kernel, n_cmp=n_cmp, n_slc=n_slc),
        out_shape=jax.ShapeDtypeStruct((b, s, B_Q), BF16),
        grid=(b // nb, s // Q_WIDE),
        in_specs=[pl.BlockSpec((nb, Q_WIDE, B_Q), lambda bi, i: (bi, i, 0)),
                  pl.BlockSpec((nb, s, LANES), lambda bi, i: (bi, 0, 0)),
                  pl.BlockSpec((nb, s, LANES), lambda bi, i: (bi, 0, 0)),
                  pl.BlockSpec((nb, n_pad, LANES), lambda bi, i: (bi, 0, 0)),
                  pl.BlockSpec((nb, Q_WIDE, LANES), lambda bi, i: (bi, i, 0)),
                  pl.BlockSpec((s, LANES), lambda bi, i: (0, 0)),
                  pl.BlockSpec((s, LANES), lambda bi, i: (0, 0)),
                  pl.BlockSpec((n_pad, LANES), lambda bi, i: (0, 0)),
                  pl.BlockSpec((B_HEADS * Q_WIDE, LANES), lambda bi, i: (0, 0)),
                  pl.BlockSpec((LANES, n_pad), lambda bi, i: (0, 0)),
                  pl.BlockSpec((3, LANES, B_Q), lambda bi, i: (0, 0, 0)),
                  pl.BlockSpec((1, B_Q), lambda bi, i: (0, 0))],
        out_specs=pl.BlockSpec((nb, Q_WIDE, B_Q), lambda bi, i: (bi, i, 0)),
        scratch_shapes=[pltpu.VMEM((nb * B_HEADS, Q_WIDE, LANES), F32),
                        pltpu.VMEM((nb * B_HEADS, Q_WIDE, LANES), F32)],
        compiler_params=pltpu.CompilerParams(dimension_semantics=("arbitrary", "arbitrary"),
                                             vmem_limit_bytes=VMEM_LIMIT),
        name="mixer_b",
    )(qb, kvs, kvw, kvc, gate, pos_feat, slc_feat, cmp_feat, _query_features(slopes, Q_WIDE), mmap_t, rep, g)


def _mixc_kernel(q_ref, k_ref, v_ref, tri_ref, g_ref, o_ref, acc_ref, carry_ref, z_ref, w_ref):
    i = pl.program_id(1)
    t0 = i * Q_WIDE
    nb = q_ref.shape[0]
    n_pair = C_HEADS // 2
    q_pairs = [qp for bb in range(nb) for qp in _stack_pairs(q_ref[bb], C_HEADS)]
    acc_ref[...] = jnp.zeros(acc_ref.shape, F32)
    row = lax.broadcasted_iota(jnp.int32, (Q_WIDE, KEY_CHUNK), 0)
    col = lax.broadcasted_iota(jnp.int32, (Q_WIDE, KEY_CHUNK), 1)
    cr = col - row
    n_chunks = (t0 + Q_WIDE + KEY_CHUNK - 1) // KEY_CHUNK

    def chunk_rows(ref, bb, c):
        return ref[bb, pl.ds(pl.multiple_of(c * KEY_CHUNK, KEY_CHUNK), KEY_CHUNK), :]

    def logits_to_scratch(c):
        for bb in range(nb):
            k = chunk_rows(k_ref, bb, c)
            for p in range(n_pair):
                z_ref[bb * n_pair + p] = _dot_nt(q_pairs[bb * n_pair + p], k[:, p * LANES:(p + 1) * LANES])

    def add_values(c):
        for bb in range(nb):
            v = chunk_rows(v_ref, bb, c)
            for p in range(n_pair):
                o = _dot(w_ref[bb * n_pair + p], v[:, p * LANES:(p + 1) * LANES])
                g = bb * C_HEADS + 2 * p
                acc_ref[g] = acc_ref[g] + o[:Q_WIDE]
                acc_ref[g + 1] = acc_ref[g + 1] + o[Q_WIDE:]

    def drop_of(z):
        neg_abs = lax.bitcast_convert_type(lax.bitcast_convert_type(z, jnp.uint32) | jnp.uint32(0x80000000), F32)
        return jnp.maximum(z, 0.0) + jnp.log2(1.0 + jnp.exp2(neg_abs))

    def publish(weights):
        for u in range(nb * n_pair):
            w_ref[u] = jnp.concatenate(weights[2 * u:2 * u + 2], axis=0)

    n_all = nb * C_HEADS
    half = Q_WIDE // 2

    def diagonal_sweep(c):
        past_top, past_bot = (cr < 0)[:half, :half], (cr < 0)[half:]
        blank = jnp.zeros((half, KEY_CHUNK - half), F32)
        log_w, parts = [], []
        for g in range(n_all):
            z = z_ref[g // 2, (g % 2) * Q_WIDE:(g % 2 + 1) * Q_WIDE, :]
            z_top, z_bot = z[:half, :half], z[half:]
            drop_top, drop_bot = drop_of(z_top), drop_of(z_bot)
            dk_top, dk_bot = jnp.where(past_top, drop_top, 0.0), jnp.where(past_bot, drop_bot, 0.0)
            parts.append(jnp.concatenate([jnp.concatenate([dk_top, blank], axis=1), dk_bot], axis=0).astype(BF16))
            log_w.append((z_top - drop_top, z_bot - drop_bot))
            carry_ref[g] = jnp.broadcast_to(jnp.concatenate([jnp.sum(dk_top, axis=-1, keepdims=True),
                                                             jnp.sum(dk_bot, axis=-1, keepdims=True)], axis=0),
                                            (Q_WIDE, LANES))
        tails = _dot(jnp.concatenate(parts, axis=0), tri_ref[...])
        logits_to_scratch(jnp.maximum(c - 1, 0))
        weights = []
        for g in range(n_all):
            tail = tails[g * Q_WIDE:(g + 1) * Q_WIDE]
            w_top = jnp.where(past_top, jnp.exp2(log_w[g][0] - tail[:half, :half]), 0.0)
            w_bot = jnp.where(past_bot, jnp.exp2(log_w[g][1] - tail[half:]), 0.0)
            weights.append(jnp.concatenate([jnp.concatenate([w_top, blank], axis=1), w_bot], axis=0).astype(BF16))
        publish(weights)

    def sweep(c):
        add_values(c + 1)
        log_w, parts = [], []
        for g in range(n_all):
            z = z_ref[g // 2, (g % 2) * Q_WIDE:(g % 2 + 1) * Q_WIDE, :]
            drop = drop_of(z)
            parts.append(drop.astype(BF16))
            log_w.append(z - drop - _lane_tile(carry_ref[g], KEY_CHUNK))
            carry_ref[g] = carry_ref[g] + jnp.sum(drop, axis=-1, keepdims=True)
        tails = _dot(jnp.concatenate(parts, axis=0), tri_ref[...])
        logits_to_scratch(jnp.maximum(c - 1, 0))
        publish([jnp.exp2(log_w[g] - tails[g * Q_WIDE:(g + 1) * Q_WIDE]).astype(BF16) for g in range(n_all)])

    logits_to_scratch(n_chunks - 1)
    diagonal_sweep(n_chunks - 1)

    def earlier(j, carry):
        sweep(n_chunks - 1 - j)
        return carry

    lax.fori_loop(1, n_chunks, earlier, 0)
    add_values(0)
    lane = lax.broadcasted_iota(jnp.int32, (Q_WIDE, LANES), 1)
    lo = lane < HEAD_DIM
    for bb in range(nb):
        oc = jnp.concatenate([jnp.where(lo, acc_ref[bb * C_HEADS + 2 * p], acc_ref[bb * C_HEADS + 2 * p + 1])
                              for p in range(n_pair)], axis=1)
        o_ref[bb] = _rms(oc, g_ref[...]).astype(BF16)


def _mixer_c(qc, kc, vc, g):
    b, s, _ = qc.shape
    tri = jnp.asarray(np.arange(KEY_CHUNK)[:, None] > np.arange(KEY_CHUNK)[None, :], BF16)
    nb = C_BATCH if b % C_BATCH == 0 else 1
    assert Q_WIDE == KEY_CHUNK
    return pl.pallas_call(
        _mixc_kernel,
        out_shape=jax.ShapeDtypeStruct((b, s, C_W), BF16),
        grid=(b // nb, s // Q_WIDE),
        in_specs=[pl.BlockSpec((nb, Q_WIDE, C_W), lambda bi, i: (bi, i, 0)),
                  pl.BlockSpec((nb, s, C_W), lambda bi, i: (bi, 0, 0)),
                  pl.BlockSpec((nb, s, C_W), lambda bi, i: (bi, 0, 0)),
                  pl.BlockSpec((KEY_CHUNK, KEY_CHUNK), lambda bi, i: (0, 0)),
                  pl.BlockSpec((1, C_W), lambda bi, i: (0, 0))],
        out_specs=pl.BlockSpec((nb, Q_WIDE, C_W), lambda bi, i: (bi, i, 0)),
        scratch_shapes=[pltpu.VMEM((nb * C_HEADS, Q_WIDE, LANES), F32),
                        pltpu.VMEM((nb * C_HEADS, Q_WIDE, LANES), F32),
                        pltpu.VMEM((nb * C_HEADS // 2, 2 * Q_WIDE, KEY_CHUNK), F32),
                        pltpu.VMEM((nb * C_HEADS // 2, 2 * Q_WIDE, KEY_CHUNK), BF16)],
        compiler_params=pltpu.CompilerParams(dimension_semantics=("arbitrary", "arbitrary"),
                                             vmem_limit_bytes=VMEM_LIMIT),
        name="mixer_c",
    )(qc, kc, vc, tri, g)


def _outffn_kernel(x_ref, ma_ref, mb_ref, mc_ref, wo_ref, gf_ref, wg_ref, wu_ref, wd_ref, gl_ref, o_ref, *, final):
    x = (x_ref[...] + _dot(ma_ref[...], wo_ref[:A_Q, :]) + _dot(mb_ref[...], wo_ref[A_Q:A_Q + B_Q, :])
         + _dot(mc_ref[...], wo_ref[A_Q + B_Q:, :]))
    h = _rms(x, gf_ref[...]).astype(BF16)
    o_ref[...] = x
    for c in range(D_FF // FF_CHUNK):
        sl = slice(c * FF_CHUNK, (c + 1) * FF_CHUNK)
        gate = _dot(h, wg_ref[:, sl])
        up = _dot(h, wu_ref[:, sl])
        o_ref[...] += _dot((jax.nn.silu(gate) * up).astype(BF16), wd_ref[sl, :])
    if final:
        o_ref[...] = _rms(o_ref[...], gl_ref[...])


def _outffn(x2, ma, mb, mc, wo, gf, wg, wu, wd, gl, final):
    t = x2.shape[0]
    tm = TOKEN_TILE
    const = lambda i: (0, 0)
    tok = lambda i: (i, 0)
    return pl.pallas_call(
        functools.partial(_outffn_kernel, final=final),
        out_shape=jax.ShapeDtypeStruct((t, D_MODEL), F32),
        grid=(t // tm,),
        in_specs=[pl.BlockSpec((tm, D_MODEL), tok),
                  pl.BlockSpec((tm, A_Q), tok),
                  pl.BlockSpec((tm, B_Q), tok),
                  pl.BlockSpec((tm, C_W), tok),
                  pl.BlockSpec((MIX_WIDTH, D_MODEL), const),
                  pl.BlockSpec((1, D_MODEL), const),
                  pl.BlockSpec((D_MODEL, D_FF), const),
                  pl.BlockSpec((D_MODEL, D_FF), const),
                  pl.BlockSpec((D_FF, D_MODEL), const),
                  pl.BlockSpec((1, D_MODEL), const)],
        out_specs=pl.BlockSpec((tm, D_MODEL), tok),
        compiler_params=pltpu.CompilerParams(dimension_semantics=("arbitrary",), vmem_limit_bytes=VMEM_LIMIT),
        name="outproj_ffn",
    )(x2, ma, mb, mc, wo, gf, wg, wu, wd, gl)


def _regroup_w_in(w_in):
    sizes = (A_Q, HEAD_DIM, HEAD_DIM, B_Q, HEAD_DIM, HEAD_DIM, HEAD_DIM, HEAD_DIM, HEAD_DIM, HEAD_DIM,
             N_GATES, C_W, C_W, C_W)
    offs = np.concatenate([[0], np.cumsum(sizes)])
    qa, ka, va, qb, kcb, vcb, ksb, vsb, kwb, vwb, gb, qc, kc, vc = [w_in[:, offs[j]:offs[j + 1]]
                                                                     for j in range(len(sizes))]
    pad = jnp.zeros((w_in.shape[0], LANES - N_GATES), w_in.dtype)
    cols = [qa * SCALE, ka, va, qb * SCALE, ksb, vsb, kwb, vwb, kcb, vcb, gb, pad, qc * SCALE, kc, vc]
    return jnp.concatenate(cols, axis=1).astype(BF16)


def _layer(x2, b, s, w_in, w_out, g_attn, g_ffn, g_out_a, g_out_b, g_out_c, sinks, cmp_pos, cmp_w1, cmp_w2,
           w_gate, w_up, w_down, g_final, final):
    row = lambda v: v.reshape(1, -1)
    qa, kva, qb, kvs, kvw, kvr, gate, qc, kc, vc = _inproj(x2, row(g_attn), _regroup_w_in(w_in))
    r3 = lambda a: a.reshape(b, s, a.shape[-1])

    half = CMP_LEN // 2
    pos = jnp.concatenate([cmp_pos[0], cmp_pos[1]], axis=-1).reshape(2, half, 1, LANES)
    z1 = jnp.zeros_like(cmp_w1[0])
    w1 = jnp.concatenate([jnp.concatenate([cmp_w1[0], z1], axis=-1), jnp.concatenate([z1, cmp_w1[1]], axis=-1)], axis=1)
    w1 = w1.reshape(2, half, LANES, 2 * CMP_HIDDEN).astype(BF16)
    z2 = jnp.zeros_like(cmp_w2[0])
    w2 = jnp.concatenate([jnp.concatenate([cmp_w2[0], z2], axis=1), jnp.concatenate([z2, cmp_w2[1]], axis=1)],
                         axis=0).astype(BF16)
    kvc = _compress(r3(kvr), pos, w1, w2)

    ma = _mixer_a(sinks, r3(qa), r3(kva), row(g_out_a))
    mb = _mixer_b(r3(qb), r3(kvs), r3(kvw), kvc, r3(gate), row(g_out_b))
    mc = _mixer_c(r3(qc), r3(kc), r3(vc), row(g_out_c))
    t = b * s
    return _outffn(x2, ma.reshape(t, A_Q), mb.reshape(t, B_Q), mc.reshape(t, C_W), w_out.astype(BF16),
                   row(g_ffn), w_gate.astype(BF16), w_up.astype(BF16), w_down.astype(BF16), row(g_final), final)


def kernel(x, w_in, w_out, g_attn, g_ffn, g_out_a, g_out_b, g_out_c, sinks, cmp_pos, cmp_w1, cmp_w2, w_gate, w_up,
           w_down, g_final):
    b, s, d = x.shape
    depth = w_in.shape[0]
    assert d == D_MODEL and s % KEY_CHUNK == 0 and (b * s) % TOKEN_TILE == 0 and (b * s) % INPROJ_TILE == 0
    assert s >= B_WINDOW + Q_WIDE and s % SLC_CHUNK == 0 and s % A_TILE == 0 and (s // CMP_STRIDE) % LANES == 0
    x2 = x.reshape(b * s, d)
    for l in range(depth):
        x2 = _layer(x2, b, s, w_in[l], w_out[l], g_attn[l], g_ffn[l], g_out_a[l], g_out_b[l], g_out_c[l], sinks[l],
                    cmp_pos[l], cmp_w1[l], cmp_w2[l], w_gate[l], w_up[l], w_down[l], g_final, l == depth - 1)
    return x2.reshape(b, s, d)
```

```python
import functools

import jax
import jax.numpy as jnp
import numpy as np
from jax import lax
from jax.experimental import pallas as pl
from jax.experimental.pallas import tpu as pltpu

F32 = jnp.float32
BF16 = jnp.bfloat16

D_MODEL = 1024
HEAD_DIM = 64
LANES = 128
Q_BLOCK = 128
Q_WIDE = 256
A_TILE = 256
B_BATCH = 4
C_BATCH = 4
A_HEADS = 8
A_WINDOW = 128
B_HEADS = 4
CMP_LEN = 32
CMP_STRIDE = 16
CMP_HIDDEN = 128
SLC_LEN = 64
SLC_TOP = 8
B_WINDOW = 512
C_HEADS = 4
A_Q = A_HEADS * HEAD_DIM
B_Q = B_HEADS * HEAD_DIM
C_W = C_HEADS * HEAD_DIM
MIX_WIDTH = A_Q + B_Q + C_W
D_FF = 2816
N_GATES = B_HEADS * 3
NEG_INF = -1e30
MASK_BIG = 2.0 ** 100
SEL_LANE = 72
FORCE = 1e4
EPS = 1e-6
SCALE = HEAD_DIM ** -0.5
LOG2E = 1.4426950408889634

KEY_CHUNK = 256
SLC_CHUNK = 512
FF_CHUNK = 256
TOKEN_TILE = 512
INPROJ_TILE = 1024
VMEM_LIMIT = 56 * 1024 * 1024

_G_QA, _G_QB, _G_QC, _G_KC, _G_VC, _G_KVA, _G_KVS, _G_KVW, _G_KVC, _G_GATE, IN_PAD = (
    0, 512, 768, 1024, 1280, 1536, 1664, 1792, 1920, 2048, 2176)


def _alibi_slopes():
    n = A_HEADS + B_HEADS
    sl = 2.0 ** (-8.0 * np.arange(1, n + 1) / n) * LOG2E
    return [float(v) for v in sl[:A_HEADS]], [float(v) for v in sl[A_HEADS:]]


def _round_bf16(x):
    bits = np.ascontiguousarray(x, np.float32).view(np.uint32).astype(np.uint64)
    bits = ((bits + 0x7FFF + ((bits >> 16) & 1)) >> 16) << 16
    return bits.astype(np.uint32).view(np.float32)


def _key_features(pos, block_len=None):
    pos = np.asarray(pos, np.int64)
    assert pos.max() < 2048
    f = np.zeros((len(pos), LANES), np.float32)
    for j in range(3):
        f[:, HEAD_DIM + 2 * j] = 256 * (pos // 256)
        f[:, HEAD_DIM + 2 * j + 1] = pos % 256
    if block_len is not None:
        assert SEL_LANE + pos.max() // block_len < LANES
        f[np.arange(len(pos)), SEL_LANE + pos // block_len] = -MASK_BIG
    return jnp.asarray(f, BF16)


def _query_features(slopes, rows):
    f = np.zeros((len(slopes) * rows, LANES), np.float32)
    for h, slope in enumerate(slopes):
        rest = np.float32(slope)
        for j in range(3):
            part = _round_bf16(rest)
            rest = np.float32(rest - part)
            f[h * rows:(h + 1) * rows, HEAD_DIM + 2 * j:HEAD_DIM + 2 * j + 2] = part
    return jnp.asarray(f, BF16)


def _dot(a, b):
    return jnp.dot(a, b, preferred_element_type=F32)


def _dot_nt(a, b):
    return lax.dot_general(a, b, (((1,), (1,)), ((), ())), preferred_element_type=F32)


def _split_bf16(x):
    hi = x.astype(BF16)
    lo = (x - hi.astype(F32)).astype(BF16)
    return hi, lo


def _rms(x, g):
    ms = jnp.mean(x * x, axis=-1, keepdims=True)
    return x * lax.rsqrt(ms + EPS) * g


def _lane_tile(x, width):
    return jnp.concatenate([x] * (width // LANES), axis=1)


def _swap_halves(x):
    return jnp.concatenate([x[:, HEAD_DIM:], x[:, :HEAD_DIM]], axis=1)


def _stack_heads(q, n_heads, feat):
    rows = q.shape[0]
    lo = lax.broadcasted_iota(jnp.int32, (rows, LANES), 1) < HEAD_DIM
    tiles = []
    for p in range(n_heads // 2):
        pair = q[:, p * LANES:(p + 1) * LANES]
        tiles.append(jnp.where(lo, pair, feat[(2 * p) * rows:(2 * p + 1) * rows]))
        tiles.append(jnp.where(lo, _swap_halves(pair), feat[(2 * p + 1) * rows:(2 * p + 2) * rows]))
    return jnp.concatenate(tiles, axis=0)


def _with_key_features(kv, feat):
    lo = lax.broadcasted_iota(jnp.int32, kv.shape, 1) < HEAD_DIM
    return jnp.where(lo, kv, feat)


def _stack_pairs(q, n_heads):
    rows = q.shape[0]
    lo = lax.broadcasted_iota(jnp.int32, (rows, LANES), 1) < HEAD_DIM
    out = []
    for p in range(n_heads // 2):
        pair = q[:, p * LANES:(p + 1) * LANES]
        zero = jnp.zeros_like(pair)
        out.append(jnp.concatenate([jnp.where(lo, pair, zero), jnp.where(lo, zero, pair)], axis=0))
    return out


def _with_ones_for_keys(kv):
    lo = lax.broadcasted_iota(jnp.int32, kv.shape, 1) < HEAD_DIM
    return jnp.where(lo, jnp.ones_like(kv), kv)


def _unstack_normalized(o, n_heads):
    rows = o.shape[0] // n_heads
    lo = lax.broadcasted_iota(jnp.int32, (rows, LANES), 1) < HEAD_DIM
    pairs = []
    for p in range(n_heads // 2):
        even = o[(2 * p) * rows:(2 * p + 1) * rows]
        odd = o[(2 * p + 1) * rows:(2 * p + 2) * rows]
        pairs.append(jnp.where(lo, pltpu.roll(even, HEAD_DIM, 1) / even, odd / pltpu.roll(odd, HEAD_DIM, 1)))
    return jnp.concatenate(pairs, axis=1)


def _unstack_heads(o, n_heads):
    rows = o.shape[0] // n_heads
    lo = lax.broadcasted_iota(jnp.int32, (rows, LANES), 1) < HEAD_DIM
    pairs = []
    for p in range(n_heads // 2):
        even = o[(2 * p) * rows:(2 * p + 1) * rows]
        odd = o[(2 * p + 1) * rows:(2 * p + 2) * rows]
        pairs.append(jnp.where(lo, pltpu.roll(even, HEAD_DIM, 1), odd))
    return jnp.concatenate(pairs, axis=1)


def _inproj_kernel(x_ref, g_ref, w_ref, qa_ref, kva_ref, qb_ref, kvs_ref, kvw_ref, kvr_ref,
                   gate_ref, qc_ref, kc_ref, vc_ref):
    h = _rms(x_ref[...], g_ref[...]).astype(BF16)

    def proj(lo, hi):
        return _dot(h, w_ref[:, lo:hi])

    qa_ref[...] = (proj(_G_QA, _G_QB) * LOG2E).astype(BF16)
    qb_ref[...] = (proj(_G_QB, _G_QC) * LOG2E).astype(BF16)
    qc_ref[...] = (proj(_G_QC, _G_KC) * LOG2E).astype(BF16)
    kc_ref[...] = proj(_G_KC, _G_VC).astype(BF16)
    vc_ref[...] = proj(_G_VC, _G_KVA).astype(BF16)
    kv_as = proj(_G_KVA, _G_KVW)
    kva_ref[...] = kv_as[:, :LANES].astype(BF16)
    kvs_ref[...] = kv_as[:, LANES:].astype(BF16)
    kv_wc = proj(_G_KVW, _G_GATE)
    kvw_ref[...] = kv_wc[:, :LANES].astype(BF16)
    kvr_ref[...] = kv_wc[:, LANES:]
    gate_ref[...] = proj(_G_GATE, IN_PAD)


def _inproj(x2, g, w):
    t = x2.shape[0]
    tm = INPROJ_TILE
    widths = [(A_Q, BF16), (LANES, BF16), (B_Q, BF16), (LANES, BF16), (LANES, BF16), (LANES, F32),
              (LANES, F32), (C_W, BF16), (C_W, BF16), (C_W, BF16)]
    return pl.pallas_call(
        _inproj_kernel,
        out_shape=[jax.ShapeDtypeStruct((t, wd), dt) for wd, dt in widths],
        grid=(t // tm,),
        in_specs=[pl.BlockSpec((tm, D_MODEL), lambda i: (i, 0)),
                  pl.BlockSpec((1, D_MODEL), lambda i: (0, 0)),
                  pl.BlockSpec((D_MODEL, IN_PAD), lambda i: (0, 0))],
        out_specs=[pl.BlockSpec((tm, wd), lambda i: (i, 0)) for wd, _ in widths],
        compiler_params=pltpu.CompilerParams(dimension_semantics=("arbitrary",), vmem_limit_bytes=VMEM_LIMIT),
        name="inproj",
    )(x2, g, w)


def _compress_kernel(x_ref, pos_ref, w1_ref, w2_ref, o_ref):
    half = CMP_LEN // 2
    n = x_ref.shape[1] // CMP_STRIDE
    h_lo = jnp.zeros((n, 2 * CMP_HIDDEN), F32)
    h_hi = jnp.zeros((n, 2 * CMP_HIDDEN), F32)
    for l in range(half):
        x = x_ref[0, pl.ds(l, n, stride=CMP_STRIDE), :]
        h_lo = h_lo + _dot((x + pos_ref[0, l]).astype(BF16), w1_ref[0, l])
        h_hi = h_hi + _dot((x + pos_ref[1, l]).astype(BF16), w1_ref[1, l])
    h = h_lo + pltpu.roll(h_hi, n - 1, 0)
    o_ref[0] = _dot(jax.nn.gelu(h).astype(BF16), w2_ref[...]).astype(BF16)


def _compress(x, pos, w1, w2):
    b, s, _ = x.shape
    n = s // CMP_STRIDE
    half = CMP_LEN // 2
    return pl.pallas_call(
        _compress_kernel,
        out_shape=jax.ShapeDtypeStruct((b, n, LANES), BF16),
        grid=(b,),
        in_specs=[pl.BlockSpec((1, s, LANES), lambda i: (i, 0, 0)),
                  pl.BlockSpec((2, half, 1, LANES), lambda i: (0, 0, 0, 0)),
                  pl.BlockSpec((2, half, LANES, 2 * CMP_HIDDEN), lambda i: (0, 0, 0, 0)),
                  pl.BlockSpec((2 * CMP_HIDDEN, LANES), lambda i: (0, 0))],
        out_specs=pl.BlockSpec((1, n, LANES), lambda i: (i, 0, 0)),
        compiler_params=pltpu.CompilerParams(dimension_semantics=("arbitrary",), vmem_limit_bytes=VMEM_LIMIT),
        name="compress",
    )(x, pos, w1, w2)


def _mixa_kernel(sink_ref, q_ref, kvp_ref, kvc_ref, pfp_ref, pfc_ref, qf_ref, g_ref, o_ref, *, slopes):
    i = pl.program_id(1)
    span = 2 * Q_BLOCK
    row = lax.broadcasted_iota(jnp.int32, (Q_BLOCK, span), 0)
    col = lax.broadcasted_iota(jnp.int32, (Q_BLOCK, span), 1)
    dist = row + Q_BLOCK - col
    band = (dist >= 0) & (dist < A_WINDOW)
    kv_rows = jnp.concatenate([kvp_ref[0], kvc_ref[0]], axis=0)
    pf_rows = jnp.concatenate([pfp_ref[...], pfc_ref[...]], axis=0)

    n_sub = A_TILE // Q_BLOCK
    kvs, scores = [], []
    for r in range(n_sub):
        q = q_ref[0, r * Q_BLOCK:(r + 1) * Q_BLOCK, :]
        kvs.append(kv_rows[r * Q_BLOCK:r * Q_BLOCK + span])
        kx = _with_key_features(kvs[r], pf_rows[r * Q_BLOCK:r * Q_BLOCK + span])
        scores.append(_dot_nt(_stack_heads(q, A_HEADS, qf_ref[...]), kx))
    for r in range(n_sub):
        kv, s_all = kvs[r], scores[r]
        valid = band & ((col >= Q_BLOCK) | (i > 0)) if r == 0 else band
        t_row = (i * A_TILE + r * Q_BLOCK + lax.broadcasted_iota(jnp.int32, (Q_BLOCK, 1), 0)).astype(F32)
        probs, inv_l = [], []
        for hd in range(A_HEADS):
            sink = sink_ref[hd] * LOG2E + slopes[hd] * t_row
            s = jnp.where(valid, s_all[hd * Q_BLOCK:(hd + 1) * Q_BLOCK], NEG_INF)
            m = jnp.maximum(jnp.max(s, axis=-1, keepdims=True), sink)
            e = jnp.exp2(s - m)
            inv_l.append(1.0 / (jnp.sum(e, axis=-1, keepdims=True) + jnp.exp2(sink - m)))
            probs.append(e.astype(BF16))
        o_all = _dot(jnp.concatenate(probs, axis=0), kv)
        o_all = jnp.concatenate([o_all[hd * Q_BLOCK:(hd + 1) * Q_BLOCK] * inv_l[hd] for hd in range(A_HEADS)], axis=0)
        o_ref[0, r * Q_BLOCK:(r + 1) * Q_BLOCK, :] = _rms(_unstack_heads(o_all, A_HEADS), g_ref[...]).astype(BF16)


def _mixer_a(sinks, qa, kva, g):
    b, s, _ = qa.shape
    slopes, _ = _alibi_slopes()
    pos_feat = _key_features(np.arange(s))
    sub = A_TILE // Q_BLOCK
    prev_block = lambda i: jnp.maximum(sub * i - 1, 0)
    return pl.pallas_call(
        functools.partial(_mixa_kernel, slopes=slopes),
        out_shape=jax.ShapeDtypeStruct((b, s, A_Q), BF16),
        grid=(b, s // A_TILE),
        in_specs=[pl.BlockSpec(memory_space=pltpu.SMEM),
                  pl.BlockSpec((1, A_TILE, A_Q), lambda bi, i: (bi, i, 0)),
                  pl.BlockSpec((1, Q_BLOCK, LANES), lambda bi, i: (bi, prev_block(i), 0)),
                  pl.BlockSpec((1, A_TILE, LANES), lambda bi, i: (bi, i, 0)),
                  pl.BlockSpec((Q_BLOCK, LANES), lambda bi, i: (prev_block(i), 0)),
                  pl.BlockSpec((A_TILE, LANES), lambda bi, i: (i, 0)),
                  pl.BlockSpec((A_HEADS * Q_BLOCK, LANES), lambda bi, i: (0, 0)),
                  pl.BlockSpec((1, A_Q), lambda bi, i: (0, 0))],
        out_specs=pl.BlockSpec((1, A_TILE, A_Q), lambda bi, i: (bi, i, 0)),
        compiler_params=pltpu.CompilerParams(dimension_semantics=("arbitrary", "arbitrary"),
                                             vmem_limit_bytes=VMEM_LIMIT),
        name="mixer_a",
    )(sinks, qa, kva, kva, pos_feat, pos_feat, _query_features(slopes, Q_BLOCK), g)


def _mixb_kernel(q_ref, kvs_ref, kvw_ref, kvc_ref, gate_ref, pf_ref, sf_ref, cf_ref, qf_ref, mmap_ref, rep_ref,
                 g_ref, o_ref, acc_ref, m_ref, *, n_cmp, n_slc):
    i = pl.program_id(1)
    t0 = i * Q_WIDE
    nb = q_ref.shape[0]
    q_all = [_stack_heads(q_ref[bb], B_HEADS, qf_ref[...]) for bb in range(nb)]

    def head_rows(x, hd, rows=Q_WIDE):
        return x[hd * rows:(hd + 1) * rows]

    kvc = [kvc_ref[bb] for bb in range(nb)]
    n_pad = kvc[0].shape[0]
    s_cmp = [_dot_nt(q_all[bb], _with_key_features(kvc[bb], cf_ref[...])) for bb in range(nb)]
    wspan = B_WINDOW + Q_BLOCK
    row_w = lax.broadcasted_iota(jnp.int32, (Q_BLOCK, wspan), 0)
    col_w = lax.broadcasted_iota(jnp.int32, (Q_BLOCK, wspan), 1)
    n_sub = Q_WIDE // Q_BLOCK
    valid_w = []
    kvw = [[None] * n_sub for _ in range(nb)]
    s_win = [[None] * n_sub for _ in range(nb)]
    for r in range(n_sub):
        r0 = t0 + r * Q_BLOCK
        w0 = pl.multiple_of(jnp.clip(r0 - B_WINDOW, 0, kvw_ref.shape[1] - wspan), LANES)
        dist_w = (row_w - col_w) + (r0 - w0)
        valid_w.append((dist_w >= 0) & (dist_w < B_WINDOW))
        for bb in range(nb):
            kvw[bb][r] = kvw_ref[bb, pl.ds(w0, wspan), :]
            q_sub = jnp.concatenate([head_rows(q_all[bb], hd)[r * Q_BLOCK:(r + 1) * Q_BLOCK] for hd in range(B_HEADS)],
                                    axis=0)
            s_win[bb][r] = _dot_nt(q_sub, _with_key_features(kvw[bb][r], pf_ref[pl.ds(w0, wspan), :]))

    gates = []
    for bb in range(nb):
        sig = jax.nn.sigmoid(gate_ref[bb]).astype(BF16)
        gates.append([_dot(sig, rep_ref[r]) for r in range(3)])

    row_c = lax.broadcasted_iota(jnp.int32, (Q_WIDE, n_pad), 0)
    col_c = lax.broadcasted_iota(jnp.int32, (Q_WIDE, n_pad), 1)
    vis_c = ((t0 + row_c) >= (col_c * CMP_STRIDE + (CMP_LEN - 1))) & (col_c < n_cmp)
    row1 = lax.broadcasted_iota(jnp.int32, (Q_WIDE, 1), 0)
    any_vis = ((t0 + row1) >= (CMP_LEN - 1)).astype(F32)
    blk = lax.broadcasted_iota(jnp.int32, (n_slc, Q_WIDE), 0)
    tq = t0 + lax.broadcasted_iota(jnp.int32, (n_slc, Q_WIDE), 1)
    cur = tq // SLC_LEN
    valid_b = blk <= cur
    forced = (blk == 0) | (blk == cur) | (blk == cur - 1)
    lane_q = lax.broadcasted_iota(jnp.int32, (B_HEADS * Q_WIDE, LANES), 1)
    sel_lanes = (lane_q >= SEL_LANE) & (lane_q < SEL_LANE + n_slc)
    o_cmp, q_slc = [], []
    for bb in range(nb):
        probs = []
        p_sum = jnp.zeros((Q_WIDE, n_pad), F32)
        for hd in range(B_HEADS):
            s = jnp.where(vis_c, head_rows(s_cmp[bb], hd), NEG_INF)
            m = jnp.max(s, axis=-1, keepdims=True)
            e = jnp.exp2(s - m)
            pr = e * (any_vis / jnp.sum(e, axis=-1, keepdims=True))
            p_sum = p_sum + pr
            probs.append(pr.astype(BF16))
        o_cmp.append(_unstack_heads(_dot(jnp.concatenate(probs, axis=0), kvc[bb]), B_HEADS))
        p_hi, p_lo = _split_bf16(p_sum)
        imp = (_dot_nt(mmap_ref[...], p_hi) + _dot_nt(mmap_ref[...], p_lo))[:n_slc]
        val = jnp.where(forced, FORCE, jnp.where(valid_b, imp, -FORCE))
        rank = jnp.zeros((n_slc, Q_WIDE), F32)
        for j in range(n_slc):
            vj = val[j:j + 1, :]
            ahead = (vj > val) | ((vj == val) & (blk > j))
            rank = rank + ahead.astype(F32)
        unsel = 1.0 - ((rank < SLC_TOP) & valid_b).astype(F32)
        unsel = jnp.concatenate([jnp.zeros((SEL_LANE, Q_WIDE), F32), unsel,
                                 jnp.zeros((LANES - SEL_LANE - n_slc, Q_WIDE), F32)], axis=0)
        unsel_q = unsel.T.astype(BF16)
        q_slc.append(jnp.where(sel_lanes, jnp.concatenate([unsel_q] * B_HEADS, axis=0), q_all[bb]))

    ob_rest = []
    for bb in range(nb):
        o_sub = [[None] * n_sub for _ in range(B_HEADS)]
        for r in range(n_sub):
            probs = []
            for hd in range(B_HEADS):
                s = jnp.where(valid_w[r], head_rows(s_win[bb][r], hd, Q_BLOCK), NEG_INF)
                probs.append(jnp.exp2(s - jnp.max(s, axis=-1, keepdims=True)).astype(BF16))
            o_all = _dot(jnp.concatenate(probs, axis=0), _with_ones_for_keys(kvw[bb][r]))
            for hd in range(B_HEADS):
                o_sub[hd][r] = head_rows(o_all, hd, Q_BLOCK)
        o_win = _unstack_normalized(
            jnp.concatenate([o_sub[hd][r] for hd in range(B_HEADS) for r in range(n_sub)], axis=0), B_HEADS)
        ob_rest.append(gates[bb][0] * o_cmp[bb] + gates[bb][2] * o_win)

    acc_ref[...] = jnp.zeros(acc_ref.shape, F32)
    m_ref[...] = jnp.full(m_ref.shape, NEG_INF, F32)

    def slc_chunk(c, width):
        k0 = pl.multiple_of(c * SLC_CHUNK, SLC_CHUNK)
        row_k = lax.broadcasted_iota(jnp.int32, (Q_WIDE, width), 0)
        col_k = lax.broadcasted_iota(jnp.int32, (Q_WIDE, width), 1)
        causal = (col_k - row_k) <= (t0 - k0)
        feat = sf_ref[pl.ds(k0, width), :]
        for bb in range(nb):
            kv = kvs_ref[bb, pl.ds(k0, width), :]
            s_all = _dot_nt(q_slc[bb], _with_key_features(kv, feat))
            probs, alphas = [], []
            for hd in range(B_HEADS):
                g = bb * B_HEADS + hd
                s = jnp.where(causal, head_rows(s_all, hd), NEG_INF)
                m_old = m_ref[g]
                m_new = jnp.maximum(m_old, jnp.max(s, axis=-1, keepdims=True))
                alpha = jnp.exp2(m_old - m_new)
                m_ref[g] = m_new
                probs.append(jnp.exp2(s - _lane_tile(m_new, width)).astype(BF16))
                alphas.append(alpha)
            o_all = _dot(jnp.concatenate(probs, axis=0), _with_ones_for_keys(kv))
            for hd in range(B_HEADS):
                g = bb * B_HEADS + hd
                acc_ref[g] = alphas[hd] * acc_ref[g] + head_rows(o_all, hd)

    n_half = lax.shift_right_logical(t0 + Q_WIDE + SLC_CHUNK // 2 - 1, (SLC_CHUNK // 2).bit_length() - 1)
    n_full = lax.shift_right_logical(n_half, 1)

    def full_chunk(c, carry):
        slc_chunk(c, SLC_CHUNK)
        return carry

    lax.fori_loop(0, n_full, full_chunk, 0)

    @pl.when((n_half & 1) == 1)
    def _():
        slc_chunk(n_full, SLC_CHUNK // 2)

    for bb in range(nb):
        o_slc = _unstack_normalized(
            jnp.concatenate([acc_ref[bb * B_HEADS + hd] for hd in range(B_HEADS)], axis=0), B_HEADS)
        ob = ob_rest[bb] + gates[bb][1] * o_slc
        o_ref[bb] = _rms(ob, g_ref[...]).astype(BF16)


def _mixer_b(qb, kvs, kvw, kvc, gate, g):
    b, s, _ = qb.shape
    n_pad = kvc.shape[1]
    n_cmp = s // CMP_STRIDE - 1
    n_slc = s // SLC_LEN
    _, slopes = _alibi_slopes()
    pos_feat = _key_features(np.arange(s))
    slc_feat = _key_features(np.arange(s), SLC_LEN)
    cmp_feat = _key_features(np.minimum(np.arange(n_pad), n_cmp - 1) * CMP_STRIDE + (CMP_LEN - 1))
    cs = np.arange(n_pad)[None, :] * CMP_STRIDE
    ss = np.arange(LANES)[:, None] * SLC_LEN
    ov = np.maximum(0, np.minimum(cs + CMP_LEN, ss + SLC_LEN) - np.maximum(cs, ss)) / CMP_STRIDE
    ov = ov * (np.arange(n_pad)[None, :] < n_cmp) * (np.arange(LANES)[:, None] < n_slc)
    mmap_t = jnp.asarray(ov, BF16)
    rep = np.zeros((3, LANES, B_Q), np.float32)
    for r in range(3):
        for h in range(B_HEADS):
            rep[r, 3 * h + r, h * HEAD_DIM:(h + 1) * HEAD_DIM] = 1.0
    rep = jnp.asarray(rep, BF16)
    nb = B_BATCH if b % B_BATCH == 0 else 1
    return pl.pallas_call(
        functools.partial(_mixb_kernel, n_cmp=n_cmp, n_slc=n_slc),
        out_shape=jax.ShapeDtypeStruct((b, s, B_Q), BF16),
        grid=(b // nb, s // Q_WIDE),
        in_specs=[pl.BlockSpec((nb, Q_WIDE, B_Q), lambda bi, i: (bi, i, 0)),
                  pl.BlockSpec((nb, s, LANES), lambda bi, i: (bi, 0, 0)),
                  pl.BlockSpec((nb, s, LANES), lambda bi, i: (bi, 0, 0)),
                  pl.BlockSpec((nb, n_pad, LANES), lambda bi, i: (bi, 0, 0)),
                  pl.BlockSpec((nb, Q_WIDE, LANES), lambda bi, i: (bi, i, 0)),
                  pl.BlockSpec((s, LANES), lambda bi, i: (0, 0)),
                  pl.BlockSpec((s, LANES), lambda bi, i: (0, 0)),
                  pl.BlockSpec((n_pad, LANES), lambda bi, i: (0, 0)),
                  pl.BlockSpec((B_HEADS * Q_WIDE, LANES), lambda bi, i: (0, 0)),
                  pl.BlockSpec((LANES, n_pad), lambda bi, i: (0, 0)),
                  pl.BlockSpec((3, LANES, B_Q), lambda bi, i: (0, 0, 0)),
                  pl.BlockSpec((1, B_Q), lambda bi, i: (0, 0))],
        out_specs=pl.BlockSpec((nb, Q_WIDE, B_Q), lambda bi, i: (bi, i, 0)),
        scratch_shapes=[pltpu.VMEM((nb * B_HEADS, Q_WIDE, LANES), F32),
                        pltpu.VMEM((nb * B_HEADS, Q_WIDE, LANES), F32)],
        compiler_params=pltpu.CompilerParams(dimension_semantics=("arbitrary", "arbitrary"),
                                             vmem_limit_bytes=VMEM_LIMIT),
        name="mixer_b",
    )(qb, kvs, kvw, kvc, gate, pos_feat, slc_feat, cmp_feat, _query_features(slopes, Q_WIDE), mmap_t, rep, g)


def _mixc_kernel(q_ref, k_ref, v_ref, tri_ref, g_ref, o_ref, acc_ref, carry_ref, z_ref, w_ref):
    i = pl.program_id(1)
    t0 = i * Q_WIDE
    nb = q_ref.shape[0]
    n_pair = C_HEADS // 2
    q_pairs = [qp for bb in range(nb) for qp in _stack_pairs(q_ref[bb], C_HEADS)]
    acc_ref[...] = jnp.zeros(acc_ref.shape, F32)
    row = lax.broadcasted_iota(jnp.int32, (Q_WIDE, KEY_CHUNK), 0)
    col = lax.broadcasted_iota(jnp.int32, (Q_WIDE, KEY_CHUNK), 1)
    cr = col - row
    n_chunks = (t0 + Q_WIDE + KEY_CHUNK - 1) // KEY_CHUNK

    def chunk_rows(ref, bb, c):
        return ref[bb, pl.ds(pl.multiple_of(c * KEY_CHUNK, KEY_CHUNK), KEY_CHUNK), :]

    def logits_to_scratch(c):
        for bb in range(nb):
            k = chunk_rows(k_ref, bb, c)
            for p in range(n_pair):
                z_ref[bb * n_pair + p] = _dot_nt(q_pairs[bb * n_pair + p], k[:, p * LANES:(p + 1) * LANES])

    def add_values(c):
        for bb in range(nb):
            v = chunk_rows(v_ref, bb, c)
            for p in range(n_pair):
                o = _dot(w_ref[bb * n_pair + p], v[:, p * LANES:(p + 1) * LANES])
                g = bb * C_HEADS + 2 * p
                acc_ref[g] = acc_ref[g] + o[:Q_WIDE]
                acc_ref[g + 1] = acc_ref[g + 1] + o[Q_WIDE:]

    def drop_of(z):
        neg_abs = lax.bitcast_convert_type(lax.bitcast_convert_type(z, jnp.uint32) | jnp.uint32(0x80000000), F32)
        return jnp.maximum(z, 0.0) + jnp.log2(1.0 + jnp.exp2(neg_abs))

    def publish(weights):
        for u in range(nb * n_pair):
            w_ref[u] = jnp.concatenate(weights[2 * u:2 * u + 2], axis=0)

    n_all = nb * C_HEADS
    half = Q_WIDE // 2

    def diagonal_sweep(c):
        past_top, past_bot = (cr < 0)[:half, :half], (cr < 0)[half:]
        blank = jnp.zeros((half, KEY_CHUNK - half), F32)
        log_w, parts = [], []
        for g in range(n_all):
            z = z_ref[g // 2, (g % 2) * Q_WIDE:(g % 2 + 1) * Q_WIDE, :]
            z_top, z_bot = z[:half, :half], z[half:]
            drop_top, drop_bot = drop_of(z_top), drop_of(z_bot)
            dk_top, dk_bot = jnp.where(past_top, drop_top, 0.0), jnp.where(past_bot, drop_bot, 0.0)
            parts.append(jnp.concatenate([jnp.concatenate([dk_top, blank], axis=1), dk_bot], axis=0).astype(BF16))
            log_w.append((z_top - drop_top, z_bot - drop_bot))
            carry_ref[g] = jnp.broadcast_to(jnp.concatenate([jnp.sum(dk_top, axis=-1, keepdims=True),
                                                             jnp.sum(dk_bot, axis=-1, keepdims=True)], axis=0),
                                            (Q_WIDE, LANES))
        tails = _dot(jnp.concatenate(parts, axis=0), tri_ref[...])
        logits_to_scratch(jnp.maximum(c - 1, 0))
        weights = []
        for g in range(n_all):
            tail = tails[g * Q_WIDE:(g + 1) * Q_WIDE]
            w_top = jnp.where(past_top, jnp.exp2(log_w[g][0] - tail[:half, :half]), 0.0)
            w_bot = jnp.where(past_bot, jnp.exp2(log_w[g][1] - tail[half:]), 0.0)
            weights.append(jnp.concatenate([jnp.concatenate([w_top, blank], axis=1), w_bot], axis=0).astype(BF16))
        publish(weights)

    def sweep(c):
        add_values(c + 1)
        log_w, parts = [], []
        for g in range(n_all):
            z = z_ref[g // 2, (g % 2) * Q_WIDE:(g % 2 + 1) * Q_WIDE, :]
            drop = drop_of(z)
            parts.append(drop.astype(BF16))
            log_w.append(z - drop - _lane_tile(carry_ref[g], KEY_CHUNK))
            carry_ref[g] = carry_ref[g] + jnp.sum(drop, axis=-1, keepdims=True)
        tails = _dot(jnp.concatenate(parts, axis=0), tri_ref[...])
        logits_to_scratch(jnp.maximum(c - 1, 0))
        publish([jnp.exp2(log_w[g] - tails[g * Q_WIDE:(g + 1) * Q_WIDE]).astype(BF16) for g in range(n_all)])

    logits_to_scratch(n_chunks - 1)
    diagonal_sweep(n_chunks - 1)

    def earlier(j, carry):
        sweep(n_chunks - 1 - j)
        return carry

    lax.fori_loop(1, n_chunks, earlier, 0)
    add_values(0)
    lane = lax.broadcasted_iota(jnp.int32, (Q_WIDE, LANES), 1)
    lo = lane < HEAD_DIM
    for bb in range(nb):
        oc = jnp.concatenate([jnp.where(lo, acc_ref[bb * C_HEADS + 2 * p], acc_ref[bb * C_HEADS + 2 * p + 1])
                              for p in range(n_pair)], axis=1)
        o_ref[bb] = _rms(oc, g_ref[...]).astype(BF16)


def _mixer_c(qc, kc, vc, g):
    b, s, _ = qc.shape
    tri = jnp.asarray(np.arange(KEY_CHUNK)[:, None] > np.arange(KEY_CHUNK)[None, :], BF16)
    nb = C_BATCH if b % C_BATCH == 0 else 1
    assert Q_WIDE == KEY_CHUNK
    return pl.pallas_call(
        _mixc_kernel,
        out_shape=jax.ShapeDtypeStruct((b, s, C_W), BF16),
        grid=(b // nb, s // Q_WIDE),
        in_specs=[pl.BlockSpec((nb, Q_WIDE, C_W), lambda bi, i: (bi, i, 0)),
                  pl.BlockSpec((nb, s, C_W), lambda bi, i: (bi, 0, 0)),
                  pl.BlockSpec((nb, s, C_W), lambda bi, i: (bi, 0, 0)),
                  pl.BlockSpec((KEY_CHUNK, KEY_CHUNK), lambda bi, i: (0, 0)),
                  pl.BlockSpec((1, C_W), lambda bi, i: (0, 0))],
        out_specs=pl.BlockSpec((nb, Q_WIDE, C_W), lambda bi, i: (bi, i, 0)),
        scratch_shapes=[pltpu.VMEM((nb * C_HEADS, Q_WIDE, LANES), F32),
                        pltpu.VMEM((nb * C_HEADS, Q_WIDE, LANES), F32),
                        pltpu.VMEM((nb * C_HEADS // 2, 2 * Q_WIDE, KEY_CHUNK), F32),
                        pltpu.VMEM((nb * C_HEADS // 2, 2 * Q_WIDE, KEY_CHUNK), BF16)],
        compiler_params=pltpu.CompilerParams(dimension_semantics=("arbitrary", "arbitrary"),
                                             vmem_limit_bytes=VMEM_LIMIT),
        name="mixer_c",
    )(qc, kc, vc, tri, g)


def _outffn_kernel(x_ref, ma_ref, mb_ref, mc_ref, wo_ref, gf_ref, wg_ref, wu_ref, wd_ref, gl_ref, o_ref, *, final):
    x = (x_ref[...] + _dot(ma_ref[...], wo_ref[:A_Q, :]) + _dot(mb_ref[...], wo_ref[A_Q:A_Q + B_Q, :])
         + _dot(mc_ref[...], wo_ref[A_Q + B_Q:, :]))
    h = _rms(x, gf_ref[...]).astype(BF16)
    o_ref[...] = x
    for c in range(D_FF // FF_CHUNK):
        sl = slice(c * FF_CHUNK, (c + 1) * FF_CHUNK)
        gate = _dot(h, wg_ref[:, sl])
        up = _dot(h, wu_ref[:, sl])
        o_ref[...] += _dot((jax.nn.silu(gate) * up).astype(BF16), wd_ref[sl, :])
    if final:
        o_ref[...] = _rms(o_ref[...], gl_ref[...])


def _outffn(x2, ma, mb, mc, wo, gf, wg, wu, wd, gl, final):
    t = x2.shape[0]
    tm = TOKEN_TILE
    const = lambda i: (0, 0)
    tok = lambda i: (i, 0)
    return pl.pallas_call(
        functools.partial(_outffn_kernel, final=final),
        out_shape=jax.ShapeDtypeStruct((t, D_MODEL), F32),
        grid=(t // tm,),
        in_specs=[pl.BlockSpec((tm, D_MODEL), tok),
                  pl.BlockSpec((tm, A_Q), tok),
                  pl.BlockSpec((tm, B_Q), tok),
                  pl.BlockSpec((tm, C_W), tok),
                  pl.BlockSpec((MIX_WIDTH, D_MODEL), const),
                  pl.BlockSpec((1, D_MODEL), const),
                  pl.BlockSpec((D_MODEL, D_FF), const),
                  pl.BlockSpec((D_MODEL, D_FF), const),
                  pl.BlockSpec((D_FF, D_MODEL), const),
                  pl.BlockSpec((1, D_MODEL), const)],
        out_specs=pl.BlockSpec((tm, D_MODEL), tok),
        compiler_params=pltpu.CompilerParams(dimension_semantics=("arbitrary",), vmem_limit_bytes=VMEM_LIMIT),
        name="outproj_ffn",
    )(x2, ma, mb, mc, wo, gf, wg, wu, wd, gl)


def _regroup_w_in(w_in):
    sizes = (A_Q, HEAD_DIM, HEAD_DIM, B_Q, HEAD_DIM, HEAD_DIM, HEAD_DIM, HEAD_DIM, HEAD_DIM, HEAD_DIM,
             N_GATES, C_W, C_W, C_W)
    offs = np.concatenate([[0], np.cumsum(sizes)])
    qa, ka, va, qb, kcb, vcb, ksb, vsb, kwb, vwb, gb, qc, kc, vc = [w_in[:, offs[j]:offs[j + 1]]
                                                                     for j in range(len(sizes))]
    pad = jnp.zeros((w_in.shape[0], LANES - N_GATES), w_in.dtype)
    cols = [qa * SCALE, qb * SCALE, qc * SCALE, kc, vc, ka, va, ksb, vsb, kwb, vwb, kcb, vcb, gb, pad]
    return jnp.concatenate(cols, axis=1).astype(BF16)


def _layer(x2, b, s, w_in, w_out, g_attn, g_ffn, g_out_a, g_out_b, g_out_c, sinks, cmp_pos, cmp_w1, cmp_w2,
           w_gate, w_up, w_down, g_final, final):
    row = lambda v: v.reshape(1, -1)
    qa, kva, qb, kvs, kvw, kvr, gate, qc, kc, vc = _inproj(x2, row(g_attn), _regroup_w_in(w_in))
    r3 = lambda a: a.reshape(b, s, a.shape[-1])

    half = CMP_LEN // 2
    pos = jnp.concatenate([cmp_pos[0], cmp_pos[1]], axis=-1).reshape(2, half, 1, LANES)
    z1 = jnp.zeros_like(cmp_w1[0])
    w1 = jnp.concatenate([jnp.concatenate([cmp_w1[0], z1], axis=-1), jnp.concatenate([z1, cmp_w1[1]], axis=-1)], axis=1)
    w1 = w1.reshape(2, half, LANES, 2 * CMP_HIDDEN).astype(BF16)
    z2 = jnp.zeros_like(cmp_w2[0])
    w2 = jnp.concatenate([jnp.concatenate([cmp_w2[0], z2], axis=1), jnp.concatenate([z2, cmp_w2[1]], axis=1)],
                         axis=0).astype(BF16)
    kvc = _compress(r3(kvr), pos, w1, w2)

    ma = _mixer_a(sinks, r3(qa), r3(kva), row(g_out_a))
    mb = _mixer_b(r3(qb), r3(kvs), r3(kvw), kvc, r3(gate), row(g_out_b))
    mc = _mixer_c(r3(qc), r3(kc), r3(vc), row(g_out_c))
    t = b * s
    return _outffn(x2, ma.reshape(t, A_Q), mb.reshape(t, B_Q), mc.reshape(t, C_W), w_out.astype(BF16),
                   row(g_ffn), w_gate.astype(BF16), w_up.astype(BF16), w_down.astype(BF16), row(g_final), final)


def kernel(x, w_in, w_out, g_attn, g_ffn, g_out_a, g_out_b, g_out_c, sinks, cmp_pos, cmp_w1, cmp_w2, w_gate, w_up,
           w_down, g_final):
    b, s, d = x.shape
    depth = w_in.shape[0]
    assert d == D_MODEL and s % KEY_CHUNK == 0 and (b * s) % TOKEN_TILE == 0 and (b * s) % INPROJ_TILE == 0
    assert s >= B_WINDOW + Q_WIDE and s % SLC_CHUNK == 0 and s % A_TILE == 0 and (s // CMP_STRIDE) % LANES == 0
    x2 = x.reshape(b * s, d)
    for l in range(depth):
        x2 = _layer(x2, b, s, w_in[l], w_out[l], g_attn[l], g_ffn[l], g_out_a[l], g_out_b[l], g_out_c[l], sinks[l],
                    cmp_pos[l], cmp_w1[l], cmp_w2[l], w_gate[l], w_up[l], w_down[l], g_final, l == depth - 1)
    return x2.reshape(b, s, d)
```

```python
import functools

import jax
import jax.numpy as jnp
import numpy as np
from jax import lax
from jax.experimental import pallas as pl
from jax.experimental.pallas import tpu as pltpu

F32 = jnp.float32
BF16 = jnp.bfloat16

D_MODEL = 1024
HEAD_DIM = 64
LANES = 128
Q_BLOCK = 128
Q_WIDE = 256
A_TILE = 256
B_BATCH = 4
C_BATCH = 4
A_HEADS = 8
A_WINDOW = 128
B_HEADS = 4
CMP_LEN = 32
CMP_STRIDE = 16
CMP_HIDDEN = 128
SLC_LEN = 64
SLC_TOP = 8
B_WINDOW = 512
C_HEADS = 4
A_Q = A_HEADS * HEAD_DIM
B_Q = B_HEADS * HEAD_DIM
C_W = C_HEADS * HEAD_DIM
MIX_WIDTH = A_Q + B_Q + C_W
D_FF = 2816
N_GATES = B_HEADS * 3
NEG_INF = -1e30
MASK_BIG = 2.0 ** 100
SEL_LANE = 72
FORCE = 1e4
EPS = 1e-6
SCALE = HEAD_DIM ** -0.5
LOG2E = 1.4426950408889634

KEY_CHUNK = 256
SLC_CHUNK = 512
FF_CHUNK = 256
TOKEN_TILE = 512
INPROJ_TILE = 1024
VMEM_LIMIT = 56 * 1024 * 1024

_G_QA, _G_QB, _G_QC, _G_KC, _G_VC, _G_KVA, _G_KVS, _G_KVW, _G_KVC, _G_GATE, IN_PAD = (
    0, 512, 768, 1024, 1280, 1536, 1664, 1792, 1920, 2048, 2176)


def _alibi_slopes():
    n = A_HEADS + B_HEADS
    sl = 2.0 ** (-8.0 * np.arange(1, n + 1) / n) * LOG2E
    return [float(v) for v in sl[:A_HEADS]], [float(v) for v in sl[A_HEADS:]]


def _round_bf16(x):
    bits = np.ascontiguousarray(x, np.float32).view(np.uint32).astype(np.uint64)
    bits = ((bits + 0x7FFF + ((bits >> 16) & 1)) >> 16) << 16
    return bits.astype(np.uint32).view(np.float32)


def _key_features(pos, block_len=None):
    pos = np.asarray(pos, np.int64)
    assert pos.max() < 2048
    f = np.zeros((len(pos), LANES), np.float32)
    for j in range(3):
        f[:, HEAD_DIM + 2 * j] = 256 * (pos // 256)
        f[:, HEAD_DIM + 2 * j + 1] = pos % 256
    if block_len is not None:
        assert SEL_LANE + pos.max() // block_len < LANES
        f[np.arange(len(pos)), SEL_LANE + pos // block_len] = -MASK_BIG
    return jnp.asarray(f, BF16)


def _query_features(slopes, rows):
    f = np.zeros((len(slopes) * rows, LANES), np.float32)
    for h, slope in enumerate(slopes):
        rest = np.float32(slope)
        for j in range(3):
            part = _round_bf16(rest)
            rest = np.float32(rest - part)
            f[h * rows:(h + 1) * rows, HEAD_DIM + 2 * j:HEAD_DIM + 2 * j + 2] = part
    return jnp.asarray(f, BF16)


def _dot(a, b):
    return jnp.dot(a, b, preferred_element_type=F32)


def _dot_nt(a, b):
    return lax.dot_general(a, b, (((1,), (1,)), ((), ())), preferred_element_type=F32)


def _split_bf16(x):
    hi = x.astype(BF16)
    lo = (x - hi.astype(F32)).astype(BF16)
    return hi, lo


def _rms(x, g):
    ms = jnp.mean(x * x, axis=-1, keepdims=True)
    return x * lax.rsqrt(ms + EPS) * g


def _lane_tile(x, width):
    return jnp.concatenate([x] * (width // LANES), axis=1)


def _swap_halves(x):
    return jnp.concatenate([x[:, HEAD_DIM:], x[:, :HEAD_DIM]], axis=1)


def _stack_heads(q, n_heads, feat):
    rows = q.shape[0]
    lo = lax.broadcasted_iota(jnp.int32, (rows, LANES), 1) < HEAD_DIM
    tiles = []
    for p in range(n_heads // 2):
        pair = q[:, p * LANES:(p + 1) * LANES]
        tiles.append(jnp.where(lo, pair, feat[(2 * p) * rows:(2 * p + 1) * rows]))
        tiles.append(jnp.where(lo, _swap_halves(pair), feat[(2 * p + 1) * rows:(2 * p + 2) * rows]))
    return jnp.concatenate(tiles, axis=0)


def _with_key_features(kv, feat):
    lo = lax.broadcasted_iota(jnp.int32, kv.shape, 1) < HEAD_DIM
    return jnp.where(lo, kv, feat)


def _stack_pairs(q, n_heads):
    rows = q.shape[0]
    lo = lax.broadcasted_iota(jnp.int32, (rows, LANES), 1) < HEAD_DIM
    out = []
    for p in range(n_heads // 2):
        pair = q[:, p * LANES:(p + 1) * LANES]
        zero = jnp.zeros_like(pair)
        out.append(jnp.concatenate([jnp.where(lo, pair, zero), jnp.where(lo, zero, pair)], axis=0))
    return out


def _with_ones_for_keys(kv):
    lo = lax.broadcasted_iota(jnp.int32, kv.shape, 1) < HEAD_DIM
    return jnp.where(lo, jnp.ones_like(kv), kv)


def _unstack_normalized(o, n_heads):
    rows = o.shape[0] // n_heads
    lo = lax.broadcasted_iota(jnp.int32, (rows, LANES), 1) < HEAD_DIM
    pairs = []
    for p in range(n_heads // 2):
        even = o[(2 * p) * rows:(2 * p + 1) * rows]
        odd = o[(2 * p + 1) * rows:(2 * p + 2) * rows]
        pairs.append(jnp.where(lo, pltpu.roll(even, HEAD_DIM, 1) / even, odd / pltpu.roll(odd, HEAD_DIM, 1)))
    return jnp.concatenate(pairs, axis=1)


def _unstack_heads(o, n_heads):
    rows = o.shape[0] // n_heads
    lo = lax.broadcasted_iota(jnp.int32, (rows, LANES), 1) < HEAD_DIM
    pairs = []
    for p in range(n_heads // 2):
        even = o[(2 * p) * rows:(2 * p + 1) * rows]
        odd = o[(2 * p + 1) * rows:(2 * p + 2) * rows]
        pairs.append(jnp.where(lo, pltpu.roll(even, HEAD_DIM, 1), odd))
    return jnp.concatenate(pairs, axis=1)


def _inproj_kernel(x_ref, g_ref, w_ref, qa_ref, kva_ref, qb_ref, kvs_ref, kvw_ref, kvr_ref,
                   gate_ref, qc_ref, kc_ref, vc_ref):
    h = _rms(x_ref[...], g_ref[...]).astype(BF16)

    def proj(lo, hi):
        return _dot(h, w_ref[:, lo:hi])

    qa_ref[...] = (proj(_G_QA, _G_QB) * LOG2E).astype(BF16)
    qb_ref[...] = (proj(_G_QB, _G_QC) * LOG2E).astype(BF16)
    qc_ref[...] = (proj(_G_QC, _G_KC) * LOG2E).astype(BF16)
    kc_ref[...] = proj(_G_KC, _G_VC).astype(BF16)
    vc_ref[...] = proj(_G_VC, _G_KVA).astype(BF16)
    kv_as = proj(_G_KVA, _G_KVW)
    kva_ref[...] = kv_as[:, :LANES].astype(BF16)
    kvs_ref[...] = kv_as[:, LANES:].astype(BF16)
    kv_wc = proj(_G_KVW, _G_GATE)
    kvw_ref[...] = kv_wc[:, :LANES].astype(BF16)
    kvr_ref[...] = kv_wc[:, LANES:]
    gate_ref[...] = proj(_G_GATE, IN_PAD)


def _inproj(x2, g, w):
    t = x2.shape[0]
    tm = INPROJ_TILE
    widths = [(A_Q, BF16), (LANES, BF16), (B_Q, BF16), (LANES, BF16), (LANES, BF16), (LANES, F32),
              (LANES, F32), (C_W, BF16), (C_W, BF16), (C_W, BF16)]
    return pl.pallas_call(
        _inproj_kernel,
        out_shape=[jax.ShapeDtypeStruct((t, wd), dt) for wd, dt in widths],
        grid=(t // tm,),
        in_specs=[pl.BlockSpec((tm, D_MODEL), lambda i: (i, 0)),
                  pl.BlockSpec((1, D_MODEL), lambda i: (0, 0)),
                  pl.BlockSpec((D_MODEL, IN_PAD), lambda i: (0, 0))],
        out_specs=[pl.BlockSpec((tm, wd), lambda i: (i, 0)) for wd, _ in widths],
        compiler_params=pltpu.CompilerParams(dimension_semantics=("arbitrary",), vmem_limit_bytes=VMEM_LIMIT),
        name="inproj",
    )(x2, g, w)


def _compress_kernel(x_ref, pos_ref, w1_ref, w2_ref, o_ref):
    half = CMP_LEN // 2
    n = x_ref.shape[1] // CMP_STRIDE
    h_lo = jnp.zeros((n, 2 * CMP_HIDDEN), F32)
    h_hi = jnp.zeros((n, 2 * CMP_HIDDEN), F32)
    for l in range(half):
        x = x_ref[0, pl.ds(l, n, stride=CMP_STRIDE), :]
        h_lo = h_lo + _dot((x + pos_ref[0, l]).astype(BF16), w1_ref[0, l])
        h_hi = h_hi + _dot((x + pos_ref[1, l]).astype(BF16), w1_ref[1, l])
    h = h_lo + pltpu.roll(h_hi, n - 1, 0)
    o_ref[0] = _dot(jax.nn.gelu(h).astype(BF16), w2_ref[...]).astype(BF16)


def _compress(x, pos, w1, w2):
    b, s, _ = x.shape
    n = s // CMP_STRIDE
    half = CMP_LEN // 2
    return pl.pallas_call(
        _compress_kernel,
        out_shape=jax.ShapeDtypeStruct((b, n, LANES), BF16),
        grid=(b,),
        in_specs=[pl.BlockSpec((1, s, LANES), lambda i: (i, 0, 0)),
                  pl.BlockSpec((2, half, 1, LANES), lambda i: (0, 0, 0, 0)),
                  pl.BlockSpec((2, half, LANES, 2 * CMP_HIDDEN), lambda i: (0, 0, 0, 0)),
                  pl.BlockSpec((2 * CMP_HIDDEN, LANES), lambda i: (0, 0))],
        out_specs=pl.BlockSpec((1, n, LANES), lambda i: (i, 0, 0)),
        compiler_params=pltpu.CompilerParams(dimension_semantics=("arbitrary",), vmem_limit_bytes=VMEM_LIMIT),
        name="compress",
    )(x, pos, w1, w2)


def _mixa_kernel(sink_ref, q_ref, kvp_ref, kvc_ref, pfp_ref, pfc_ref, qf_ref, g_ref, o_ref, *, slopes):
    i = pl.program_id(1)
    assert A_WINDOW == Q_BLOCK
    span = 2 * Q_BLOCK
    upper = (lax.broadcasted_iota(jnp.int32, (Q_BLOCK, Q_BLOCK), 1)
             > lax.broadcasted_iota(jnp.int32, (Q_BLOCK, Q_BLOCK), 0))
    lo_q = lax.broadcasted_iota(jnp.int32, (Q_BLOCK, LANES), 1) < HEAD_DIM
    kv_rows = jnp.concatenate([kvp_ref[0], kvc_ref[0]], axis=0)
    pf_rows = jnp.concatenate([pfp_ref[...], pfc_ref[...]], axis=0)

    n_sub = A_TILE // Q_BLOCK
    kvs, scores = [], []
    for r in range(n_sub):
        q = q_ref[0, r * Q_BLOCK:(r + 1) * Q_BLOCK, :]
        kvs.append(kv_rows[r * Q_BLOCK:r * Q_BLOCK + span])
        kx = _with_key_features(kvs[r], pf_rows[r * Q_BLOCK:r * Q_BLOCK + span])
        scores.append(_dot_nt(_stack_heads(q, A_HEADS, qf_ref[...]), kx))
    for r in range(n_sub):
        kv, s_all = kvs[r], scores[r]
        t_row = (i * A_TILE + r * Q_BLOCK + lax.broadcasted_iota(jnp.int32, (Q_BLOCK, 1), 0)).astype(F32)
        probs, sink_w = [], []
        for hd in range(A_HEADS):
            sink = sink_ref[hd] * LOG2E + slopes[hd] * t_row
            s_prev = s_all[hd * Q_BLOCK:(hd + 1) * Q_BLOCK, :Q_BLOCK]
            if r == 0:
                s_prev = jnp.where(i > 0, s_prev, NEG_INF)
            s = jnp.where(upper, s_prev, s_all[hd * Q_BLOCK:(hd + 1) * Q_BLOCK, Q_BLOCK:])
            m = jnp.maximum(jnp.max(s, axis=-1, keepdims=True), sink)
            e = jnp.exp2(s - m)
            sink_w.append(jnp.exp2(sink - m))
            zero = jnp.zeros_like(e)
            probs.append(jnp.concatenate([jnp.where(upper, e, zero), jnp.where(upper, zero, e)], axis=1).astype(BF16))
        o_all = _dot(jnp.concatenate(probs, axis=0), _with_ones_for_keys(kv))
        o_all = jnp.concatenate([o_all[hd * Q_BLOCK:(hd + 1) * Q_BLOCK] + jnp.where(lo_q, sink_w[hd], 0.0)
                                 for hd in range(A_HEADS)], axis=0)
        o_ref[0, r * Q_BLOCK:(r + 1) * Q_BLOCK, :] = _rms(_unstack_normalized(o_all, A_HEADS), g_ref[...]).astype(BF16)


def _mixer_a(sinks, qa, kva, g):
    b, s, _ = qa.shape
    slopes, _ = _alibi_slopes()
    pos_feat = _key_features(np.arange(s))
    sub = A_TILE // Q_BLOCK
    prev_block = lambda i: jnp.maximum(sub * i - 1, 0)
    return pl.pallas_call(
        functools.partial(_mixa_kernel, slopes=slopes),
        out_shape=jax.ShapeDtypeStruct((b, s, A_Q), BF16),
        grid=(b, s // A_TILE),
        in_specs=[pl.BlockSpec(memory_space=pltpu.SMEM),
                  pl.BlockSpec((1, A_TILE, A_Q), lambda bi, i: (bi, i, 0)),
                  pl.BlockSpec((1, Q_BLOCK, LANES), lambda bi, i: (bi, prev_block(i), 0)),
                  pl.BlockSpec((1, A_TILE, LANES), lambda bi, i: (bi, i, 0)),
                  pl.BlockSpec((Q_BLOCK, LANES), lambda bi, i: (prev_block(i), 0)),
                  pl.BlockSpec((A_TILE, LANES), lambda bi, i: (i, 0)),
                  pl.BlockSpec((A_HEADS * Q_BLOCK, LANES), lambda bi, i: (0, 0)),
                  pl.BlockSpec((1, A_Q), lambda bi, i: (0, 0))],
        out_specs=pl.BlockSpec((1, A_TILE, A_Q), lambda bi, i: (bi, i, 0)),
        compiler_params=pltpu.CompilerParams(dimension_semantics=("arbitrary", "arbitrary"),
                                             vmem_limit_bytes=VMEM_LIMIT),
        name="mixer_a",
    )(sinks, qa, kva, kva, pos_feat, pos_feat, _query_features(slopes, Q_BLOCK), g)


def _mixb_kernel(q_ref, kvs_ref, kvw_ref, kvc_ref, gate_ref, pf_ref, sf_ref, cf_ref, qf_ref, mmap_ref, rep_ref,
                 g_ref, o_ref, acc_ref, m_ref, *, n_cmp, n_slc):
    i = pl.program_id(1)
    t0 = i * Q_WIDE
    nb = q_ref.shape[0]
    q_all = [_stack_heads(q_ref[bb], B_HEADS, qf_ref[...]) for bb in range(nb)]

    def head_rows(x, hd, rows=Q_WIDE):
        return x[hd * rows:(hd + 1) * rows]

    kvc = [kvc_ref[bb] for bb in range(nb)]
    n_pad = kvc[0].shape[0]
    s_cmp = [_dot_nt(q_all[bb], _with_key_features(kvc[bb], cf_ref[...])) for bb in range(nb)]
    wspan = B_WINDOW + Q_BLOCK
    row_w = lax.broadcasted_iota(jnp.int32, (Q_BLOCK, wspan), 0)
    col_w = lax.broadcasted_iota(jnp.int32, (Q_BLOCK, wspan), 1)
    n_sub = Q_WIDE // Q_BLOCK
    valid_w = []
    kvw = [[None] * n_sub for _ in range(nb)]
    s_win = [[None] * n_sub for _ in range(nb)]
    for r in range(n_sub):
        r0 = t0 + r * Q_BLOCK
        w0 = pl.multiple_of(jnp.clip(r0 - B_WINDOW, 0, kvw_ref.shape[1] - wspan), LANES)
        dist_w = (row_w - col_w) + (r0 - w0)
        valid_w.append((dist_w >= 0) & (dist_w < B_WINDOW))
        for bb in range(nb):
            kvw[bb][r] = kvw_ref[bb, pl.ds(w0, wspan), :]
            q_sub = jnp.concatenate([head_rows(q_all[bb], hd)[r * Q_BLOCK:(r + 1) * Q_BLOCK] for hd in range(B_HEADS)],
                                    axis=0)
            s_win[bb][r] = _dot_nt(q_sub, _with_key_features(kvw[bb][r], pf_ref[pl.ds(w0, wspan), :]))

    gates = []
    for bb in range(nb):
        sig = jax.nn.sigmoid(gate_ref[bb]).astype(BF16)
        gates.append([_dot(sig, rep_ref[r]) for r in range(3)])

    row_c = lax.broadcasted_iota(jnp.int32, (Q_WIDE, n_pad), 0)
    col_c = lax.broadcasted_iota(jnp.int32, (Q_WIDE, n_pad), 1)
    vis_c = ((t0 + row_c) >= (col_c * CMP_STRIDE + (CMP_LEN - 1))) & (col_c < n_cmp)
    row1 = lax.broadcasted_iota(jnp.int32, (Q_WIDE, 1), 0)
    any_vis = ((t0 + row1) >= (CMP_LEN - 1)).astype(F32)
    blk = lax.broadcasted_iota(jnp.int32, (n_slc, Q_WIDE), 0)
    tq = t0 + lax.broadcasted_iota(jnp.int32, (n_slc, Q_WIDE), 1)
    cur = tq // SLC_LEN
    valid_b = blk <= cur
    forced = (blk == 0) | (blk == cur) | (blk == cur - 1)
    lane_q = lax.broadcasted_iota(jnp.int32, (B_HEADS * Q_WIDE, LANES), 1)
    sel_lanes = (lane_q >= SEL_LANE) & (lane_q < SEL_LANE + n_slc)
    o_cmp, q_slc = [], []
    for bb in range(nb):
        probs = []
        p_sum = jnp.zeros((Q_WIDE, n_pad), F32)
        for hd in range(B_HEADS):
            s = jnp.where(vis_c, head_rows(s_cmp[bb], hd), NEG_INF)
            m = jnp.max(s, axis=-1, keepdims=True)
            e = jnp.exp2(s - m)
            pr = e * (any_vis / jnp.sum(e, axis=-1, keepdims=True))
            p_sum = p_sum + pr
            probs.append(pr.astype(BF16))
        o_cmp.append(_unstack_heads(_dot(jnp.concatenate(probs, axis=0), kvc[bb]), B_HEADS))
        p_hi, p_lo = _split_bf16(p_sum)
        imp = (_dot_nt(mmap_ref[...], p_hi) + _dot_nt(mmap_ref[...], p_lo))[:n_slc]
        val = jnp.where(forced, FORCE, jnp.where(valid_b, imp, -FORCE))
        rank = jnp.zeros((n_slc, Q_WIDE), F32)
        for j in range(n_slc):
            vj = val[j:j + 1, :]
            ahead = (vj > val) | ((vj == val) & (blk > j))
            rank = rank + ahead.astype(F32)
        unsel = 1.0 - ((rank < SLC_TOP) & valid_b).astype(F32)
        unsel = jnp.concatenate([jnp.zeros((SEL_LANE, Q_WIDE), F32), unsel,
                                 jnp.zeros((LANES - SEL_LANE - n_slc, Q_WIDE), F32)], axis=0)
        unsel_q = unsel.T.astype(BF16)
        q_slc.append(jnp.where(sel_lanes, jnp.concatenate([unsel_q] * B_HEADS, axis=0), q_all[bb]))

    ob_rest = []
    for bb in range(nb):
        o_sub = [[None] * n_sub for _ in range(B_HEADS)]
        for r in range(n_sub):
            probs = []
            for hd in range(B_HEADS):
                s = jnp.where(valid_w[r], head_rows(s_win[bb][r], hd, Q_BLOCK), NEG_INF)
                probs.append(jnp.exp2(s - jnp.max(s, axis=-1, keepdims=True)).astype(BF16))
            o_all = _dot(jnp.concatenate(probs, axis=0), _with_ones_for_keys(kvw[bb][r]))
            for hd in range(B_HEADS):
                o_sub[hd][r] = head_rows(o_all, hd, Q_BLOCK)
        o_win = _unstack_normalized(
            jnp.concatenate([o_sub[hd][r] for hd in range(B_HEADS) for r in range(n_sub)], axis=0), B_HEADS)
        ob_rest.append(gates[bb][0] * o_cmp[bb] + gates[bb][2] * o_win)

    acc_ref[...] = jnp.zeros(acc_ref.shape, F32)
    m_ref[...] = jnp.full(m_ref.shape, NEG_INF, F32)

    def slc_chunk(c, width):
        k0 = pl.multiple_of(c * SLC_CHUNK, SLC_CHUNK)
        row_k = lax.broadcasted_iota(jnp.int32, (Q_WIDE, width), 0)
        col_k = lax.broadcasted_iota(jnp.int32, (Q_WIDE, width), 1)
        causal = (col_k - row_k) <= (t0 - k0)
        feat = sf_ref[pl.ds(k0, width), :]
        for bb in range(nb):
            kv = kvs_ref[bb, pl.ds(k0, width), :]
            s_all = _dot_nt(q_slc[bb], _with_key_features(kv, feat))
            probs, alphas = [], []
            for hd in range(B_HEADS):
                g = bb * B_HEADS + hd
                s = jnp.where(causal, head_rows(s_all, hd), NEG_INF)
                m_old = m_ref[g]
                m_new = jnp.maximum(m_old, jnp.max(s, axis=-1, keepdims=True))
                alpha = jnp.exp2(m_old - m_new)
                m_ref[g] = m_new
                probs.append(jnp.exp2(s - _lane_tile(m_new, width)).astype(BF16))
                alphas.append(alpha)
            o_all = _dot(jnp.concatenate(probs, axis=0), _with_ones_for_keys(kv))
            for hd in range(B_HEADS):
                g = bb * B_HEADS + hd
                acc_ref[g] = alphas[hd] * acc_ref[g] + head_rows(o_all, hd)

    n_half = lax.shift_right_logical(t0 + Q_WIDE + SLC_CHUNK // 2 - 1, (SLC_CHUNK // 2).bit_length() - 1)
    n_full = lax.shift_right_logical(n_half, 1)

    def full_chunk(c, carry):
        slc_chunk(c, SLC_CHUNK)
        return carry

    lax.fori_loop(0, n_full, full_chunk, 0)

    @pl.when((n_half & 1) == 1)
    def _():
        slc_chunk(n_full, SLC_CHUNK // 2)

    for bb in range(nb):
        o_slc = _unstack_normalized(
            jnp.concatenate([acc_ref[bb * B_HEADS + hd] for hd in range(B_HEADS)], axis=0), B_HEADS)
        ob = ob_rest[bb] + gates[bb][1] * o_slc
        o_ref[bb] = _rms(ob, g_ref[...]).astype(BF16)


def _mixer_b(qb, kvs, kvw, kvc, gate, g):
    b, s, _ = qb.shape
    n_pad = kvc.shape[1]
    n_cmp = s // CMP_STRIDE - 1
    n_slc = s // SLC_LEN
    _, slopes = _alibi_slopes()
    pos_feat = _key_features(np.arange(s))
    slc_feat = _key_features(np.arange(s), SLC_LEN)
    cmp_feat = _key_features(np.minimum(np.arange(n_pad), n_cmp - 1) * CMP_STRIDE + (CMP_LEN - 1))
    cs = np.arange(n_pad)[None, :] * CMP_STRIDE
    ss = np.arange(LANES)[:, None] * SLC_LEN
    ov = np.maximum(0, np.minimum(cs + CMP_LEN, ss + SLC_LEN) - np.maximum(cs, ss)) / CMP_STRIDE
    ov = ov * (np.arange(n_pad)[None, :] < n_cmp) * (np.arange(LANES)[:, None] < n_slc)
    mmap_t = jnp.asarray(ov, BF16)
    rep = np.zeros((3, LANES, B_Q), np.float32)
    for r in range(3):
        for h in range(B_HEADS):
            rep[r, 3 * h + r, h * HEAD_DIM:(h + 1) * HEAD_DIM] = 1.0
    rep = jnp.asarray(rep, BF16)
    nb = B_BATCH if b % B_BATCH == 0 else 1
    return pl.pallas_call(
        functools.partial(_mixb_kernel, n_cmp=n_cmp, n_slc=n_slc),
        out_shape=jax.ShapeDtypeStruct((b, s, B_Q), BF16),
        grid=(b // nb, s // Q_WIDE),
        in_specs=[pl.BlockSpec((nb, Q_WIDE, B_Q), lambda bi, i: (bi, i, 0)),
                  pl.BlockSpec((nb, s, LANES), lambda bi, i: (bi, 0, 0)),
                  pl.BlockSpec((nb, s, LANES), lambda bi, i: (bi, 0, 0)),
                  pl.BlockSpec((nb, n_pad, LANES), lambda bi, i: (bi, 0, 0)),
                  pl.BlockSpec((nb, Q_WIDE, LANES), lambda bi, i: (bi, i, 0)),
                  pl.BlockSpec((s, LANES), lambda bi, i: (0, 0)),
                  pl.BlockSpec((s, LANES), lambda bi, i: (0, 0)),
                  pl.BlockSpec((n_pad, LANES), lambda bi, i: (0, 0)),
                  pl.BlockSpec((B_HEADS * Q_WIDE, LANES), lambda bi, i: (0, 0)),
                  pl.BlockSpec((LANES, n_pad), lambda bi, i: (0, 0)),
                  pl.BlockSpec((3, LANES, B_Q), lambda bi, i: (0, 0, 0)),
                  pl.BlockSpec((1, B_Q), lambda bi, i: (0, 0))],
        out_specs=pl.BlockSpec((nb, Q_WIDE, B_Q), lambda bi, i: (bi, i, 0)),
        scratch_shapes=[pltpu.VMEM((nb * B_HEADS, Q_WIDE, LANES), F32),
                        pltpu.VMEM((nb * B_HEADS, Q_WIDE, LANES), F32)],
        compiler_params=pltpu.CompilerParams(dimension_semantics=("arbitrary", "arbitrary"),
                                             vmem_limit_bytes=VMEM_LIMIT),
        name="mixer_b",
    )(qb, kvs, kvw, kvc, gate, pos_feat, slc_feat, cmp_feat, _query_features(slopes, Q_WIDE), mmap_t, rep, g)


def _mixc_kernel(q_ref, k_ref, v_ref, tri_ref, g_ref, o_ref, acc_ref, carry_ref, z_ref, w_ref):
    i = pl.program_id(1)
    t0 = i * Q_WIDE
    nb = q_ref.shape[0]
    n_pair = C_HEADS // 2
    q_pairs = [qp for bb in range(nb) for qp in _stack_pairs(q_ref[bb], C_HEADS)]
    acc_ref[...] = jnp.zeros(acc_ref.shape, F32)
    row = lax.broadcasted_iota(jnp.int32, (Q_WIDE, KEY_CHUNK), 0)
    col = lax.broadcasted_iota(jnp.int32, (Q_WIDE, KEY_CHUNK), 1)
    cr = col - row
    n_chunks = (t0 + Q_WIDE + KEY_CHUNK - 1) // KEY_CHUNK

    def chunk_rows(ref, bb, c):
        return ref[bb, pl.ds(pl.multiple_of(c * KEY_CHUNK, KEY_CHUNK), KEY_CHUNK), :]

    def logits_to_scratch(c):
        for bb in range(nb):
            k = chunk_rows(k_ref, bb, c)
            for p in range(n_pair):
                z_ref[bb * n_pair + p] = _dot_nt(q_pairs[bb * n_pair + p], k[:, p * LANES:(p + 1) * LANES])

    def add_values(c):
        for bb in range(nb):
            v = chunk_rows(v_ref, bb, c)
            for p in range(n_pair):
                o = _dot(w_ref[bb * n_pair + p], v[:, p * LANES:(p + 1) * LANES])
                g = bb * C_HEADS + 2 * p
                acc_ref[g] = acc_ref[g] + o[:Q_WIDE]
                acc_ref[g + 1] = acc_ref[g + 1] + o[Q_WIDE:]

    def drop_of(z):
        neg_abs = lax.bitcast_convert_type(lax.bitcast_convert_type(z, jnp.uint32) | jnp.uint32(0x80000000), F32)
        return jnp.maximum(z, 0.0) + jnp.log2(1.0 + jnp.exp2(neg_abs))

    def publish(weights):
        for u in range(nb * n_pair):
            w_ref[u] = jnp.concatenate(weights[2 * u:2 * u + 2], axis=0)

    n_all = nb * C_HEADS
    half = Q_WIDE // 2

    def diagonal_sweep(c):
        past_top, past_bot = (cr < 0)[:half, :half], (cr < 0)[half:]
        blank = jnp.zeros((half, KEY_CHUNK - half), F32)
        log_w, parts = [], []
        for g in range(n_all):
            z = z_ref[g // 2, (g % 2) * Q_WIDE:(g % 2 + 1) * Q_WIDE, :]
            z_top, z_bot = z[:half, :half], z[half:]
            drop_top, drop_bot = drop_of(z_top), drop_of(z_bot)
            dk_top, dk_bot = jnp.where(past_top, drop_top, 0.0), jnp.where(past_bot, drop_bot, 0.0)
            parts.append(jnp.concatenate([jnp.concatenate([dk_top, blank], axis=1), dk_bot], axis=0).astype(BF16))
            log_w.append((z_top - drop_top, z_bot - drop_bot))
            carry_ref[g] = jnp.broadcast_to(jnp.concatenate([jnp.sum(dk_top, axis=-1, keepdims=True),
                                                             jnp.sum(dk_bot, axis=-1, keepdims=True)], axis=0),
                                            (Q_WIDE, LANES))
        tails = _dot(jnp.concatenate(parts, axis=0), tri_ref[...])
        logits_to_scratch(jnp.maximum(c - 1, 0))
        weights = []
        for g in range(n_all):
            tail = tails[g * Q_WIDE:(g + 1) * Q_WIDE]
            w_top = jnp.where(past_top, jnp.exp2(log_w[g][0] - tail[:half, :half]), 0.0)
            w_bot = jnp.where(past_bot, jnp.exp2(log_w[g][1] - tail[half:]), 0.0)
            weights.append(jnp.concatenate([jnp.concatenate([w_top, blank], axis=1), w_bot], axis=0).astype(BF16))
        publish(weights)

    def sweep(c):
        add_values(c + 1)
        log_w, parts = [], []
        for g in range(n_all):
            z = z_ref[g // 2, (g % 2) * Q_WIDE:(g % 2 + 1) * Q_WIDE, :]
            drop = drop_of(z)
            parts.append(drop.astype(BF16))
            log_w.append(z - drop - _lane_tile(carry_ref[g], KEY_CHUNK))
            carry_ref[g] = carry_ref[g] + jnp.sum(drop, axis=-1, keepdims=True)
        tails = _dot(jnp.concatenate(parts, axis=0), tri_ref[...])
        logits_to_scratch(jnp.maximum(c - 1, 0))
        publish([jnp.exp2(log_w[g] - tails[g * Q_WIDE:(g + 1) * Q_WIDE]).astype(BF16) for g in range(n_all)])

    logits_to_scratch(n_chunks - 1)
    diagonal_sweep(n_chunks - 1)

    def earlier(j, carry):
        sweep(n_chunks - 1 - j)
        return carry

    lax.fori_loop(1, n_chunks, earlier, 0)
    add_values(0)
    lane = lax.broadcasted_iota(jnp.int32, (Q_WIDE, LANES), 1)
    lo = lane < HEAD_DIM
    for bb in range(nb):
        oc = jnp.concatenate([jnp.where(lo, acc_ref[bb * C_HEADS + 2 * p], acc_ref[bb * C_HEADS + 2 * p + 1])
                              for p in range(n_pair)], axis=1)
        o_ref[bb] = _rms(oc, g_ref[...]).astype(BF16)


def _mixer_c(qc, kc, vc, g):
    b, s, _ = qc.shape
    tri = jnp.asarray(np.arange(KEY_CHUNK)[:, None] > np.arange(KEY_CHUNK)[None, :], BF16)
    nb = C_BATCH if b % C_BATCH == 0 else 1
    assert Q_WIDE == KEY_CHUNK
    return pl.pallas_call(
        _mixc_kernel,
        out_shape=jax.ShapeDtypeStruct((b, s, C_W), BF16),
        grid=(b // nb, s // Q_WIDE),
        in_specs=[pl.BlockSpec((nb, Q_WIDE, C_W), lambda bi, i: (bi, i, 0)),
                  pl.BlockSpec((nb, s, C_W), lambda bi, i: (bi, 0, 0)),
                  pl.BlockSpec((nb, s, C_W), lambda bi, i: (bi, 0, 0)),
                  pl.BlockSpec((KEY_CHUNK, KEY_CHUNK), lambda bi, i: (0, 0)),
                  pl.BlockSpec((1, C_W), lambda bi, i: (0, 0))],
        out_specs=pl.BlockSpec((nb, Q_WIDE, C_W), lambda bi, i: (bi, i, 0)),
        scratch_shapes=[pltpu.VMEM((nb * C_HEADS, Q_WIDE, LANES), F32),
                        pltpu.VMEM((nb * C_HEADS, Q_WIDE, LANES), F32),
                        pltpu.VMEM((nb * C_HEADS // 2, 2 * Q_WIDE, KEY_CHUNK), F32),
                        pltpu.VMEM((nb * C_HEADS // 2, 2 * Q_WIDE, KEY_CHUNK), BF16)],
        compiler_params=pltpu.CompilerParams(dimension_semantics=("arbitrary", "arbitrary"),
                                             vmem_limit_bytes=VMEM_LIMIT),
        name="mixer_c",
    )(qc, kc, vc, tri, g)


def _outffn_kernel(x_ref, ma_ref, mb_ref, mc_ref, wo_ref, gf_ref, wg_ref, wu_ref, wd_ref, gl_ref, o_ref, *, final):
    x = (x_ref[...] + _dot(ma_ref[...], wo_ref[:A_Q, :]) + _dot(mb_ref[...], wo_ref[A_Q:A_Q + B_Q, :])
         + _dot(mc_ref[...], wo_ref[A_Q + B_Q:, :]))
    h = _rms(x, gf_ref[...]).astype(BF16)
    o_ref[...] = x
    for c in range(D_FF // FF_CHUNK):
        sl = slice(c * FF_CHUNK, (c + 1) * FF_CHUNK)
        gate = _dot(h, wg_ref[:, sl])
        up = _dot(h, wu_ref[:, sl])
        o_ref[...] += _dot((jax.nn.silu(gate) * up).astype(BF16), wd_ref[sl, :])
    if final:
        o_ref[...] = _rms(o_ref[...], gl_ref[...])


def _outffn(x2, ma, mb, mc, wo, gf, wg, wu, wd, gl, final):
    t = x2.shape[0]
    tm = TOKEN_TILE
    const = lambda i: (0, 0)
    tok = lambda i: (i, 0)
    return pl.pallas_call(
        functools.partial(_outffn_kernel, final=final),
        out_shape=jax.ShapeDtypeStruct((t, D_MODEL), F32),
        grid=(t // tm,),
        in_specs=[pl.BlockSpec((tm, D_MODEL), tok),
                  pl.BlockSpec((tm, A_Q), tok),
                  pl.BlockSpec((tm, B_Q), tok),
                  pl.BlockSpec((tm, C_W), tok),
                  pl.BlockSpec((MIX_WIDTH, D_MODEL), const),
                  pl.BlockSpec((1, D_MODEL), const),
                  pl.BlockSpec((D_MODEL, D_FF), const),
                  pl.BlockSpec((D_MODEL, D_FF), const),
                  pl.BlockSpec((D_FF, D_MODEL), const),
                  pl.BlockSpec((1, D_MODEL), const)],
        out_specs=pl.BlockSpec((tm, D_MODEL), tok),
        compiler_params=pltpu.CompilerParams(dimension_semantics=("arbitrary",), vmem_limit_bytes=VMEM_LIMIT),
        name="outproj_ffn",
    )(x2, ma, mb, mc, wo, gf, wg, wu, wd, gl)


def _regroup_w_in(w_in):
    sizes = (A_Q, HEAD_DIM, HEAD_DIM, B_Q, HEAD_DIM, HEAD_DIM, HEAD_DIM, HEAD_DIM, HEAD_DIM, HEAD_DIM,
             N_GATES, C_W, C_W, C_W)
    offs = np.concatenate([[0], np.cumsum(sizes)])
    qa, ka, va, qb, kcb, vcb, ksb, vsb, kwb, vwb, gb, qc, kc, vc = [w_in[:, offs[j]:offs[j + 1]]
                                                                     for j in range(len(sizes))]
    pad = jnp.zeros((w_in.shape[0], LANES - N_GATES), w_in.dtype)
    cols = [qa * SCALE, qb * SCALE, qc * SCALE, kc, vc, ka, va, ksb, vsb, kwb, vwb, kcb, vcb, gb, pad]
    return jnp.concatenate(cols, axis=1).astype(BF16)


def _layer(x2, b, s, w_in, w_out, g_attn, g_ffn, g_out_a, g_out_b, g_out_c, sinks, cmp_pos, cmp_w1, cmp_w2,
           w_gate, w_up, w_down, g_final, final):
    row = lambda v: v.reshape(1, -1)
    qa, kva, qb, kvs, kvw, kvr, gate, qc, kc, vc = _inproj(x2, row(g_attn), _regroup_w_in(w_in))
    r3 = lambda a: a.reshape(b, s, a.shape[-1])

    half = CMP_LEN // 2
    pos = jnp.concatenate([cmp_pos[0], cmp_pos[1]], axis=-1).reshape(2, half, 1, LANES)
    z1 = jnp.zeros_like(cmp_w1[0])
    w1 = jnp.concatenate([jnp.concatenate([cmp_w1[0], z1], axis=-1), jnp.concatenate([z1, cmp_w1[1]], axis=-1)], axis=1)
    w1 = w1.reshape(2, half, LANES, 2 * CMP_HIDDEN).astype(BF16)
    z2 = jnp.zeros_like(cmp_w2[0])
    w2 = jnp.concatenate([jnp.concatenate([cmp_w2[0], z2], axis=1), jnp.concatenate([z2, cmp_w2[1]], axis=1)],
                         axis=0).astype(BF16)
    kvc = _compress(r3(kvr), pos, w1, w2)

    ma = _mixer_a(sinks, r3(qa), r3(kva), row(g_out_a))
    mb = _mixer_b(r3(qb), r3(kvs), r3(kvw), kvc, r3(gate), row(g_out_b))
    mc = _mixer_c(r3(qc), r3(kc), r3(vc), row(g_out_c))
    t = b * s
    return _outffn(x2, ma.reshape(t, A_Q), mb.reshape(t, B_Q), mc.reshape(t, C_W), w_out.astype(BF16),
                   row(g_ffn), w_gate.astype(BF16), w_up.astype(BF16), w_down.astype(BF16), row(g_final), final)


def kernel(x, w_in, w_out, g_attn, g_ffn, g_out_a, g_out_b, g_out_c, sinks, cmp_pos, cmp_w1, cmp_w2, w_gate, w_up,
           w_down, g_final):
    b, s, d = x.shape
    depth = w_in.shape[0]
    assert d == D_MODEL and s % KEY_CHUNK == 0 and (b * s) % TOKEN_TILE == 0 and (b * s) % INPROJ_TILE == 0
    assert s >= B_WINDOW + Q_WIDE and s % SLC_CHUNK == 0 and s % A_TILE == 0 and (s // CMP_STRIDE) % LANES == 0
    x2 = x.reshape(b * s, d)
    for l in range(depth):
        x2 = _layer(x2, b, s, w_in[l], w_out[l], g_attn[l], g_ffn[l], g_out_a[l], g_out_b[l], g_out_c[l], sinks[l],
                    cmp_pos[l], cmp_w1[l], cmp_w2[l], w_gate[l], w_up[l], w_down[l], g_final, l == depth - 1)
    return x2.reshape(b, s, d)
```

```python
import functools

import jax
import jax.numpy as jnp
import numpy as np
from jax import lax
from jax.experimental import pallas as pl
from jax.experimental.pallas import tpu as pltpu

F32 = jnp.float32
BF16 = jnp.bfloat16

D_MODEL = 1024
HEAD_DIM = 64
LANES = 128
Q_BLOCK = 128
Q_WIDE = 256
A_TILE = 256
B_BATCH = 4
C_BATCH = 4
A_HEADS = 8
A_WINDOW = 128
B_HEADS = 4
CMP_LEN = 32
CMP_STRIDE = 16
CMP_HIDDEN = 128
SLC_LEN = 64
SLC_TOP = 8
B_WINDOW = 512
C_HEADS = 4
A_Q = A_HEADS * HEAD_DIM
B_Q = B_HEADS * HEAD_DIM
C_W = C_HEADS * HEAD_DIM
MIX_WIDTH = A_Q + B_Q + C_W
D_FF = 2816
N_GATES = B_HEADS * 3
NEG_INF = -1e30
MASK_BIG = 2.0 ** 100
SEL_LANE = 72
FORCE = 1e4
EPS = 1e-6
SCALE = HEAD_DIM ** -0.5
LOG2E = 1.4426950408889634

KEY_CHUNK = 256
SLC_CHUNK = 512
FF_CHUNK = 256
TOKEN_TILE = 512
INPROJ_TILE = 1024
VMEM_LIMIT = 56 * 1024 * 1024

_G_QA, _G_QB, _G_QC, _G_KC, _G_VC, _G_KVA, _G_KVS, _G_KVW, _G_KVC, _G_GATE, IN_PAD = (
    0, 512, 768, 1024, 1280, 1536, 1664, 1792, 1920, 2048, 2176)


def _alibi_slopes():
    n = A_HEADS + B_HEADS
    sl = 2.0 ** (-8.0 * np.arange(1, n + 1) / n) * LOG2E
    return [float(v) for v in sl[:A_HEADS]], [float(v) for v in sl[A_HEADS:]]


def _round_bf16(x):
    bits = np.ascontiguousarray(x, np.float32).view(np.uint32).astype(np.uint64)
    bits = ((bits + 0x7FFF + ((bits >> 16) & 1)) >> 16) << 16
    return bits.astype(np.uint32).view(np.float32)


def _key_features(pos, block_len=None):
    pos = np.asarray(pos, np.int64)
    assert pos.max() < 2048
    f = np.zeros((len(pos), LANES), np.float32)
    for j in range(3):
        f[:, HEAD_DIM + 2 * j] = 256 * (pos // 256)
        f[:, HEAD_DIM + 2 * j + 1] = pos % 256
    if block_len is not None:
        assert SEL_LANE + pos.max() // block_len < LANES
        f[np.arange(len(pos)), SEL_LANE + pos // block_len] = -MASK_BIG
    return jnp.asarray(f, BF16)


def _query_features(slopes, rows):
    f = np.zeros((len(slopes) * rows, LANES), np.float32)
    for h, slope in enumerate(slopes):
        rest = np.float32(slope)
        for j in range(3):
            part = _round_bf16(rest)
            rest = np.float32(rest - part)
            f[h * rows:(h + 1) * rows, HEAD_DIM + 2 * j:HEAD_DIM + 2 * j + 2] = part
    return jnp.asarray(f, BF16)


def _dot(a, b):
    return jnp.dot(a, b, preferred_element_type=F32)


def _dot_nt(a, b):
    return lax.dot_general(a, b, (((1,), (1,)), ((), ())), preferred_element_type=F32)


def _split_bf16(x):
    hi = x.astype(BF16)
    lo = (x - hi.astype(F32)).astype(BF16)
    return hi, lo


def _rms(x, g):
    ms = jnp.mean(x * x, axis=-1, keepdims=True)
    return x * lax.rsqrt(ms + EPS) * g


def _lane_tile(x, width):
    return jnp.concatenate([x] * (width // LANES), axis=1)


def _swap_halves(x):
    return jnp.concatenate([x[:, HEAD_DIM:], x[:, :HEAD_DIM]], axis=1)


def _stack_heads(q, n_heads, feat):
    rows = q.shape[0]
    lo = lax.broadcasted_iota(jnp.int32, (rows, LANES), 1) < HEAD_DIM
    tiles = []
    for p in range(n_heads // 2):
        pair = q[:, p * LANES:(p + 1) * LANES]
        tiles.append(jnp.where(lo, pair, feat[(2 * p) * rows:(2 * p + 1) * rows]))
        tiles.append(jnp.where(lo, _swap_halves(pair), feat[(2 * p + 1) * rows:(2 * p + 2) * rows]))
    return jnp.concatenate(tiles, axis=0)


def _with_key_features(kv, feat):
    lo = lax.broadcasted_iota(jnp.int32, kv.shape, 1) < HEAD_DIM
    return jnp.where(lo, kv, feat)


def _stack_pairs(q, n_heads):
    rows = q.shape[0]
    lo = lax.broadcasted_iota(jnp.int32, (rows, LANES), 1) < HEAD_DIM
    out = []
    for p in range(n_heads // 2):
        pair = q[:, p * LANES:(p + 1) * LANES]
        zero = jnp.zeros_like(pair)
        out.append(jnp.concatenate([jnp.where(lo, pair, zero), jnp.where(lo, zero, pair)], axis=0))
    return out


def _with_ones_for_keys(kv):
    lo = lax.broadcasted_iota(jnp.int32, kv.shape, 1) < HEAD_DIM
    return jnp.where(lo, jnp.ones_like(kv), kv)


def _unstack_normalized(o, n_heads):
    rows = o.shape[0] // n_heads
    lo = lax.broadcasted_iota(jnp.int32, (rows, LANES), 1) < HEAD_DIM
    pairs = []
    for p in range(n_heads // 2):
        even = o[(2 * p) * rows:(2 * p + 1) * rows]
        odd = o[(2 * p + 1) * rows:(2 * p + 2) * rows]
        pairs.append(jnp.where(lo, pltpu.roll(even, HEAD_DIM, 1) / even, odd / pltpu.roll(odd, HEAD_DIM, 1)))
    return jnp.concatenate(pairs, axis=1)


def _unstack_heads(o, n_heads):
    rows = o.shape[0] // n_heads
    lo = lax.broadcasted_iota(jnp.int32, (rows, LANES), 1) < HEAD_DIM
    pairs = []
    for p in range(n_heads // 2):
        even = o[(2 * p) * rows:(2 * p + 1) * rows]
        odd = o[(2 * p + 1) * rows:(2 * p + 2) * rows]
        pairs.append(jnp.where(lo, pltpu.roll(even, HEAD_DIM, 1), odd))
    return jnp.concatenate(pairs, axis=1)


def _inproj_kernel(x_ref, g_ref, w_ref, qa_ref, kva_ref, qb_ref, kvs_ref, kvw_ref, kvr_ref,
                   gate_ref, qc_ref, kc_ref, vc_ref):
    h = _rms(x_ref[...], g_ref[...]).astype(BF16)

    def proj(lo, hi):
        return _dot(h, w_ref[:, lo:hi])

    qa_ref[...] = (proj(_G_QA, _G_QB) * LOG2E).astype(BF16)
    qb_ref[...] = (proj(_G_QB, _G_QC) * LOG2E).astype(BF16)
    qc_ref[...] = (proj(_G_QC, _G_KC) * LOG2E).astype(BF16)
    kc_ref[...] = proj(_G_KC, _G_VC).astype(BF16)
    vc_ref[...] = proj(_G_VC, _G_KVA).astype(BF16)
    kv_as = proj(_G_KVA, _G_KVW)
    kva_ref[...] = kv_as[:, :LANES].astype(BF16)
    kvs_ref[...] = kv_as[:, LANES:].astype(BF16)
    kv_wc = proj(_G_KVW, _G_GATE)
    kvw_ref[...] = kv_wc[:, :LANES].astype(BF16)
    kvr_ref[...] = kv_wc[:, LANES:]
    gate_ref[...] = proj(_G_GATE, IN_PAD)


def _inproj(x2, g, w):
    t = x2.shape[0]
    tm = INPROJ_TILE
    widths = [(A_Q, BF16), (LANES, BF16), (B_Q, BF16), (LANES, BF16), (LANES, BF16), (LANES, F32),
              (LANES, F32), (C_W, BF16), (C_W, BF16), (C_W, BF16)]
    return pl.pallas_call(
        _inproj_kernel,
        out_shape=[jax.ShapeDtypeStruct((t, wd), dt) for wd, dt in widths],
        grid=(t // tm,),
        in_specs=[pl.BlockSpec((tm, D_MODEL), lambda i: (i, 0)),
                  pl.BlockSpec((1, D_MODEL), lambda i: (0, 0)),
                  pl.BlockSpec((D_MODEL, IN_PAD), lambda i: (0, 0))],
        out_specs=[pl.BlockSpec((tm, wd), lambda i: (i, 0)) for wd, _ in widths],
        compiler_params=pltpu.CompilerParams(dimension_semantics=("arbitrary",), vmem_limit_bytes=VMEM_LIMIT),
        name="inproj",
    )(x2, g, w)


def _compress_kernel(x_ref, pos_ref, w1_ref, w2_ref, o_ref):
    half = CMP_LEN // 2
    n = x_ref.shape[1] // CMP_STRIDE
    h_lo = jnp.zeros((n, 2 * CMP_HIDDEN), F32)
    h_hi = jnp.zeros((n, 2 * CMP_HIDDEN), F32)
    for l in range(half):
        x = x_ref[0, pl.ds(l, n, stride=CMP_STRIDE), :]
        h_lo = h_lo + _dot((x + pos_ref[0, l]).astype(BF16), w1_ref[0, l])
        h_hi = h_hi + _dot((x + pos_ref[1, l]).astype(BF16), w1_ref[1, l])
    h = h_lo + pltpu.roll(h_hi, n - 1, 0)
    o_ref[0] = _dot(jax.nn.gelu(h).astype(BF16), w2_ref[...]).astype(BF16)


def _compress(x, pos, w1, w2):
    b, s, _ = x.shape
    n = s // CMP_STRIDE
    half = CMP_LEN // 2
    return pl.pallas_call(
        _compress_kernel,
        out_shape=jax.ShapeDtypeStruct((b, n, LANES), BF16),
        grid=(b,),
        in_specs=[pl.BlockSpec((1, s, LANES), lambda i: (i, 0, 0)),
                  pl.BlockSpec((2, half, 1, LANES), lambda i: (0, 0, 0, 0)),
                  pl.BlockSpec((2, half, LANES, 2 * CMP_HIDDEN), lambda i: (0, 0, 0, 0)),
                  pl.BlockSpec((2 * CMP_HIDDEN, LANES), lambda i: (0, 0))],
        out_specs=pl.BlockSpec((1, n, LANES), lambda i: (i, 0, 0)),
        compiler_params=pltpu.CompilerParams(dimension_semantics=("arbitrary",), vmem_limit_bytes=VMEM_LIMIT),
        name="compress",
    )(x, pos, w1, w2)


def _mixa_tile(i, bb, sink_ref, q_ref, kvp_ref, kvc_ref, pf_prev, pf_cur, qf_ref, g_ref, o_ref, slopes):
    assert A_WINDOW == Q_BLOCK
    span = 2 * Q_BLOCK
    upper = (lax.broadcasted_iota(jnp.int32, (Q_BLOCK, Q_BLOCK), 1)
             > lax.broadcasted_iota(jnp.int32, (Q_BLOCK, Q_BLOCK), 0))
    lo_q = lax.broadcasted_iota(jnp.int32, (Q_BLOCK, LANES), 1) < HEAD_DIM
    kv_rows = jnp.concatenate([kvp_ref[bb], kvc_ref[bb]], axis=0)
    pf_rows = jnp.concatenate([pf_prev, pf_cur], axis=0)

    n_sub = A_TILE // Q_BLOCK
    kvs, scores = [], []
    for r in range(n_sub):
        q = q_ref[bb, r * Q_BLOCK:(r + 1) * Q_BLOCK, :]
        kvs.append(kv_rows[r * Q_BLOCK:r * Q_BLOCK + span])
        kx = _with_key_features(kvs[r], pf_rows[r * Q_BLOCK:r * Q_BLOCK + span])
        scores.append(_dot_nt(_stack_heads(q, A_HEADS, qf_ref[...]), kx))
    for r in range(n_sub):
        kv, s_all = kvs[r], scores[r]
        t_row = (i * A_TILE + r * Q_BLOCK + lax.broadcasted_iota(jnp.int32, (Q_BLOCK, 1), 0)).astype(F32)
        probs, sink_w = [], []
        for hd in range(A_HEADS):
            sink = sink_ref[hd] * LOG2E + slopes[hd] * t_row
            s_prev = s_all[hd * Q_BLOCK:(hd + 1) * Q_BLOCK, :Q_BLOCK]
            if r == 0:
                s_prev = jnp.where(i > 0, s_prev, NEG_INF)
            s = jnp.where(upper, s_prev, s_all[hd * Q_BLOCK:(hd + 1) * Q_BLOCK, Q_BLOCK:])
            m = jnp.maximum(jnp.max(s, axis=-1, keepdims=True), sink)
            e = jnp.exp2(s - m)
            sink_w.append(jnp.exp2(sink - m))
            zero = jnp.zeros_like(e)
            probs.append(jnp.concatenate([jnp.where(upper, e, zero), jnp.where(upper, zero, e)], axis=1).astype(BF16))
        o_all = _dot(jnp.concatenate(probs, axis=0), _with_ones_for_keys(kv))
        o_all = jnp.concatenate([o_all[hd * Q_BLOCK:(hd + 1) * Q_BLOCK] + jnp.where(lo_q, sink_w[hd], 0.0)
                                 for hd in range(A_HEADS)], axis=0)
        o_ref[bb, r * Q_BLOCK:(r + 1) * Q_BLOCK, :] = _rms(_unstack_normalized(o_all, A_HEADS), g_ref[...]).astype(BF16)


def _mixb_kernel(q_ref, kvs_ref, kvw_ref, kvc_ref, gate_ref, pf_ref, sf_ref, cf_ref, qf_ref, mmap_ref, rep_ref,
                 g_ref, sink_ref, qa_ref, kvap_ref, kvac_ref, qfa_ref, ga_ref, o_ref, oa_ref, acc_ref, m_ref,
                 *, n_cmp, n_slc, slopes_a):
    i = pl.program_id(1)
    t0 = i * Q_WIDE
    nb = q_ref.shape[0]
    q_all = [_stack_heads(q_ref[bb], B_HEADS, qf_ref[...]) for bb in range(nb)]

    def head_rows(x, hd, rows=Q_WIDE):
        return x[hd * rows:(hd + 1) * rows]

    kvc = [kvc_ref[bb] for bb in range(nb)]
    n_pad = kvc[0].shape[0]
    s_cmp = [_dot_nt(q_all[bb], _with_key_features(kvc[bb], cf_ref[...])) for bb in range(nb)]
    wspan = B_WINDOW + Q_BLOCK
    row_w = lax.broadcasted_iota(jnp.int32, (Q_BLOCK, wspan), 0)
    col_w = lax.broadcasted_iota(jnp.int32, (Q_BLOCK, wspan), 1)
    n_sub = Q_WIDE // Q_BLOCK
    valid_w = []
    kvw = [[None] * n_sub for _ in range(nb)]
    s_win = [[None] * n_sub for _ in range(nb)]
    for r in range(n_sub):
        r0 = t0 + r * Q_BLOCK
        w0 = pl.multiple_of(jnp.clip(r0 - B_WINDOW, 0, kvw_ref.shape[1] - wspan), LANES)
        dist_w = (row_w - col_w) + (r0 - w0)
        valid_w.append((dist_w >= 0) & (dist_w < B_WINDOW))
        for bb in range(nb):
            kvw[bb][r] = kvw_ref[bb, pl.ds(w0, wspan), :]
            q_sub = jnp.concatenate([head_rows(q_all[bb], hd)[r * Q_BLOCK:(r + 1) * Q_BLOCK] for hd in range(B_HEADS)],
                                    axis=0)
            s_win[bb][r] = _dot_nt(q_sub, _with_key_features(kvw[bb][r], pf_ref[pl.ds(w0, wspan), :]))

    gates = []
    for bb in range(nb):
        sig = jax.nn.sigmoid(gate_ref[bb]).astype(BF16)
        gates.append([_dot(sig, rep_ref[r]) for r in range(3)])

    row_c = lax.broadcasted_iota(jnp.int32, (Q_WIDE, n_pad), 0)
    col_c = lax.broadcasted_iota(jnp.int32, (Q_WIDE, n_pad), 1)
    vis_c = ((t0 + row_c) >= (col_c * CMP_STRIDE + (CMP_LEN - 1))) & (col_c < n_cmp)
    row1 = lax.broadcasted_iota(jnp.int32, (Q_WIDE, 1), 0)
    any_vis = ((t0 + row1) >= (CMP_LEN - 1)).astype(F32)
    blk = lax.broadcasted_iota(jnp.int32, (n_slc, Q_WIDE), 0)
    tq = t0 + lax.broadcasted_iota(jnp.int32, (n_slc, Q_WIDE), 1)
    cur = tq // SLC_LEN
    valid_b = blk <= cur
    forced = (blk == 0) | (blk == cur) | (blk == cur - 1)
    lane_q = lax.broadcasted_iota(jnp.int32, (B_HEADS * Q_WIDE, LANES), 1)
    sel_lanes = (lane_q >= SEL_LANE) & (lane_q < SEL_LANE + n_slc)
    o_cmp, q_slc = [], []
    for bb in range(nb):
        probs = []
        p_sum = jnp.zeros((Q_WIDE, n_pad), F32)
        for hd in range(B_HEADS):
            s = jnp.where(vis_c, head_rows(s_cmp[bb], hd), NEG_INF)
            m = jnp.max(s, axis=-1, keepdims=True)
            e = jnp.exp2(s - m)
            pr = e * (any_vis / jnp.sum(e, axis=-1, keepdims=True))
            p_sum = p_sum + pr
            probs.append(pr.astype(BF16))
        o_cmp.append(_unstack_heads(_dot(jnp.concatenate(probs, axis=0), kvc[bb]), B_HEADS))
        p_hi, p_lo = _split_bf16(p_sum)
        imp = (_dot_nt(mmap_ref[...], p_hi) + _dot_nt(mmap_ref[...], p_lo))[:n_slc]
        val = jnp.where(forced, FORCE, jnp.where(valid_b, imp, -FORCE))
        rank = jnp.zeros((n_slc, Q_WIDE), F32)
        for j in range(n_slc):
            vj = val[j:j + 1, :]
            ahead = (vj > val) | ((vj == val) & (blk > j))
            rank = rank + ahead.astype(F32)
        unsel = 1.0 - ((rank < SLC_TOP) & valid_b).astype(F32)
        unsel = jnp.concatenate([jnp.zeros((SEL_LANE, Q_WIDE), F32), unsel,
                                 jnp.zeros((LANES - SEL_LANE - n_slc, Q_WIDE), F32)], axis=0)
        unsel_q = unsel.T.astype(BF16)
        q_slc.append(jnp.where(sel_lanes, jnp.concatenate([unsel_q] * B_HEADS, axis=0), q_all[bb]))

    ob_rest = []
    for bb in range(nb):
        o_sub = [[None] * n_sub for _ in range(B_HEADS)]
        for r in range(n_sub):
            probs = []
            for hd in range(B_HEADS):
                s = jnp.where(valid_w[r], head_rows(s_win[bb][r], hd, Q_BLOCK), NEG_INF)
                probs.append(jnp.exp2(s - jnp.max(s, axis=-1, keepdims=True)).astype(BF16))
            o_all = _dot(jnp.concatenate(probs, axis=0), _with_ones_for_keys(kvw[bb][r]))
            for hd in range(B_HEADS):
                o_sub[hd][r] = head_rows(o_all, hd, Q_BLOCK)
        o_win = _unstack_normalized(
            jnp.concatenate([o_sub[hd][r] for hd in range(B_HEADS) for r in range(n_sub)], axis=0), B_HEADS)
        ob_rest.append(gates[bb][0] * o_cmp[bb] + gates[bb][2] * o_win)

    pf_prev = pf_ref[pl.ds(pl.multiple_of(jnp.maximum(t0 - Q_BLOCK, 0), Q_BLOCK), Q_BLOCK), :]
    pf_cur = pf_ref[pl.ds(pl.multiple_of(t0, Q_WIDE), Q_WIDE), :]
    for bb in range(nb):
        _mixa_tile(i, bb, sink_ref, qa_ref, kvap_ref, kvac_ref, pf_prev, pf_cur, qfa_ref, ga_ref, oa_ref, slopes_a)

    acc_ref[...] = jnp.zeros(acc_ref.shape, F32)
    m_ref[...] = jnp.full(m_ref.shape, NEG_INF, F32)

    def slc_chunk(c, width):
        k0 = pl.multiple_of(c * SLC_CHUNK, SLC_CHUNK)
        row_k = lax.broadcasted_iota(jnp.int32, (Q_WIDE, width), 0)
        col_k = lax.broadcasted_iota(jnp.int32, (Q_WIDE, width), 1)
        causal = (col_k - row_k) <= (t0 - k0)
        feat = sf_ref[pl.ds(k0, width), :]
        for bb in range(nb):
            kv = kvs_ref[bb, pl.ds(k0, width), :]
            s_all = _dot_nt(q_slc[bb], _with_key_features(kv, feat))
            probs, alphas = [], []
            for hd in range(B_HEADS):
                g = bb * B_HEADS + hd
                s = jnp.where(causal, head_rows(s_all, hd), NEG_INF)
                m_old = m_ref[g]
                m_new = jnp.maximum(m_old, jnp.max(s, axis=-1, keepdims=True))
                alpha = jnp.exp2(m_old - m_new)
                m_ref[g] = m_new
                probs.append(jnp.exp2(s - _lane_tile(m_new, width)).astype(BF16))
                alphas.append(alpha)
            o_all = _dot(jnp.concatenate(probs, axis=0), _with_ones_for_keys(kv))
            for hd in range(B_HEADS):
                g = bb * B_HEADS + hd
                acc_ref[g] = alphas[hd] * acc_ref[g] + head_rows(o_all, hd)

    n_half = lax.shift_right_logical(t0 + Q_WIDE + SLC_CHUNK // 2 - 1, (SLC_CHUNK // 2).bit_length() - 1)
    n_full = lax.shift_right_logical(n_half, 1)

    def full_chunk(c, carry):
        slc_chunk(c, SLC_CHUNK)
        return carry

    lax.fori_loop(0, n_full, full_chunk, 0)

    @pl.when((n_half & 1) == 1)
    def _():
        slc_chunk(n_full, SLC_CHUNK // 2)

    for bb in range(nb):
        o_slc = _unstack_normalized(
            jnp.concatenate([acc_ref[bb * B_HEADS + hd] for hd in range(B_HEADS)], axis=0), B_HEADS)
        ob = ob_rest[bb] + gates[bb][1] * o_slc
        o_ref[bb] = _rms(ob, g_ref[...]).astype(BF16)


def _mixers_ab(qb, kvs, kvw, kvc, gate, g, sinks, qa, kva, g_a):
    b, s, _ = qb.shape
    assert A_TILE == Q_WIDE
    n_pad = kvc.shape[1]
    n_cmp = s // CMP_STRIDE - 1
    n_slc = s // SLC_LEN
    slopes_a, slopes = _alibi_slopes()
    prev_block = lambda i: jnp.maximum((Q_WIDE // Q_BLOCK) * i - 1, 0)
    pos_feat = _key_features(np.arange(s))
    slc_feat = _key_features(np.arange(s), SLC_LEN)
    cmp_feat = _key_features(np.minimum(np.arange(n_pad), n_cmp - 1) * CMP_STRIDE + (CMP_LEN - 1))
    cs = np.arange(n_pad)[None, :] * CMP_STRIDE
    ss = np.arange(LANES)[:, None] * SLC_LEN
    ov = np.maximum(0, np.minimum(cs + CMP_LEN, ss + SLC_LEN) - np.maximum(cs, ss)) / CMP_STRIDE
    ov = ov * (np.arange(n_pad)[None, :] < n_cmp) * (np.arange(LANES)[:, None] < n_slc)
    mmap_t = jnp.asarray(ov, BF16)
    rep = np.zeros((3, LANES, B_Q), np.float32)
    for r in range(3):
        for h in range(B_HEADS):
            rep[r, 3 * h + r, h * HEAD_DIM:(h + 1) * HEAD_DIM] = 1.0
    rep = jnp.asarray(rep, BF16)
    nb = B_BATCH if b % B_BATCH == 0 else 1
    return pl.pallas_call(
        functools.partial(_mixb_kernel, n_cmp=n_cmp, n_slc=n_slc, slopes_a=slopes_a),
        out_shape=[jax.ShapeDtypeStruct((b, s, B_Q), BF16), jax.ShapeDtypeStruct((b, s, A_Q), BF16)],
        grid=(b // nb, s // Q_WIDE),
        in_specs=[pl.BlockSpec((nb, Q_WIDE, B_Q), lambda bi, i: (bi, i, 0)),
                  pl.BlockSpec((nb, s, LANES), lambda bi, i: (bi, 0, 0)),
                  pl.BlockSpec((nb, s, LANES), lambda bi, i: (bi, 0, 0)),
                  pl.BlockSpec((nb, n_pad, LANES), lambda bi, i: (bi, 0, 0)),
                  pl.BlockSpec((nb, Q_WIDE, LANES), lambda bi, i: (bi, i, 0)),
                  pl.BlockSpec((s, LANES), lambda bi, i: (0, 0)),
                  pl.BlockSpec((s, LANES), lambda bi, i: (0, 0)),
                  pl.BlockSpec((n_pad, LANES), lambda bi, i: (0, 0)),
                  pl.BlockSpec((B_HEADS * Q_WIDE, LANES), lambda bi, i: (0, 0)),
                  pl.BlockSpec((LANES, n_pad), lambda bi, i: (0, 0)),
                  pl.BlockSpec((3, LANES, B_Q), lambda bi, i: (0, 0, 0)),
                  pl.BlockSpec((1, B_Q), lambda bi, i: (0, 0)),
                  pl.BlockSpec(memory_space=pltpu.SMEM),
                  pl.BlockSpec((nb, Q_WIDE, A_Q), lambda bi, i: (bi, i, 0)),
                  pl.BlockSpec((nb, Q_BLOCK, LANES), lambda bi, i: (bi, prev_block(i), 0)),
                  pl.BlockSpec((nb, Q_WIDE, LANES), lambda bi, i: (bi, i, 0)),
                  pl.BlockSpec((A_HEADS * Q_BLOCK, LANES), lambda bi, i: (0, 0)),
                  pl.BlockSpec((1, A_Q), lambda bi, i: (0, 0))],
        out_specs=[pl.BlockSpec((nb, Q_WIDE, B_Q), lambda bi, i: (bi, i, 0)),
                   pl.BlockSpec((nb, Q_WIDE, A_Q), lambda bi, i: (bi, i, 0))],
        scratch_shapes=[pltpu.VMEM((nb * B_HEADS, Q_WIDE, LANES), F32),
                        pltpu.VMEM((nb * B_HEADS, Q_WIDE, LANES), F32)],
        compiler_params=pltpu.CompilerParams(dimension_semantics=("arbitrary", "arbitrary"),
                                             vmem_limit_bytes=VMEM_LIMIT),
        name="mixer_b",
    )(qb, kvs, kvw, kvc, gate, pos_feat, slc_feat, cmp_feat, _query_features(slopes, Q_WIDE), mmap_t, rep, g,
      sinks, qa, kva, kva, _query_features(slopes_a, Q_BLOCK), g_a)


def _mixc_kernel(q_ref, k_ref, v_ref, tri_ref, g_ref, o_ref, acc_ref, carry_ref, z_ref, w_ref):
    i = pl.program_id(1)
    t0 = i * Q_WIDE
    nb = q_ref.shape[0]
    n_pair = C_HEADS // 2
    q_pairs = [qp for bb in range(nb) for qp in _stack_pairs(q_ref[bb], C_HEADS)]
    acc_ref[...] = jnp.zeros(acc_ref.shape, F32)
    row = lax.broadcasted_iota(jnp.int32, (Q_WIDE, KEY_CHUNK), 0)
    col = lax.broadcasted_iota(jnp.int32, (Q_WIDE, KEY_CHUNK), 1)
    cr = col - row
    n_chunks = (t0 + Q_WIDE + KEY_CHUNK - 1) // KEY_CHUNK

    def chunk_rows(ref, bb, c):
        return ref[bb, pl.ds(pl.multiple_of(c * KEY_CHUNK, KEY_CHUNK), KEY_CHUNK), :]

    def logits_to_scratch(c):
        for bb in range(nb):
            k = chunk_rows(k_ref, bb, c)
            for p in range(n_pair):
                z_ref[bb * n_pair + p] = _dot_nt(q_pairs[bb * n_pair + p], k[:, p * LANES:(p + 1) * LANES])

    def add_values(c):
        for bb in range(nb):
            v = chunk_rows(v_ref, bb, c)
            for p in range(n_pair):
                o = _dot(w_ref[bb * n_pair + p], v[:, p * LANES:(p + 1) * LANES])
                g = bb * C_HEADS + 2 * p
                acc_ref[g] = acc_ref[g] + o[:Q_WIDE]
                acc_ref[g + 1] = acc_ref[g + 1] + o[Q_WIDE:]

    def drop_of(z):
        neg_abs = lax.bitcast_convert_type(lax.bitcast_convert_type(z, jnp.uint32) | jnp.uint32(0x80000000), F32)
        return jnp.maximum(z, 0.0) + jnp.log2(1.0 + jnp.exp2(neg_abs))

    def publish(weights):
        for u in range(nb * n_pair):
            w_ref[u] = jnp.concatenate(weights[2 * u:2 * u + 2], axis=0)

    n_all = nb * C_HEADS
    half = Q_WIDE // 2

    def diagonal_sweep(c):
        past_top, past_bot = (cr < 0)[:half, :half], (cr < 0)[half:]
        blank = jnp.zeros((half, KEY_CHUNK - half), F32)
        log_w, parts = [], []
        for g in range(n_all):
            z = z_ref[g // 2, (g % 2) * Q_WIDE:(g % 2 + 1) * Q_WIDE, :]
            z_top, z_bot = z[:half, :half], z[half:]
            drop_top, drop_bot = drop_of(z_top), drop_of(z_bot)
            dk_top, dk_bot = jnp.where(past_top, drop_top, 0.0), jnp.where(past_bot, drop_bot, 0.0)
            parts.append(jnp.concatenate([jnp.concatenate([dk_top, blank], axis=1), dk_bot], axis=0).astype(BF16))
            log_w.append((z_top - drop_top, z_bot - drop_bot))
            carry_ref[g] = jnp.broadcast_to(jnp.concatenate([jnp.sum(dk_top, axis=-1, keepdims=True),
                                                             jnp.sum(dk_bot, axis=-1, keepdims=True)], axis=0),
                                            (Q_WIDE, LANES))
        tails = _dot(jnp.concatenate(parts, axis=0), tri_ref[...])
        logits_to_scratch(jnp.maximum(c - 1, 0))
        weights = []
        for g in range(n_all):
            tail = tails[g * Q_WIDE:(g + 1) * Q_WIDE]
            w_top = jnp.where(past_top, jnp.exp2(log_w[g][0] - tail[:half, :half]), 0.0)
            w_bot = jnp.where(past_bot, jnp.exp2(log_w[g][1] - tail[half:]), 0.0)
            weights.append(jnp.concatenate([jnp.concatenate([w_top, blank], axis=1), w_bot], axis=0).astype(BF16))
        publish(weights)

    def sweep(c):
        add_values(c + 1)
        log_w, parts = [], []
        for g in range(n_all):
            z = z_ref[g // 2, (g % 2) * Q_WIDE:(g % 2 + 1) * Q_WIDE, :]
            drop = drop_of(z)
            parts.append(drop.astype(BF16))
            log_w.append(z - drop - _lane_tile(carry_ref[g], KEY_CHUNK))
            carry_ref[g] = carry_ref[g] + jnp.sum(drop, axis=-1, keepdims=True)
        tails = _dot(jnp.concatenate(parts, axis=0), tri_ref[...])
        logits_to_scratch(jnp.maximum(c - 1, 0))
        publish([jnp.exp2(log_w[g] - tails[g * Q_WIDE:(g + 1) * Q_WIDE]).astype(BF16) for g in range(n_all)])

    logits_to_scratch(n_chunks - 1)
    diagonal_sweep(n_chunks - 1)

    def earlier(j, carry):
        sweep(n_chunks - 1 - j)
        return carry

    lax.fori_loop(1, n_chunks, earlier, 0)
    add_values(0)
    lane = lax.broadcasted_iota(jnp.int32, (Q_WIDE, LANES), 1)
    lo = lane < HEAD_DIM
    for bb in range(nb):
        oc = jnp.concatenate([jnp.where(lo, acc_ref[bb * C_HEADS + 2 * p], acc_ref[bb * C_HEADS + 2 * p + 1])
                              for p in range(n_pair)], axis=1)
        o_ref[bb] = _rms(oc, g_ref[...]).astype(BF16)


def _mixer_c(qc, kc, vc, g):
    b, s, _ = qc.shape
    tri = jnp.asarray(np.arange(KEY_CHUNK)[:, None] > np.arange(KEY_CHUNK)[None, :], BF16)
    nb = C_BATCH if b % C_BATCH == 0 else 1
    assert Q_WIDE == KEY_CHUNK
    return pl.pallas_call(
        _mixc_kernel,
        out_shape=jax.ShapeDtypeStruct((b, s, C_W), BF16),
        grid=(b // nb, s // Q_WIDE),
        in_specs=[pl.BlockSpec((nb, Q_WIDE, C_W), lambda bi, i: (bi, i, 0)),
                  pl.BlockSpec((nb, s, C_W), lambda bi, i: (bi, 0, 0)),
                  pl.BlockSpec((nb, s, C_W), lambda bi, i: (bi, 0, 0)),
                  pl.BlockSpec((KEY_CHUNK, KEY_CHUNK), lambda bi, i: (0, 0)),
                  pl.BlockSpec((1, C_W), lambda bi, i: (0, 0))],
        out_specs=pl.BlockSpec((nb, Q_WIDE, C_W), lambda bi, i: (bi, i, 0)),
        scratch_shapes=[pltpu.VMEM((nb * C_HEADS, Q_WIDE, LANES), F32),
                        pltpu.VMEM((nb * C_HEADS, Q_WIDE, LANES), F32),
                        pltpu.VMEM((nb * C_HEADS // 2, 2 * Q_WIDE, KEY_CHUNK), F32),
                        pltpu.VMEM((nb * C_HEADS // 2, 2 * Q_WIDE, KEY_CHUNK), BF16)],
        compiler_params=pltpu.CompilerParams(dimension_semantics=("arbitrary", "arbitrary"),
                                             vmem_limit_bytes=VMEM_LIMIT),
        name="mixer_c",
    )(qc, kc, vc, tri, g)


def _outffn_kernel(x_ref, ma_ref, mb_ref, mc_ref, wo_ref, gf_ref, wg_ref, wu_ref, wd_ref, gl_ref, o_ref, *, final):
    x = (x_ref[...] + _dot(ma_ref[...], wo_ref[:A_Q, :]) + _dot(mb_ref[...], wo_ref[A_Q:A_Q + B_Q, :])
         + _dot(mc_ref[...], wo_ref[A_Q + B_Q:, :]))
    h = _rms(x, gf_ref[...]).astype(BF16)
    o_ref[...] = x
    for c in range(D_FF // FF_CHUNK):
        sl = slice(c * FF_CHUNK, (c + 1) * FF_CHUNK)
        gate = _dot(h, wg_ref[:, sl])
        up = _dot(h, wu_ref[:, sl])
        o_ref[...] += _dot((jax.nn.silu(gate) * up).astype(BF16), wd_ref[sl, :])
    if final:
        o_ref[...] = _rms(o_ref[...], gl_ref[...])


def _outffn(x2, ma, mb, mc, wo, gf, wg, wu, wd, gl, final):
    t = x2.shape[0]
    tm = TOKEN_TILE
    const = lambda i: (0, 0)
    tok = lambda i: (i, 0)
    return pl.pallas_call(
        functools.partial(_outffn_kernel, final=final),
        out_shape=jax.ShapeDtypeStruct((t, D_MODEL), F32),
        grid=(t // tm,),
        in_specs=[pl.BlockSpec((tm, D_MODEL), tok),
                  pl.BlockSpec((tm, A_Q), tok),
                  pl.BlockSpec((tm, B_Q), tok),
                  pl.BlockSpec((tm, C_W), tok),
                  pl.BlockSpec((MIX_WIDTH, D_MODEL), const),
                  pl.BlockSpec((1, D_MODEL), const),
                  pl.BlockSpec((D_MODEL, D_FF), const),
                  pl.BlockSpec((D_MODEL, D_FF), const),
                  pl.BlockSpec((D_FF, D_MODEL), const),
                  pl.BlockSpec((1, D_MODEL), const)],
        out_specs=pl.BlockSpec((tm, D_MODEL), tok),
        compiler_params=pltpu.CompilerParams(dimension_semantics=("arbitrary",), vmem_limit_bytes=VMEM_LIMIT),
        name="outproj_ffn",
    )(x2, ma, mb, mc, wo, gf, wg, wu, wd, gl)


def _regroup_w_in(w_in):
    sizes = (A_Q, HEAD_DIM, HEAD_DIM, B_Q, HEAD_DIM, HEAD_DIM, HEAD_DIM, HEAD_DIM, HEAD_DIM, HEAD_DIM,
             N_GATES, C_W, C_W, C_W)
    offs = np.concatenate([[0], np.cumsum(sizes)])
    qa, ka, va, qb, kcb, vcb, ksb, vsb, kwb, vwb, gb, qc, kc, vc = [w_in[:, offs[j]:offs[j + 1]]
                                                                     for j in range(len(sizes))]
    pad = jnp.zeros((w_in.shape[0], LANES - N_GATES), w_in.dtype)
    cols = [qa * SCALE, qb * SCALE, qc * SCALE, kc, vc, ka, va, ksb, vsb, kwb, vwb, kcb, vcb, gb, pad]
    return jnp.concatenate(cols, axis=1).astype(BF16)


def _layer(x2, b, s, w_in, w_out, g_attn, g_ffn, g_out_a, g_out_b, g_out_c, sinks, cmp_pos, cmp_w1, cmp_w2,
           w_gate, w_up, w_down, g_final, final):
    row = lambda v: v.reshape(1, -1)
    qa, kva, qb, kvs, kvw, kvr, gate, qc, kc, vc = _inproj(x2, row(g_attn), _regroup_w_in(w_in))
    r3 = lambda a: a.reshape(b, s, a.shape[-1])

    half = CMP_LEN // 2
    pos = jnp.concatenate([cmp_pos[0], cmp_pos[1]], axis=-1).reshape(2, half, 1, LANES)
    z1 = jnp.zeros_like(cmp_w1[0])
    w1 = jnp.concatenate([jnp.concatenate([cmp_w1[0], z1], axis=-1), jnp.concatenate([z1, cmp_w1[1]], axis=-1)], axis=1)
    w1 = w1.reshape(2, half, LANES, 2 * CMP_HIDDEN).astype(BF16)
    z2 = jnp.zeros_like(cmp_w2[0])
    w2 = jnp.concatenate([jnp.concatenate([cmp_w2[0], z2], axis=1), jnp.concatenate([z2, cmp_w2[1]], axis=1)],
                         axis=0).astype(BF16)
    kvc = _compress(r3(kvr), pos, w1, w2)

    mb, ma = _mixers_ab(r3(qb), r3(kvs), r3(kvw), kvc, r3(gate), row(g_out_b), sinks, r3(qa), r3(kva), row(g_out_a))
    mc = _mixer_c(r3(qc), r3(kc), r3(vc), row(g_out_c))
    t = b * s
    return _outffn(x2, ma.reshape(t, A_Q), mb.reshape(t, B_Q), mc.reshape(t, C_W), w_out.astype(BF16),
                   row(g_ffn), w_gate.astype(BF16), w_up.astype(BF16), w_down.astype(BF16), row(g_final), final)


def kernel(x, w_in, w_out, g_attn, g_ffn, g_out_a, g_out_b, g_out_c, sinks, cmp_pos, cmp_w1, cmp_w2, w_gate, w_up,
           w_down, g_final):
    b, s, d = x.shape
    depth = w_in.shape[0]
    assert d == D_MODEL and s % KEY_CHUNK == 0 and (b * s) % TOKEN_TILE == 0 and (b * s) % INPROJ_TILE == 0
    assert s >= B_WINDOW + Q_WIDE and s % SLC_CHUNK == 0 and s % A_TILE == 0 and (s // CMP_STRIDE) % LANES == 0
    x2 = x.reshape(b * s, d)
    for l in range(depth):
        x2 = _layer(x2, b, s, w_in[l], w_out[l], g_attn[l], g_ffn[l], g_out_a[l], g_out_b[l], g_out_c[l], sinks[l],
                    cmp_pos[l], cmp_w1[l], cmp_w2[l], w_gate[l], w_up[l], w_down[l], g_final, l == depth - 1)
    return x2.reshape(b, s, d)
```
